```python
import jax, jax.numpy as jnp
from jax import lax
import numpy as np

D_MODEL = 2048
BATCH = 16
SEQ = 2048
DEPTH = 2

N_EVEN = (DEPTH + 1) // 2
N_ODD = DEPTH // 2
D_FF = 5632
EPS = 1e-6
V_HEAD = 128
MLA_HEADS = (D_MODEL // 2) // V_HEAD
QK_NOPE = 128
QK_ROPE = 64
Q_LORA = 512
KV_LORA = 512
ROPE_THETA = 10000.0
Q_BLOCK = 128
CONV_CH = D_MODEL - MLA_HEADS * V_HEAD
CONV_GROUPS = 8
CONV_WIDTH = 31
GM_WIDTH = D_MODEL
GM_GROUPS = 8
CHUNK = 128
OFF_KV = Q_LORA
OFF_KR = Q_LORA + KV_LORA
OFF_CONV = Q_LORA + KV_LORA + QK_ROPE
IN_EVEN = OFF_CONV + 2 * CONV_CH

kernel_name = "hybrid_mla_conv_gmlp_macaron"


def rmsnorm(x, g):
    xf = x.astype(jnp.float32)
    y = xf * lax.rsqrt(jnp.mean(xf * xf, axis=-1, keepdims=True) + EPS)
    return (y * g.astype(jnp.float32)).astype(x.dtype)


def layernorm(x, g, b):
    xf = x.astype(jnp.float32)
    mu = jnp.mean(xf, axis=-1, keepdims=True)
    var = jnp.mean(jnp.square(xf - mu), axis=-1, keepdims=True)
    y = (xf - mu) * lax.rsqrt(var + EPS)
    return (y * g.astype(jnp.float32) + b.astype(jnp.float32)).astype(x.dtype)


def swiglu(x, w_gate, w_up, w_down):
    return (jax.nn.silu(x @ w_gate) * (x @ w_up)) @ w_down


def apply_rope(x, cos, sin):
    half = x.shape[-1] // 2
    x1, x2 = x[..., :half], x[..., half:]
    return jnp.concatenate([x1 * cos - x2 * sin, x2 * cos + x1 * sin], axis=-1)


def mla_attention(c_q, c_kv, k_rope, cos, sin, q_norm_g, kv_norm_g, w_uq, w_ukv):
    B, S, _ = c_q.shape
    q = (rmsnorm(c_q, q_norm_g) @ w_uq).reshape(B, S, MLA_HEADS, QK_NOPE + QK_ROPE)
    q_nope = q[..., :QK_NOPE]
    q_pe = apply_rope(q[..., QK_NOPE:], cos[:, :, None, :], sin[:, :, None, :])
    kv = (rmsnorm(c_kv, kv_norm_g) @ w_ukv).reshape(B, S, MLA_HEADS, QK_NOPE + V_HEAD)
    k_nope, v = kv[..., :QK_NOPE], kv[..., QK_NOPE:]
    k_pe = apply_rope(k_rope, cos, sin)
    scale = (QK_NOPE + QK_ROPE) ** -0.5
    n_blk = S // Q_BLOCK
    qn_b = q_nope.reshape(B, n_blk, Q_BLOCK, MLA_HEADS, QK_NOPE).transpose(1, 0, 2, 3, 4)
    qp_b = q_pe.reshape(B, n_blk, Q_BLOCK, MLA_HEADS, QK_ROPE).transpose(1, 0, 2, 3, 4)
    key_idx = jnp.arange(S)

    def block(args):
        qn, qp, i = args
        s = (jnp.einsum('bqhd,bkhd->bhqk', qn, k_nope)
             + jnp.einsum('bqhr,bkr->bhqk', qp, k_pe)).astype(jnp.float32) * scale
        q_idx = i * Q_BLOCK + jnp.arange(Q_BLOCK)
        s = jnp.where(key_idx[None, :] <= q_idx[:, None], s, -jnp.inf)
        p = jax.nn.softmax(s, axis=-1).astype(v.dtype)
        return jnp.einsum('bhqk,bkhd->bqhd', p, v)

    o = lax.map(block, (qn_b, qp_b, jnp.arange(n_blk)))
    return o.transpose(1, 0, 2, 3, 4).reshape(B, S, MLA_HEADS * V_HEAD)


def conformer_conv(h, conv_w, conv_b, norm_g, norm_b):
    B, S, _ = h.shape
    a, gate = h[..., :CONV_CH], h[..., CONV_CH:]
    z = a * jax.nn.sigmoid(gate)
    z = lax.conv_general_dilated(z, conv_w[:, None, :], window_strides=(1,),
                                 padding=[(CONV_WIDTH - 1, 0)],
                                 dimension_numbers=('NWC', 'WIO', 'NWC'),
                                 feature_group_count=CONV_CH) + conv_b
    gsz = CONV_CH // CONV_GROUPS
    z = layernorm(z.reshape(B, S, CONV_GROUPS, gsz),
                  norm_g.reshape(CONV_GROUPS, gsz), norm_b.reshape(CONV_GROUPS, gsz))
    return jax.nn.silu(z).reshape(B, S, CONV_CH)


def chunked_sgu(h, v_norm_g, v_norm_b, w_s, b_s):
    B, S, _ = h.shape
    z = jax.nn.gelu(h)
    u, v = z[..., :GM_WIDTH], z[..., GM_WIDTH:]
    v = layernorm(v, v_norm_g, v_norm_b)
    v = v.reshape(B, S // CHUNK, CHUNK, GM_GROUPS, GM_WIDTH // GM_GROUPS)
    w = w_s * jnp.tril(jnp.ones((CHUNK, CHUNK), w_s.dtype))[None]
    s = jnp.einsum('gts,bcsgd->bctgd', w, v) + b_s.T[None, None, :, :, None]
    return u * s.reshape(B, S, GM_WIDTH)


def _fwd_setup_inputs(seed: int = 0) -> dict:
    key = jax.random.key(seed)
    ks = iter(jax.random.split(key, 48))
    f32 = jnp.float32

    def w(shape, fan_in):
        return jax.random.normal(next(ks), shape, f32) * fan_in ** -0.5

    def gain(shape):
        return 1.0 + 0.02 * jax.random.normal(next(ks), shape, f32)

    def small(shape):
        return 0.02 * jax.random.normal(next(ks), shape, f32)

    x = jax.random.normal(next(ks), (BATCH, SEQ, D_MODEL), f32)
    offs = jax.random.randint(next(ks), (BATCH, 1), 0, 4096, dtype=jnp.int32)
    positions = (offs + jnp.arange(SEQ, dtype=jnp.int32)[None, :]).astype(jnp.int32)
    return {
        "x": x,
        "positions": positions,
        "ffn_a_pre_g": gain((DEPTH, D_MODEL)),
        "ffn_a_post_g": gain((DEPTH, D_MODEL)),
        "ffn_a_w_gate": w((DEPTH, D_MODEL, D_FF), D_MODEL),
        "ffn_a_w_up": w((DEPTH, D_MODEL, D_FF), D_MODEL),
        "ffn_a_w_down": w((DEPTH, D_FF, D_MODEL), D_FF),
        "ffn_b_pre_g": gain((DEPTH, D_MODEL)),
        "ffn_b_post_g": gain((DEPTH, D_MODEL)),
        "ffn_b_w_gate": w((DEPTH, D_MODEL, D_FF), D_MODEL),
        "ffn_b_w_up": w((DEPTH, D_MODEL, D_FF), D_MODEL),
        "ffn_b_w_down": w((DEPTH, D_FF, D_MODEL), D_FF),
        "even_pre_g": gain((N_EVEN, D_MODEL)),
        "even_post_g": gain((N_EVEN, D_MODEL)),
        "even_w_in": w((N_EVEN, D_MODEL, IN_EVEN), D_MODEL),
        "even_q_norm_g": gain((N_EVEN, Q_LORA)),
        "even_kv_norm_g": gain((N_EVEN, KV_LORA)),
        "even_w_uq": w((N_EVEN, Q_LORA, MLA_HEADS * (QK_NOPE + QK_ROPE)), Q_LORA),
        "even_w_ukv": w((N_EVEN, KV_LORA, MLA_HEADS * (QK_NOPE + V_HEAD)), KV_LORA),
        "even_conv_w": w((N_EVEN, CONV_WIDTH, CONV_CH), CONV_WIDTH),
        "even_conv_b": small((N_EVEN, CONV_CH)),
        "even_conv_norm_g": gain((N_EVEN, CONV_CH)),
        "even_conv_norm_b": small((N_EVEN, CONV_CH)),
        "even_w_out": w((N_EVEN, D_MODEL, D_MODEL), D_MODEL),
        "odd_pre_g": gain((N_ODD, D_MODEL)),
        "odd_post_g": gain((N_ODD, D_MODEL)),
        "odd_w_in": w((N_ODD, D_MODEL, 2 * GM_WIDTH), D_MODEL),
        "odd_v_norm_g": gain((N_ODD, GM_WIDTH)),
        "odd_v_norm_b": small((N_ODD, GM_WIDTH)),
        "odd_w_s": w((N_ODD, GM_GROUPS, CHUNK, CHUNK), CHUNK),
        "odd_b_s": gain((N_ODD, GM_GROUPS, CHUNK)),
        "odd_w_out": w((N_ODD, GM_WIDTH, D_MODEL), GM_WIDTH),
    }


def _fwd_reference(x, positions,
              ffn_a_pre_g, ffn_a_post_g, ffn_a_w_gate, ffn_a_w_up, ffn_a_w_down,
              ffn_b_pre_g, ffn_b_post_g, ffn_b_w_gate, ffn_b_w_up, ffn_b_w_down,
              even_pre_g, even_post_g, even_w_in, even_q_norm_g, even_kv_norm_g,
              even_w_uq, even_w_ukv, even_conv_w, even_conv_b, even_conv_norm_g,
              even_conv_norm_b, even_w_out,
              odd_pre_g, odd_post_g, odd_w_in, odd_v_norm_g, odd_v_norm_b,
              odd_w_s, odd_b_s, odd_w_out):
    inv_freq = ROPE_THETA ** (-jnp.arange(0, QK_ROPE, 2, dtype=jnp.float32) / QK_ROPE)
    ang = positions.astype(jnp.float32)[..., None] * inv_freq
    cos, sin = jnp.cos(ang).astype(x.dtype), jnp.sin(ang).astype(x.dtype)

    for l in range(DEPTH):
        h = rmsnorm(x, ffn_a_pre_g[l])
        x = x + 0.5 * rmsnorm(swiglu(h, ffn_a_w_gate[l], ffn_a_w_up[l], ffn_a_w_down[l]), ffn_a_post_g[l])
        if l % 2 == 0:
            i = l // 2
            h = rmsnorm(x, even_pre_g[i])
            p = h @ even_w_in[i]
            a = mla_attention(p[..., :OFF_KV], p[..., OFF_KV:OFF_KR], p[..., OFF_KR:OFF_CONV],
                              cos, sin, even_q_norm_g[i], even_kv_norm_g[i],
                              even_w_uq[i], even_w_ukv[i])
            c = conformer_conv(p[..., OFF_CONV:], even_conv_w[i], even_conv_b[i],
                               even_conv_norm_g[i], even_conv_norm_b[i])
            y = jnp.concatenate([a, c], axis=-1) @ even_w_out[i]
            x = x + rmsnorm(y, even_post_g[i])
        else:
            i = l // 2
            h = rmsnorm(x, odd_pre_g[i])
            y = chunked_sgu(h @ odd_w_in[i], odd_v_norm_g[i], odd_v_norm_b[i],
                            odd_w_s[i], odd_b_s[i]) @ odd_w_out[i]
            x = x + rmsnorm(y, odd_post_g[i])
        h = rmsnorm(x, ffn_b_pre_g[l])
        x = x + 0.5 * rmsnorm(swiglu(h, ffn_b_w_gate[l], ffn_b_w_up[l], ffn_b_w_down[l]), ffn_b_post_g[l])
    return x


import jax as _jax
import jax.numpy as _jnp

TWIN_FORMAT = 'train_step'
FWD_PARAMS = ['x', 'positions', 'ffn_a_pre_g', 'ffn_a_post_g', 'ffn_a_w_gate', 'ffn_a_w_up', 'ffn_a_w_down', 'ffn_b_pre_g', 'ffn_b_post_g', 'ffn_b_w_gate', 'ffn_b_w_up', 'ffn_b_w_down', 'even_pre_g', 'even_post_g', 'even_w_in', 'even_q_norm_g', 'even_kv_norm_g', 'even_w_uq', 'even_w_ukv', 'even_conv_w', 'even_conv_b', 'even_conv_norm_g', 'even_conv_norm_b', 'even_w_out', 'odd_pre_g', 'odd_post_g', 'odd_w_in', 'odd_v_norm_g', 'odd_v_norm_b', 'odd_w_s', 'odd_b_s', 'odd_w_out']
TWIN_WEIGHTS = ['ffn_a_pre_g', 'ffn_a_post_g', 'ffn_a_w_gate', 'ffn_a_w_up', 'ffn_a_w_down', 'ffn_b_pre_g', 'ffn_b_post_g', 'ffn_b_w_gate', 'ffn_b_w_up', 'ffn_b_w_down', 'even_pre_g', 'even_post_g', 'even_w_in', 'even_q_norm_g', 'even_kv_norm_g', 'even_w_uq', 'even_w_ukv', 'even_conv_w', 'even_conv_b', 'even_conv_norm_g', 'even_conv_norm_b', 'even_w_out', 'odd_pre_g', 'odd_post_g', 'odd_w_in', 'odd_v_norm_g', 'odd_v_norm_b', 'odd_w_s', 'odd_b_s', 'odd_w_out']
TWIN_DIFF_INPUT = 'x'
TWIN_INPUTS = ['x', 'positions', 'ffn_a_pre_g', 'ffn_a_post_g', 'ffn_a_w_gate', 'ffn_a_w_up', 'ffn_a_w_down', 'ffn_b_pre_g', 'ffn_b_post_g', 'ffn_b_w_gate', 'ffn_b_w_up', 'ffn_b_w_down', 'even_pre_g', 'even_post_g', 'even_w_in', 'even_q_norm_g', 'even_kv_norm_g', 'even_w_uq', 'even_w_ukv', 'even_conv_w', 'even_conv_b', 'even_conv_norm_g', 'even_conv_norm_b', 'even_w_out', 'odd_pre_g', 'odd_post_g', 'odd_w_in', 'odd_v_norm_g', 'odd_v_norm_b', 'odd_w_s', 'odd_b_s', 'odd_w_out', 'loss_target', 'm_ffn_a_pre_g', 'm_ffn_a_post_g', 'm_ffn_a_w_gate', 'm_ffn_a_w_up', 'm_ffn_a_w_down', 'm_ffn_b_pre_g', 'm_ffn_b_post_g', 'm_ffn_b_w_gate', 'm_ffn_b_w_up', 'm_ffn_b_w_down', 'm_even_pre_g', 'm_even_post_g', 'm_even_w_in', 'm_even_q_norm_g', 'm_even_kv_norm_g', 'm_even_w_uq', 'm_even_w_ukv', 'm_even_conv_w', 'm_even_conv_b', 'm_even_conv_norm_g', 'm_even_conv_norm_b', 'm_even_w_out', 'm_odd_pre_g', 'm_odd_post_g', 'm_odd_w_in', 'm_odd_v_norm_g', 'm_odd_v_norm_b', 'm_odd_w_s', 'm_odd_b_s', 'm_odd_w_out', 'v_ffn_a_pre_g', 'v_ffn_a_post_g', 'v_ffn_a_w_gate', 'v_ffn_a_w_up', 'v_ffn_a_w_down', 'v_ffn_b_pre_g', 'v_ffn_b_post_g', 'v_ffn_b_w_gate', 'v_ffn_b_w_up', 'v_ffn_b_w_down', 'v_even_pre_g', 'v_even_post_g', 'v_even_w_in', 'v_even_q_norm_g', 'v_even_kv_norm_g', 'v_even_w_uq', 'v_even_w_ukv', 'v_even_conv_w', 'v_even_conv_b', 'v_even_conv_norm_g', 'v_even_conv_norm_b', 'v_even_w_out', 'v_odd_pre_g', 'v_odd_post_g', 'v_odd_w_in', 'v_odd_v_norm_g', 'v_odd_v_norm_b', 'v_odd_w_s', 'v_odd_b_s', 'v_odd_w_out']
TWIN_OUTPUTS = ['loss', 'grad_x', 'grad_ffn_a_pre_g', 'grad_ffn_a_post_g', 'grad_ffn_a_w_gate', 'grad_ffn_a_w_up', 'grad_ffn_a_w_down', 'grad_ffn_b_pre_g', 'grad_ffn_b_post_g', 'grad_ffn_b_w_gate', 'grad_ffn_b_w_up', 'grad_ffn_b_w_down', 'grad_even_pre_g', 'grad_even_post_g', 'grad_even_w_in', 'grad_even_q_norm_g', 'grad_even_kv_norm_g', 'grad_even_w_uq', 'grad_even_w_ukv', 'grad_even_conv_w', 'grad_even_conv_b', 'grad_even_conv_norm_g', 'grad_even_conv_norm_b', 'grad_even_w_out', 'grad_odd_pre_g', 'grad_odd_post_g', 'grad_odd_w_in', 'grad_odd_v_norm_g', 'grad_odd_v_norm_b', 'grad_odd_w_s', 'grad_odd_b_s', 'grad_odd_w_out', 'delta_ffn_a_pre_g', 'delta_ffn_a_post_g', 'delta_ffn_a_w_gate', 'delta_ffn_a_w_up', 'delta_ffn_a_w_down', 'delta_ffn_b_pre_g', 'delta_ffn_b_post_g', 'delta_ffn_b_w_gate', 'delta_ffn_b_w_up', 'delta_ffn_b_w_down', 'delta_even_pre_g', 'delta_even_post_g', 'delta_even_w_in', 'delta_even_q_norm_g', 'delta_even_kv_norm_g', 'delta_even_w_uq', 'delta_even_w_ukv', 'delta_even_conv_w', 'delta_even_conv_b', 'delta_even_conv_norm_g', 'delta_even_conv_norm_b', 'delta_even_w_out', 'delta_odd_pre_g', 'delta_odd_post_g', 'delta_odd_w_in', 'delta_odd_v_norm_g', 'delta_odd_v_norm_b', 'delta_odd_w_s', 'delta_odd_b_s', 'delta_odd_w_out', 'new_m_ffn_a_pre_g', 'new_m_ffn_a_post_g', 'new_m_ffn_a_w_gate', 'new_m_ffn_a_w_up', 'new_m_ffn_a_w_down', 'new_m_ffn_b_pre_g', 'new_m_ffn_b_post_g', 'new_m_ffn_b_w_gate', 'new_m_ffn_b_w_up', 'new_m_ffn_b_w_down', 'new_m_even_pre_g', 'new_m_even_post_g', 'new_m_even_w_in', 'new_m_even_q_norm_g', 'new_m_even_kv_norm_g', 'new_m_even_w_uq', 'new_m_even_w_ukv', 'new_m_even_conv_w', 'new_m_even_conv_b', 'new_m_even_conv_norm_g', 'new_m_even_conv_norm_b', 'new_m_even_w_out', 'new_m_odd_pre_g', 'new_m_odd_post_g', 'new_m_odd_w_in', 'new_m_odd_v_norm_g', 'new_m_odd_v_norm_b', 'new_m_odd_w_s', 'new_m_odd_b_s', 'new_m_odd_w_out', 'new_v_ffn_a_pre_g', 'new_v_ffn_a_post_g', 'new_v_ffn_a_w_gate', 'new_v_ffn_a_w_up', 'new_v_ffn_a_w_down', 'new_v_ffn_b_pre_g', 'new_v_ffn_b_post_g', 'new_v_ffn_b_w_gate', 'new_v_ffn_b_w_up', 'new_v_ffn_b_w_down', 'new_v_even_pre_g', 'new_v_even_post_g', 'new_v_even_w_in', 'new_v_even_q_norm_g', 'new_v_even_kv_norm_g', 'new_v_even_w_uq', 'new_v_even_w_ukv', 'new_v_even_conv_w', 'new_v_even_conv_b', 'new_v_even_conv_norm_g', 'new_v_even_conv_norm_b', 'new_v_even_w_out', 'new_v_odd_pre_g', 'new_v_odd_post_g', 'new_v_odd_w_in', 'new_v_odd_v_norm_g', 'new_v_odd_v_norm_b', 'new_v_odd_w_s', 'new_v_odd_b_s', 'new_v_odd_w_out']
TWIN_LEAF_KINDS = {'loss': 'loss', 'grad_x': 'grad_x', 'grad_ffn_a_pre_g': 'grad_w', 'grad_ffn_a_post_g': 'grad_w', 'grad_ffn_a_w_gate': 'grad_w', 'grad_ffn_a_w_up': 'grad_w', 'grad_ffn_a_w_down': 'grad_w', 'grad_ffn_b_pre_g': 'grad_w', 'grad_ffn_b_post_g': 'grad_w', 'grad_ffn_b_w_gate': 'grad_w', 'grad_ffn_b_w_up': 'grad_w', 'grad_ffn_b_w_down': 'grad_w', 'grad_even_pre_g': 'grad_w', 'grad_even_post_g': 'grad_w', 'grad_even_w_in': 'grad_w', 'grad_even_q_norm_g': 'grad_w', 'grad_even_kv_norm_g': 'grad_w', 'grad_even_w_uq': 'grad_w', 'grad_even_w_ukv': 'grad_w', 'grad_even_conv_w': 'grad_w', 'grad_even_conv_b': 'grad_w', 'grad_even_conv_norm_g': 'grad_w', 'grad_even_conv_norm_b': 'grad_w', 'grad_even_w_out': 'grad_w', 'grad_odd_pre_g': 'grad_w', 'grad_odd_post_g': 'grad_w', 'grad_odd_w_in': 'grad_w', 'grad_odd_v_norm_g': 'grad_w', 'grad_odd_v_norm_b': 'grad_w', 'grad_odd_w_s': 'grad_w', 'grad_odd_b_s': 'grad_w', 'grad_odd_w_out': 'grad_w', 'delta_ffn_a_pre_g': 'delta_w', 'delta_ffn_a_post_g': 'delta_w', 'delta_ffn_a_w_gate': 'delta_w', 'delta_ffn_a_w_up': 'delta_w', 'delta_ffn_a_w_down': 'delta_w', 'delta_ffn_b_pre_g': 'delta_w', 'delta_ffn_b_post_g': 'delta_w', 'delta_ffn_b_w_gate': 'delta_w', 'delta_ffn_b_w_up': 'delta_w', 'delta_ffn_b_w_down': 'delta_w', 'delta_even_pre_g': 'delta_w', 'delta_even_post_g': 'delta_w', 'delta_even_w_in': 'delta_w', 'delta_even_q_norm_g': 'delta_w', 'delta_even_kv_norm_g': 'delta_w', 'delta_even_w_uq': 'delta_w', 'delta_even_w_ukv': 'delta_w', 'delta_even_conv_w': 'delta_w', 'delta_even_conv_b': 'delta_w', 'delta_even_conv_norm_g': 'delta_w', 'delta_even_conv_norm_b': 'delta_w', 'delta_even_w_out': 'delta_w', 'delta_odd_pre_g': 'delta_w', 'delta_odd_post_g': 'delta_w', 'delta_odd_w_in': 'delta_w', 'delta_odd_v_norm_g': 'delta_w', 'delta_odd_v_norm_b': 'delta_w', 'delta_odd_w_s': 'delta_w', 'delta_odd_b_s': 'delta_w', 'delta_odd_w_out': 'delta_w', 'new_m_ffn_a_pre_g': 'new_m', 'new_m_ffn_a_post_g': 'new_m', 'new_m_ffn_a_w_gate': 'new_m', 'new_m_ffn_a_w_up': 'new_m', 'new_m_ffn_a_w_down': 'new_m', 'new_m_ffn_b_pre_g': 'new_m', 'new_m_ffn_b_post_g': 'new_m', 'new_m_ffn_b_w_gate': 'new_m', 'new_m_ffn_b_w_up': 'new_m', 'new_m_ffn_b_w_down': 'new_m', 'new_m_even_pre_g': 'new_m', 'new_m_even_post_g': 'new_m', 'new_m_even_w_in': 'new_m', 'new_m_even_q_norm_g': 'new_m', 'new_m_even_kv_norm_g': 'new_m', 'new_m_even_w_uq': 'new_m', 'new_m_even_w_ukv': 'new_m', 'new_m_even_conv_w': 'new_m', 'new_m_even_conv_b': 'new_m', 'new_m_even_conv_norm_g': 'new_m', 'new_m_even_conv_norm_b': 'new_m', 'new_m_even_w_out': 'new_m', 'new_m_odd_pre_g': 'new_m', 'new_m_odd_post_g': 'new_m', 'new_m_odd_w_in': 'new_m', 'new_m_odd_v_norm_g': 'new_m', 'new_m_odd_v_norm_b': 'new_m', 'new_m_odd_w_s': 'new_m', 'new_m_odd_b_s': 'new_m', 'new_m_odd_w_out': 'new_m', 'new_v_ffn_a_pre_g': 'new_v', 'new_v_ffn_a_post_g': 'new_v', 'new_v_ffn_a_w_gate': 'new_v', 'new_v_ffn_a_w_up': 'new_v', 'new_v_ffn_a_w_down': 'new_v', 'new_v_ffn_b_pre_g': 'new_v', 'new_v_ffn_b_post_g': 'new_v', 'new_v_ffn_b_w_gate': 'new_v', 'new_v_ffn_b_w_up': 'new_v', 'new_v_ffn_b_w_down': 'new_v', 'new_v_even_pre_g': 'new_v', 'new_v_even_post_g': 'new_v', 'new_v_even_w_in': 'new_v', 'new_v_even_q_norm_g': 'new_v', 'new_v_even_kv_norm_g': 'new_v', 'new_v_even_w_uq': 'new_v', 'new_v_even_w_ukv': 'new_v', 'new_v_even_conv_w': 'new_v', 'new_v_even_conv_b': 'new_v', 'new_v_even_conv_norm_g': 'new_v', 'new_v_even_conv_norm_b': 'new_v', 'new_v_even_w_out': 'new_v', 'new_v_odd_pre_g': 'new_v', 'new_v_odd_post_g': 'new_v', 'new_v_odd_w_in': 'new_v', 'new_v_odd_v_norm_g': 'new_v', 'new_v_odd_v_norm_b': 'new_v', 'new_v_odd_w_s': 'new_v', 'new_v_odd_b_s': 'new_v', 'new_v_odd_w_out': 'new_v'}


def _forward(args):
    return _fwd_reference(*[args[k] for k in FWD_PARAMS])


def _output_shape():
    out = _jax.eval_shape(lambda: _forward(_fwd_setup_inputs(0)))
    return out.shape, out.dtype

N_MICROBATCH = 1
ADAM_LR = 0.001
ADAM_B1 = 0.9
ADAM_B2 = 0.999
ADAM_EPS = 1e-08
ADAM_WD = 0.01
ADAM_STEP = 10
PER_EXAMPLE_BATCH_AXIS = {'x': 0, 'positions': 0, 'loss_target': 0}
SHARED_INPUTS = []
_WEIGHT_DTYPES = {'ffn_a_pre_g': _jnp.float32, 'ffn_a_post_g': _jnp.float32, 'ffn_a_w_gate': _jnp.float32, 'ffn_a_w_up': _jnp.float32, 'ffn_a_w_down': _jnp.float32, 'ffn_b_pre_g': _jnp.float32, 'ffn_b_post_g': _jnp.float32, 'ffn_b_w_gate': _jnp.float32, 'ffn_b_w_up': _jnp.float32, 'ffn_b_w_down': _jnp.float32, 'even_pre_g': _jnp.float32, 'even_post_g': _jnp.float32, 'even_w_in': _jnp.float32, 'even_q_norm_g': _jnp.float32, 'even_kv_norm_g': _jnp.float32, 'even_w_uq': _jnp.float32, 'even_w_ukv': _jnp.float32, 'even_conv_w': _jnp.float32, 'even_conv_b': _jnp.float32, 'even_conv_norm_g': _jnp.float32, 'even_conv_norm_b': _jnp.float32, 'even_w_out': _jnp.float32, 'odd_pre_g': _jnp.float32, 'odd_post_g': _jnp.float32, 'odd_w_in': _jnp.float32, 'odd_v_norm_g': _jnp.float32, 'odd_v_norm_b': _jnp.float32, 'odd_w_s': _jnp.float32, 'odd_b_s': _jnp.float32, 'odd_w_out': _jnp.float32}
MOMENT_SCALE = {'ffn_a_pre_g': 3.334675e-01, 'ffn_a_post_g': 3.914466e+00, 'ffn_a_w_gate': 1.241398e-01, 'ffn_a_w_up': 1.513535e-01, 'ffn_a_w_down': 2.476280e-01, 'ffn_b_pre_g': 4.036514e-01, 'ffn_b_post_g': 3.980277e+00, 'ffn_b_w_gate': 1.271107e-01, 'ffn_b_w_up': 2.215125e-01, 'ffn_b_w_down': 3.618303e-01, 'even_pre_g': 3.721590e-01, 'even_post_g': 1.636546e+01, 'even_w_in': 2.880177e-01, 'even_q_norm_g': 1.648586e-01, 'even_kv_norm_g': 3.006571e-01, 'even_w_uq': 9.326527e-02, 'even_w_ukv': 1.462475e-01, 'even_conv_w': 7.522843e-01, 'even_conv_b': 1.422392e+01, 'even_conv_norm_g': 5.443525e+00, 'even_conv_norm_b': 7.885736e+00, 'even_w_out': 2.140429e+00, 'odd_pre_g': 6.772636e-01, 'odd_post_g': 1.649095e+01, 'odd_w_in': 4.873603e-01, 'odd_v_norm_g': 1.081828e-01, 'odd_v_norm_b': 1.268394e-01, 'odd_w_s': 1.518722e-01, 'odd_b_s': 2.676992e-01, 'odd_w_out': 2.158534e+00}


def _to_microbatches(a, axis):
    t = _jnp.moveaxis(a, axis, 0)
    t = t.reshape((N_MICROBATCH, t.shape[0] // N_MICROBATCH) + t.shape[1:])
    return _jnp.moveaxis(t, 1, axis + 1)


def setup_inputs(seed: int = 0) -> dict:
    inp = _fwd_setup_inputs(seed)
    key = _jax.random.fold_in(_jax.random.key(seed), 7919)
    shape, _ = _output_shape()
    out = dict(inp)
    out["loss_target"] = _jax.random.normal(_jax.random.fold_in(key, 0), shape, _jnp.float32)
    for i, name in enumerate(TWIN_WEIGHTS):
        w = inp[name].astype(_jnp.float32)
        if MOMENT_SCALE is None:
            s = _jnp.sqrt(_jnp.mean(_jnp.square(w)) + 1e-30)
        else:
            s = MOMENT_SCALE[name]
        km, kv = _jax.random.split(_jax.random.fold_in(key, i + 1))
        out[name] = w
        out["m_" + name] = s * _jax.random.normal(km, w.shape, _jnp.float32)
        out["v_" + name] = (s * s) * _jax.random.uniform(kv, w.shape, _jnp.float32, 0.5, 1.5)
    if N_MICROBATCH > 1:
        for name, axis in PER_EXAMPLE_BATCH_AXIS.items():
            out[name] = _to_microbatches(out[name], axis)
    return {'x': out['x'], 'positions': out['positions'], 'ffn_a_pre_g': out['ffn_a_pre_g'], 'ffn_a_post_g': out['ffn_a_post_g'], 'ffn_a_w_gate': out['ffn_a_w_gate'], 'ffn_a_w_up': out['ffn_a_w_up'], 'ffn_a_w_down': out['ffn_a_w_down'], 'ffn_b_pre_g': out['ffn_b_pre_g'], 'ffn_b_post_g': out['ffn_b_post_g'], 'ffn_b_w_gate': out['ffn_b_w_gate'], 'ffn_b_w_up': out['ffn_b_w_up'], 'ffn_b_w_down': out['ffn_b_w_down'], 'even_pre_g': out['even_pre_g'], 'even_post_g': out['even_post_g'], 'even_w_in': out['even_w_in'], 'even_q_norm_g': out['even_q_norm_g'], 'even_kv_norm_g': out['even_kv_norm_g'], 'even_w_uq': out['even_w_uq'], 'even_w_ukv': out['even_w_ukv'], 'even_conv_w': out['even_conv_w'], 'even_conv_b': out['even_conv_b'], 'even_conv_norm_g': out['even_conv_norm_g'], 'even_conv_norm_b': out['even_conv_norm_b'], 'even_w_out': out['even_w_out'], 'odd_pre_g': out['odd_pre_g'], 'odd_post_g': out['odd_post_g'], 'odd_w_in': out['odd_w_in'], 'odd_v_norm_g': out['odd_v_norm_g'], 'odd_v_norm_b': out['odd_v_norm_b'], 'odd_w_s': out['odd_w_s'], 'odd_b_s': out['odd_b_s'], 'odd_w_out': out['odd_w_out'], 'loss_target': out['loss_target'], 'm_ffn_a_pre_g': out['m_ffn_a_pre_g'], 'm_ffn_a_post_g': out['m_ffn_a_post_g'], 'm_ffn_a_w_gate': out['m_ffn_a_w_gate'], 'm_ffn_a_w_up': out['m_ffn_a_w_up'], 'm_ffn_a_w_down': out['m_ffn_a_w_down'], 'm_ffn_b_pre_g': out['m_ffn_b_pre_g'], 'm_ffn_b_post_g': out['m_ffn_b_post_g'], 'm_ffn_b_w_gate': out['m_ffn_b_w_gate'], 'm_ffn_b_w_up': out['m_ffn_b_w_up'], 'm_ffn_b_w_down': out['m_ffn_b_w_down'], 'm_even_pre_g': out['m_even_pre_g'], 'm_even_post_g': out['m_even_post_g'], 'm_even_w_in': out['m_even_w_in'], 'm_even_q_norm_g': out['m_even_q_norm_g'], 'm_even_kv_norm_g': out['m_even_kv_norm_g'], 'm_even_w_uq': out['m_even_w_uq'], 'm_even_w_ukv': out['m_even_w_ukv'], 'm_even_conv_w': out['m_even_conv_w'], 'm_even_conv_b': out['m_even_conv_b'], 'm_even_conv_norm_g': out['m_even_conv_norm_g'], 'm_even_conv_norm_b': out['m_even_conv_norm_b'], 'm_even_w_out': out['m_even_w_out'], 'm_odd_pre_g': out['m_odd_pre_g'], 'm_odd_post_g': out['m_odd_post_g'], 'm_odd_w_in': out['m_odd_w_in'], 'm_odd_v_norm_g': out['m_odd_v_norm_g'], 'm_odd_v_norm_b': out['m_odd_v_norm_b'], 'm_odd_w_s': out['m_odd_w_s'], 'm_odd_b_s': out['m_odd_b_s'], 'm_odd_w_out': out['m_odd_w_out'], 'v_ffn_a_pre_g': out['v_ffn_a_pre_g'], 'v_ffn_a_post_g': out['v_ffn_a_post_g'], 'v_ffn_a_w_gate': out['v_ffn_a_w_gate'], 'v_ffn_a_w_up': out['v_ffn_a_w_up'], 'v_ffn_a_w_down': out['v_ffn_a_w_down'], 'v_ffn_b_pre_g': out['v_ffn_b_pre_g'], 'v_ffn_b_post_g': out['v_ffn_b_post_g'], 'v_ffn_b_w_gate': out['v_ffn_b_w_gate'], 'v_ffn_b_w_up': out['v_ffn_b_w_up'], 'v_ffn_b_w_down': out['v_ffn_b_w_down'], 'v_even_pre_g': out['v_even_pre_g'], 'v_even_post_g': out['v_even_post_g'], 'v_even_w_in': out['v_even_w_in'], 'v_even_q_norm_g': out['v_even_q_norm_g'], 'v_even_kv_norm_g': out['v_even_kv_norm_g'], 'v_even_w_uq': out['v_even_w_uq'], 'v_even_w_ukv': out['v_even_w_ukv'], 'v_even_conv_w': out['v_even_conv_w'], 'v_even_conv_b': out['v_even_conv_b'], 'v_even_conv_norm_g': out['v_even_conv_norm_g'], 'v_even_conv_norm_b': out['v_even_conv_norm_b'], 'v_even_w_out': out['v_even_w_out'], 'v_odd_pre_g': out['v_odd_pre_g'], 'v_odd_post_g': out['v_odd_post_g'], 'v_odd_w_in': out['v_odd_w_in'], 'v_odd_v_norm_g': out['v_odd_v_norm_g'], 'v_odd_v_norm_b': out['v_odd_v_norm_b'], 'v_odd_w_s': out['v_odd_w_s'], 'v_odd_b_s': out['v_odd_b_s'], 'v_odd_w_out': out['v_odd_w_out']}


def _loss(weights, diff, rest, loss_target):
    with _jax.named_scope("forward"):
        args = {**rest, TWIN_DIFF_INPUT: diff, **{k: w.astype(_WEIGHT_DTYPES[k]) for k, w in weights.items()}}
        y = _forward(args)
    with _jax.named_scope("loss_head"):
        err = _jnp.square(y.astype(_jnp.float32) - loss_target)
        return 0.5 * _jnp.sum(_jnp.mean(err, axis=-1)) if err.ndim else 0.5 * err


def _adamw(w, g, m, v):
    m = ADAM_B1 * m + (1.0 - ADAM_B1) * g
    v = ADAM_B2 * v + (1.0 - ADAM_B2) * _jnp.square(g)
    m_hat = m / (1.0 - ADAM_B1 ** ADAM_STEP)
    v_hat = v / (1.0 - ADAM_B2 ** ADAM_STEP)
    delta = -ADAM_LR * (m_hat / (_jnp.sqrt(v_hat) + ADAM_EPS) + ADAM_WD * w)
    return delta, m, v


def reference(x, positions, ffn_a_pre_g, ffn_a_post_g, ffn_a_w_gate, ffn_a_w_up, ffn_a_w_down, ffn_b_pre_g, ffn_b_post_g, ffn_b_w_gate, ffn_b_w_up, ffn_b_w_down, even_pre_g, even_post_g, even_w_in, even_q_norm_g, even_kv_norm_g, even_w_uq, even_w_ukv, even_conv_w, even_conv_b, even_conv_norm_g, even_conv_norm_b, even_w_out, odd_pre_g, odd_post_g, odd_w_in, odd_v_norm_g, odd_v_norm_b, odd_w_s, odd_b_s, odd_w_out, loss_target, m_ffn_a_pre_g, m_ffn_a_post_g, m_ffn_a_w_gate, m_ffn_a_w_up, m_ffn_a_w_down, m_ffn_b_pre_g, m_ffn_b_post_g, m_ffn_b_w_gate, m_ffn_b_w_up, m_ffn_b_w_down, m_even_pre_g, m_even_post_g, m_even_w_in, m_even_q_norm_g, m_even_kv_norm_g, m_even_w_uq, m_even_w_ukv, m_even_conv_w, m_even_conv_b, m_even_conv_norm_g, m_even_conv_norm_b, m_even_w_out, m_odd_pre_g, m_odd_post_g, m_odd_w_in, m_odd_v_norm_g, m_odd_v_norm_b, m_odd_w_s, m_odd_b_s, m_odd_w_out, v_ffn_a_pre_g, v_ffn_a_post_g, v_ffn_a_w_gate, v_ffn_a_w_up, v_ffn_a_w_down, v_ffn_b_pre_g, v_ffn_b_post_g, v_ffn_b_w_gate, v_ffn_b_w_up, v_ffn_b_w_down, v_even_pre_g, v_even_post_g, v_even_w_in, v_even_q_norm_g, v_even_kv_norm_g, v_even_w_uq, v_even_w_ukv, v_even_conv_w, v_even_conv_b, v_even_conv_norm_g, v_even_conv_norm_b, v_even_w_out, v_odd_pre_g, v_odd_post_g, v_odd_w_in, v_odd_v_norm_g, v_odd_v_norm_b, v_odd_w_s, v_odd_b_s, v_odd_w_out):
    given = dict(x=x, positions=positions, ffn_a_pre_g=ffn_a_pre_g, ffn_a_post_g=ffn_a_post_g, ffn_a_w_gate=ffn_a_w_gate, ffn_a_w_up=ffn_a_w_up, ffn_a_w_down=ffn_a_w_down, ffn_b_pre_g=ffn_b_pre_g, ffn_b_post_g=ffn_b_post_g, ffn_b_w_gate=ffn_b_w_gate, ffn_b_w_up=ffn_b_w_up, ffn_b_w_down=ffn_b_w_down, even_pre_g=even_pre_g, even_post_g=even_post_g, even_w_in=even_w_in, even_q_norm_g=even_q_norm_g, even_kv_norm_g=even_kv_norm_g, even_w_uq=even_w_uq, even_w_ukv=even_w_ukv, even_conv_w=even_conv_w, even_conv_b=even_conv_b, even_conv_norm_g=even_conv_norm_g, even_conv_norm_b=even_conv_norm_b, even_w_out=even_w_out, odd_pre_g=odd_pre_g, odd_post_g=odd_post_g, odd_w_in=odd_w_in, odd_v_norm_g=odd_v_norm_g, odd_v_norm_b=odd_v_norm_b, odd_w_s=odd_w_s, odd_b_s=odd_b_s, odd_w_out=odd_w_out, loss_target=loss_target, m_ffn_a_pre_g=m_ffn_a_pre_g, m_ffn_a_post_g=m_ffn_a_post_g, m_ffn_a_w_gate=m_ffn_a_w_gate, m_ffn_a_w_up=m_ffn_a_w_up, m_ffn_a_w_down=m_ffn_a_w_down, m_ffn_b_pre_g=m_ffn_b_pre_g, m_ffn_b_post_g=m_ffn_b_post_g, m_ffn_b_w_gate=m_ffn_b_w_gate, m_ffn_b_w_up=m_ffn_b_w_up, m_ffn_b_w_down=m_ffn_b_w_down, m_even_pre_g=m_even_pre_g, m_even_post_g=m_even_post_g, m_even_w_in=m_even_w_in, m_even_q_norm_g=m_even_q_norm_g, m_even_kv_norm_g=m_even_kv_norm_g, m_even_w_uq=m_even_w_uq, m_even_w_ukv=m_even_w_ukv, m_even_conv_w=m_even_conv_w, m_even_conv_b=m_even_conv_b, m_even_conv_norm_g=m_even_conv_norm_g, m_even_conv_norm_b=m_even_conv_norm_b, m_even_w_out=m_even_w_out, m_odd_pre_g=m_odd_pre_g, m_odd_post_g=m_odd_post_g, m_odd_w_in=m_odd_w_in, m_odd_v_norm_g=m_odd_v_norm_g, m_odd_v_norm_b=m_odd_v_norm_b, m_odd_w_s=m_odd_w_s, m_odd_b_s=m_odd_b_s, m_odd_w_out=m_odd_w_out, v_ffn_a_pre_g=v_ffn_a_pre_g, v_ffn_a_post_g=v_ffn_a_post_g, v_ffn_a_w_gate=v_ffn_a_w_gate, v_ffn_a_w_up=v_ffn_a_w_up, v_ffn_a_w_down=v_ffn_a_w_down, v_ffn_b_pre_g=v_ffn_b_pre_g, v_ffn_b_post_g=v_ffn_b_post_g, v_ffn_b_w_gate=v_ffn_b_w_gate, v_ffn_b_w_up=v_ffn_b_w_up, v_ffn_b_w_down=v_ffn_b_w_down, v_even_pre_g=v_even_pre_g, v_even_post_g=v_even_post_g, v_even_w_in=v_even_w_in, v_even_q_norm_g=v_even_q_norm_g, v_even_kv_norm_g=v_even_kv_norm_g, v_even_w_uq=v_even_w_uq, v_even_w_ukv=v_even_w_ukv, v_even_conv_w=v_even_conv_w, v_even_conv_b=v_even_conv_b, v_even_conv_norm_g=v_even_conv_norm_g, v_even_conv_norm_b=v_even_conv_norm_b, v_even_w_out=v_even_w_out, v_odd_pre_g=v_odd_pre_g, v_odd_post_g=v_odd_post_g, v_odd_w_in=v_odd_w_in, v_odd_v_norm_g=v_odd_v_norm_g, v_odd_v_norm_b=v_odd_v_norm_b, v_odd_w_s=v_odd_w_s, v_odd_b_s=v_odd_b_s, v_odd_w_out=v_odd_w_out)
    weights = {n: given[n] for n in TWIN_WEIGHTS}
    shared = {n: given[n] for n in SHARED_INPUTS}
    per_example = {n: given[n] for n in ['x', 'positions']}
    grad_fn = _jax.value_and_grad(_loss, argnums=(0, 1))

    def one_microbatch(ex, loss_target):
        ex = dict(ex)
        diff = ex.pop(TWIN_DIFF_INPUT)
        return grad_fn(weights, diff, {**shared, **ex}, loss_target)

    if N_MICROBATCH == 1:
        loss, (grad_w, grad_x) = one_microbatch(per_example, given["loss_target"])
    else:
        def body(carry, xs):
            loss_sum, grad_sum = carry
            l_k, (gw_k, gx_k) = one_microbatch(xs[0], xs[1])
            with _jax.named_scope("update"):
                return (loss_sum + l_k, _jax.tree.map(_jnp.add, grad_sum, gw_k)), gx_k

        init = (_jnp.zeros((), _jnp.float32), _jax.tree.map(_jnp.zeros_like, weights))
        (loss, grad_w), grad_x = _jax.lax.scan(body, init, (per_example, given["loss_target"]))
    with _jax.named_scope("update"):
        delta_w, new_m, new_v = {}, {}, {}
        for n in TWIN_WEIGHTS:
            delta_w[n], new_m[n], new_v[n] = _adamw(weights[n], grad_w[n], given["m_" + n], given["v_" + n])
    return (loss, grad_x, *[grad_w[n] for n in TWIN_WEIGHTS], *[delta_w[n] for n in TWIN_WEIGHTS],
            *[new_m[n] for n in TWIN_WEIGHTS], *[new_v[n] for n in TWIN_WEIGHTS])
```

```python
import functools

import jax
import jax.numpy as jnp
from jax import lax
from jax.experimental import pallas as pl
from jax.experimental.pallas import tpu as pltpu

F32 = jnp.float32
BF16 = jnp.bfloat16
MXU_DTYPE = BF16
MESH = pl.DeviceIdType.MESH
VMEM_LIMIT_BYTES = 56 * 1024 * 1024
LANES = 128

D_MODEL = 2048
D_FF = 5632
EPS = 1e-6
HEADS = 8
V_HEAD = 128
QK_NOPE = 128
QK_ROPE = 64
Q_LORA = 512
KV_LORA = 512
ROPE_THETA = 10000.0
CONV_CH = 1024
CONV_WIDTH = 31
GROUPS = 8
CHUNK = 128
GM_WIDTH = 2048
ADAM_LR = 0.001
ADAM_B1 = 0.9
ADAM_B2 = 0.999
ADAM_EPS = 1e-08
ADAM_WD = 0.01
ADAM_STEP = 10

N_CHIPS = 4
N_DEV = 8
P_KV = Q_LORA
P_A = Q_LORA + KV_LORA
P_GATE = P_A + CONV_CH
P_KR = P_GATE + CONV_CH
P_WIDTH = P_KR + LANES
ROPE_HALF = QK_ROPE // 2

WEIGHTS = ['ffn_a_pre_g', 'ffn_a_post_g', 'ffn_a_w_gate', 'ffn_a_w_up', 'ffn_a_w_down', 'ffn_b_pre_g', 'ffn_b_post_g',
           'ffn_b_w_gate', 'ffn_b_w_up', 'ffn_b_w_down', 'even_pre_g', 'even_post_g', 'even_w_in', 'even_q_norm_g',
           'even_kv_norm_g', 'even_w_uq', 'even_w_ukv', 'even_conv_w', 'even_conv_b', 'even_conv_norm_g',
           'even_conv_norm_b', 'even_w_out', 'odd_pre_g', 'odd_post_g', 'odd_w_in', 'odd_v_norm_g', 'odd_v_norm_b',
           'odd_w_s', 'odd_b_s', 'odd_w_out']
BIG = ['ffn_a_w_gate', 'ffn_a_w_up', 'ffn_a_w_down', 'ffn_b_w_gate', 'ffn_b_w_up', 'ffn_b_w_down', 'even_w_in',
       'even_w_uq', 'even_w_ukv', 'even_w_out', 'odd_w_in', 'odd_w_out']
SMALL = [n for n in WEIGHTS if n not in BIG]
SMALL_SHARDED = ['even_conv_w', 'odd_pre_g', 'odd_post_g', 'odd_v_norm_g', 'odd_v_norm_b']


def _call(name, body, grid, in_specs, out_specs, out_shape, scratch=(), prefetch=0, aliases=None):
    params = pltpu.CompilerParams(dimension_semantics=("arbitrary",) * len(grid), vmem_limit_bytes=VMEM_LIMIT_BYTES)
    if prefetch:
        spec = pltpu.PrefetchScalarGridSpec(num_scalar_prefetch=prefetch, grid=grid, in_specs=in_specs,
                                            out_specs=out_specs, scratch_shapes=list(scratch))
        return pl.pallas_call(body, grid_spec=spec, out_shape=out_shape, compiler_params=params, name=name,
                              input_output_aliases=aliases or {})
    return pl.pallas_call(body, grid=grid, in_specs=in_specs, out_specs=out_specs, out_shape=out_shape,
                          scratch_shapes=list(scratch), compiler_params=params, name=name,
                          input_output_aliases=aliases or {})


def _spec(block, index_map):
    return pl.BlockSpec(block, index_map)


def _sds(shape, dtype):
    return jax.ShapeDtypeStruct(tuple(shape), dtype)


def _tile(n, want, mult=8):
    if n <= want:
        return n
    best = None
    for t in range(mult, want + 1, mult):
        if n % t == 0:
            best = t
    assert best is not None, (n, want, mult)
    return best


_DIMS = {"nn": (((1,), (0,)), ((), ())), "nt": (((1,), (1,)), ((), ())), "tn": (((0,), (0,)), ((), ()))}


def _dot(a, b, mode="nn"):
    return lax.dot_general(a.astype(MXU_DTYPE), b.astype(MXU_DTYPE), _DIMS[mode], preferred_element_type=F32)


def _matmul(name, mode, grid, groups, outs, acc_shape, extras=(), epilogue=None):
    flat, specs = [], []
    for grp in groups:
        for a, a_spec, b, b_spec in grp:
            flat += [a, b]
            specs += [a_spec, b_spec]
    for e, e_spec in extras:
        flat.append(e)
        specs.append(e_spec)
    n_pairs = [len(g) for g in groups]
    n_in, n_ex, n_out, n_acc = 2 * sum(n_pairs), len(extras), len(outs), len(groups)
    last_k = grid[2] - 1

    def body(*refs):
        ins, ex = refs[:n_in], refs[n_in:n_in + n_ex]
        out_refs = refs[n_in + n_ex:n_in + n_ex + n_out]
        accs = refs[n_in + n_ex + n_out:]
        k = pl.program_id(2)

        @pl.when(k == 0)
        def _():
            for acc in accs:
                acc[...] = jnp.zeros(acc.shape, F32)

        pos = 0
        for gi, n in enumerate(n_pairs):
            total = None
            for _ in range(n):
                d = _dot(ins[pos][...], ins[pos + 1][...], mode)
                total = d if total is None else total + d
                pos += 2
            accs[gi][...] += total

        @pl.when(k == last_k)
        def _():
            vals = [acc[...] for acc in accs]
            res = epilogue(vals, [e[...] for e in ex]) if epilogue else vals
            for o, r in zip(out_refs, res):
                o[...] = r.astype(o.dtype)

    res = _call(name, body, grid, specs, [o[2] for o in outs], [_sds(o[0], o[1]) for o in outs],
                scratch=[pltpu.VMEM(acc_shape, F32)] * n_acc)(*flat)
    return res


def _rows(name, fn, rows, consts, outs, accs, tm):
    t_rows = rows[0][0].shape[0]
    grid = (t_rows // tm,)
    in_specs = [_spec((tm, bw), functools.partial(lambda i, cb: (i, cb), cb=cb)) for _, bw, cb in rows]
    in_specs += [_spec(c.shape, functools.partial(lambda i, nd: (0,) * nd, nd=c.ndim)) for c in consts]
    out_shape = [_sds((t_rows, w), dt) for w, dt in outs] + [_sds(s, F32) for s in accs]
    out_specs = [_spec((tm, w), lambda i: (i, 0)) for w, _ in outs]
    out_specs += [_spec(s, functools.partial(lambda i, nd: (0,) * nd, nd=len(s))) for s in accs]
    nr, nc, no = len(rows), len(consts), len(outs)

    def body(*refs):
        r = [ref[...] for ref in refs[:nr]]
        c = [ref[...] for ref in refs[nr:nr + nc]]
        o_refs, a_refs = refs[nr + nc:nr + nc + no], refs[nr + nc + no:]
        o_vals, a_vals = fn(r, c)
        for ref, v in zip(o_refs, o_vals):
            ref[...] = v.astype(ref.dtype)
        if a_refs:
            @pl.when(pl.program_id(0) == 0)
            def _():
                for ref in a_refs:
                    ref[...] = jnp.zeros(ref.shape, F32)

            for ref, v in zip(a_refs, a_vals):
                ref[...] += v

    return _call(name, body, grid, in_specs, out_specs, out_shape)(*[a for a, _, _ in rows], *consts)


def _rmsnorm(x, g):
    return x * lax.rsqrt(jnp.mean(x * x, axis=-1, keepdims=True) + EPS) * g


def _layernorm(x, g, b):
    mu = jnp.mean(x, axis=-1, keepdims=True)
    var = jnp.mean(jnp.square(x - mu), axis=-1, keepdims=True)
    return (x - mu) * lax.rsqrt(var + EPS) * g + b


def _swiglu_act(g, u):
    return jax.nn.silu(g) * u


def _rope(x, cos, sin_signed):
    return x * cos + pltpu.roll(x, 2 * ROPE_HALF, 1) * sin_signed


def _rope_transposed(dy, cos, sin_signed):
    return dy * cos - pltpu.roll(dy, 2 * ROPE_HALF, 1) * sin_signed


def _rms_fwd(x, g, tm):
    d = x.shape[1]
    return _rows("rms_fwd", lambda r, c: ([_rmsnorm(r[0], c[0])], []), [(x, d, 0)], [g], [(d, MXU_DTYPE)], [], tm)[0]


def _rms_residual(x, y, g, scale, tm):
    d = x.shape[1]
    return _rows("rms_residual", lambda r, c: ([r[0] + scale * _rmsnorm(r[1], c[0])], []),
                 [(x, d, 0), (y, d, 0)], [g], [(d, F32)], [], tm)[0]


def _rms_bwd(y, g, dout, scale, out_dtype, tm, resid=None):
    d = y.shape[1]

    def fn(r, c):
        _, vjp = jax.vjp(_rmsnorm, r[0], c[0])
        dy, dg = vjp(scale * r[1].astype(F32))
        if resid is not None:
            dy = dy + r[2]
        return [dy], [dg]

    rows = [(y, d, 0), (dout, d, 0)] + ([(resid, d, 0)] if resid is not None else [])
    return _rows("rms_bwd" if resid is None else "rms_bwd_resid", fn, rows, [g], [(d, out_dtype)], [(1, d)], tm)


def _loss_head(y, target, tm):
    d = y.shape[1]

    def fn(r, c):
        err = r[0] - r[1]
        sq = jnp.sum(jnp.sum(err * err, axis=1, keepdims=True), axis=0, keepdims=True)
        return [err * (1.0 / d)], [jnp.broadcast_to(sq, (1, LANES))]

    return _rows("loss_head", fn, [(y, d, 0), (target, d, 0)], [], [(d, F32)], [(1, LANES)], tm)


def _ffn_fwd(x, pre_g, post_g, wg, wu, wd, layer, tm, tr):
    t, d = x.shape
    fs = wg.shape[3]
    tk = _tile(d, 1024, LANES)
    tn = _tile(d, 1024, LANES)
    h = _rms_fwd(x, pre_g, tr)
    h_spec = _spec((tm, tk), lambda i, j, k: (i, k))
    w_spec = _spec((None, None, tk, fs), lambda i, j, k: (j, layer, k, 0))
    o_spec = _spec((tm, fs), lambda i, j, k: (i, j))
    g, u, a = _matmul("ffn_up", "nn", (t // tm, N_CHIPS, d // tk),
                      [[(h, h_spec, wg, w_spec)], [(h, h_spec, wu, w_spec)]],
                      [((t, N_CHIPS * fs), MXU_DTYPE, o_spec)] * 3, (tm, fs),
                      epilogue=lambda accs, _: (accs[0], accs[1], _swiglu_act(accs[0], accs[1])))
    (y,) = _matmul("ffn_down", "nn", (t // tm, d // tn, N_CHIPS),
                   [[(a, _spec((tm, fs), lambda i, j, k: (i, k)), wd, _spec((None, None, fs, tn), lambda i, j, k: (k, layer, 0, j)))]],
                   [((t, d), F32, _spec((tm, tn), lambda i, j, k: (i, j)))], (tm, tn))
    out = _rms_residual(x, y, post_g, 0.5, tr)
    return out, (x, h, g, u, a, y)


def _ffn_bwd(dout, saved, pre_g, post_g, wg, wu, wd, layer, tm, tr):
    x, h, g, u, a, y = saved
    t, d = x.shape
    fs = wg.shape[3]
    tk = _tile(d, 1024, LANES)
    tn = _tile(d, 1024, LANES)
    tkt = _tile(t, 512, 16)
    dy, dpost = _rms_bwd(y, post_g, dout, 0.5, MXU_DTYPE, tr)

    def act_bwd(accs, ex):
        _, vjp = jax.vjp(_swiglu_act, ex[0].astype(F32), ex[1].astype(F32))
        return vjp(accs[0])

    gu_spec = _spec((tm, fs), lambda i, j, k: (i, j))
    dg, du = _matmul("ffn_dact", "nt", (t // tm, N_CHIPS, d // tk),
                     [[(dy, _spec((tm, tk), lambda i, j, k: (i, k)), wd, _spec((None, None, fs, tk), lambda i, j, k: (j, layer, 0, k)))]],
                     [((t, N_CHIPS * fs), MXU_DTYPE, gu_spec)] * 2, (tm, fs),
                     extras=[(g, gu_spec), (u, gu_spec)], epilogue=act_bwd)
    (dwd,) = _matmul("ffn_dwdown", "tn", (N_CHIPS, d // tn, t // tkt),
                     [[(a, _spec((tkt, fs), lambda i, j, k: (k, i)), dy, _spec((tkt, tn), lambda i, j, k: (k, j)))]],
                     [((N_CHIPS, fs, d), MXU_DTYPE, _spec((None, fs, tn), lambda i, j, k: (i, 0, j)))], (fs, tn))
    da_spec = _spec((tm, fs), lambda i, j, k: (i, k))
    wt_spec = _spec((None, None, tn, fs), lambda i, j, k: (k, layer, j, 0))
    (dh,) = _matmul("ffn_dh", "nt", (t // tm, d // tn, N_CHIPS),
                    [[(dg, da_spec, wg, wt_spec), (du, da_spec, wu, wt_spec)]],
                    [((t, d), F32, _spec((tm, tn), lambda i, j, k: (i, j)))], (tm, tn))
    tmw = _tile(d, 512, LANES)
    h_spec = _spec((tkt, tmw), lambda i, j, k: (k, i))
    dgu_spec = _spec((tkt, fs), lambda i, j, k: (k, j))
    dw_spec = _spec((None, tmw, fs), lambda i, j, k: (j, i, 0))
    dwg, dwu = _matmul("ffn_dwup", "tn", (d // tmw, N_CHIPS, t // tkt),
                       [[(h, h_spec, dg, dgu_spec)], [(h, h_spec, du, dgu_spec)]],
                       [((N_CHIPS, d, fs), MXU_DTYPE, dw_spec)] * 2, (tmw, fs))
    dx, dpre = _rms_bwd(x, pre_g, dh, 1.0, F32, tr, resid=dout)
    return dx, (dpre, dpost, dwg, dwu, dwd)


def _attn_scores(qn, qpe, kn, kpe, qi, tq, seq):
    s = (_dot(qn, kn, "nt") + _dot(qpe, kpe, "nt")) * ((QK_NOPE + QK_ROPE) ** -0.5)
    rows = qi * tq + lax.broadcasted_iota(jnp.int32, (tq, seq), 0)
    cols = lax.broadcasted_iota(jnp.int32, (tq, seq), 1)
    s = jnp.where(cols <= rows, s, -jnp.inf)
    e = jnp.exp(s - jnp.max(s, axis=1, keepdims=True))
    return e / jnp.sum(e, axis=1, keepdims=True)


def _attn_specs(nq, tq, seq):
    q_rows = lambda b, h, qi: b * nq + qi
    return [
        _spec((tq, LANES), lambda b, h, qi: (q_rows(b, h, qi), h)),
        _spec((tq, LANES), lambda b, h, qi: (q_rows(b, h, qi), HEADS + h)),
        _spec((seq, 2 * LANES), lambda b, h, qi: (b, h)),
        _spec((seq, LANES), lambda b, h, qi: (b, P_KR // LANES)),
        _spec((tq, LANES), lambda b, h, qi: (q_rows(b, h, qi), 0)),
        _spec((tq, LANES), lambda b, h, qi: (q_rows(b, h, qi), 0)),
        _spec((seq, LANES), lambda b, h, qi: (b, 0)),
        _spec((seq, LANES), lambda b, h, qi: (b, 0)),
    ]


def _attn_fwd(q_all, kv, p, cos, sin, batch, seq):
    t = q_all.shape[0]
    tq = _tile(seq, 256, 16)
    nq = seq // tq

    def body(qn_ref, qp_ref, kv_ref, kr_ref, cq, sq, ck, sk, o_ref):
        qpe = _rope(qp_ref[...], cq[...], sq[...])
        kpe = _rope(kr_ref[...], ck[...], sk[...])
        prob = _attn_scores(qn_ref[...], qpe, kv_ref[:, :LANES], kpe, pl.program_id(2), tq, seq)
        o_ref[...] = _dot(prob, kv_ref[:, LANES:]).astype(o_ref.dtype)

    return _call("attn_fwd", body, (batch, HEADS, nq), _attn_specs(nq, tq, seq),
                 _spec((tq, LANES), lambda b, h, qi: (b * nq + qi, h)), _sds((t, 2 * HEADS * V_HEAD), MXU_DTYPE)
                 )(q_all, q_all, kv, p, cos, sin, cos, sin)


def _attn_bwd(q_all, kv, p, cos, sin, d_cat, batch, seq):
    t = q_all.shape[0]
    tq = _tile(seq, 256, 16)
    nq = seq // tq

    def body(qn_ref, qp_ref, kv_ref, kr_ref, cq, sq, ck, sk, do_ref, dqn_ref, dqp_ref, dkv_ref, dkr_ref, dkpe_acc):
        h, qi = pl.program_id(1), pl.program_id(2)
        qn = qn_ref[...]
        qpe = _rope(qp_ref[...], cq[...], sq[...])
        kn, v = kv_ref[:, :LANES], kv_ref[:, LANES:]
        kpe = _rope(kr_ref[...], ck[...], sk[...])
        prob = _attn_scores(qn, qpe, kn, kpe, qi, tq, seq)
        do = do_ref[...]
        dprob = _dot(do, v, "nt")
        ds = prob * (dprob - jnp.sum(prob * dprob, axis=1, keepdims=True)) * ((QK_NOPE + QK_ROPE) ** -0.5)
        dqn_ref[...] = _dot(ds, kn)
        dqp_ref[...] = _rope_transposed(_dot(ds, kpe), cq[...], sq[...])

        @pl.when(qi == 0)
        def _():
            dkv_ref[...] = jnp.zeros(dkv_ref.shape, F32)

        dkv_ref[:, :LANES] += _dot(ds, qn, "tn")
        dkv_ref[:, LANES:] += _dot(prob, do, "tn")

        @pl.when((h == 0) & (qi == 0))
        def _():
            dkpe_acc[...] = jnp.zeros(dkpe_acc.shape, F32)

        dkpe_acc[...] += _dot(ds, qpe, "tn")

        @pl.when((h == HEADS - 1) & (qi == nq - 1))
        def _():
            dkr_ref[...] = _rope_transposed(dkpe_acc[...], ck[...], sk[...])

    q_out = _spec((tq, LANES), lambda b, h, qi: (b * nq + qi, h))
    return _call("attn_bwd", body, (batch, HEADS, nq),
                 _attn_specs(nq, tq, seq) + [_spec((tq, LANES), lambda b, h, qi: (b * nq + qi, h))],
                 [q_out, q_out, _spec((seq, 2 * LANES), lambda b, h, qi: (b, h)), _spec((seq, LANES), lambda b, h, qi: (b, 0))],
                 [_sds((t, HEADS * LANES), F32), _sds((t, HEADS * LANES), F32), _sds((t, HEADS * 2 * LANES), F32), _sds((t, LANES), F32)],
                 scratch=[pltpu.VMEM((seq, LANES), F32)])(q_all, q_all, kv, p, cos, sin, cos, sin, d_cat)


CONV_PAD = 32


def _conv_taps(w_ref, src_ref, first_row, n_rows, init, offset):
    acc = init
    for k in range(CONV_WIDTH):
        acc = acc + w_ref[k:k + 1, :] * src_ref[pl.ds(first_row + offset(k), n_rows), :]
    return acc


def _norm_act(conv, g, b):
    return jax.nn.silu(_layernorm(conv, g, b))


def _conv_fwd(p, cat, conv_w, conv_b, norm_g, norm_b, batch, seq):
    rc = _tile(seq, 256, 8)
    a_blk, gate_blk, out_blk = P_A // LANES, P_GATE // LANES, (HEADS * V_HEAD) // LANES

    def body(a_ref, gate_ref, w_ref, cb_ref, ng_ref, nb_ref, cat_in, o_ref, zp):
        del cat_in
        zp[pl.ds(0, CONV_PAD), :] = jnp.zeros((CONV_PAD, LANES), F32)
        zp[pl.ds(CONV_PAD, seq), :] = a_ref[...] * jax.nn.sigmoid(gate_ref[...])
        for r0 in range(0, seq, rc):
            conv = _conv_taps(w_ref, zp, r0, rc, jnp.broadcast_to(cb_ref[...], (rc, LANES)),
                              lambda k: CONV_PAD - (CONV_WIDTH - 1) + k)
            o_ref[pl.ds(r0, rc), :] = _norm_act(conv, ng_ref[...], nb_ref[...]).astype(o_ref.dtype)

    vec = _spec((1, LANES), lambda b, g: (0, g))
    return _call("conv_fwd", body, (batch, GROUPS),
                 [_spec((seq, LANES), lambda b, g: (b, a_blk + g)), _spec((seq, LANES), lambda b, g: (b, gate_blk + g)),
                  _spec((CONV_WIDTH, LANES), lambda b, g: (0, g)), vec, vec, vec, pl.BlockSpec(memory_space=pl.ANY)],
                 _spec((seq, LANES), lambda b, g: (b, out_blk + g)), _sds(cat.shape, cat.dtype),
                 scratch=[pltpu.VMEM((seq + CONV_PAD, LANES), F32)], aliases={6: 0}
                 )(p, p, conv_w, conv_b, norm_g, norm_b, cat)


def _conv_bwd(p, d_cat, conv_w, conv_b, norm_g, norm_b, batch, seq):
    t = p.shape[0]
    rc = _tile(seq, 256, 8)
    a_blk, gate_blk, out_blk = P_A // LANES, P_GATE // LANES, (HEADS * V_HEAD) // LANES

    def body(a_ref, gate_ref, w_ref, cb_ref, ng_ref, nb_ref, dc_ref, da_ref, dgate_ref, dw_ref, db_ref, dng_ref, dnb_ref,
             zp, dcp, buf):
        b = pl.program_id(1)
        a, sig = a_ref[...], jax.nn.sigmoid(gate_ref[...])
        zp[pl.ds(0, CONV_PAD), :] = jnp.zeros((CONV_PAD, LANES), F32)
        zp[pl.ds(CONV_PAD, seq), :] = a * sig
        for r0 in range(0, seq, rc):
            buf[pl.ds(r0, rc), :] = _conv_taps(w_ref, zp, r0, rc, jnp.broadcast_to(cb_ref[...], (rc, LANES)),
                                               lambda k: CONV_PAD - (CONV_WIDTH - 1) + k)
        _, vjp = jax.vjp(_norm_act, buf[...], ng_ref[...], nb_ref[...])
        dconv, dng, dnb = vjp(dc_ref[...].astype(F32))
        dcp[pl.ds(0, seq), :] = dconv
        dcp[pl.ds(seq, CONV_PAD), :] = jnp.zeros((CONV_PAD, LANES), F32)

        @pl.when(b == 0)
        def _():
            dw_ref[...] = jnp.zeros(dw_ref.shape, F32)
            db_ref[...] = jnp.zeros(db_ref.shape, F32)
            dng_ref[...] = jnp.zeros(dng_ref.shape, F32)
            dnb_ref[...] = jnp.zeros(dnb_ref.shape, F32)

        db_ref[...] += jnp.sum(dconv, axis=0, keepdims=True)
        dng_ref[...] += dng
        dnb_ref[...] += dnb
        for k in range(CONV_WIDTH):
            shifted = zp[pl.ds(CONV_PAD - (CONV_WIDTH - 1) + k, seq), :]
            dw_ref[k:k + 1, :] += jnp.sum(dconv * shifted, axis=0, keepdims=True)
        for r0 in range(0, seq, rc):
            buf[pl.ds(r0, rc), :] = _conv_taps(w_ref, dcp, r0, rc, jnp.zeros((rc, LANES), F32),
                                               lambda k: CONV_WIDTH - 1 - k)
        dz = buf[...]
        da_ref[...] = (dz * sig).astype(da_ref.dtype)
        dgate_ref[...] = (dz * a * sig * (1.0 - sig)).astype(dgate_ref.dtype)

    vec = _spec((1, LANES), lambda g, b: (0, g))
    row_out = _spec((seq, LANES), lambda g, b: (b, g))
    return _call("conv_bwd", body, (GROUPS, batch),
                 [_spec((seq, LANES), lambda g, b: (b, a_blk + g)), _spec((seq, LANES), lambda g, b: (b, gate_blk + g)),
                  _spec((CONV_WIDTH, LANES), lambda g, b: (0, g)), vec, vec, vec,
                  _spec((seq, LANES), lambda g, b: (b, out_blk + g))],
                 [row_out, row_out, _spec((CONV_WIDTH, LANES), lambda g, b: (0, g)), vec, vec, vec],
                 [_sds((t, CONV_CH), MXU_DTYPE), _sds((t, CONV_CH), MXU_DTYPE), _sds((CONV_WIDTH, CONV_CH), F32),
                  _sds((1, CONV_CH), F32), _sds((1, CONV_CH), F32), _sds((1, CONV_CH), F32)],
                 scratch=[pltpu.VMEM((seq + CONV_PAD, LANES), F32), pltpu.VMEM((seq + CONV_PAD, LANES), F32),
                          pltpu.VMEM((seq, LANES), F32)])(p, p, conv_w, conv_b, norm_g, norm_b, d_cat)


def _sgu_pre(hw, g, b):
    z = jax.nn.gelu(hw)
    return z[:, :GM_WIDTH], _layernorm(z[:, GM_WIDTH:], g, b)


def _causal(w):
    keep = lax.broadcasted_iota(jnp.int32, (CHUNK, CHUNK), 0) >= lax.broadcasted_iota(jnp.int32, (CHUNK, CHUNK), 1)
    return jnp.where(keep, w, 0.0)


def _sgu_fwd(hw, vg, vb, w_s, b_st):
    t = hw.shape[0]
    gw = GM_WIDTH // GROUPS

    def body(hw_ref, vg_ref, vb_ref, w_ref, b_ref, o_ref):
        u, v = _sgu_pre(hw_ref[...], vg_ref[...], vb_ref[...])
        for g in range(GROUPS):
            cols = slice(g * gw, (g + 1) * gw)
            s = _dot(_causal(w_ref[g]), v[:, cols]) + b_ref[:, g:g + 1]
            o_ref[:, cols] = (u[:, cols] * s).astype(o_ref.dtype)

    return _call("sgu_fwd", body, (t // CHUNK,),
                 [_spec((CHUNK, 2 * GM_WIDTH), lambda i: (i, 0)), _spec((1, GM_WIDTH), lambda i: (0, 0)),
                  _spec((1, GM_WIDTH), lambda i: (0, 0)), _spec((GROUPS, CHUNK, CHUNK), lambda i: (0, 0, 0)),
                  _spec((CHUNK, GROUPS), lambda i: (0, 0))],
                 _spec((CHUNK, GM_WIDTH), lambda i: (i, 0)), _sds((t, GM_WIDTH), MXU_DTYPE))(hw, vg, vb, w_s, b_st)


def _sgu_bwd(hw, vg, vb, w_s, b_st, dout):
    t = hw.shape[0]
    gw = GM_WIDTH // GROUPS

    def body(hw_ref, vg_ref, vb_ref, w_ref, b_ref, do_ref, dhw_ref, dvg_ref, dvb_ref, dw_ref, db_ref, du_buf, dv_buf):
        (u, v), vjp = jax.vjp(_sgu_pre, hw_ref[...], vg_ref[...], vb_ref[...])

        @pl.when(pl.program_id(0) == 0)
        def _():
            dvg_ref[...] = jnp.zeros(dvg_ref.shape, F32)
            dvb_ref[...] = jnp.zeros(dvb_ref.shape, F32)
            dw_ref[...] = jnp.zeros(dw_ref.shape, F32)
            db_ref[...] = jnp.zeros(db_ref.shape, F32)

        lane = lax.broadcasted_iota(jnp.int32, (CHUNK, LANES), 1)
        db = jnp.zeros((CHUNK, LANES), F32)
        for g in range(GROUPS):
            cols = slice(g * gw, (g + 1) * gw)
            w = _causal(w_ref[g])
            s = _dot(w, v[:, cols]) + b_ref[:, g:g + 1]
            do = do_ref[:, cols]
            ds = do * u[:, cols]
            du_buf[:, cols] = do * s
            dw_ref[g] += _causal(_dot(ds, v[:, cols], "nt"))
            dv_buf[:, cols] = _dot(w, ds, "tn")
            db = db + jnp.where(lane == g, jnp.sum(ds, axis=1, keepdims=True), 0.0)
        db_ref[...] += db
        dhw, dvg, dvb = vjp((du_buf[...], dv_buf[...]))
        dhw_ref[...] = dhw.astype(dhw_ref.dtype)
        dvg_ref[...] += dvg
        dvb_ref[...] += dvb

    vec = _spec((1, GM_WIDTH), lambda i: (0, 0))
    w_spec = _spec((GROUPS, CHUNK, CHUNK), lambda i: (0, 0, 0))
    return _call("sgu_bwd", body, (t // CHUNK,),
                 [_spec((CHUNK, 2 * GM_WIDTH), lambda i: (i, 0)), vec, vec, w_spec, _spec((CHUNK, GROUPS), lambda i: (0, 0)),
                  _spec((CHUNK, GM_WIDTH), lambda i: (i, 0))],
                 [_spec((CHUNK, 2 * GM_WIDTH), lambda i: (i, 0)), vec, vec, w_spec, _spec((CHUNK, LANES), lambda i: (0, 0))],
                 [_sds((t, 2 * GM_WIDTH), MXU_DTYPE), _sds((1, GM_WIDTH), F32), _sds((1, GM_WIDTH), F32),
                  _sds((GROUPS, CHUNK, CHUNK), F32), _sds((CHUNK, LANES), F32)],
                 scratch=[pltpu.VMEM((CHUNK, GM_WIDTH), F32), pltpu.VMEM((CHUNK, GM_WIDTH), F32)]
                 )(hw, vg, vb, w_s, b_st, dout)


def _mm_nn(name, a, b, out_dtype, tm, a_col0=0, k_width=None, tn_want=1024, tk_want=1024):
    t = a.shape[0]
    kk, n = b.shape
    tk = _tile(kk, tk_want, LANES)
    tn = _tile(n, tn_want, LANES)
    k0 = a_col0 // tk
    assert a_col0 % tk == 0
    return _matmul(name, "nn", (t // tm, n // tn, kk // tk),
                   [[(a, _spec((tm, tk), lambda i, j, k: (i, k0 + k)), b, _spec((tk, tn), lambda i, j, k: (k, j)))]],
                   [((t, n), out_dtype, _spec((tm, tn), lambda i, j, k: (i, j)))], (tm, tn))[0]


def _mm_nt(name, pairs, out_dtype, tm, tn_want=1024, tk_want=1024):
    t, kk = pairs[0][0].shape
    n = pairs[0][1].shape[0]
    tk = _tile(kk, tk_want, LANES)
    tn = _tile(n, tn_want, 16)
    grp = []
    for a, b, b_col0 in pairs:
        assert b_col0 % tk == 0 and a.shape == (t, kk)
        grp.append((a, _spec((tm, tk), lambda i, j, k: (i, k)), b,
                    _spec((tn, tk), functools.partial(lambda i, j, k, k0: (j, k0 + k), k0=b_col0 // tk))))
    return _matmul(name, "nt", (t // tm, n // tn, kk // tk), [grp],
                   [((t, n), out_dtype, _spec((tm, tn), lambda i, j, k: (i, j)))], (tm, tn))[0]


def _mm_tn(name, a, bs, out_dtype, a_col0=0, m_width=None, tm_want=512, tn_want=1024):
    t = a.shape[0]
    m = m_width or a.shape[1]
    n = bs[0].shape[1]
    tmw = _tile(m, tm_want, LANES)
    tn = _tile(n, tn_want, LANES)
    tkt = _tile(t, 512, 16)
    assert a_col0 % tmw == 0
    i0 = a_col0 // tmw
    a_spec = _spec((tkt, tmw), lambda i, j, k: (k, i0 + i))
    groups = [[(a, a_spec, b, _spec((tkt, tn), lambda i, j, k: (k, j)))] for b in bs]
    return _matmul(name, "tn", (m // tmw, n // tn, t // tkt), groups,
                   [((m, n), out_dtype, _spec((tmw, tn), lambda i, j, k: (i, j)))] * len(bs), (tmw, tn))


_ANY = pl.BlockSpec(memory_space=pl.ANY)


def _comm_call(name, body, ins, out_shapes, sems):
    return pl.pallas_call(body, name=name, out_shape=out_shapes, in_specs=[_ANY] * len(ins),
                          out_specs=[_ANY] * len(out_shapes),
                          scratch_shapes=[pltpu.SemaphoreType.DMA((n,)) for n in sems])(*ins)


def _place():
    return lax.axis_index("x"), lax.axis_index("y"), lax.axis_index("c")


def _other_chips(x, y):
    return [(1 - x, y), (x, 1 - y), (1 - x, 1 - y)]


def _gather_chips(shards):
    n = len(shards)

    def body(*refs):
        src, dst = refs[:n], refs[n:2 * n]
        send_sems, recv_sems, local_sems = refs[2 * n:]
        x, y, c = _place()
        me = 2 * x + y
        local = [pltpu.make_async_copy(src[t], dst[t].at[me], local_sems.at[t]) for t in range(n)]
        for cp in local:
            cp.start()
        remote = []
        for t in range(n):
            for r, (px, py) in enumerate(_other_chips(x, y)):
                remote.append(pltpu.make_async_remote_copy(
                    src_ref=src[t], dst_ref=dst[t].at[me], send_sem=send_sems.at[3 * t + r], recv_sem=recv_sems.at[3 * t + r],
                    device_id=(px, py, c), device_id_type=MESH))
        for cp in remote:
            cp.start()
        for cp in remote:
            cp.wait()
        for cp in local:
            cp.wait()

    return _comm_call("gather_chips", body, shards, [_sds((N_CHIPS,) + s.shape, s.dtype) for s in shards], [3 * n, 3 * n, n])


def _swap_halves(grads):
    n = len(grads)

    def body(*refs):
        src, dst = refs[:n], refs[n:2 * n]
        send_sems, recv_sems = refs[2 * n:]
        x, y, c = _place()
        copies = [pltpu.make_async_remote_copy(
            src_ref=src[t].at[:, 1 - c], dst_ref=dst[t], send_sem=send_sems.at[t], recv_sem=recv_sems.at[t],
            device_id=(x, y, 1 - c), device_id_type=MESH) for t in range(n)]
        for cp in copies:
            cp.start()
        for cp in copies:
            cp.wait()

    return _comm_call("swap_halves", body, grads, [_sds((g.shape[0],) + g.shape[2:], g.dtype) for g in grads], [n, n])


def _scatter_chips(partials):
    n = len(partials)

    def body(*refs):
        src, dst = refs[:n], refs[n:2 * n]
        send_sems, recv_sems = refs[2 * n:]
        x, y, c = _place()
        copies = []
        for t in range(n):
            for r, (px, py) in enumerate(_other_chips(x, y)):
                copies.append(pltpu.make_async_remote_copy(
                    src_ref=src[t].at[2 * px + py], dst_ref=dst[t].at[r], send_sem=send_sems.at[3 * t + r],
                    recv_sem=recv_sems.at[3 * t + r], device_id=(px, py, c), device_id_type=MESH))
        for cp in copies:
            cp.start()
        for cp in copies:
            cp.wait()

    return _comm_call("scatter_chips", body, partials, [_sds((3,) + p.shape[1:], p.dtype) for p in partials], [3 * n, 3 * n])


def _join_halves(halves, layout):
    n = len(halves)
    where, out_shapes = layout

    def body(*refs):
        src, dst = refs[:n], refs[n:n + len(out_shapes)]
        send_sems, recv_sems, local_sems = refs[n + len(out_shapes):]
        x, y, c = _place()
        local, remote = [], []
        for t in range(n):
            o, layer = where[t]
            local.append(pltpu.make_async_copy(src[t], dst[o].at[layer, c], local_sems.at[t]))
            remote.append(pltpu.make_async_remote_copy(
                src_ref=src[t], dst_ref=dst[o].at[layer, c], send_sem=send_sems.at[t], recv_sem=recv_sems.at[t],
                device_id=(x, y, 1 - c), device_id_type=MESH))
        for cp in local + remote:
            cp.start()
        for cp in remote:
            cp.wait()
        for cp in local:
            cp.wait()

    return _comm_call("join_halves", body, halves, [_sds(s, F32) for s in out_shapes], [n, n, n])


def _gather_devices(packed):
    def body(src, dst, send_sems, recv_sems, local_sem):
        x, y, c = _place()
        me = 4 * x + 2 * y + c
        local = pltpu.make_async_copy(src, dst.at[me], local_sem.at[0])
        local.start()
        copies = []
        for r in range(1, N_DEV):
            fx, fy, fc = (r >> 2) & 1, (r >> 1) & 1, r & 1
            peer = (1 - x if fx else x, 1 - y if fy else y, 1 - c if fc else c)
            copies.append(pltpu.make_async_remote_copy(
                src_ref=src, dst_ref=dst.at[me], send_sem=send_sems.at[r - 1], recv_sem=recv_sems.at[r - 1],
                device_id=peer, device_id_type=MESH))
        for cp in copies:
            cp.start()
        for cp in copies:
            cp.wait()
        local.wait()

    return _comm_call("gather_devices", body, [packed], [_sds((N_DEV,) + packed.shape, packed.dtype)],
                      [N_DEV - 1, N_DEV - 1, 1])[0]


def _add_halves(grad, got, place):
    _, _, rows, cols = grad.shape
    tr = _tile(rows, 256, 16)

    def body(s, a, b, o):
        del s
        o[...] = (a[...].astype(F32) + b[...].astype(F32)).astype(o.dtype)

    return _call("add_halves", body, (N_CHIPS, rows // tr),
                 [_spec((None, None, tr, cols), lambda j, i, s: (j, s[1], i, 0)), _spec((None, tr, cols), lambda j, i, s: (j, i, 0))],
                 _spec((None, tr, cols), lambda j, i, s: (j, i, 0)), _sds((N_CHIPS, rows, cols), grad.dtype), prefetch=1
                 )(place, grad, got)


def _add_chips(partial, got, place):
    _, rows, cols = partial.shape
    tr = _tile(rows, 256, 16)

    def body(s, a, b, o):
        del s
        o[...] = a[...].astype(F32) + b[0].astype(F32) + b[1].astype(F32) + b[2].astype(F32)

    return _call("add_chips", body, (rows // tr,),
                 [_spec((None, tr, cols), lambda i, s: (s[0], i, 0)), _spec((3, tr, cols), lambda i, s: (0, i, 0))],
                 _spec((tr, cols), lambda i, s: (i, 0)), _sds((rows, cols), F32), prefetch=1)(place, partial, got)


def _add_devices(got):
    _, rows, cols = got.shape
    tr = _tile(rows, 512, 8)

    def body(a, o):
        total = a[0]
        for d in range(1, N_DEV):
            total = total + a[d]
        o[...] = total

    return _call("add_devices", body, (rows // tr,), [_spec((N_DEV, tr, cols), lambda i: (0, i, 0))],
                 _spec((tr, cols), lambda i: (i, 0)), _sds((rows, cols), F32))(got)


def _adamw(w, g, m, v):
    layers, rows, cols = w.shape
    tr = _tile(rows, 256, 8)

    def body(w_ref, g_ref, m_ref, v_ref, d_ref, nm_ref, nv_ref):
        grad = g_ref[...]
        new_m = ADAM_B1 * m_ref[...] + (1.0 - ADAM_B1) * grad
        new_v = ADAM_B2 * v_ref[...] + (1.0 - ADAM_B2) * jnp.square(grad)
        m_hat = new_m / (1.0 - ADAM_B1 ** ADAM_STEP)
        v_hat = new_v / (1.0 - ADAM_B2 ** ADAM_STEP)
        d_ref[...] = -ADAM_LR * (m_hat / (jnp.sqrt(v_hat) + ADAM_EPS) + ADAM_WD * w_ref[...])
        nm_ref[...] = new_m
        nv_ref[...] = new_v

    blk = _spec((None, tr, cols), lambda l, i: (l, i, 0))
    return _call("adamw", body, (layers, rows // tr), [blk] * 4, [blk] * 3, [_sds(w.shape, F32)] * 3)(w, g, m, v)


def _rope_layout(w):
    z = jnp.zeros(w.shape[:-1] + (ROPE_HALF,), w.dtype)
    return jnp.concatenate([w[..., :ROPE_HALF], z, w[..., ROPE_HALF:], z], axis=-1)


def _rope_layout_inv(w):
    return jnp.concatenate([w[..., :ROPE_HALF], w[..., 2 * ROPE_HALF:3 * ROPE_HALF]], axis=-1)


def _cols_from_chips(g):
    return jnp.moveaxis(g, 0, 1).reshape(g.shape[1], N_CHIPS * g.shape[2])


def _cols_to_chips(w):
    return jnp.moveaxis(w.reshape(w.shape[0], N_CHIPS, w.shape[1] // N_CHIPS), 1, 0)


def _w_in_layout(w):
    off_kr = Q_LORA + KV_LORA
    return jnp.concatenate([w[:, :off_kr], w[:, off_kr + QK_ROPE:], _rope_layout(w[:, off_kr:off_kr + QK_ROPE])], axis=1)


def _w_in_layout_inv(dw):
    return jnp.concatenate([dw[:, :P_A], _rope_layout_inv(dw[:, P_KR:]), dw[:, P_A:P_KR]], axis=1)


def _w_uq_layout(w):
    w = w.reshape(Q_LORA, HEADS, QK_NOPE + QK_ROPE)
    return jnp.concatenate([w[..., :QK_NOPE].reshape(Q_LORA, HEADS * QK_NOPE),
                            _rope_layout(w[..., QK_NOPE:]).reshape(Q_LORA, HEADS * LANES)], axis=1)


def _w_uq_layout_inv(d_nope, d_rope):
    d_nope = d_nope.reshape(Q_LORA, HEADS, QK_NOPE)
    d_rope = _rope_layout_inv(d_rope.reshape(Q_LORA, HEADS, LANES))
    return jnp.concatenate([d_nope, d_rope], axis=-1).reshape(Q_LORA, HEADS * (QK_NOPE + QK_ROPE))


def _to_lanes(a):
    flat = a.reshape(-1)
    pad = (-flat.shape[0]) % LANES
    if pad:
        flat = jnp.concatenate([flat, jnp.zeros((pad,), flat.dtype)])
    return flat.reshape(-1, LANES)


PACK_ROWS = 64


def _pack(arrays):
    parts, ranges, row = [], [], 0
    for a in arrays:
        p = _to_lanes(a)
        parts.append(p)
        ranges.append((row, row + p.shape[0]))
        row += p.shape[0]
    pad = (-row) % PACK_ROWS
    if pad:
        parts.append(jnp.zeros((pad, LANES), F32))
    return jnp.concatenate(parts, axis=0), ranges


def _unpack(packed, rng, shape):
    n = 1
    for s in shape:
        n *= s
    return packed[rng[0]:rng[1]].reshape(-1)[:n].reshape(shape)


def kernel(x, positions, ffn_a_pre_g, ffn_a_post_g, ffn_a_w_gate, ffn_a_w_up, ffn_a_w_down, ffn_b_pre_g, ffn_b_post_g, ffn_b_w_gate, ffn_b_w_up, ffn_b_w_down, even_pre_g, even_post_g, even_w_in, even_q_norm_g, even_kv_norm_g, even_w_uq, even_w_ukv, even_conv_w, even_conv_b, even_conv_norm_g, even_conv_norm_b, even_w_out, odd_pre_g, odd_post_g, odd_w_in, odd_v_norm_g, odd_v_norm_b, odd_w_s, odd_b_s, odd_w_out, loss_target, m_ffn_a_pre_g, m_ffn_a_post_g, m_ffn_a_w_gate, m_ffn_a_w_up, m_ffn_a_w_down, m_ffn_b_pre_g, m_ffn_b_post_g, m_ffn_b_w_gate, m_ffn_b_w_up, m_ffn_b_w_down, m_even_pre_g, m_even_post_g, m_even_w_in, m_even_q_norm_g, m_even_kv_norm_g, m_even_w_uq, m_even_w_ukv, m_even_conv_w, m_even_conv_b, m_even_conv_norm_g, m_even_conv_norm_b, m_even_w_out, m_odd_pre_g, m_odd_post_g, m_odd_w_in, m_odd_v_norm_g, m_odd_v_norm_b, m_odd_w_s, m_odd_b_s, m_odd_w_out, v_ffn_a_pre_g, v_ffn_a_post_g, v_ffn_a_w_gate, v_ffn_a_w_up, v_ffn_a_w_down, v_ffn_b_pre_g, v_ffn_b_post_g, v_ffn_b_w_gate, v_ffn_b_w_up, v_ffn_b_w_down, v_even_pre_g, v_even_post_g, v_even_w_in, v_even_q_norm_g, v_even_kv_norm_g, v_even_w_uq, v_even_w_ukv, v_even_conv_w, v_even_conv_b, v_even_conv_norm_g, v_even_conv_norm_b, v_even_w_out, v_odd_pre_g, v_odd_post_g, v_odd_w_in, v_odd_v_norm_g, v_odd_v_norm_b, v_odd_w_s, v_odd_b_s, v_odd_w_out):
    given = dict(locals())
    w = {n: given[n] for n in WEIGHTS}
    batch, seq, d = x.shape
    t = batch * seq
    tm = _tile(t, 512, 16)
    tr = _tile(t, 256, 16)
    chip = 2 * lax.axis_index("x") + lax.axis_index("y")
    place = jnp.stack([chip, lax.axis_index("c")]).astype(jnp.int32)

    small_shard, small_shard_rng = _pack([w[n] for n in SMALL_SHARDED])
    gathered = _gather_chips([w[n].astype(MXU_DTYPE) for n in BIG] + [small_shard])
    wf = dict(zip(BIG, gathered[:-1]))
    shards = gathered[-1]
    full = dict(w)
    for n, rng in zip(SMALL_SHARDED, small_shard_rng):
        per_chip = [_unpack(shards[j], rng, w[n].shape) for j in range(N_CHIPS)]
        full[n] = jnp.concatenate(per_chip, axis=-1)
    w_in = _w_in_layout(_cols_from_chips(wf['even_w_in'][:, 0]))
    w_uq = _w_uq_layout(_cols_from_chips(wf['even_w_uq'][:, 0]))
    w_ukv = _cols_from_chips(wf['even_w_ukv'][:, 0])
    w_out_e = wf['even_w_out'][:, 0].reshape(d, d)
    w_out_o = wf['odd_w_out'][:, 0].reshape(GM_WIDTH, d)
    w_in_o = wf['odd_w_in']

    inv_freq = ROPE_THETA ** (-jnp.arange(0, QK_ROPE, 2, dtype=F32) / QK_ROPE)
    ang = positions.astype(F32).reshape(t, 1) * inv_freq
    zeros = jnp.zeros((t, ROPE_HALF), F32)
    cos = jnp.concatenate([jnp.cos(ang), zeros, jnp.cos(ang), zeros], axis=1)
    sin = jnp.concatenate([-jnp.sin(ang), zeros, jnp.sin(ang), zeros], axis=1)

    def ffn_args(tag, layer):
        return (w[f'ffn_{tag}_pre_g'][layer:layer + 1], w[f'ffn_{tag}_post_g'][layer:layer + 1], wf[f'ffn_{tag}_w_gate'],
                wf[f'ffn_{tag}_w_up'], wf[f'ffn_{tag}_w_down'], layer, tm, tr)

    xs = x.reshape(t, d)
    x1, ffn_a0 = _ffn_fwd(xs, *ffn_args('a', 0))
    h_e = _rms_fwd(x1, w['even_pre_g'], tr)
    p = _mm_nn("mm_w_in", h_e, w_in, F32, tm, tn_want=640)
    lat = _rows("latent_norm", lambda r, c: ([jnp.concatenate([_rmsnorm(r[0], c[0]), _rmsnorm(r[1], c[1])], axis=1)], []),
                [(p, Q_LORA, 0), (p, KV_LORA, 1)], [w['even_q_norm_g'], w['even_kv_norm_g']], [(Q_LORA + KV_LORA, MXU_DTYPE)], [], tr)[0]
    q_all = _mm_nn("mm_w_uq", lat, w_uq, F32, tm, a_col0=0, tk_want=Q_LORA)
    kv = _mm_nn("mm_w_ukv", lat, w_ukv, MXU_DTYPE, tm, a_col0=Q_LORA, tk_want=KV_LORA)
    cat = _attn_fwd(q_all, kv, p, cos, sin, batch, seq)
    cat = _conv_fwd(p, cat, full['even_conv_w'][0], w['even_conv_b'], w['even_conv_norm_g'], w['even_conv_norm_b'], batch, seq)
    y_e = _mm_nn("mm_w_out", cat, w_out_e, F32, tm)
    x2 = _rms_residual(x1, y_e, w['even_post_g'], 1.0, tr)
    x3, ffn_b0 = _ffn_fwd(x2, *ffn_args('b', 0))
    x4, ffn_a1 = _ffn_fwd(x3, *ffn_args('a', 1))
    h_o = _rms_fwd(x4, full['odd_pre_g'], tr)
    ns = w_in_o.shape[3]
    tk_o = _tile(d, 1024, LANES)
    (hw,) = _matmul("mm_w_in_odd", "nn", (t // tm, N_CHIPS, d // tk_o),
                    [[(h_o, _spec((tm, tk_o), lambda i, j, k: (i, k)), w_in_o, _spec((None, None, tk_o, ns), lambda i, j, k: (j, 0, k, 0)))]],
                    [((t, N_CHIPS * ns), F32, _spec((tm, ns), lambda i, j, k: (i, j)))], (tm, ns))
    b_st = w['odd_b_s'][0].T
    sg = _sgu_fwd(hw, full['odd_v_norm_g'], full['odd_v_norm_b'], w['odd_w_s'][0], b_st)
    y_o = _mm_nn("mm_w_out", sg, w_out_o, F32, tm)
    x5 = _rms_residual(x4, y_o, full['odd_post_g'], 1.0, tr)
    x6, ffn_b1 = _ffn_fwd(x5, *ffn_args('b', 1))

    dy, sq_err = _loss_head(x6, loss_target.reshape(t, d), tr)
    loss = lax.psum(0.5 * sq_err[0, 0] / d, ("x", "y", "c"))

    grads = {}
    dx, (gpre, gpost, gwg, gwu, gwd) = _ffn_bwd(dy, ffn_b1, *ffn_args('b', 1))
    fg = {('b', 1): (gpre, gpost, gwg, gwu, gwd)}
    dy_o, grads['odd_post_g'] = _rms_bwd(y_o, full['odd_post_g'], dx, 1.0, MXU_DTYPE, tr)
    dsg = _mm_nt("mm_dsg", [(dy_o, w_out_o, 0)], F32, tm)
    (g_w_out_o,) = _mm_tn("mm_dw_out", sg, [dy_o], MXU_DTYPE)
    dhw, grads['odd_v_norm_g'], grads['odd_v_norm_b'], g_ws, g_bst = _sgu_bwd(
        hw, full['odd_v_norm_g'], full['odd_v_norm_b'], w['odd_w_s'][0], b_st, dsg)
    grads['odd_w_s'] = g_ws[None]
    grads['odd_b_s'] = g_bst[:, :GROUPS].T[None]
    tn_o = _tile(d, 1024, LANES)
    (dh_o,) = _matmul("mm_dh_odd", "nt", (t // tm, d // tn_o, N_CHIPS),
                      [[(dhw, _spec((tm, ns), lambda i, j, k: (i, k)), w_in_o, _spec((None, None, tn_o, ns), lambda i, j, k: (k, 0, j, 0)))]],
                      [((t, d), F32, _spec((tm, tn_o), lambda i, j, k: (i, j)))], (tm, tn_o))
    tmw = _tile(d, 512, LANES)
    tkt = _tile(t, 512, 16)
    (g_w_in_o,) = _matmul("mm_dw_in_odd", "tn", (d // tmw, N_CHIPS, t // tkt),
                          [[(h_o, _spec((tkt, tmw), lambda i, j, k: (k, i)), dhw, _spec((tkt, ns), lambda i, j, k: (k, j)))]],
                          [((N_CHIPS, d, ns), MXU_DTYPE, _spec((None, tmw, ns), lambda i, j, k: (j, i, 0)))], (tmw, ns))
    dx, grads['odd_pre_g'] = _rms_bwd(x4, full['odd_pre_g'], dh_o, 1.0, F32, tr, resid=dx)
    dx, fg[('a', 1)] = _ffn_bwd(dx, ffn_a1, *ffn_args('a', 1))
    dx, fg[('b', 0)] = _ffn_bwd(dx, ffn_b0, *ffn_args('b', 0))
    dy_e, grads['even_post_g'] = _rms_bwd(y_e, w['even_post_g'], dx, 1.0, MXU_DTYPE, tr)
    d_cat = _mm_nt("mm_dcat", [(dy_e, w_out_e, 0)], F32, tm)
    (g_w_out_e,) = _mm_tn("mm_dw_out", cat, [dy_e], MXU_DTYPE)
    dqn, dqp, dkv, dkr = _attn_bwd(q_all, kv, p, cos, sin, d_cat, batch, seq)
    hq = HEADS * LANES
    d_latq = _mm_nt("mm_dlat_q", [(dqn, w_uq, 0), (dqp, w_uq, hq)], F32, tm, tn_want=Q_LORA)
    d_latkv = _mm_nt("mm_dlat_kv", [(dkv, w_ukv, 0)], F32, tm, tn_want=KV_LORA)
    g_uq_n, g_uq_r = _mm_tn("mm_dw_uq", lat, [dqn, dqp], MXU_DTYPE, a_col0=0, m_width=Q_LORA)
    (g_ukv,) = _mm_tn("mm_dw_ukv", lat, [dkv], MXU_DTYPE, a_col0=Q_LORA, m_width=KV_LORA)

    def latent_bwd(r, c):
        _, vjp_q = jax.vjp(_rmsnorm, r[0], c[0])
        _, vjp_kv = jax.vjp(_rmsnorm, r[1], c[1])
        dq, dqg = vjp_q(r[2])
        dk, dkg = vjp_kv(r[3])
        return [jnp.concatenate([dq, dk], axis=1)], [dqg, dkg]

    d_lat, grads['even_q_norm_g'], grads['even_kv_norm_g'] = _rows(
        "latent_norm_bwd", latent_bwd, [(p, Q_LORA, 0), (p, KV_LORA, 1), (d_latq, Q_LORA, 0), (d_latkv, KV_LORA, 0)],
        [w['even_q_norm_g'], w['even_kv_norm_g']], [(Q_LORA + KV_LORA, MXU_DTYPE)], [(1, Q_LORA), (1, KV_LORA)], tr)
    da, dgate, g_conv_w, grads['even_conv_b'], grads['even_conv_norm_g'], grads['even_conv_norm_b'] = _conv_bwd(
        p, d_cat, full['even_conv_w'][0], w['even_conv_b'], w['even_conv_norm_g'], w['even_conv_norm_b'], batch, seq)
    dp = jnp.concatenate([d_lat, da, dgate, dkr.astype(MXU_DTYPE)], axis=1)
    dh_e = _mm_nt("mm_dh_even", [(dp, w_in, 0)], F32, tm, tk_want=640)
    (g_w_in,) = _mm_tn("mm_dw_in", h_e, [dp], MXU_DTYPE, tn_want=640)
    dx, grads['even_pre_g'] = _rms_bwd(x1, w['even_pre_g'], dh_e, 1.0, F32, tr, resid=dx)
    dx, fg[('a', 0)] = _ffn_bwd(dx, ffn_a0, *ffn_args('a', 0))
    grad_x = dx.reshape(batch, seq, d)

    items, layout_where, layout_shapes = [], [], []
    for o, n in enumerate(BIG):
        layers, rows, cols = w[n].shape
        layout_shapes.append((layers, 2, rows // 2, cols))
        for layer in range(layers):
            if n.startswith('ffn_'):
                tag = n[4]
                it = fg[(tag, layer)][2 + ['w_gate', 'w_up', 'w_down'].index(n[6:])]
            else:
                it = {'even_w_in': lambda: _cols_to_chips(_w_in_layout_inv(g_w_in)),
                      'even_w_uq': lambda: _cols_to_chips(_w_uq_layout_inv(g_uq_n, g_uq_r)),
                      'even_w_ukv': lambda: _cols_to_chips(g_ukv),
                      'even_w_out': lambda: g_w_out_e.reshape(N_CHIPS, rows, cols),
                      'odd_w_in': lambda: g_w_in_o,
                      'odd_w_out': lambda: g_w_out_o.reshape(N_CHIPS, rows, cols)}[n]()
            items.append(it.reshape(N_CHIPS, 2, rows // 2, cols))
            layout_where.append((o, layer))
    got = _swap_halves(items)
    partials = [_add_halves(g, r, place) for g, r in zip(items, got)]
    got = _scatter_chips(partials)
    halves = [_add_chips(pt, r, place) for pt, r in zip(partials, got)]
    joined = _join_halves(halves, (layout_where, layout_shapes))
    for n, g in zip(BIG, joined):
        grads[n] = g.reshape(w[n].shape)

    for tag in ('a', 'b'):
        grads[f'ffn_{tag}_pre_g'] = jnp.concatenate([fg[(tag, 0)][0], fg[(tag, 1)][0]], axis=0)
        grads[f'ffn_{tag}_post_g'] = jnp.concatenate([fg[(tag, 0)][1], fg[(tag, 1)][1]], axis=0)
    grads['even_conv_w'] = g_conv_w[None]
    packed, rngs = _pack([grads[n] for n in SMALL])
    summed = _add_devices(_gather_devices(packed))
    for n, rng in zip(SMALL, rngs):
        g = _unpack(summed, rng, grads[n].shape)
        if n in SMALL_SHARDED:
            width = w[n].shape[-1]
            g = lax.dynamic_slice_in_dim(g, chip * width, width, axis=g.ndim - 1)
        grads[n] = g

    delta, new_m, new_v = {}, {}, {}
    for n in BIG:
        delta[n], new_m[n], new_v[n] = _adamw(w[n], grads[n], given['m_' + n], given['v_' + n])
    packs = [_pack([src[n] for n in SMALL])[0][None] for src in
             (w, grads, {n: given['m_' + n] for n in SMALL}, {n: given['v_' + n] for n in SMALL})]
    _, rngs = _pack([w[n] for n in SMALL])
    small_out = _adamw(*packs)
    for n, rng in zip(SMALL, rngs):
        delta[n], new_m[n], new_v[n] = (_unpack(o[0], rng, w[n].shape) for o in small_out)

    return (loss, grad_x, *[grads[n] for n in WEIGHTS], *[delta[n] for n in WEIGHTS],
            *[new_m[n] for n in WEIGHTS], *[new_v[n] for n in WEIGHTS])
```

```python
import functools

import jax
import jax.numpy as jnp
from jax import lax
from jax.experimental import pallas as pl
from jax.experimental.pallas import tpu as pltpu

F32 = jnp.float32
BF16 = jnp.bfloat16
MXU_DTYPE = BF16
MESH = pl.DeviceIdType.MESH
VMEM_LIMIT_BYTES = 56 * 1024 * 1024
LANES = 128

D_MODEL = 2048
D_FF = 5632
EPS = 1e-6
HEADS = 8
V_HEAD = 128
QK_NOPE = 128
QK_ROPE = 64
Q_LORA = 512
KV_LORA = 512
ROPE_THETA = 10000.0
CONV_CH = 1024
CONV_WIDTH = 31
GROUPS = 8
CHUNK = 128
GM_WIDTH = 2048
ADAM_LR = 0.001
ADAM_B1 = 0.9
ADAM_B2 = 0.999
ADAM_EPS = 1e-08
ADAM_WD = 0.01
ADAM_STEP = 10

N_CHIPS = 4
N_DEV = 8
P_KV = Q_LORA
P_A = Q_LORA + KV_LORA
P_GATE = P_A + CONV_CH
P_KR = P_GATE + CONV_CH
P_WIDTH = P_KR + LANES
ROPE_HALF = QK_ROPE // 2

WEIGHTS = ['ffn_a_pre_g', 'ffn_a_post_g', 'ffn_a_w_gate', 'ffn_a_w_up', 'ffn_a_w_down', 'ffn_b_pre_g', 'ffn_b_post_g',
           'ffn_b_w_gate', 'ffn_b_w_up', 'ffn_b_w_down', 'even_pre_g', 'even_post_g', 'even_w_in', 'even_q_norm_g',
           'even_kv_norm_g', 'even_w_uq', 'even_w_ukv', 'even_conv_w', 'even_conv_b', 'even_conv_norm_g',
           'even_conv_norm_b', 'even_w_out', 'odd_pre_g', 'odd_post_g', 'odd_w_in', 'odd_v_norm_g', 'odd_v_norm_b',
           'odd_w_s', 'odd_b_s', 'odd_w_out']
BIG = ['ffn_a_w_gate', 'ffn_a_w_up', 'ffn_a_w_down', 'ffn_b_w_gate', 'ffn_b_w_up', 'ffn_b_w_down', 'even_w_in',
       'even_w_uq', 'even_w_ukv', 'even_w_out', 'odd_w_in', 'odd_w_out']
SMALL = [n for n in WEIGHTS if n not in BIG]
SMALL_SHARDED = ['even_conv_w', 'odd_pre_g', 'odd_post_g', 'odd_v_norm_g', 'odd_v_norm_b']


def _call(name, body, grid, in_specs, out_specs, out_shape, scratch=(), prefetch=0, aliases=None):
    params = pltpu.CompilerParams(dimension_semantics=("arbitrary",) * len(grid), vmem_limit_bytes=VMEM_LIMIT_BYTES)
    if prefetch:
        spec = pltpu.PrefetchScalarGridSpec(num_scalar_prefetch=prefetch, grid=grid, in_specs=in_specs,
                                            out_specs=out_specs, scratch_shapes=list(scratch))
        return pl.pallas_call(body, grid_spec=spec, out_shape=out_shape, compiler_params=params, name=name,
                              input_output_aliases=aliases or {})
    return pl.pallas_call(body, grid=grid, in_specs=in_specs, out_specs=out_specs, out_shape=out_shape,
                          scratch_shapes=list(scratch), compiler_params=params, name=name,
                          input_output_aliases=aliases or {})


def _spec(block, index_map):
    return pl.BlockSpec(block, index_map)


def _sds(shape, dtype):
    return jax.ShapeDtypeStruct(tuple(shape), dtype)


def _tile(n, want, mult=8):
    if n <= want:
        return n
    best = None
    for t in range(mult, want + 1, mult):
        if n % t == 0:
            best = t
    assert best is not None, (n, want, mult)
    return best


_DIMS = {"nn": (((1,), (0,)), ((), ())), "nt": (((1,), (1,)), ((), ())), "tn": (((0,), (0,)), ((), ()))}


def _dot(a, b, mode="nn"):
    return lax.dot_general(a.astype(MXU_DTYPE), b.astype(MXU_DTYPE), _DIMS[mode], preferred_element_type=F32)


def _matmul(name, mode, grid, groups, outs, acc_shape, extras=(), epilogue=None):
    flat, specs = [], []
    for grp in groups:
        for a, a_spec, b, b_spec in grp:
            flat += [a, b]
            specs += [a_spec, b_spec]
    for e, e_spec in extras:
        flat.append(e)
        specs.append(e_spec)
    n_pairs = [len(g) for g in groups]
    n_in, n_ex, n_out, n_acc = 2 * sum(n_pairs), len(extras), len(outs), len(groups)
    last_k = grid[2] - 1

    def body(*refs):
        ins, ex = refs[:n_in], refs[n_in:n_in + n_ex]
        out_refs = refs[n_in + n_ex:n_in + n_ex + n_out]
        accs = refs[n_in + n_ex + n_out:]
        k = pl.program_id(2)

        @pl.when(k == 0)
        def _():
            for acc in accs:
                acc[...] = jnp.zeros(acc.shape, F32)

        pos = 0
        for gi, n in enumerate(n_pairs):
            total = None
            for _ in range(n):
                d = _dot(ins[pos][...], ins[pos + 1][...], mode)
                total = d if total is None else total + d
                pos += 2
            accs[gi][...] += total

        @pl.when(k == last_k)
        def _():
            vals = [acc[...] for acc in accs]
            res = epilogue(vals, [e[...] for e in ex]) if epilogue else vals
            for o, r in zip(out_refs, res):
                o[...] = r.astype(o.dtype)

    res = _call(name, body, grid, specs, [o[2] for o in outs], [_sds(o[0], o[1]) for o in outs],
                scratch=[pltpu.VMEM(acc_shape, F32)] * n_acc)(*flat)
    return res


def _rows(name, fn, rows, consts, outs, accs, tm):
    t_rows = rows[0][0].shape[0]
    grid = (t_rows // tm,)
    in_specs = [_spec((tm, bw), functools.partial(lambda i, cb: (i, cb), cb=cb)) for _, bw, cb in rows]
    in_specs += [_spec(c.shape, functools.partial(lambda i, nd: (0,) * nd, nd=c.ndim)) for c in consts]
    out_shape = [_sds((t_rows, w), dt) for w, dt in outs] + [_sds(s, F32) for s in accs]
    out_specs = [_spec((tm, w), lambda i: (i, 0)) for w, _ in outs]
    out_specs += [_spec(s, functools.partial(lambda i, nd: (0,) * nd, nd=len(s))) for s in accs]
    nr, nc, no = len(rows), len(consts), len(outs)

    def body(*refs):
        r = [ref[...] for ref in refs[:nr]]
        c = [ref[...] for ref in refs[nr:nr + nc]]
        o_refs, a_refs = refs[nr + nc:nr + nc + no], refs[nr + nc + no:]
        o_vals, a_vals = fn(r, c)
        for ref, v in zip(o_refs, o_vals):
            ref[...] = v.astype(ref.dtype)
        if a_refs:
            @pl.when(pl.program_id(0) == 0)
            def _():
                for ref in a_refs:
                    ref[...] = jnp.zeros(ref.shape, F32)

            for ref, v in zip(a_refs, a_vals):
                ref[...] += v

    return _call(name, body, grid, in_specs, out_specs, out_shape)(*[a for a, _, _ in rows], *consts)


def _rmsnorm(x, g):
    return x * lax.rsqrt(jnp.mean(x * x, axis=-1, keepdims=True) + EPS) * g


def _layernorm(x, g, b):
    mu = jnp.mean(x, axis=-1, keepdims=True)
    var = jnp.mean(jnp.square(x - mu), axis=-1, keepdims=True)
    return (x - mu) * lax.rsqrt(var + EPS) * g + b


def _swiglu_act(g, u):
    return jax.nn.silu(g) * u


def _rope(x, cos, sin_signed):
    return x * cos + pltpu.roll(x, 2 * ROPE_HALF, 1) * sin_signed


def _rope_transposed(dy, cos, sin_signed):
    return dy * cos - pltpu.roll(dy, 2 * ROPE_HALF, 1) * sin_signed


def _rms_fwd(x, g, tm):
    d = x.shape[1]
    return _rows("rms_fwd", lambda r, c: ([_rmsnorm(r[0], c[0])], []), [(x, d, 0)], [g], [(d, MXU_DTYPE)], [], tm)[0]


def _rms_residual(x, y, g, scale, tm):
    d = x.shape[1]
    return _rows("rms_residual", lambda r, c: ([r[0] + scale * _rmsnorm(r[1], c[0])], []),
                 [(x, d, 0), (y, d, 0)], [g], [(d, F32)], [], tm)[0]


def _rms_bwd(y, g, dout, scale, out_dtype, tm, resid=None):
    d = y.shape[1]

    def fn(r, c):
        _, vjp = jax.vjp(_rmsnorm, r[0], c[0])
        dy, dg = vjp(scale * r[1].astype(F32))
        if resid is not None:
            dy = dy + r[2]
        return [dy], [dg]

    rows = [(y, d, 0), (dout, d, 0)] + ([(resid, d, 0)] if resid is not None else [])
    return _rows("rms_bwd" if resid is None else "rms_bwd_resid", fn, rows, [g], [(d, out_dtype)], [(1, d)], tm)


def _loss_head(y, target, tm):
    d = y.shape[1]

    def fn(r, c):
        err = r[0] - r[1]
        sq = jnp.sum(jnp.sum(err * err, axis=1, keepdims=True), axis=0, keepdims=True)
        return [err * (1.0 / d)], [jnp.broadcast_to(sq, (1, LANES))]

    return _rows("loss_head", fn, [(y, d, 0), (target, d, 0)], [], [(d, F32)], [(1, LANES)], tm)


def _ffn_fwd(x, pre_g, post_g, wg, wu, wd, layer, tm, tr):
    t, d = x.shape
    fs = wg.shape[3]
    tk = _tile(d, 1024, LANES)
    tn = _tile(d, 1024, LANES)
    h = _rms_fwd(x, pre_g, tr)
    h_spec = _spec((tm, tk), lambda i, j, k: (i, k))
    w_spec = _spec((None, None, tk, fs), lambda i, j, k: (j, layer, k, 0))
    o_spec = _spec((tm, fs), lambda i, j, k: (i, j))
    g, u, a = _matmul("ffn_up", "nn", (t // tm, N_CHIPS, d // tk),
                      [[(h, h_spec, wg, w_spec)], [(h, h_spec, wu, w_spec)]],
                      [((t, N_CHIPS * fs), MXU_DTYPE, o_spec)] * 3, (tm, fs),
                      epilogue=lambda accs, _: (accs[0], accs[1], _swiglu_act(accs[0], accs[1])))
    (y,) = _matmul("ffn_down", "nn", (t // tm, d // tn, N_CHIPS),
                   [[(a, _spec((tm, fs), lambda i, j, k: (i, k)), wd, _spec((None, None, fs, tn), lambda i, j, k: (k, layer, 0, j)))]],
                   [((t, d), F32, _spec((tm, tn), lambda i, j, k: (i, j)))], (tm, tn))
    out = _rms_residual(x, y, post_g, 0.5, tr)
    return out, (x, h, g, u, a, y)


def _ffn_bwd(dout, saved, pre_g, post_g, wg, wu, wd, layer, tm, tr):
    x, h, g, u, a, y = saved
    t, d = x.shape
    fs = wg.shape[3]
    tk = _tile(d, 1024, LANES)
    tn = _tile(d, 1024, LANES)
    tkt = _tile(t, 512, 16)
    dy, dpost = _rms_bwd(y, post_g, dout, 0.5, MXU_DTYPE, tr)

    def act_bwd(accs, ex):
        _, vjp = jax.vjp(_swiglu_act, ex[0].astype(F32), ex[1].astype(F32))
        return vjp(accs[0])

    gu_spec = _spec((tm, fs), lambda i, j, k: (i, j))
    dg, du = _matmul("ffn_dact", "nt", (t // tm, N_CHIPS, d // tk),
                     [[(dy, _spec((tm, tk), lambda i, j, k: (i, k)), wd, _spec((None, None, fs, tk), lambda i, j, k: (j, layer, 0, k)))]],
                     [((t, N_CHIPS * fs), MXU_DTYPE, gu_spec)] * 2, (tm, fs),
                     extras=[(g, gu_spec), (u, gu_spec)], epilogue=act_bwd)
    (dwd,) = _matmul("ffn_dwdown", "tn", (N_CHIPS, d // tn, t // tkt),
                     [[(a, _spec((tkt, fs), lambda i, j, k: (k, i)), dy, _spec((tkt, tn), lambda i, j, k: (k, j)))]],
                     [((N_CHIPS, fs, d), MXU_DTYPE, _spec((None, fs, tn), lambda i, j, k: (i, 0, j)))], (fs, tn))
    da_spec = _spec((tm, fs), lambda i, j, k: (i, k))
    wt_spec = _spec((None, None, tn, fs), lambda i, j, k: (k, layer, j, 0))
    (dh,) = _matmul("ffn_dh", "nt", (t // tm, d // tn, N_CHIPS),
                    [[(dg, da_spec, wg, wt_spec), (du, da_spec, wu, wt_spec)]],
                    [((t, d), F32, _spec((tm, tn), lambda i, j, k: (i, j)))], (tm, tn))
    tmw = _tile(d, 512, LANES)
    h_spec = _spec((tkt, tmw), lambda i, j, k: (k, i))
    dgu_spec = _spec((tkt, fs), lambda i, j, k: (k, j))
    dw_spec = _spec((None, tmw, fs), lambda i, j, k: (j, i, 0))
    dwg, dwu = _matmul("ffn_dwup", "tn", (d // tmw, N_CHIPS, t // tkt),
                       [[(h, h_spec, dg, dgu_spec)], [(h, h_spec, du, dgu_spec)]],
                       [((N_CHIPS, d, fs), MXU_DTYPE, dw_spec)] * 2, (tmw, fs))
    dx, dpre = _rms_bwd(x, pre_g, dh, 1.0, F32, tr, resid=dout)
    return dx, (dpre, dpost, dwg, dwu, dwd)


def _attn_scores(qn, qpe, kn, kpe, qi, tq, seq):
    s = (_dot(qn, kn, "nt") + _dot(qpe, kpe, "nt")) * ((QK_NOPE + QK_ROPE) ** -0.5)
    rows = qi * tq + lax.broadcasted_iota(jnp.int32, (tq, seq), 0)
    cols = lax.broadcasted_iota(jnp.int32, (tq, seq), 1)
    s = jnp.where(cols <= rows, s, -jnp.inf)
    e = jnp.exp(s - jnp.max(s, axis=1, keepdims=True))
    return e / jnp.sum(e, axis=1, keepdims=True)


def _attn_specs(nq, tq, seq):
    q_rows = lambda b, h, qi: b * nq + qi
    return [
        _spec((tq, LANES), lambda b, h, qi: (q_rows(b, h, qi), h)),
        _spec((tq, LANES), lambda b, h, qi: (q_rows(b, h, qi), HEADS + h)),
        _spec((seq, 2 * LANES), lambda b, h, qi: (b, h)),
        _spec((seq, LANES), lambda b, h, qi: (b, P_KR // LANES)),
        _spec((tq, LANES), lambda b, h, qi: (q_rows(b, h, qi), 0)),
        _spec((tq, LANES), lambda b, h, qi: (q_rows(b, h, qi), 0)),
        _spec((seq, LANES), lambda b, h, qi: (b, 0)),
        _spec((seq, LANES), lambda b, h, qi: (b, 0)),
    ]


def _attn_fwd(q_all, kv, p, cos, sin, batch, seq):
    t = q_all.shape[0]
    tq = _tile(seq, 256, 16)
    nq = seq // tq

    def body(qn_ref, qp_ref, kv_ref, kr_ref, cq, sq, ck, sk, o_ref):
        qpe = _rope(qp_ref[...], cq[...], sq[...])
        kpe = _rope(kr_ref[...], ck[...], sk[...])
        prob = _attn_scores(qn_ref[...], qpe, kv_ref[:, :LANES], kpe, pl.program_id(2), tq, seq)
        o_ref[...] = _dot(prob, kv_ref[:, LANES:]).astype(o_ref.dtype)

    return _call("attn_fwd", body, (batch, HEADS, nq), _attn_specs(nq, tq, seq),
                 _spec((tq, LANES), lambda b, h, qi: (b * nq + qi, h)), _sds((t, 2 * HEADS * V_HEAD), MXU_DTYPE)
                 )(q_all, q_all, kv, p, cos, sin, cos, sin)


def _attn_bwd(q_all, kv, p, cos, sin, d_cat, batch, seq):
    t = q_all.shape[0]
    tq = _tile(seq, 256, 16)
    nq = seq // tq

    def body(qn_ref, qp_ref, kv_ref, kr_ref, cq, sq, ck, sk, do_ref, dqn_ref, dqp_ref, dkv_ref, dkr_ref, dkpe_acc):
        h, qi = pl.program_id(1), pl.program_id(2)
        qn = qn_ref[...]
        qpe = _rope(qp_ref[...], cq[...], sq[...])
        kn, v = kv_ref[:, :LANES], kv_ref[:, LANES:]
        kpe = _rope(kr_ref[...], ck[...], sk[...])
        prob = _attn_scores(qn, qpe, kn, kpe, qi, tq, seq)
        do = do_ref[...]
        dprob = _dot(do, v, "nt")
        ds = prob * (dprob - jnp.sum(prob * dprob, axis=1, keepdims=True)) * ((QK_NOPE + QK_ROPE) ** -0.5)
        dqn_ref[...] = _dot(ds, kn)
        dqp_ref[...] = _rope_transposed(_dot(ds, kpe), cq[...], sq[...])

        @pl.when(qi == 0)
        def _():
            dkv_ref[...] = jnp.zeros(dkv_ref.shape, F32)

        dkv_ref[:, :LANES] += _dot(ds, qn, "tn")
        dkv_ref[:, LANES:] += _dot(prob, do, "tn")

        @pl.when((h == 0) & (qi == 0))
        def _():
            dkpe_acc[...] = jnp.zeros(dkpe_acc.shape, F32)

        dkpe_acc[...] += _dot(ds, qpe, "tn")

        @pl.when((h == HEADS - 1) & (qi == nq - 1))
        def _():
            dkr_ref[...] = _rope_transposed(dkpe_acc[...], ck[...], sk[...])

    q_out = _spec((tq, LANES), lambda b, h, qi: (b * nq + qi, h))
    return _call("attn_bwd", body, (batch, HEADS, nq),
                 _attn_specs(nq, tq, seq) + [_spec((tq, LANES), lambda b, h, qi: (b * nq + qi, h))],
                 [q_out, q_out, _spec((seq, 2 * LANES), lambda b, h, qi: (b, h)), _spec((seq, LANES), lambda b, h, qi: (b, 0))],
                 [_sds((t, HEADS * LANES), F32), _sds((t, HEADS * LANES), F32), _sds((t, HEADS * 2 * LANES), F32), _sds((t, LANES), F32)],
                 scratch=[pltpu.VMEM((seq, LANES), F32)])(q_all, q_all, kv, p, cos, sin, cos, sin, d_cat)


CONV_PAD = 32


def _conv_taps(w_ref, src_ref, first_row, n_rows, init, offset):
    acc = init
    for k in range(CONV_WIDTH):
        acc = acc + w_ref[k:k + 1, :] * src_ref[pl.ds(first_row + offset(k), n_rows), :]
    return acc


def _norm_act(conv, g, b):
    return jax.nn.silu(_layernorm(conv, g, b))


def _conv_fwd(p, cat, conv_w, conv_b, norm_g, norm_b, batch, seq):
    rc = _tile(seq, 256, 8)
    a_blk, gate_blk, out_blk = P_A // LANES, P_GATE // LANES, (HEADS * V_HEAD) // LANES

    def body(a_ref, gate_ref, w_ref, cb_ref, ng_ref, nb_ref, cat_in, o_ref, zp):
        del cat_in
        zp[pl.ds(0, CONV_PAD), :] = jnp.zeros((CONV_PAD, LANES), F32)
        zp[pl.ds(CONV_PAD, seq), :] = a_ref[...] * jax.nn.sigmoid(gate_ref[...])
        for r0 in range(0, seq, rc):
            conv = _conv_taps(w_ref, zp, r0, rc, jnp.broadcast_to(cb_ref[...], (rc, LANES)),
                              lambda k: CONV_PAD - (CONV_WIDTH - 1) + k)
            o_ref[pl.ds(r0, rc), :] = _norm_act(conv, ng_ref[...], nb_ref[...]).astype(o_ref.dtype)

    vec = _spec((1, LANES), lambda b, g: (0, g))
    return _call("conv_fwd", body, (batch, GROUPS),
                 [_spec((seq, LANES), lambda b, g: (b, a_blk + g)), _spec((seq, LANES), lambda b, g: (b, gate_blk + g)),
                  _spec((CONV_WIDTH, LANES), lambda b, g: (0, g)), vec, vec, vec, pl.BlockSpec(memory_space=pl.ANY)],
                 _spec((seq, LANES), lambda b, g: (b, out_blk + g)), _sds(cat.shape, cat.dtype),
                 scratch=[pltpu.VMEM((seq + CONV_PAD, LANES), F32)], aliases={6: 0}
                 )(p, p, conv_w, conv_b, norm_g, norm_b, cat)


def _conv_bwd(p, d_cat, conv_w, conv_b, norm_g, norm_b, batch, seq):
    t = p.shape[0]
    rc = _tile(seq, 256, 8)
    a_blk, gate_blk, out_blk = P_A // LANES, P_GATE // LANES, (HEADS * V_HEAD) // LANES

    def body(a_ref, gate_ref, w_ref, cb_ref, ng_ref, nb_ref, dc_ref, da_ref, dgate_ref, dw_ref, db_ref, dng_ref, dnb_ref,
             zp, dcp, buf):
        b = pl.program_id(1)
        a, sig = a_ref[...], jax.nn.sigmoid(gate_ref[...])
        zp[pl.ds(0, CONV_PAD), :] = jnp.zeros((CONV_PAD, LANES), F32)
        zp[pl.ds(CONV_PAD, seq), :] = a * sig
        for r0 in range(0, seq, rc):
            buf[pl.ds(r0, rc), :] = _conv_taps(w_ref, zp, r0, rc, jnp.broadcast_to(cb_ref[...], (rc, LANES)),
                                               lambda k: CONV_PAD - (CONV_WIDTH - 1) + k)
        _, vjp = jax.vjp(_norm_act, buf[...], ng_ref[...], nb_ref[...])
        dconv, dng, dnb = vjp(dc_ref[...].astype(F32))
        dcp[pl.ds(0, seq), :] = dconv
        dcp[pl.ds(seq, CONV_PAD), :] = jnp.zeros((CONV_PAD, LANES), F32)

        @pl.when(b == 0)
        def _():
            dw_ref[...] = jnp.zeros(dw_ref.shape, F32)
            db_ref[...] = jnp.zeros(db_ref.shape, F32)
            dng_ref[...] = jnp.zeros(dng_ref.shape, F32)
            dnb_ref[...] = jnp.zeros(dnb_ref.shape, F32)

        db_ref[...] += jnp.sum(dconv, axis=0, keepdims=True)
        dng_ref[...] += dng
        dnb_ref[...] += dnb
        for k in range(CONV_WIDTH):
            shifted = zp[pl.ds(CONV_PAD - (CONV_WIDTH - 1) + k, seq), :]
            dw_ref[k:k + 1, :] += jnp.sum(dconv * shifted, axis=0, keepdims=True)
        for r0 in range(0, seq, rc):
            buf[pl.ds(r0, rc), :] = _conv_taps(w_ref, dcp, r0, rc, jnp.zeros((rc, LANES), F32),
                                               lambda k: CONV_WIDTH - 1 - k)
        dz = buf[...]
        da_ref[...] = (dz * sig).astype(da_ref.dtype)
        dgate_ref[...] = (dz * a * sig * (1.0 - sig)).astype(dgate_ref.dtype)

    vec = _spec((1, LANES), lambda g, b: (0, g))
    row_out = _spec((seq, LANES), lambda g, b: (b, g))
    return _call("conv_bwd", body, (GROUPS, batch),
                 [_spec((seq, LANES), lambda g, b: (b, a_blk + g)), _spec((seq, LANES), lambda g, b: (b, gate_blk + g)),
                  _spec((CONV_WIDTH, LANES), lambda g, b: (0, g)), vec, vec, vec,
                  _spec((seq, LANES), lambda g, b: (b, out_blk + g))],
                 [row_out, row_out, _spec((CONV_WIDTH, LANES), lambda g, b: (0, g)), vec, vec, vec],
                 [_sds((t, CONV_CH), MXU_DTYPE), _sds((t, CONV_CH), MXU_DTYPE), _sds((CONV_WIDTH, CONV_CH), F32),
                  _sds((1, CONV_CH), F32), _sds((1, CONV_CH), F32), _sds((1, CONV_CH), F32)],
                 scratch=[pltpu.VMEM((seq + CONV_PAD, LANES), F32), pltpu.VMEM((seq + CONV_PAD, LANES), F32),
                          pltpu.VMEM((seq, LANES), F32)])(p, p, conv_w, conv_b, norm_g, norm_b, d_cat)


def _sgu_pre(hw, g, b):
    z = jax.nn.gelu(hw)
    return z[:, :GM_WIDTH], _layernorm(z[:, GM_WIDTH:], g, b)


def _causal(w):
    keep = lax.broadcasted_iota(jnp.int32, (CHUNK, CHUNK), 0) >= lax.broadcasted_iota(jnp.int32, (CHUNK, CHUNK), 1)
    return jnp.where(keep, w, 0.0)


def _sgu_fwd(hw, vg, vb, w_s, b_st):
    t = hw.shape[0]
    gw = GM_WIDTH // GROUPS

    def body(hw_ref, vg_ref, vb_ref, w_ref, b_ref, o_ref):
        u, v = _sgu_pre(hw_ref[...], vg_ref[...], vb_ref[...])
        for g in range(GROUPS):
            cols = slice(g * gw, (g + 1) * gw)
            s = _dot(_causal(w_ref[g]), v[:, cols]) + b_ref[:, g:g + 1]
            o_ref[:, cols] = (u[:, cols] * s).astype(o_ref.dtype)

    return _call("sgu_fwd", body, (t // CHUNK,),
                 [_spec((CHUNK, 2 * GM_WIDTH), lambda i: (i, 0)), _spec((1, GM_WIDTH), lambda i: (0, 0)),
                  _spec((1, GM_WIDTH), lambda i: (0, 0)), _spec((GROUPS, CHUNK, CHUNK), lambda i: (0, 0, 0)),
                  _spec((CHUNK, GROUPS), lambda i: (0, 0))],
                 _spec((CHUNK, GM_WIDTH), lambda i: (i, 0)), _sds((t, GM_WIDTH), MXU_DTYPE))(hw, vg, vb, w_s, b_st)


def _sgu_bwd(hw, vg, vb, w_s, b_st, dout):
    t = hw.shape[0]
    gw = GM_WIDTH // GROUPS

    def body(hw_ref, vg_ref, vb_ref, w_ref, b_ref, do_ref, dhw_ref, dvg_ref, dvb_ref, dw_ref, db_ref, du_buf, dv_buf):
        (u, v), vjp = jax.vjp(_sgu_pre, hw_ref[...], vg_ref[...], vb_ref[...])

        @pl.when(pl.program_id(0) == 0)
        def _():
            dvg_ref[...] = jnp.zeros(dvg_ref.shape, F32)
            dvb_ref[...] = jnp.zeros(dvb_ref.shape, F32)
            dw_ref[...] = jnp.zeros(dw_ref.shape, F32)
            db_ref[...] = jnp.zeros(db_ref.shape, F32)

        lane = lax.broadcasted_iota(jnp.int32, (CHUNK, LANES), 1)
        db = jnp.zeros((CHUNK, LANES), F32)
        for g in range(GROUPS):
            cols = slice(g * gw, (g + 1) * gw)
            w = _causal(w_ref[g])
            s = _dot(w, v[:, cols]) + b_ref[:, g:g + 1]
            do = do_ref[:, cols]
            ds = do * u[:, cols]
            du_buf[:, cols] = do * s
            dw_ref[g] += _causal(_dot(ds, v[:, cols], "nt"))
            dv_buf[:, cols] = _dot(w, ds, "tn")
            db = db + jnp.where(lane == g, jnp.sum(ds, axis=1, keepdims=True), 0.0)
        db_ref[...] += db
        dhw, dvg, dvb = vjp((du_buf[...], dv_buf[...]))
        dhw_ref[...] = dhw.astype(dhw_ref.dtype)
        dvg_ref[...] += dvg
        dvb_ref[...] += dvb

    vec = _spec((1, GM_WIDTH), lambda i: (0, 0))
    w_spec = _spec((GROUPS, CHUNK, CHUNK), lambda i: (0, 0, 0))
    return _call("sgu_bwd", body, (t // CHUNK,),
                 [_spec((CHUNK, 2 * GM_WIDTH), lambda i: (i, 0)), vec, vec, w_spec, _spec((CHUNK, GROUPS), lambda i: (0, 0)),
                  _spec((CHUNK, GM_WIDTH), lambda i: (i, 0))],
                 [_spec((CHUNK, 2 * GM_WIDTH), lambda i: (i, 0)), vec, vec, w_spec, _spec((CHUNK, LANES), lambda i: (0, 0))],
                 [_sds((t, 2 * GM_WIDTH), MXU_DTYPE), _sds((1, GM_WIDTH), F32), _sds((1, GM_WIDTH), F32),
                  _sds((GROUPS, CHUNK, CHUNK), F32), _sds((CHUNK, LANES), F32)],
                 scratch=[pltpu.VMEM((CHUNK, GM_WIDTH), F32), pltpu.VMEM((CHUNK, GM_WIDTH), F32)]
                 )(hw, vg, vb, w_s, b_st, dout)


def _mm_nn(name, a, b, out_dtype, tm, a_col0=0, k_width=None, tn_want=1024, tk_want=1024):
    t = a.shape[0]
    kk, n = b.shape
    tk = _tile(kk, tk_want, LANES)
    tn = _tile(n, tn_want, LANES)
    k0 = a_col0 // tk
    assert a_col0 % tk == 0
    return _matmul(name, "nn", (t // tm, n // tn, kk // tk),
                   [[(a, _spec((tm, tk), lambda i, j, k: (i, k0 + k)), b, _spec((tk, tn), lambda i, j, k: (k, j)))]],
                   [((t, n), out_dtype, _spec((tm, tn), lambda i, j, k: (i, j)))], (tm, tn))[0]


def _mm_nt(name, pairs, out_dtype, tm, tn_want=1024, tk_want=1024):
    t, kk = pairs[0][0].shape
    n = pairs[0][1].shape[0]
    tk = _tile(kk, tk_want, LANES)
    tn = _tile(n, tn_want, 16)
    grp = []
    for a, b, b_col0 in pairs:
        assert b_col0 % tk == 0 and a.shape == (t, kk)
        grp.append((a, _spec((tm, tk), lambda i, j, k: (i, k)), b,
                    _spec((tn, tk), functools.partial(lambda i, j, k, k0: (j, k0 + k), k0=b_col0 // tk))))
    return _matmul(name, "nt", (t // tm, n // tn, kk // tk), [grp],
                   [((t, n), out_dtype, _spec((tm, tn), lambda i, j, k: (i, j)))], (tm, tn))[0]


def _mm_tn(name, a, bs, out_dtype, a_col0=0, m_width=None, tm_want=512, tn_want=1024):
    t = a.shape[0]
    m = m_width or a.shape[1]
    n = bs[0].shape[1]
    tmw = _tile(m, tm_want, LANES)
    tn = _tile(n, tn_want, LANES)
    tkt = _tile(t, 512, 16)
    assert a_col0 % tmw == 0
    i0 = a_col0 // tmw
    a_spec = _spec((tkt, tmw), lambda i, j, k: (k, i0 + i))
    groups = [[(a, a_spec, b, _spec((tkt, tn), lambda i, j, k: (k, j)))] for b in bs]
    return _matmul(name, "tn", (m // tmw, n // tn, t // tkt), groups,
                   [((m, n), out_dtype, _spec((tmw, tn), lambda i, j, k: (i, j)))] * len(bs), (tmw, tn))


_ANY = pl.BlockSpec(memory_space=pl.ANY)


def _comm_call(name, body, ins, out_shapes, sems, aliases=None):
    return pl.pallas_call(body, name=name, out_shape=out_shapes, in_specs=[_ANY] * len(ins),
                          out_specs=[_ANY] * len(out_shapes), input_output_aliases=aliases or {},
                          scratch_shapes=[pltpu.SemaphoreType.DMA((n,)) for n in sems])(*ins)


def _place():
    return lax.axis_index("x"), lax.axis_index("y"), lax.axis_index("c")


def _other_chips(x, y):
    return [(1 - x, y), (x, 1 - y), (1 - x, 1 - y)]


def _cast_into_slot(w, place, dtype):
    layers, rows, cols = w.shape
    half = rows // 2
    tr = _tile(half, 256, 16)
    nt = half // tr

    def body(s, w_ref, o_ref):
        del s
        o_ref[...] = w_ref[...].astype(o_ref.dtype)

    return _call("cast_into_slot", body, (layers, 2, nt),
                 [_spec((None, tr, cols), lambda l, h, i, s: (l, h * nt + i, 0))],
                 _spec((None, None, None, tr, cols), lambda l, h, i, s: (s[0], l, h, i, 0)),
                 _sds((N_CHIPS, layers, 2, half, cols), dtype), prefetch=1)(place, w)


def _gather_chips(bufs):
    n = len(bufs)

    def body(*refs):
        buf = refs[n:2 * n]
        send_sems, recv_sems, pass_send_sems, pass_recv_sems = refs[2 * n:]
        x, y, c = _place()
        me = 2 * x + y
        chips = _other_chips(x, y)
        sent = [[pltpu.make_async_remote_copy(
            src_ref=buf[t].at[me, :, c], dst_ref=buf[t].at[me, :, c], send_sem=send_sems.at[3 * t + r],
            recv_sem=recv_sems.at[3 * t + r], device_id=(px, py, c), device_id_type=MESH)
            for r, (px, py) in enumerate(chips)] for t in range(n)]
        for row in sent:
            for cp in row:
                cp.start()
        passed = []
        for t in range(n):
            for r, (px, py) in enumerate(chips):
                landed = buf[t].at[2 * px + py, :, c]
                sent[t][r].wait_recv()
                cp = pltpu.make_async_remote_copy(
                    src_ref=landed, dst_ref=landed, send_sem=pass_send_sems.at[3 * t + r],
                    recv_sem=pass_recv_sems.at[3 * t + r], device_id=(x, y, 1 - c), device_id_type=MESH)
                cp.start()
                passed.append(cp)
        for cp in passed:
            cp.wait()
        for row in sent:
            for cp in row:
                cp.wait_send()

    return _comm_call("gather_chips", body, bufs, [_sds(b.shape, b.dtype) for b in bufs], [3 * n] * 4,
                      aliases={t: t for t in range(n)})


def _swap_halves(grads):
    n = len(grads)

    def body(*refs):
        src, dst = refs[:n], refs[n:2 * n]
        send_sems, recv_sems = refs[2 * n:]
        x, y, c = _place()
        copies = [pltpu.make_async_remote_copy(
            src_ref=src[t].at[:, 1 - c], dst_ref=dst[t], send_sem=send_sems.at[t], recv_sem=recv_sems.at[t],
            device_id=(x, y, 1 - c), device_id_type=MESH) for t in range(n)]
        for cp in copies:
            cp.start()
        for cp in copies:
            cp.wait()

    return _comm_call("swap_halves", body, grads, [_sds((g.shape[0],) + g.shape[2:], g.dtype) for g in grads], [n, n])


def _scatter_chips(partials):
    n = len(partials)

    def body(*refs):
        src, dst = refs[:n], refs[n:2 * n]
        send_sems, recv_sems = refs[2 * n:]
        x, y, c = _place()
        copies = []
        for t in range(n):
            for r, (px, py) in enumerate(_other_chips(x, y)):
                copies.append(pltpu.make_async_remote_copy(
                    src_ref=src[t].at[2 * px + py], dst_ref=dst[t].at[r], send_sem=send_sems.at[3 * t + r],
                    recv_sem=recv_sems.at[3 * t + r], device_id=(px, py, c), device_id_type=MESH))
        for cp in copies:
            cp.start()
        for cp in copies:
            cp.wait()

    return _comm_call("scatter_chips", body, partials, [_sds((3,) + p.shape[1:], p.dtype) for p in partials], [3 * n, 3 * n])


def _join_halves(grads):
    n = len(grads)

    def body(*refs):
        buf = refs[n:2 * n]
        send_sems, recv_sems = refs[2 * n:]
        x, y, c = _place()
        copies = [pltpu.make_async_remote_copy(
            src_ref=buf[t].at[:, c], dst_ref=buf[t].at[:, c], send_sem=send_sems.at[t], recv_sem=recv_sems.at[t],
            device_id=(x, y, 1 - c), device_id_type=MESH) for t in range(n)]
        for cp in copies:
            cp.start()
        for cp in copies:
            cp.wait()

    return _comm_call("join_halves", body, grads, [_sds(g.shape, g.dtype) for g in grads], [n, n],
                      aliases={t: t for t in range(n)})


def _gather_devices(slots):
    def body(src, buf, send_sems, recv_sems):
        del src
        x, y, c = _place()
        me = 4 * x + 2 * y + c
        copies = []
        for r in range(1, N_DEV):
            fx, fy, fc = (r >> 2) & 1, (r >> 1) & 1, r & 1
            peer = (1 - x if fx else x, 1 - y if fy else y, 1 - c if fc else c)
            copies.append(pltpu.make_async_remote_copy(
                src_ref=buf.at[me], dst_ref=buf.at[me], send_sem=send_sems.at[r - 1], recv_sem=recv_sems.at[r - 1],
                device_id=peer, device_id_type=MESH))
        for cp in copies:
            cp.start()
        for cp in copies:
            cp.wait()

    return _comm_call("gather_devices", body, [slots], [_sds(slots.shape, slots.dtype)], [N_DEV - 1, N_DEV - 1],
                      aliases={0: 0})[0]


def _add_halves(grad, got, place):
    _, _, rows, cols = grad.shape
    tr = _tile(rows, 256, 16)

    def body(s, a, b, o):
        del s
        o[...] = (a[...].astype(F32) + b[...].astype(F32)).astype(o.dtype)

    return _call("add_halves", body, (N_CHIPS, rows // tr),
                 [_spec((None, None, tr, cols), lambda j, i, s: (j, s[1], i, 0)), _spec((None, tr, cols), lambda j, i, s: (j, i, 0))],
                 _spec((None, tr, cols), lambda j, i, s: (j, i, 0)), _sds((N_CHIPS, rows, cols), grad.dtype), prefetch=1
                 )(place, grad, got)


def _add_chips(partial, got, place, layer, layers, into=None):
    _, rows, cols = partial.shape
    tr = _tile(rows, 256, 16)

    def body(s, a, b, *rest):
        o = rest[-1]
        o[...] = a[...].astype(F32) + b[0].astype(F32) + b[1].astype(F32) + b[2].astype(F32)

    in_specs = [_spec((None, tr, cols), lambda i, s: (s[0], i, 0)), _spec((3, tr, cols), lambda i, s: (0, i, 0))]
    args = [place, partial, got]
    if into is not None:
        in_specs.append(pl.BlockSpec(memory_space=pl.ANY))
        args.append(into)
    return _call("add_chips", body, (rows // tr,), in_specs,
                 _spec((None, None, tr, cols), lambda i, s: (layer, s[1], i, 0)), _sds((layers, 2, rows, cols), F32),
                 prefetch=1, aliases={3: 0} if into is not None else None)(*args)


def _add_devices(got):
    _, rows, cols = got.shape
    tr = _tile(rows, 512, 8)

    def body(a, o):
        total = a[0]
        for d in range(1, N_DEV):
            total = total + a[d]
        o[...] = total

    return _call("add_devices", body, (rows // tr,), [_spec((N_DEV, tr, cols), lambda i: (0, i, 0))],
                 _spec((tr, cols), lambda i: (i, 0)), _sds((rows, cols), F32))(got)


def _adamw(w, g, m, v):
    layers, rows, cols = w.shape
    tr = _tile(rows, 256, 8)

    def body(w_ref, g_ref, m_ref, v_ref, d_ref, nm_ref, nv_ref):
        grad = g_ref[...]
        new_m = ADAM_B1 * m_ref[...] + (1.0 - ADAM_B1) * grad
        new_v = ADAM_B2 * v_ref[...] + (1.0 - ADAM_B2) * jnp.square(grad)
        m_hat = new_m / (1.0 - ADAM_B1 ** ADAM_STEP)
        v_hat = new_v / (1.0 - ADAM_B2 ** ADAM_STEP)
        d_ref[...] = -ADAM_LR * (m_hat / (jnp.sqrt(v_hat) + ADAM_EPS) + ADAM_WD * w_ref[...])
        nm_ref[...] = new_m
        nv_ref[...] = new_v

    blk = _spec((None, tr, cols), lambda l, i: (l, i, 0))
    return _call("adamw", body, (layers, rows // tr), [blk] * 4, [blk] * 3, [_sds(w.shape, F32)] * 3)(w, g, m, v)


def _rope_layout(w):
    z = jnp.zeros(w.shape[:-1] + (ROPE_HALF,), w.dtype)
    return jnp.concatenate([w[..., :ROPE_HALF], z, w[..., ROPE_HALF:], z], axis=-1)


def _rope_layout_inv(w):
    return jnp.concatenate([w[..., :ROPE_HALF], w[..., 2 * ROPE_HALF:3 * ROPE_HALF]], axis=-1)


def _cols_from_chips(g):
    return jnp.moveaxis(g, 0, 1).reshape(g.shape[1], N_CHIPS * g.shape[2])


def _cols_to_chips(w):
    return jnp.moveaxis(w.reshape(w.shape[0], N_CHIPS, w.shape[1] // N_CHIPS), 1, 0)


def _w_in_layout(w):
    off_kr = Q_LORA + KV_LORA
    return jnp.concatenate([w[:, :off_kr], w[:, off_kr + QK_ROPE:], _rope_layout(w[:, off_kr:off_kr + QK_ROPE])], axis=1)


def _w_in_layout_inv(dw):
    return jnp.concatenate([dw[:, :P_A], _rope_layout_inv(dw[:, P_KR:]), dw[:, P_A:P_KR]], axis=1)


def _w_uq_layout(w):
    w = w.reshape(Q_LORA, HEADS, QK_NOPE + QK_ROPE)
    return jnp.concatenate([w[..., :QK_NOPE].reshape(Q_LORA, HEADS * QK_NOPE),
                            _rope_layout(w[..., QK_NOPE:]).reshape(Q_LORA, HEADS * LANES)], axis=1)


def _w_uq_layout_inv(d_nope, d_rope):
    d_nope = d_nope.reshape(Q_LORA, HEADS, QK_NOPE)
    d_rope = _rope_layout_inv(d_rope.reshape(Q_LORA, HEADS, LANES))
    return jnp.concatenate([d_nope, d_rope], axis=-1).reshape(Q_LORA, HEADS * (QK_NOPE + QK_ROPE))


def _to_lanes(a):
    flat = a.reshape(-1)
    pad = (-flat.shape[0]) % LANES
    if pad:
        flat = jnp.concatenate([flat, jnp.zeros((pad,), flat.dtype)])
    return flat.reshape(-1, LANES)


PACK_ROWS = 64


def _pack(arrays):
    parts, ranges, row = [], [], 0
    for a in arrays:
        p = _to_lanes(a)
        parts.append(p)
        ranges.append((row, row + p.shape[0]))
        row += p.shape[0]
    pad = (-row) % PACK_ROWS
    if pad:
        parts.append(jnp.zeros((pad, LANES), F32))
    return jnp.concatenate(parts, axis=0), ranges


def _unpack(packed, rng, shape):
    n = 1
    for s in shape:
        n *= s
    return packed[rng[0]:rng[1]].reshape(-1)[:n].reshape(shape)


def kernel(x, positions, ffn_a_pre_g, ffn_a_post_g, ffn_a_w_gate, ffn_a_w_up, ffn_a_w_down, ffn_b_pre_g, ffn_b_post_g, ffn_b_w_gate, ffn_b_w_up, ffn_b_w_down, even_pre_g, even_post_g, even_w_in, even_q_norm_g, even_kv_norm_g, even_w_uq, even_w_ukv, even_conv_w, even_conv_b, even_conv_norm_g, even_conv_norm_b, even_w_out, odd_pre_g, odd_post_g, odd_w_in, odd_v_norm_g, odd_v_norm_b, odd_w_s, odd_b_s, odd_w_out, loss_target, m_ffn_a_pre_g, m_ffn_a_post_g, m_ffn_a_w_gate, m_ffn_a_w_up, m_ffn_a_w_down, m_ffn_b_pre_g, m_ffn_b_post_g, m_ffn_b_w_gate, m_ffn_b_w_up, m_ffn_b_w_down, m_even_pre_g, m_even_post_g, m_even_w_in, m_even_q_norm_g, m_even_kv_norm_g, m_even_w_uq, m_even_w_ukv, m_even_conv_w, m_even_conv_b, m_even_conv_norm_g, m_even_conv_norm_b, m_even_w_out, m_odd_pre_g, m_odd_post_g, m_odd_w_in, m_odd_v_norm_g, m_odd_v_norm_b, m_odd_w_s, m_odd_b_s, m_odd_w_out, v_ffn_a_pre_g, v_ffn_a_post_g, v_ffn_a_w_gate, v_ffn_a_w_up, v_ffn_a_w_down, v_ffn_b_pre_g, v_ffn_b_post_g, v_ffn_b_w_gate, v_ffn_b_w_up, v_ffn_b_w_down, v_even_pre_g, v_even_post_g, v_even_w_in, v_even_q_norm_g, v_even_kv_norm_g, v_even_w_uq, v_even_w_ukv, v_even_conv_w, v_even_conv_b, v_even_conv_norm_g, v_even_conv_norm_b, v_even_w_out, v_odd_pre_g, v_odd_post_g, v_odd_w_in, v_odd_v_norm_g, v_odd_v_norm_b, v_odd_w_s, v_odd_b_s, v_odd_w_out):
    given = dict(locals())
    w = {n: given[n] for n in WEIGHTS}
    batch, seq, d = x.shape
    t = batch * seq
    tm = _tile(t, 512, 16)
    tr = _tile(t, 256, 16)
    chip = 2 * lax.axis_index("x") + lax.axis_index("y")
    place = jnp.stack([chip, lax.axis_index("c")]).astype(jnp.int32)

    small_shard, small_shard_rng = _pack([w[n] for n in SMALL_SHARDED])
    gathered = _gather_chips([_cast_into_slot(w[n], place, MXU_DTYPE) for n in BIG]
                             + [_cast_into_slot(small_shard[None], place, F32)])
    wf = {n: g.reshape((N_CHIPS,) + w[n].shape) for n, g in zip(BIG, gathered[:-1])}
    shards = gathered[-1].reshape((N_CHIPS,) + small_shard.shape)
    full = dict(w)
    for n, rng in zip(SMALL_SHARDED, small_shard_rng):
        per_chip = [_unpack(shards[j], rng, w[n].shape) for j in range(N_CHIPS)]
        full[n] = jnp.concatenate(per_chip, axis=-1)
    w_in = _w_in_layout(_cols_from_chips(wf['even_w_in'][:, 0]))
    w_uq = _w_uq_layout(_cols_from_chips(wf['even_w_uq'][:, 0]))
    w_ukv = _cols_from_chips(wf['even_w_ukv'][:, 0])
    w_out_e = wf['even_w_out'][:, 0].reshape(d, d)
    w_out_o = wf['odd_w_out'][:, 0].reshape(GM_WIDTH, d)
    w_in_o = wf['odd_w_in']

    inv_freq = ROPE_THETA ** (-jnp.arange(0, QK_ROPE, 2, dtype=F32) / QK_ROPE)
    ang = positions.astype(F32).reshape(t, 1) * inv_freq
    zeros = jnp.zeros((t, ROPE_HALF), F32)
    cos = jnp.concatenate([jnp.cos(ang), zeros, jnp.cos(ang), zeros], axis=1)
    sin = jnp.concatenate([-jnp.sin(ang), zeros, jnp.sin(ang), zeros], axis=1)

    def ffn_args(tag, layer):
        return (w[f'ffn_{tag}_pre_g'][layer:layer + 1], w[f'ffn_{tag}_post_g'][layer:layer + 1], wf[f'ffn_{tag}_w_gate'],
                wf[f'ffn_{tag}_w_up'], wf[f'ffn_{tag}_w_down'], layer, tm, tr)

    xs = x.reshape(t, d)
    x1, ffn_a0 = _ffn_fwd(xs, *ffn_args('a', 0))
    h_e = _rms_fwd(x1, w['even_pre_g'], tr)
    p = _mm_nn("mm_w_in", h_e, w_in, F32, tm, tn_want=640)
    lat = _rows("latent_norm", lambda r, c: ([jnp.concatenate([_rmsnorm(r[0], c[0]), _rmsnorm(r[1], c[1])], axis=1)], []),
                [(p, Q_LORA, 0), (p, KV_LORA, 1)], [w['even_q_norm_g'], w['even_kv_norm_g']], [(Q_LORA + KV_LORA, MXU_DTYPE)], [], tr)[0]
    q_all = _mm_nn("mm_w_uq", lat, w_uq, F32, tm, a_col0=0, tk_want=Q_LORA)
    kv = _mm_nn("mm_w_ukv", lat, w_ukv, MXU_DTYPE, tm, a_col0=Q_LORA, tk_want=KV_LORA)
    cat = _attn_fwd(q_all, kv, p, cos, sin, batch, seq)
    cat = _conv_fwd(p, cat, full['even_conv_w'][0], w['even_conv_b'], w['even_conv_norm_g'], w['even_conv_norm_b'], batch, seq)
    y_e = _mm_nn("mm_w_out", cat, w_out_e, F32, tm)
    x2 = _rms_residual(x1, y_e, w['even_post_g'], 1.0, tr)
    x3, ffn_b0 = _ffn_fwd(x2, *ffn_args('b', 0))
    x4, ffn_a1 = _ffn_fwd(x3, *ffn_args('a', 1))
    h_o = _rms_fwd(x4, full['odd_pre_g'], tr)
    ns = w_in_o.shape[3]
    tk_o = _tile(d, 1024, LANES)
    (hw,) = _matmul("mm_w_in_odd", "nn", (t // tm, N_CHIPS, d // tk_o),
                    [[(h_o, _spec((tm, tk_o), lambda i, j, k: (i, k)), w_in_o, _spec((None, None, tk_o, ns), lambda i, j, k: (j, 0, k, 0)))]],
                    [((t, N_CHIPS * ns), F32, _spec((tm, ns), lambda i, j, k: (i, j)))], (tm, ns))
    b_st = w['odd_b_s'][0].T
    sg = _sgu_fwd(hw, full['odd_v_norm_g'], full['odd_v_norm_b'], w['odd_w_s'][0], b_st)
    y_o = _mm_nn("mm_w_out", sg, w_out_o, F32, tm)
    x5 = _rms_residual(x4, y_o, full['odd_post_g'], 1.0, tr)
    x6, ffn_b1 = _ffn_fwd(x5, *ffn_args('b', 1))

    dy, sq_err = _loss_head(x6, loss_target.reshape(t, d), tr)
    loss = lax.psum(0.5 * sq_err[0, 0] / d, ("x", "y", "c"))

    grads = {}
    dx, (gpre, gpost, gwg, gwu, gwd) = _ffn_bwd(dy, ffn_b1, *ffn_args('b', 1))
    fg = {('b', 1): (gpre, gpost, gwg, gwu, gwd)}
    dy_o, grads['odd_post_g'] = _rms_bwd(y_o, full['odd_post_g'], dx, 1.0, MXU_DTYPE, tr)
    dsg = _mm_nt("mm_dsg", [(dy_o, w_out_o, 0)], F32, tm)
    (g_w_out_o,) = _mm_tn("mm_dw_out", sg, [dy_o], MXU_DTYPE)
    dhw, grads['odd_v_norm_g'], grads['odd_v_norm_b'], g_ws, g_bst = _sgu_bwd(
        hw, full['odd_v_norm_g'], full['odd_v_norm_b'], w['odd_w_s'][0], b_st, dsg)
    grads['odd_w_s'] = g_ws[None]
    grads['odd_b_s'] = g_bst[:, :GROUPS].T[None]
    tn_o = _tile(d, 1024, LANES)
    (dh_o,) = _matmul("mm_dh_odd", "nt", (t // tm, d // tn_o, N_CHIPS),
                      [[(dhw, _spec((tm, ns), lambda i, j, k: (i, k)), w_in_o, _spec((None, None, tn_o, ns), lambda i, j, k: (k, 0, j, 0)))]],
                      [((t, d), F32, _spec((tm, tn_o), lambda i, j, k: (i, j)))], (tm, tn_o))
    tmw = _tile(d, 512, LANES)
    tkt = _tile(t, 512, 16)
    (g_w_in_o,) = _matmul("mm_dw_in_odd", "tn", (d // tmw, N_CHIPS, t // tkt),
                          [[(h_o, _spec((tkt, tmw), lambda i, j, k: (k, i)), dhw, _spec((tkt, ns), lambda i, j, k: (k, j)))]],
                          [((N_CHIPS, d, ns), MXU_DTYPE, _spec((None, tmw, ns), lambda i, j, k: (j, i, 0)))], (tmw, ns))
    dx, grads['odd_pre_g'] = _rms_bwd(x4, full['odd_pre_g'], dh_o, 1.0, F32, tr, resid=dx)
    dx, fg[('a', 1)] = _ffn_bwd(dx, ffn_a1, *ffn_args('a', 1))
    dx, fg[('b', 0)] = _ffn_bwd(dx, ffn_b0, *ffn_args('b', 0))
    dy_e, grads['even_post_g'] = _rms_bwd(y_e, w['even_post_g'], dx, 1.0, MXU_DTYPE, tr)
    d_cat = _mm_nt("mm_dcat", [(dy_e, w_out_e, 0)], F32, tm)
    (g_w_out_e,) = _mm_tn("mm_dw_out", cat, [dy_e], MXU_DTYPE)
    dqn, dqp, dkv, dkr = _attn_bwd(q_all, kv, p, cos, sin, d_cat, batch, seq)
    hq = HEADS * LANES
    d_latq = _mm_nt("mm_dlat_q", [(dqn, w_uq, 0), (dqp, w_uq, hq)], F32, tm, tn_want=Q_LORA)
    d_latkv = _mm_nt("mm_dlat_kv", [(dkv, w_ukv, 0)], F32, tm, tn_want=KV_LORA)
    g_uq_n, g_uq_r = _mm_tn("mm_dw_uq", lat, [dqn, dqp], MXU_DTYPE, a_col0=0, m_width=Q_LORA)
    (g_ukv,) = _mm_tn("mm_dw_ukv", lat, [dkv], MXU_DTYPE, a_col0=Q_LORA, m_width=KV_LORA)

    def latent_bwd(r, c):
        _, vjp_q = jax.vjp(_rmsnorm, r[0], c[0])
        _, vjp_kv = jax.vjp(_rmsnorm, r[1], c[1])
        dq, dqg = vjp_q(r[2])
        dk, dkg = vjp_kv(r[3])
        return [jnp.concatenate([dq, dk], axis=1)], [dqg, dkg]

    d_lat, grads['even_q_norm_g'], grads['even_kv_norm_g'] = _rows(
        "latent_norm_bwd", latent_bwd, [(p, Q_LORA, 0), (p, KV_LORA, 1), (d_latq, Q_LORA, 0), (d_latkv, KV_LORA, 0)],
        [w['even_q_norm_g'], w['even_kv_norm_g']], [(Q_LORA + KV_LORA, MXU_DTYPE)], [(1, Q_LORA), (1, KV_LORA)], tr)
    da, dgate, g_conv_w, grads['even_conv_b'], grads['even_conv_norm_g'], grads['even_conv_norm_b'] = _conv_bwd(
        p, d_cat, full['even_conv_w'][0], w['even_conv_b'], w['even_conv_norm_g'], w['even_conv_norm_b'], batch, seq)
    dp = jnp.concatenate([d_lat, da, dgate, dkr.astype(MXU_DTYPE)], axis=1)
    dh_e = _mm_nt("mm_dh_even", [(dp, w_in, 0)], F32, tm, tk_want=640)
    (g_w_in,) = _mm_tn("mm_dw_in", h_e, [dp], MXU_DTYPE, tn_want=640)
    dx, grads['even_pre_g'] = _rms_bwd(x1, w['even_pre_g'], dh_e, 1.0, F32, tr, resid=dx)
    dx, fg[('a', 0)] = _ffn_bwd(dx, ffn_a0, *ffn_args('a', 0))
    grad_x = dx.reshape(batch, seq, d)

    items, layout_where, layout_shapes = [], [], []
    for o, n in enumerate(BIG):
        layers, rows, cols = w[n].shape
        layout_shapes.append((layers, 2, rows // 2, cols))
        for layer in range(layers):
            if n.startswith('ffn_'):
                tag = n[4]
                it = fg[(tag, layer)][2 + ['w_gate', 'w_up', 'w_down'].index(n[6:])]
            else:
                it = {'even_w_in': lambda: _cols_to_chips(_w_in_layout_inv(g_w_in)),
                      'even_w_uq': lambda: _cols_to_chips(_w_uq_layout_inv(g_uq_n, g_uq_r)),
                      'even_w_ukv': lambda: _cols_to_chips(g_ukv),
                      'even_w_out': lambda: g_w_out_e.reshape(N_CHIPS, rows, cols),
                      'odd_w_in': lambda: g_w_in_o,
                      'odd_w_out': lambda: g_w_out_o.reshape(N_CHIPS, rows, cols)}[n]()
            items.append(it.reshape(N_CHIPS, 2, rows // 2, cols))
            layout_where.append((o, layer))
    got = _swap_halves(items)
    partials = [_add_halves(g, r, place) for g, r in zip(items, got)]
    got = _scatter_chips(partials)
    reduced = [None] * len(BIG)
    for pt, r, (o, layer) in zip(partials, got, layout_where):
        reduced[o] = _add_chips(pt, r, place, layer, layout_shapes[o][0], into=reduced[o])
    joined = _join_halves(reduced)
    for n, g in zip(BIG, joined):
        grads[n] = g.reshape(w[n].shape)

    for tag in ('a', 'b'):
        grads[f'ffn_{tag}_pre_g'] = jnp.concatenate([fg[(tag, 0)][0], fg[(tag, 1)][0]], axis=0)
        grads[f'ffn_{tag}_post_g'] = jnp.concatenate([fg[(tag, 0)][1], fg[(tag, 1)][1]], axis=0)
    grads['even_conv_w'] = g_conv_w[None]
    packed, rngs = _pack([grads[n] for n in SMALL])
    device = 2 * chip + lax.axis_index("c")
    slots = lax.dynamic_update_slice_in_dim(jnp.zeros((N_DEV,) + packed.shape, F32), packed[None], device, axis=0)
    summed = _add_devices(_gather_devices(slots))
    for n, rng in zip(SMALL, rngs):
        g = _unpack(summed, rng, grads[n].shape)
        if n in SMALL_SHARDED:
            width = w[n].shape[-1]
            g = lax.dynamic_slice_in_dim(g, chip * width, width, axis=g.ndim - 1)
        grads[n] = g

    delta, new_m, new_v = {}, {}, {}
    for n in BIG:
        delta[n], new_m[n], new_v[n] = _adamw(w[n], grads[n], given['m_' + n], given['v_' + n])
    packs = [_pack([src[n] for n in SMALL])[0][None] for src in
             (w, grads, {n: given['m_' + n] for n in SMALL}, {n: given['v_' + n] for n in SMALL})]
    _, rngs = _pack([w[n] for n in SMALL])
    small_out = _adamw(*packs)
    for n, rng in zip(SMALL, rngs):
        delta[n], new_m[n], new_v[n] = (_unpack(o[0], rng, w[n].shape) for o in small_out)

    return (loss, grad_x, *[grads[n] for n in WEIGHTS], *[delta[n] for n in WEIGHTS],
            *[new_m[n] for n in WEIGHTS], *[new_v[n] for n in WEIGHTS])
```

```python
import functools

import jax
import jax.numpy as jnp
from jax import lax
from jax.experimental import pallas as pl
from jax.experimental.pallas import tpu as pltpu

F32 = jnp.float32
BF16 = jnp.bfloat16
MXU_DTYPE = BF16
MESH = pl.DeviceIdType.MESH
VMEM_LIMIT_BYTES = 56 * 1024 * 1024
LANES = 128

D_MODEL = 2048
D_FF = 5632
EPS = 1e-6
HEADS = 8
V_HEAD = 128
QK_NOPE = 128
QK_ROPE = 64
Q_LORA = 512
KV_LORA = 512
ROPE_THETA = 10000.0
CONV_CH = 1024
CONV_WIDTH = 31
GROUPS = 8
CHUNK = 128
GM_WIDTH = 2048
ADAM_LR = 0.001
ADAM_B1 = 0.9
ADAM_B2 = 0.999
ADAM_EPS = 1e-08
ADAM_WD = 0.01
ADAM_STEP = 10

N_CHIPS = 4
N_DEV = 8
P_KV = Q_LORA
P_A = Q_LORA + KV_LORA
P_GATE = P_A + CONV_CH
P_KR = P_GATE + CONV_CH
P_WIDTH = P_KR + LANES
ROPE_HALF = QK_ROPE // 2

WEIGHTS = ['ffn_a_pre_g', 'ffn_a_post_g', 'ffn_a_w_gate', 'ffn_a_w_up', 'ffn_a_w_down', 'ffn_b_pre_g', 'ffn_b_post_g',
           'ffn_b_w_gate', 'ffn_b_w_up', 'ffn_b_w_down', 'even_pre_g', 'even_post_g', 'even_w_in', 'even_q_norm_g',
           'even_kv_norm_g', 'even_w_uq', 'even_w_ukv', 'even_conv_w', 'even_conv_b', 'even_conv_norm_g',
           'even_conv_norm_b', 'even_w_out', 'odd_pre_g', 'odd_post_g', 'odd_w_in', 'odd_v_norm_g', 'odd_v_norm_b',
           'odd_w_s', 'odd_b_s', 'odd_w_out']
BIG = ['ffn_a_w_gate', 'ffn_a_w_up', 'ffn_a_w_down', 'ffn_b_w_gate', 'ffn_b_w_up', 'ffn_b_w_down', 'even_w_in',
       'even_w_uq', 'even_w_ukv', 'even_w_out', 'odd_w_in', 'odd_w_out']
SMALL = [n for n in WEIGHTS if n not in BIG]
SMALL_SHARDED = ['even_conv_w', 'odd_pre_g', 'odd_post_g', 'odd_v_norm_g', 'odd_v_norm_b']


def _call(name, body, grid, in_specs, out_specs, out_shape, scratch=(), prefetch=0, aliases=None):
    params = pltpu.CompilerParams(dimension_semantics=("arbitrary",) * len(grid), vmem_limit_bytes=VMEM_LIMIT_BYTES)
    if prefetch:
        spec = pltpu.PrefetchScalarGridSpec(num_scalar_prefetch=prefetch, grid=grid, in_specs=in_specs,
                                            out_specs=out_specs, scratch_shapes=list(scratch))
        return pl.pallas_call(body, grid_spec=spec, out_shape=out_shape, compiler_params=params, name=name,
                              input_output_aliases=aliases or {})
    return pl.pallas_call(body, grid=grid, in_specs=in_specs, out_specs=out_specs, out_shape=out_shape,
                          scratch_shapes=list(scratch), compiler_params=params, name=name,
                          input_output_aliases=aliases or {})


def _spec(block, index_map):
    return pl.BlockSpec(block, index_map)


def _sds(shape, dtype):
    return jax.ShapeDtypeStruct(tuple(shape), dtype)


def _tile(n, want, mult=8):
    if n <= want:
        return n
    best = None
    for t in range(mult, want + 1, mult):
        if n % t == 0:
            best = t
    assert best is not None, (n, want, mult)
    return best


_DIMS = {"nn": (((1,), (0,)), ((), ())), "nt": (((1,), (1,)), ((), ())), "tn": (((0,), (0,)), ((), ()))}


def _dot(a, b, mode="nn"):
    return lax.dot_general(a.astype(MXU_DTYPE), b.astype(MXU_DTYPE), _DIMS[mode], preferred_element_type=F32)


def _matmul(name, mode, grid, groups, outs, acc_shape, extras=(), epilogue=None, courier=None):
    flat, specs = [], []
    for grp in groups:
        for a, a_spec, b, b_spec in grp:
            flat += [a, b]
            specs += [a_spec, b_spec]
    for e, e_spec in extras:
        flat.append(e)
        specs.append(e_spec)
    n_pairs = [len(g) for g in groups]
    n_in, n_ex, n_out, n_acc = 2 * sum(n_pairs), len(extras), len(outs), len(groups)
    last = tuple(g - 1 for g in grid)
    c_arrays = list(courier.arrays) if courier else []
    c_shapes = list(courier.out_shapes) if courier else []
    n_ci, n_co = len(c_arrays), len(c_shapes)
    any_spec = pl.BlockSpec(memory_space=pl.ANY)

    def body(*refs):
        ins, ex = refs[:n_in], refs[n_in:n_in + n_ex]
        pos = n_in + n_ex
        c_in = refs[pos:pos + n_ci]
        out_refs = refs[pos + n_ci:pos + n_ci + n_out]
        c_out = refs[pos + n_ci + n_out:pos + n_ci + n_out + n_co]
        accs = refs[pos + n_ci + n_out + n_co:pos + n_ci + n_out + n_co + n_acc]
        sems = refs[pos + n_ci + n_out + n_co + n_acc:]
        i, j, k = pl.program_id(0), pl.program_id(1), pl.program_id(2)

        if courier:
            @pl.when((i == 0) & (j == 0) & (k == 0))
            def _():
                courier.start(c_in, c_out, sems)

        @pl.when(k == 0)
        def _():
            for acc in accs:
                acc[...] = jnp.zeros(acc.shape, F32)

        pos = 0
        for gi, n in enumerate(n_pairs):
            total = None
            for _ in range(n):
                d = _dot(ins[pos][...], ins[pos + 1][...], mode)
                total = d if total is None else total + d
                pos += 2
            accs[gi][...] += total

        @pl.when(k == last[2])
        def _():
            vals = [acc[...] for acc in accs]
            res = epilogue(vals, [e[...] for e in ex]) if epilogue else vals
            for o, r in zip(out_refs, res):
                o[...] = r.astype(o.dtype)

        if courier:
            @pl.when((i == last[0]) & (j == last[1]) & (k == last[2]))
            def _():
                courier.finish(c_in, c_out, sems)

    aliases = {n_in + n_ex + t: n_out + t for t in range(n_ci)} if courier and courier.aliased else None
    scratch = [pltpu.VMEM(acc_shape, F32)] * n_acc
    scratch += [pltpu.SemaphoreType.DMA((n,)) for n in courier.sems] if courier else []
    res = _call(name, body, grid, specs + [any_spec] * n_ci, [o[2] for o in outs] + [any_spec] * n_co,
                [_sds(o[0], o[1]) for o in outs] + c_shapes, scratch=scratch, aliases=aliases)(*flat, *c_arrays)
    return list(res[:n_out]), list(res[n_out:])


def _rows(name, fn, rows, consts, outs, accs, tm):
    t_rows = rows[0][0].shape[0]
    grid = (t_rows // tm,)
    in_specs = [_spec((tm, bw), functools.partial(lambda i, cb: (i, cb), cb=cb)) for _, bw, cb in rows]
    in_specs += [_spec(c.shape, functools.partial(lambda i, nd: (0,) * nd, nd=c.ndim)) for c in consts]
    out_shape = [_sds((t_rows, w), dt) for w, dt in outs] + [_sds(s, F32) for s in accs]
    out_specs = [_spec((tm, w), lambda i: (i, 0)) for w, _ in outs]
    out_specs += [_spec(s, functools.partial(lambda i, nd: (0,) * nd, nd=len(s))) for s in accs]
    nr, nc, no = len(rows), len(consts), len(outs)

    def body(*refs):
        r = [ref[...] for ref in refs[:nr]]
        c = [ref[...] for ref in refs[nr:nr + nc]]
        o_refs, a_refs = refs[nr + nc:nr + nc + no], refs[nr + nc + no:]
        o_vals, a_vals = fn(r, c)
        for ref, v in zip(o_refs, o_vals):
            ref[...] = v.astype(ref.dtype)
        if a_refs:
            @pl.when(pl.program_id(0) == 0)
            def _():
                for ref in a_refs:
                    ref[...] = jnp.zeros(ref.shape, F32)

            for ref, v in zip(a_refs, a_vals):
                ref[...] += v

    return _call(name, body, grid, in_specs, out_specs, out_shape)(*[a for a, _, _ in rows], *consts)


def _rmsnorm(x, g):
    return x * lax.rsqrt(jnp.mean(x * x, axis=-1, keepdims=True) + EPS) * g


def _layernorm(x, g, b):
    mu = jnp.mean(x, axis=-1, keepdims=True)
    var = jnp.mean(jnp.square(x - mu), axis=-1, keepdims=True)
    return (x - mu) * lax.rsqrt(var + EPS) * g + b


def _swiglu_act(g, u):
    return jax.nn.silu(g) * u


def _rope(x, cos, sin_signed):
    return x * cos + pltpu.roll(x, 2 * ROPE_HALF, 1) * sin_signed


def _rope_transposed(dy, cos, sin_signed):
    return dy * cos - pltpu.roll(dy, 2 * ROPE_HALF, 1) * sin_signed


def _rms_fwd(x, g, tm):
    d = x.shape[1]
    return _rows("rms_fwd", lambda r, c: ([_rmsnorm(r[0], c[0])], []), [(x, d, 0)], [g], [(d, MXU_DTYPE)], [], tm)[0]


def _rms_residual(x, y, g, scale, tm):
    d = x.shape[1]
    return _rows("rms_residual", lambda r, c: ([r[0] + scale * _rmsnorm(r[1], c[0])], []),
                 [(x, d, 0), (y, d, 0)], [g], [(d, F32)], [], tm)[0]


def _rms_bwd(y, g, dout, scale, out_dtype, tm, resid=None):
    d = y.shape[1]

    def fn(r, c):
        _, vjp = jax.vjp(_rmsnorm, r[0], c[0])
        dy, dg = vjp(scale * r[1].astype(F32))
        if resid is not None:
            dy = dy + r[2]
        return [dy], [dg]

    rows = [(y, d, 0), (dout, d, 0)] + ([(resid, d, 0)] if resid is not None else [])
    return _rows("rms_bwd" if resid is None else "rms_bwd_resid", fn, rows, [g], [(d, out_dtype)], [(1, d)], tm)


def _loss_head(y, target, tm):
    d = y.shape[1]

    def fn(r, c):
        err = r[0] - r[1]
        sq = jnp.sum(jnp.sum(err * err, axis=1, keepdims=True), axis=0, keepdims=True)
        return [err * (1.0 / d)], [jnp.broadcast_to(sq, (1, LANES))]

    return _rows("loss_head", fn, [(y, d, 0), (target, d, 0)], [], [(d, F32)], [(1, LANES)], tm)


def _ffn_fwd(x, pre_g, post_g, wg, wu, wd, layer, tm, tr, couriers=None):
    t, d = x.shape
    fs = wg.shape[3]
    tk = _tile(d, 1024, LANES)
    tn = _tile(d, 1024, LANES)
    h = _rms_fwd(x, pre_g, tr)
    h_spec = _spec((tm, tk), lambda i, j, k: (i, k))
    w_spec = _spec((None, None, tk, fs), lambda i, j, k: (j, layer, k, 0))
    o_spec = _spec((tm, fs), lambda i, j, k: (i, j))
    couriers = couriers or {}
    carried = {}
    (g, u, a), carried["up"] = _matmul(
        "ffn_up", "nn", (t // tm, N_CHIPS, d // tk), [[(h, h_spec, wg, w_spec)], [(h, h_spec, wu, w_spec)]],
        [((t, N_CHIPS * fs), MXU_DTYPE, o_spec)] * 3, (tm, fs),
        epilogue=lambda accs, _: (accs[0], accs[1], _swiglu_act(accs[0], accs[1])), courier=couriers.get("up"))
    (y,), carried["down"] = _matmul(
        "ffn_down", "nn", (t // tm, d // tn, N_CHIPS),
        [[(a, _spec((tm, fs), lambda i, j, k: (i, k)), wd, _spec((None, None, fs, tn), lambda i, j, k: (k, layer, 0, j)))]],
        [((t, d), F32, _spec((tm, tn), lambda i, j, k: (i, j)))], (tm, tn), courier=couriers.get("down"))
    out = _rms_residual(x, y, post_g, 0.5, tr)
    return out, (x, h, g, u, a, y), carried


def _ffn_bwd(dout, saved, pre_g, post_g, wg, wu, wd, layer, tm, tr, couriers=None):
    x, h, g, u, a, y = saved
    t, d = x.shape
    fs = wg.shape[3]
    tk = _tile(d, 1024, LANES)
    tn = _tile(d, 1024, LANES)
    tkt = _tile(t, 512, 16)
    dy, dpost = _rms_bwd(y, post_g, dout, 0.5, MXU_DTYPE, tr)

    def act_bwd(accs, ex):
        _, vjp = jax.vjp(_swiglu_act, ex[0].astype(F32), ex[1].astype(F32))
        return vjp(accs[0])

    couriers = couriers or {}
    carried = {}
    gu_spec = _spec((tm, fs), lambda i, j, k: (i, j))
    (dg, du), carried["dact"] = _matmul(
        "ffn_dact", "nt", (t // tm, N_CHIPS, d // tk),
        [[(dy, _spec((tm, tk), lambda i, j, k: (i, k)), wd, _spec((None, None, fs, tk), lambda i, j, k: (j, layer, 0, k)))]],
        [((t, N_CHIPS * fs), MXU_DTYPE, gu_spec)] * 2, (tm, fs),
        extras=[(g, gu_spec), (u, gu_spec)], epilogue=act_bwd, courier=couriers.get("dact"))
    (dwd,), carried["dwdown"] = _matmul(
        "ffn_dwdown", "tn", (N_CHIPS, d // tn, t // tkt),
        [[(a, _spec((tkt, fs), lambda i, j, k: (k, i)), dy, _spec((tkt, tn), lambda i, j, k: (k, j)))]],
        [((N_CHIPS, fs, d), MXU_DTYPE, _spec((None, fs, tn), lambda i, j, k: (i, 0, j)))], (fs, tn),
        courier=couriers.get("dwdown"))
    da_spec = _spec((tm, fs), lambda i, j, k: (i, k))
    wt_spec = _spec((None, None, tn, fs), lambda i, j, k: (k, layer, j, 0))
    (dh,), carried["dh"] = _matmul(
        "ffn_dh", "nt", (t // tm, d // tn, N_CHIPS), [[(dg, da_spec, wg, wt_spec), (du, da_spec, wu, wt_spec)]],
        [((t, d), F32, _spec((tm, tn), lambda i, j, k: (i, j)))], (tm, tn), courier=couriers.get("dh"))
    tmw = _tile(d, 512, LANES)
    h_spec = _spec((tkt, tmw), lambda i, j, k: (k, i))
    dgu_spec = _spec((tkt, fs), lambda i, j, k: (k, j))
    dw_spec = _spec((None, tmw, fs), lambda i, j, k: (j, i, 0))
    (dwg, dwu), carried["dwup"] = _matmul(
        "ffn_dwup", "tn", (d // tmw, N_CHIPS, t // tkt), [[(h, h_spec, dg, dgu_spec)], [(h, h_spec, du, dgu_spec)]],
        [((N_CHIPS, d, fs), MXU_DTYPE, dw_spec)] * 2, (tmw, fs), courier=couriers.get("dwup"))
    dx, dpre = _rms_bwd(x, pre_g, dh, 1.0, F32, tr, resid=dout)
    return dx, (dpre, dpost, dwg, dwu, dwd), carried


def _attn_scores(qn, qpe, kn, kpe, qi, tq, seq):
    s = (_dot(qn, kn, "nt") + _dot(qpe, kpe, "nt")) * ((QK_NOPE + QK_ROPE) ** -0.5)
    rows = qi * tq + lax.broadcasted_iota(jnp.int32, (tq, seq), 0)
    cols = lax.broadcasted_iota(jnp.int32, (tq, seq), 1)
    s = jnp.where(cols <= rows, s, -jnp.inf)
    e = jnp.exp(s - jnp.max(s, axis=1, keepdims=True))
    return e / jnp.sum(e, axis=1, keepdims=True)


def _attn_specs(nq, tq, seq):
    q_rows = lambda b, h, qi: b * nq + qi
    return [
        _spec((tq, LANES), lambda b, h, qi: (q_rows(b, h, qi), h)),
        _spec((tq, LANES), lambda b, h, qi: (q_rows(b, h, qi), HEADS + h)),
        _spec((seq, 2 * LANES), lambda b, h, qi: (b, h)),
        _spec((seq, LANES), lambda b, h, qi: (b, P_KR // LANES)),
        _spec((tq, LANES), lambda b, h, qi: (q_rows(b, h, qi), 0)),
        _spec((tq, LANES), lambda b, h, qi: (q_rows(b, h, qi), 0)),
        _spec((seq, LANES), lambda b, h, qi: (b, 0)),
        _spec((seq, LANES), lambda b, h, qi: (b, 0)),
    ]


def _attn_fwd(q_all, kv, p, cos, sin, batch, seq):
    t = q_all.shape[0]
    tq = _tile(seq, 256, 16)
    nq = seq // tq

    def body(qn_ref, qp_ref, kv_ref, kr_ref, cq, sq, ck, sk, o_ref):
        qpe = _rope(qp_ref[...], cq[...], sq[...])
        kpe = _rope(kr_ref[...], ck[...], sk[...])
        prob = _attn_scores(qn_ref[...], qpe, kv_ref[:, :LANES], kpe, pl.program_id(2), tq, seq)
        o_ref[...] = _dot(prob, kv_ref[:, LANES:]).astype(o_ref.dtype)

    return _call("attn_fwd", body, (batch, HEADS, nq), _attn_specs(nq, tq, seq),
                 _spec((tq, LANES), lambda b, h, qi: (b * nq + qi, h)), _sds((t, 2 * HEADS * V_HEAD), MXU_DTYPE)
                 )(q_all, q_all, kv, p, cos, sin, cos, sin)


def _attn_bwd(q_all, kv, p, cos, sin, d_cat, batch, seq):
    t = q_all.shape[0]
    tq = _tile(seq, 256, 16)
    nq = seq // tq

    def body(qn_ref, qp_ref, kv_ref, kr_ref, cq, sq, ck, sk, do_ref, dqn_ref, dqp_ref, dkv_ref, dkr_ref, dkpe_acc):
        h, qi = pl.program_id(1), pl.program_id(2)
        qn = qn_ref[...]
        qpe = _rope(qp_ref[...], cq[...], sq[...])
        kn, v = kv_ref[:, :LANES], kv_ref[:, LANES:]
        kpe = _rope(kr_ref[...], ck[...], sk[...])
        prob = _attn_scores(qn, qpe, kn, kpe, qi, tq, seq)
        do = do_ref[...]
        dprob = _dot(do, v, "nt")
        ds = prob * (dprob - jnp.sum(prob * dprob, axis=1, keepdims=True)) * ((QK_NOPE + QK_ROPE) ** -0.5)
        dqn_ref[...] = _dot(ds, kn)
        dqp_ref[...] = _rope_transposed(_dot(ds, kpe), cq[...], sq[...])

        @pl.when(qi == 0)
        def _():
            dkv_ref[...] = jnp.zeros(dkv_ref.shape, F32)

        dkv_ref[:, :LANES] += _dot(ds, qn, "tn")
        dkv_ref[:, LANES:] += _dot(prob, do, "tn")

        @pl.when((h == 0) & (qi == 0))
        def _():
            dkpe_acc[...] = jnp.zeros(dkpe_acc.shape, F32)

        dkpe_acc[...] += _dot(ds, qpe, "tn")

        @pl.when((h == HEADS - 1) & (qi == nq - 1))
        def _():
            dkr_ref[...] = _rope_transposed(dkpe_acc[...], ck[...], sk[...])

    q_out = _spec((tq, LANES), lambda b, h, qi: (b * nq + qi, h))
    return _call("attn_bwd", body, (batch, HEADS, nq),
                 _attn_specs(nq, tq, seq) + [_spec((tq, LANES), lambda b, h, qi: (b * nq + qi, h))],
                 [q_out, q_out, _spec((seq, 2 * LANES), lambda b, h, qi: (b, h)), _spec((seq, LANES), lambda b, h, qi: (b, 0))],
                 [_sds((t, HEADS * LANES), F32), _sds((t, HEADS * LANES), F32), _sds((t, HEADS * 2 * LANES), F32), _sds((t, LANES), F32)],
                 scratch=[pltpu.VMEM((seq, LANES), F32)])(q_all, q_all, kv, p, cos, sin, cos, sin, d_cat)


CONV_PAD = 32


def _conv_taps(w_ref, src_ref, first_row, n_rows, init, offset):
    acc = init
    for k in range(CONV_WIDTH):
        acc = acc + w_ref[k:k + 1, :] * src_ref[pl.ds(first_row + offset(k), n_rows), :]
    return acc


def _norm_act(conv, g, b):
    return jax.nn.silu(_layernorm(conv, g, b))


def _conv_fwd(p, cat, conv_w, conv_b, norm_g, norm_b, batch, seq):
    rc = _tile(seq, 256, 8)
    a_blk, gate_blk, out_blk = P_A // LANES, P_GATE // LANES, (HEADS * V_HEAD) // LANES

    def body(a_ref, gate_ref, w_ref, cb_ref, ng_ref, nb_ref, cat_in, o_ref, zp):
        del cat_in
        zp[pl.ds(0, CONV_PAD), :] = jnp.zeros((CONV_PAD, LANES), F32)
        zp[pl.ds(CONV_PAD, seq), :] = a_ref[...] * jax.nn.sigmoid(gate_ref[...])
        for r0 in range(0, seq, rc):
            conv = _conv_taps(w_ref, zp, r0, rc, jnp.broadcast_to(cb_ref[...], (rc, LANES)),
                              lambda k: CONV_PAD - (CONV_WIDTH - 1) + k)
            o_ref[pl.ds(r0, rc), :] = _norm_act(conv, ng_ref[...], nb_ref[...]).astype(o_ref.dtype)

    vec = _spec((1, LANES), lambda b, g: (0, g))
    return _call("conv_fwd", body, (batch, GROUPS),
                 [_spec((seq, LANES), lambda b, g: (b, a_blk + g)), _spec((seq, LANES), lambda b, g: (b, gate_blk + g)),
                  _spec((CONV_WIDTH, LANES), lambda b, g: (0, g)), vec, vec, vec, pl.BlockSpec(memory_space=pl.ANY)],
                 _spec((seq, LANES), lambda b, g: (b, out_blk + g)), _sds(cat.shape, cat.dtype),
                 scratch=[pltpu.VMEM((seq + CONV_PAD, LANES), F32)], aliases={6: 0}
                 )(p, p, conv_w, conv_b, norm_g, norm_b, cat)


def _conv_bwd(p, d_cat, conv_w, conv_b, norm_g, norm_b, batch, seq):
    t = p.shape[0]
    rc = _tile(seq, 256, 8)
    a_blk, gate_blk, out_blk = P_A // LANES, P_GATE // LANES, (HEADS * V_HEAD) // LANES

    def body(a_ref, gate_ref, w_ref, cb_ref, ng_ref, nb_ref, dc_ref, da_ref, dgate_ref, dw_ref, db_ref, dng_ref, dnb_ref,
             zp, dcp, buf):
        b = pl.program_id(1)
        a, sig = a_ref[...], jax.nn.sigmoid(gate_ref[...])
        zp[pl.ds(0, CONV_PAD), :] = jnp.zeros((CONV_PAD, LANES), F32)
        zp[pl.ds(CONV_PAD, seq), :] = a * sig
        for r0 in range(0, seq, rc):
            buf[pl.ds(r0, rc), :] = _conv_taps(w_ref, zp, r0, rc, jnp.broadcast_to(cb_ref[...], (rc, LANES)),
                                               lambda k: CONV_PAD - (CONV_WIDTH - 1) + k)
        _, vjp = jax.vjp(_norm_act, buf[...], ng_ref[...], nb_ref[...])
        dconv, dng, dnb = vjp(dc_ref[...].astype(F32))
        dcp[pl.ds(0, seq), :] = dconv
        dcp[pl.ds(seq, CONV_PAD), :] = jnp.zeros((CONV_PAD, LANES), F32)

        @pl.when(b == 0)
        def _():
            dw_ref[...] = jnp.zeros(dw_ref.shape, F32)
            db_ref[...] = jnp.zeros(db_ref.shape, F32)
            dng_ref[...] = jnp.zeros(dng_ref.shape, F32)
            dnb_ref[...] = jnp.zeros(dnb_ref.shape, F32)

        db_ref[...] += jnp.sum(dconv, axis=0, keepdims=True)
        dng_ref[...] += dng
        dnb_ref[...] += dnb
        for k in range(CONV_WIDTH):
            shifted = zp[pl.ds(CONV_PAD - (CONV_WIDTH - 1) + k, seq), :]
            dw_ref[k:k + 1, :] += jnp.sum(dconv * shifted, axis=0, keepdims=True)
        for r0 in range(0, seq, rc):
            buf[pl.ds(r0, rc), :] = _conv_taps(w_ref, dcp, r0, rc, jnp.zeros((rc, LANES), F32),
                                               lambda k: CONV_WIDTH - 1 - k)
        dz = buf[...]
        da_ref[...] = (dz * sig).astype(da_ref.dtype)
        dgate_ref[...] = (dz * a * sig * (1.0 - sig)).astype(dgate_ref.dtype)

    vec = _spec((1, LANES), lambda g, b: (0, g))
    row_out = _spec((seq, LANES), lambda g, b: (b, g))
    return _call("conv_bwd", body, (GROUPS, batch),
                 [_spec((seq, LANES), lambda g, b: (b, a_blk + g)), _spec((seq, LANES), lambda g, b: (b, gate_blk + g)),
                  _spec((CONV_WIDTH, LANES), lambda g, b: (0, g)), vec, vec, vec,
                  _spec((seq, LANES), lambda g, b: (b, out_blk + g))],
                 [row_out, row_out, _spec((CONV_WIDTH, LANES), lambda g, b: (0, g)), vec, vec, vec],
                 [_sds((t, CONV_CH), MXU_DTYPE), _sds((t, CONV_CH), MXU_DTYPE), _sds((CONV_WIDTH, CONV_CH), F32),
                  _sds((1, CONV_CH), F32), _sds((1, CONV_CH), F32), _sds((1, CONV_CH), F32)],
                 scratch=[pltpu.VMEM((seq + CONV_PAD, LANES), F32), pltpu.VMEM((seq + CONV_PAD, LANES), F32),
                          pltpu.VMEM((seq, LANES), F32)])(p, p, conv_w, conv_b, norm_g, norm_b, d_cat)


def _sgu_pre(hw, g, b):
    z = jax.nn.gelu(hw)
    return z[:, :GM_WIDTH], _layernorm(z[:, GM_WIDTH:], g, b)


def _causal(w):
    keep = lax.broadcasted_iota(jnp.int32, (CHUNK, CHUNK), 0) >= lax.broadcasted_iota(jnp.int32, (CHUNK, CHUNK), 1)
    return jnp.where(keep, w, 0.0)


def _sgu_fwd(hw, vg, vb, w_s, b_st):
    t = hw.shape[0]
    gw = GM_WIDTH // GROUPS

    def body(hw_ref, vg_ref, vb_ref, w_ref, b_ref, o_ref):
        u, v = _sgu_pre(hw_ref[...], vg_ref[...], vb_ref[...])
        for g in range(GROUPS):
            cols = slice(g * gw, (g + 1) * gw)
            s = _dot(_causal(w_ref[g]), v[:, cols]) + b_ref[:, g:g + 1]
            o_ref[:, cols] = (u[:, cols] * s).astype(o_ref.dtype)

    return _call("sgu_fwd", body, (t // CHUNK,),
                 [_spec((CHUNK, 2 * GM_WIDTH), lambda i: (i, 0)), _spec((1, GM_WIDTH), lambda i: (0, 0)),
                  _spec((1, GM_WIDTH), lambda i: (0, 0)), _spec((GROUPS, CHUNK, CHUNK), lambda i: (0, 0, 0)),
                  _spec((CHUNK, GROUPS), lambda i: (0, 0))],
                 _spec((CHUNK, GM_WIDTH), lambda i: (i, 0)), _sds((t, GM_WIDTH), MXU_DTYPE))(hw, vg, vb, w_s, b_st)


def _sgu_bwd(hw, vg, vb, w_s, b_st, dout):
    t = hw.shape[0]
    gw = GM_WIDTH // GROUPS

    def body(hw_ref, vg_ref, vb_ref, w_ref, b_ref, do_ref, dhw_ref, dvg_ref, dvb_ref, dw_ref, db_ref, du_buf, dv_buf):
        (u, v), vjp = jax.vjp(_sgu_pre, hw_ref[...], vg_ref[...], vb_ref[...])

        @pl.when(pl.program_id(0) == 0)
        def _():
            dvg_ref[...] = jnp.zeros(dvg_ref.shape, F32)
            dvb_ref[...] = jnp.zeros(dvb_ref.shape, F32)
            dw_ref[...] = jnp.zeros(dw_ref.shape, F32)
            db_ref[...] = jnp.zeros(db_ref.shape, F32)

        lane = lax.broadcasted_iota(jnp.int32, (CHUNK, LANES), 1)
        db = jnp.zeros((CHUNK, LANES), F32)
        for g in range(GROUPS):
            cols = slice(g * gw, (g + 1) * gw)
            w = _causal(w_ref[g])
            s = _dot(w, v[:, cols]) + b_ref[:, g:g + 1]
            do = do_ref[:, cols]
            ds = do * u[:, cols]
            du_buf[:, cols] = do * s
            dw_ref[g] += _causal(_dot(ds, v[:, cols], "nt"))
            dv_buf[:, cols] = _dot(w, ds, "tn")
            db = db + jnp.where(lane == g, jnp.sum(ds, axis=1, keepdims=True), 0.0)
        db_ref[...] += db
        dhw, dvg, dvb = vjp((du_buf[...], dv_buf[...]))
        dhw_ref[...] = dhw.astype(dhw_ref.dtype)
        dvg_ref[...] += dvg
        dvb_ref[...] += dvb

    vec = _spec((1, GM_WIDTH), lambda i: (0, 0))
    w_spec = _spec((GROUPS, CHUNK, CHUNK), lambda i: (0, 0, 0))
    return _call("sgu_bwd", body, (t // CHUNK,),
                 [_spec((CHUNK, 2 * GM_WIDTH), lambda i: (i, 0)), vec, vec, w_spec, _spec((CHUNK, GROUPS), lambda i: (0, 0)),
                  _spec((CHUNK, GM_WIDTH), lambda i: (i, 0))],
                 [_spec((CHUNK, 2 * GM_WIDTH), lambda i: (i, 0)), vec, vec, w_spec, _spec((CHUNK, LANES), lambda i: (0, 0))],
                 [_sds((t, 2 * GM_WIDTH), MXU_DTYPE), _sds((1, GM_WIDTH), F32), _sds((1, GM_WIDTH), F32),
                  _sds((GROUPS, CHUNK, CHUNK), F32), _sds((CHUNK, LANES), F32)],
                 scratch=[pltpu.VMEM((CHUNK, GM_WIDTH), F32), pltpu.VMEM((CHUNK, GM_WIDTH), F32)]
                 )(hw, vg, vb, w_s, b_st, dout)


def _mm_nn(name, a, b, out_dtype, tm, a_col0=0, tn_want=1024, tk_want=1024, courier=None):
    t = a.shape[0]
    kk, n = b.shape
    tk = _tile(kk, tk_want, LANES)
    tn = _tile(n, tn_want, LANES)
    k0 = a_col0 // tk
    assert a_col0 % tk == 0
    outs, carried = _matmul(
        name, "nn", (t // tm, n // tn, kk // tk),
        [[(a, _spec((tm, tk), lambda i, j, k: (i, k0 + k)), b, _spec((tk, tn), lambda i, j, k: (k, j)))]],
        [((t, n), out_dtype, _spec((tm, tn), lambda i, j, k: (i, j)))], (tm, tn), courier=courier)
    return (outs[0], carried) if courier else outs[0]


def _mm_nt(name, pairs, out_dtype, tm, tn_want=1024, tk_want=1024, courier=None):
    t, kk = pairs[0][0].shape
    n = pairs[0][1].shape[0]
    tk = _tile(kk, tk_want, LANES)
    tn = _tile(n, tn_want, 16)
    grp = []
    for a, b, b_col0 in pairs:
        assert b_col0 % tk == 0 and a.shape == (t, kk)
        grp.append((a, _spec((tm, tk), lambda i, j, k: (i, k)), b,
                    _spec((tn, tk), functools.partial(lambda i, j, k, k0: (j, k0 + k), k0=b_col0 // tk))))
    outs, carried = _matmul(name, "nt", (t // tm, n // tn, kk // tk), [grp],
                            [((t, n), out_dtype, _spec((tm, tn), lambda i, j, k: (i, j)))], (tm, tn), courier=courier)
    return (outs[0], carried) if courier else outs[0]


def _mm_tn(name, a, bs, out_dtype, a_col0=0, m_width=None, tm_want=512, tn_want=1024, courier=None):
    t = a.shape[0]
    m = m_width or a.shape[1]
    n = bs[0].shape[1]
    tmw = _tile(m, tm_want, LANES)
    tn = _tile(n, tn_want, LANES)
    tkt = _tile(t, 512, 16)
    assert a_col0 % tmw == 0
    i0 = a_col0 // tmw
    a_spec = _spec((tkt, tmw), lambda i, j, k: (k, i0 + i))
    groups = [[(a, a_spec, b, _spec((tkt, tn), lambda i, j, k: (k, j)))] for b in bs]
    outs, carried = _matmul(name, "tn", (m // tmw, n // tn, t // tkt), groups,
                            [((m, n), out_dtype, _spec((tmw, tn), lambda i, j, k: (i, j)))] * len(bs), (tmw, tn),
                            courier=courier)
    return (outs, carried) if courier else outs


_ANY = pl.BlockSpec(memory_space=pl.ANY)


def _comm_call(name, body, ins, out_shapes, sems, aliases=None):
    return pl.pallas_call(body, name=name, out_shape=out_shapes, in_specs=[_ANY] * len(ins),
                          out_specs=[_ANY] * len(out_shapes), input_output_aliases=aliases or {},
                          scratch_shapes=[pltpu.SemaphoreType.DMA((n,)) for n in sems])(*ins)


def _place():
    return lax.axis_index("x"), lax.axis_index("y"), lax.axis_index("c")


def _other_chips(x, y):
    return [(1 - x, y), (x, 1 - y), (1 - x, 1 - y)]


def _cast_into_slot(w, place, dtype):
    layers, rows, cols = w.shape
    half = rows // 2
    tr = _tile(half, 256, 16)
    nt = half // tr

    def body(s, w_ref, o_ref):
        del s
        o_ref[...] = w_ref[...].astype(o_ref.dtype)

    return _call("cast_into_slot", body, (layers, 2, nt),
                 [_spec((None, tr, cols), lambda l, h, i, s: (l, h * nt + i, 0))],
                 _spec((None, None, None, tr, cols), lambda l, h, i, s: (s[0], l, h, i, 0)),
                 _sds((N_CHIPS, layers, 2, half, cols), dtype), prefetch=1)(place, w)


class _GatherChips:
    aliased = True

    def __init__(self, bufs):
        self.arrays = list(bufs)
        self.out_shapes = [_sds(b.shape, b.dtype) for b in bufs]
        self.sems = [3 * len(bufs)] * 4

    def _sent(self, buf, sems):
        x, y, c = _place()
        me = 2 * x + y
        return [[pltpu.make_async_remote_copy(
            src_ref=buf[t].at[me, :, c], dst_ref=buf[t].at[me, :, c], send_sem=sems[0].at[3 * t + r],
            recv_sem=sems[1].at[3 * t + r], device_id=(px, py, c), device_id_type=MESH)
            for r, (px, py) in enumerate(_other_chips(x, y))] for t in range(len(buf))]

    def start(self, _, buf, sems):
        for row in self._sent(buf, sems):
            for cp in row:
                cp.start()

    def finish(self, _, buf, sems):
        x, y, c = _place()
        sent = self._sent(buf, sems)
        passed = []
        for t in range(len(buf)):
            for r, (px, py) in enumerate(_other_chips(x, y)):
                landed = buf[t].at[2 * px + py, :, c]
                sent[t][r].wait_recv()
                cp = pltpu.make_async_remote_copy(
                    src_ref=landed, dst_ref=landed, send_sem=sems[2].at[3 * t + r], recv_sem=sems[3].at[3 * t + r],
                    device_id=(x, y, 1 - c), device_id_type=MESH)
                cp.start()
                passed.append(cp)
        for cp in passed:
            cp.wait()
        for row in sent:
            for cp in row:
                cp.wait_send()


class _ScatterChips:
    aliased = False

    def __init__(self, partials):
        self.arrays = list(partials)
        self.out_shapes = [_sds((3,) + p.shape[1:], p.dtype) for p in partials]
        self.sems = [3 * len(partials)] * 2

    def _copies(self, src, dst, sems):
        x, y, c = _place()
        return [pltpu.make_async_remote_copy(
            src_ref=src[t].at[2 * px + py], dst_ref=dst[t].at[r], send_sem=sems[0].at[3 * t + r],
            recv_sem=sems[1].at[3 * t + r], device_id=(px, py, c), device_id_type=MESH)
            for t in range(len(src)) for r, (px, py) in enumerate(_other_chips(x, y))]

    def start(self, src, dst, sems):
        for cp in self._copies(src, dst, sems):
            cp.start()

    def finish(self, src, dst, sems):
        for cp in self._copies(src, dst, sems):
            cp.wait()


def _run_courier(name, courier):
    n_in, n_out = len(courier.arrays), len(courier.out_shapes)

    def body(*refs):
        src, dst, sems = refs[:n_in], refs[n_in:n_in + n_out], refs[n_in + n_out:]
        courier.start(src, dst, sems)
        courier.finish(src, dst, sems)

    return _comm_call(name, body, courier.arrays, courier.out_shapes, courier.sems,
                      aliases={t: t for t in range(n_in)} if courier.aliased else None)


def _swap_halves(grads):
    n = len(grads)

    def body(*refs):
        src, dst = refs[:n], refs[n:2 * n]
        send_sems, recv_sems = refs[2 * n:]
        x, y, c = _place()
        copies = [pltpu.make_async_remote_copy(
            src_ref=src[t].at[:, 1 - c], dst_ref=dst[t], send_sem=send_sems.at[t], recv_sem=recv_sems.at[t],
            device_id=(x, y, 1 - c), device_id_type=MESH) for t in range(n)]
        for cp in copies:
            cp.start()
        for cp in copies:
            cp.wait()

    return _comm_call("swap_halves", body, grads, [_sds((g.shape[0],) + g.shape[2:], g.dtype) for g in grads], [n, n])


def _join_halves(grads):
    n = len(grads)

    def body(*refs):
        buf = refs[n:2 * n]
        send_sems, recv_sems = refs[2 * n:]
        x, y, c = _place()
        copies = [pltpu.make_async_remote_copy(
            src_ref=buf[t].at[:, c], dst_ref=buf[t].at[:, c], send_sem=send_sems.at[t], recv_sem=recv_sems.at[t],
            device_id=(x, y, 1 - c), device_id_type=MESH) for t in range(n)]
        for cp in copies:
            cp.start()
        for cp in copies:
            cp.wait()

    return _comm_call("join_halves", body, grads, [_sds(g.shape, g.dtype) for g in grads], [n, n],
                      aliases={t: t for t in range(n)})


def _gather_devices(slots):
    def body(src, buf, send_sems, recv_sems):
        del src
        x, y, c = _place()
        me = 4 * x + 2 * y + c
        copies = []
        for r in range(1, N_DEV):
            fx, fy, fc = (r >> 2) & 1, (r >> 1) & 1, r & 1
            peer = (1 - x if fx else x, 1 - y if fy else y, 1 - c if fc else c)
            copies.append(pltpu.make_async_remote_copy(
                src_ref=buf.at[me], dst_ref=buf.at[me], send_sem=send_sems.at[r - 1], recv_sem=recv_sems.at[r - 1],
                device_id=peer, device_id_type=MESH))
        for cp in copies:
            cp.start()
        for cp in copies:
            cp.wait()

    return _comm_call("gather_devices", body, [slots], [_sds(slots.shape, slots.dtype)], [N_DEV - 1, N_DEV - 1],
                      aliases={0: 0})[0]


def _add_halves(grad, got, place):
    _, _, rows, cols = grad.shape
    tr = _tile(rows, 256, 16)

    def body(s, a, b, o):
        del s
        o[...] = (a[...].astype(F32) + b[...].astype(F32)).astype(o.dtype)

    return _call("add_halves", body, (N_CHIPS, rows // tr),
                 [_spec((None, None, tr, cols), lambda j, i, s: (j, s[1], i, 0)), _spec((None, tr, cols), lambda j, i, s: (j, i, 0))],
                 _spec((None, tr, cols), lambda j, i, s: (j, i, 0)), _sds((N_CHIPS, rows, cols), grad.dtype), prefetch=1
                 )(place, grad, got)


def _add_chips(partial, got, place, layer, layers, into=None):
    _, rows, cols = partial.shape
    tr = _tile(rows, 256, 16)

    def body(s, a, b, *rest):
        o = rest[-1]
        o[...] = a[...].astype(F32) + b[0].astype(F32) + b[1].astype(F32) + b[2].astype(F32)

    in_specs = [_spec((None, tr, cols), lambda i, s: (s[0], i, 0)), _spec((3, tr, cols), lambda i, s: (0, i, 0))]
    args = [place, partial, got]
    if into is not None:
        in_specs.append(pl.BlockSpec(memory_space=pl.ANY))
        args.append(into)
    return _call("add_chips", body, (rows // tr,), in_specs,
                 _spec((None, None, tr, cols), lambda i, s: (layer, s[1], i, 0)), _sds((layers, 2, rows, cols), F32),
                 prefetch=1, aliases={3: 0} if into is not None else None)(*args)


def _add_devices(got):
    _, rows, cols = got.shape
    tr = _tile(rows, 512, 8)

    def body(a, o):
        total = a[0]
        for d in range(1, N_DEV):
            total = total + a[d]
        o[...] = total

    return _call("add_devices", body, (rows // tr,), [_spec((N_DEV, tr, cols), lambda i: (0, i, 0))],
                 _spec((tr, cols), lambda i: (i, 0)), _sds((rows, cols), F32))(got)


def _adamw(w, g, m, v):
    layers, rows, cols = w.shape
    tr = _tile(rows, 256, 8)

    def body(w_ref, g_ref, m_ref, v_ref, d_ref, nm_ref, nv_ref):
        grad = g_ref[...]
        new_m = ADAM_B1 * m_ref[...] + (1.0 - ADAM_B1) * grad
        new_v = ADAM_B2 * v_ref[...] + (1.0 - ADAM_B2) * jnp.square(grad)
        m_hat = new_m / (1.0 - ADAM_B1 ** ADAM_STEP)
        v_hat = new_v / (1.0 - ADAM_B2 ** ADAM_STEP)
        d_ref[...] = -ADAM_LR * (m_hat / (jnp.sqrt(v_hat) + ADAM_EPS) + ADAM_WD * w_ref[...])
        nm_ref[...] = new_m
        nv_ref[...] = new_v

    blk = _spec((None, tr, cols), lambda l, i: (l, i, 0))
    return _call("adamw", body, (layers, rows // tr), [blk] * 4, [blk] * 3, [_sds(w.shape, F32)] * 3)(w, g, m, v)


def _rope_layout(w):
    z = jnp.zeros(w.shape[:-1] + (ROPE_HALF,), w.dtype)
    return jnp.concatenate([w[..., :ROPE_HALF], z, w[..., ROPE_HALF:], z], axis=-1)


def _rope_layout_inv(w):
    return jnp.concatenate([w[..., :ROPE_HALF], w[..., 2 * ROPE_HALF:3 * ROPE_HALF]], axis=-1)


def _cols_from_chips(g):
    return jnp.moveaxis(g, 0, 1).reshape(g.shape[1], N_CHIPS * g.shape[2])


def _cols_to_chips(w):
    return jnp.moveaxis(w.reshape(w.shape[0], N_CHIPS, w.shape[1] // N_CHIPS), 1, 0)


def _w_in_layout(w):
    off_kr = Q_LORA + KV_LORA
    return jnp.concatenate([w[:, :off_kr], w[:, off_kr + QK_ROPE:], _rope_layout(w[:, off_kr:off_kr + QK_ROPE])], axis=1)


def _w_in_layout_inv(dw):
    return jnp.concatenate([dw[:, :P_A], _rope_layout_inv(dw[:, P_KR:]), dw[:, P_A:P_KR]], axis=1)


def _w_uq_layout(w):
    w = w.reshape(Q_LORA, HEADS, QK_NOPE + QK_ROPE)
    return jnp.concatenate([w[..., :QK_NOPE].reshape(Q_LORA, HEADS * QK_NOPE),
                            _rope_layout(w[..., QK_NOPE:]).reshape(Q_LORA, HEADS * LANES)], axis=1)


def _w_uq_layout_inv(d_nope, d_rope):
    d_nope = d_nope.reshape(Q_LORA, HEADS, QK_NOPE)
    d_rope = _rope_layout_inv(d_rope.reshape(Q_LORA, HEADS, LANES))
    return jnp.concatenate([d_nope, d_rope], axis=-1).reshape(Q_LORA, HEADS * (QK_NOPE + QK_ROPE))


def _to_lanes(a):
    flat = a.reshape(-1)
    pad = (-flat.shape[0]) % LANES
    if pad:
        flat = jnp.concatenate([flat, jnp.zeros((pad,), flat.dtype)])
    return flat.reshape(-1, LANES)


PACK_ROWS = 64


def _pack(arrays):
    parts, ranges, row = [], [], 0
    for a in arrays:
        p = _to_lanes(a)
        parts.append(p)
        ranges.append((row, row + p.shape[0]))
        row += p.shape[0]
    pad = (-row) % PACK_ROWS
    if pad:
        parts.append(jnp.zeros((pad, LANES), F32))
    return jnp.concatenate(parts, axis=0), ranges


def _unpack(packed, rng, shape):
    n = 1
    for s in shape:
        n *= s
    return packed[rng[0]:rng[1]].reshape(-1)[:n].reshape(shape)


def kernel(x, positions, ffn_a_pre_g, ffn_a_post_g, ffn_a_w_gate, ffn_a_w_up, ffn_a_w_down, ffn_b_pre_g, ffn_b_post_g, ffn_b_w_gate, ffn_b_w_up, ffn_b_w_down, even_pre_g, even_post_g, even_w_in, even_q_norm_g, even_kv_norm_g, even_w_uq, even_w_ukv, even_conv_w, even_conv_b, even_conv_norm_g, even_conv_norm_b, even_w_out, odd_pre_g, odd_post_g, odd_w_in, odd_v_norm_g, odd_v_norm_b, odd_w_s, odd_b_s, odd_w_out, loss_target, m_ffn_a_pre_g, m_ffn_a_post_g, m_ffn_a_w_gate, m_ffn_a_w_up, m_ffn_a_w_down, m_ffn_b_pre_g, m_ffn_b_post_g, m_ffn_b_w_gate, m_ffn_b_w_up, m_ffn_b_w_down, m_even_pre_g, m_even_post_g, m_even_w_in, m_even_q_norm_g, m_even_kv_norm_g, m_even_w_uq, m_even_w_ukv, m_even_conv_w, m_even_conv_b, m_even_conv_norm_g, m_even_conv_norm_b, m_even_w_out, m_odd_pre_g, m_odd_post_g, m_odd_w_in, m_odd_v_norm_g, m_odd_v_norm_b, m_odd_w_s, m_odd_b_s, m_odd_w_out, v_ffn_a_pre_g, v_ffn_a_post_g, v_ffn_a_w_gate, v_ffn_a_w_up, v_ffn_a_w_down, v_ffn_b_pre_g, v_ffn_b_post_g, v_ffn_b_w_gate, v_ffn_b_w_up, v_ffn_b_w_down, v_even_pre_g, v_even_post_g, v_even_w_in, v_even_q_norm_g, v_even_kv_norm_g, v_even_w_uq, v_even_w_ukv, v_even_conv_w, v_even_conv_b, v_even_conv_norm_g, v_even_conv_norm_b, v_even_w_out, v_odd_pre_g, v_odd_post_g, v_odd_w_in, v_odd_v_norm_g, v_odd_v_norm_b, v_odd_w_s, v_odd_b_s, v_odd_w_out):
    given = dict(locals())
    w = {n: given[n] for n in WEIGHTS}
    batch, seq, d = x.shape
    t = batch * seq
    tm = _tile(t, 512, 16)
    tr = _tile(t, 256, 16)
    chip = 2 * lax.axis_index("x") + lax.axis_index("y")
    place = jnp.stack([chip, lax.axis_index("c")]).astype(jnp.int32)

    small_shard, small_shard_rng = _pack([w[n] for n in SMALL_SHARDED])
    wbuf = {(n, layer): _cast_into_slot(w[n][layer:layer + 1], place, MXU_DTYPE)
            for n in BIG for layer in range(w[n].shape[0])}
    wbuf['small'] = _cast_into_slot(small_shard[None], place, F32)

    def ffn_keys(tag, layer):
        return [(f'ffn_{tag}_w_{part}', layer) for part in ('gate', 'up', 'down')]

    def gather(keys):
        return _GatherChips([wbuf[k] for k in keys]), keys

    def landed(order, results):
        for k, r in zip(order[1], results):
            wbuf[k] = r

    def weight(key):
        n, _ = key
        return wbuf[key].reshape((N_CHIPS, 1) + w[n].shape[1:])

    first = gather(ffn_keys('a', 0) + ['small'])
    landed(first, _run_courier("gather_chips", first[0]))
    shards = wbuf['small'].reshape((N_CHIPS,) + small_shard.shape)
    full = dict(w)
    for n, rng in zip(SMALL_SHARDED, small_shard_rng):
        per_chip = [_unpack(shards[j], rng, w[n].shape) for j in range(N_CHIPS)]
        full[n] = jnp.concatenate(per_chip, axis=-1)

    inv_freq = ROPE_THETA ** (-jnp.arange(0, QK_ROPE, 2, dtype=F32) / QK_ROPE)
    ang = positions.astype(F32).reshape(t, 1) * inv_freq
    zeros = jnp.zeros((t, ROPE_HALF), F32)
    cos = jnp.concatenate([jnp.cos(ang), zeros, jnp.cos(ang), zeros], axis=1)
    sin = jnp.concatenate([-jnp.sin(ang), zeros, jnp.sin(ang), zeros], axis=1)

    def ffn_args(tag, layer):
        gate, up, down = (weight(k) for k in ffn_keys(tag, layer))
        return (w[f'ffn_{tag}_pre_g'][layer:layer + 1], w[f'ffn_{tag}_post_g'][layer:layer + 1], gate, up, down, 0, tm, tr)

    def ffn_forward(xin, tag, layer, up_keys, down_keys):
        c_up, c_down = gather(up_keys), gather(down_keys)
        out, saved, carried = _ffn_fwd(xin, *ffn_args(tag, layer), couriers={'up': c_up[0], 'down': c_down[0]})
        landed(c_up, carried['up'])
        landed(c_down, carried['down'])
        return out, saved

    even_keys = [('even_w_in', 0), ('even_w_uq', 0), ('even_w_ukv', 0), ('even_w_out', 0)]
    odd_keys = [('odd_w_in', 0), ('odd_w_out', 0)]

    xs = x.reshape(t, d)
    x1, ffn_a0 = ffn_forward(xs, 'a', 0, even_keys + ffn_keys('b', 0)[:1], ffn_keys('b', 0)[1:2])
    w_in = _w_in_layout(_cols_from_chips(weight(('even_w_in', 0))[:, 0]))
    w_uq = _w_uq_layout(_cols_from_chips(weight(('even_w_uq', 0))[:, 0]))
    w_ukv = _cols_from_chips(weight(('even_w_ukv', 0))[:, 0])
    w_out_e = weight(('even_w_out', 0))[:, 0].reshape(d, d)
    h_e = _rms_fwd(x1, w['even_pre_g'], tr)
    order = gather(ffn_keys('b', 0)[2:])
    p, res = _mm_nn("mm_w_in", h_e, w_in, F32, tm, tn_want=640, courier=order[0])
    landed(order, res)
    lat = _rows("latent_norm", lambda r, c: ([jnp.concatenate([_rmsnorm(r[0], c[0]), _rmsnorm(r[1], c[1])], axis=1)], []),
                [(p, Q_LORA, 0), (p, KV_LORA, 1)], [w['even_q_norm_g'], w['even_kv_norm_g']], [(Q_LORA + KV_LORA, MXU_DTYPE)], [], tr)[0]
    q_all = _mm_nn("mm_w_uq", lat, w_uq, F32, tm, a_col0=0, tk_want=Q_LORA)
    kv = _mm_nn("mm_w_ukv", lat, w_ukv, MXU_DTYPE, tm, a_col0=Q_LORA, tk_want=KV_LORA)
    cat = _attn_fwd(q_all, kv, p, cos, sin, batch, seq)
    cat = _conv_fwd(p, cat, full['even_conv_w'][0], w['even_conv_b'], w['even_conv_norm_g'], w['even_conv_norm_b'], batch, seq)
    y_e = _mm_nn("mm_w_out", cat, w_out_e, F32, tm)
    x2 = _rms_residual(x1, y_e, w['even_post_g'], 1.0, tr)
    x3, ffn_b0 = ffn_forward(x2, 'b', 0, ffn_keys('a', 1)[:2], ffn_keys('a', 1)[2:])
    x4, ffn_a1 = ffn_forward(x3, 'a', 1, odd_keys + ffn_keys('b', 1)[:1], ffn_keys('b', 1)[1:2])
    w_out_o = weight(('odd_w_out', 0))[:, 0].reshape(GM_WIDTH, d)
    w_in_o = weight(('odd_w_in', 0))
    h_o = _rms_fwd(x4, full['odd_pre_g'], tr)
    ns = w_in_o.shape[3]
    tk_o = _tile(d, 1024, LANES)
    order = gather(ffn_keys('b', 1)[2:])
    (hw,), res = _matmul("mm_w_in_odd", "nn", (t // tm, N_CHIPS, d // tk_o),
                         [[(h_o, _spec((tm, tk_o), lambda i, j, k: (i, k)), w_in_o, _spec((None, None, tk_o, ns), lambda i, j, k: (j, 0, k, 0)))]],
                         [((t, N_CHIPS * ns), F32, _spec((tm, ns), lambda i, j, k: (i, j)))], (tm, ns), courier=order[0])
    landed(order, res)
    b_st = w['odd_b_s'][0].T
    sg = _sgu_fwd(hw, full['odd_v_norm_g'], full['odd_v_norm_b'], w['odd_w_s'][0], b_st)
    y_o = _mm_nn("mm_w_out", sg, w_out_o, F32, tm)
    x5 = _rms_residual(x4, y_o, full['odd_post_g'], 1.0, tr)
    x6, ffn_b1, _ = _ffn_fwd(x5, *ffn_args('b', 1))

    dy, sq_err = _loss_head(x6, loss_target.reshape(t, d), tr)
    loss = lax.psum(0.5 * sq_err[0, 0] / d, ("x", "y", "c"))

    grads = {}
    received = {}

    def core_sums(named):
        items = [g.reshape(N_CHIPS, 2, g.shape[1] // 2, g.shape[2]) for _, g in named]
        got = _swap_halves(items)
        return [(k, _add_halves(g, r, place)) for (k, _), g, r in zip(named, items, got)]

    def scatter(named):
        return _ScatterChips([pt for _, pt in named]), named

    def arrived(order, results):
        for (k, pt), r in zip(order[1], results):
            received[k] = (pt, r)

    def ffn_backward(dout, saved, tag, layer, orders):
        dxin, g, carried = _ffn_bwd(dout, saved, *ffn_args(tag, layer), couriers={k: o[0] for k, o in orders.items()})
        for k, o in orders.items():
            arrived(o, carried[k])
        return dxin, g, core_sums(list(zip(ffn_keys(tag, layer), g[2:])))

    fg = {}
    dx, fg[('b', 1)], pend = ffn_backward(dy, ffn_b1, 'b', 1, {})
    dy_o, grads['odd_post_g'] = _rms_bwd(y_o, full['odd_post_g'], dx, 1.0, MXU_DTYPE, tr)
    dsg = _mm_nt("mm_dsg", [(dy_o, w_out_o, 0)], F32, tm)
    (g_w_out_o,) = _mm_tn("mm_dw_out", sg, [dy_o], MXU_DTYPE)
    dhw, grads['odd_v_norm_g'], grads['odd_v_norm_b'], g_ws, g_bst = _sgu_bwd(
        hw, full['odd_v_norm_g'], full['odd_v_norm_b'], w['odd_w_s'][0], b_st, dsg)
    grads['odd_w_s'] = g_ws[None]
    grads['odd_b_s'] = g_bst[:, :GROUPS].T[None]
    tn_o = _tile(d, 1024, LANES)
    order = scatter(pend[0:1])
    (dh_o,), res = _matmul("mm_dh_odd", "nt", (t // tm, d // tn_o, N_CHIPS),
                           [[(dhw, _spec((tm, ns), lambda i, j, k: (i, k)), w_in_o, _spec((None, None, tn_o, ns), lambda i, j, k: (k, 0, j, 0)))]],
                           [((t, d), F32, _spec((tm, tn_o), lambda i, j, k: (i, j)))], (tm, tn_o), courier=order[0])
    arrived(order, res)
    tmw = _tile(d, 512, LANES)
    tkt = _tile(t, 512, 16)
    order = scatter(pend[1:2])
    (g_w_in_o,), res = _matmul("mm_dw_in_odd", "tn", (d // tmw, N_CHIPS, t // tkt),
                               [[(h_o, _spec((tkt, tmw), lambda i, j, k: (k, i)), dhw, _spec((tkt, ns), lambda i, j, k: (k, j)))]],
                               [((N_CHIPS, d, ns), MXU_DTYPE, _spec((None, tmw, ns), lambda i, j, k: (j, i, 0)))], (tmw, ns),
                               courier=order[0])
    arrived(order, res)
    pend_odd = core_sums([(('odd_w_in', 0), g_w_in_o),
                          (('odd_w_out', 0), g_w_out_o.reshape((N_CHIPS,) + w['odd_w_out'].shape[1:]))])
    dx, grads['odd_pre_g'] = _rms_bwd(x4, full['odd_pre_g'], dh_o, 1.0, F32, tr, resid=dx)
    dx, fg[('a', 1)], pend = ffn_backward(dx, ffn_a1, 'a', 1, {'dact': scatter(pend[2:3]), 'dwdown': scatter(pend_odd)})
    dx, fg[('b', 0)], pend = ffn_backward(dx, ffn_b0, 'b', 0, {'dact': scatter(pend[0:1]), 'dh': scatter(pend[1:2]),
                                                              'dwup': scatter(pend[2:3])})
    dy_e, grads['even_post_g'] = _rms_bwd(y_e, w['even_post_g'], dx, 1.0, MXU_DTYPE, tr)
    d_cat = _mm_nt("mm_dcat", [(dy_e, w_out_e, 0)], F32, tm)
    (g_w_out_e,) = _mm_tn("mm_dw_out", cat, [dy_e], MXU_DTYPE)
    dqn, dqp, dkv, dkr = _attn_bwd(q_all, kv, p, cos, sin, d_cat, batch, seq)
    hq = HEADS * LANES
    d_latq = _mm_nt("mm_dlat_q", [(dqn, w_uq, 0), (dqp, w_uq, hq)], F32, tm, tn_want=Q_LORA)
    d_latkv = _mm_nt("mm_dlat_kv", [(dkv, w_ukv, 0)], F32, tm, tn_want=KV_LORA)
    g_uq_n, g_uq_r = _mm_tn("mm_dw_uq", lat, [dqn, dqp], MXU_DTYPE, a_col0=0, m_width=Q_LORA)
    (g_ukv,) = _mm_tn("mm_dw_ukv", lat, [dkv], MXU_DTYPE, a_col0=Q_LORA, m_width=KV_LORA)

    def latent_bwd(r, c):
        _, vjp_q = jax.vjp(_rmsnorm, r[0], c[0])
        _, vjp_kv = jax.vjp(_rmsnorm, r[1], c[1])
        dq, dqg = vjp_q(r[2])
        dk, dkg = vjp_kv(r[3])
        return [jnp.concatenate([dq, dk], axis=1)], [dqg, dkg]

    d_lat, grads['even_q_norm_g'], grads['even_kv_norm_g'] = _rows(
        "latent_norm_bwd", latent_bwd, [(p, Q_LORA, 0), (p, KV_LORA, 1), (d_latq, Q_LORA, 0), (d_latkv, KV_LORA, 0)],
        [w['even_q_norm_g'], w['even_kv_norm_g']], [(Q_LORA + KV_LORA, MXU_DTYPE)], [(1, Q_LORA), (1, KV_LORA)], tr)
    da, dgate, g_conv_w, grads['even_conv_b'], grads['even_conv_norm_g'], grads['even_conv_norm_b'] = _conv_bwd(
        p, d_cat, full['even_conv_w'][0], w['even_conv_b'], w['even_conv_norm_g'], w['even_conv_norm_b'], batch, seq)
    dp = jnp.concatenate([d_lat, da, dgate, dkr.astype(MXU_DTYPE)], axis=1)
    order = scatter(pend[0:1])
    dh_e, res = _mm_nt("mm_dh_even", [(dp, w_in, 0)], F32, tm, tk_want=640, courier=order[0])
    arrived(order, res)
    order = scatter(pend[1:2])
    (g_w_in,), res = _mm_tn("mm_dw_in", h_e, [dp], MXU_DTYPE, tn_want=640, courier=order[0])
    arrived(order, res)
    pend_even = core_sums([(('even_w_in', 0), _cols_to_chips(_w_in_layout_inv(g_w_in))),
                           (('even_w_uq', 0), _cols_to_chips(_w_uq_layout_inv(g_uq_n, g_uq_r))),
                           (('even_w_ukv', 0), _cols_to_chips(g_ukv)),
                           (('even_w_out', 0), g_w_out_e.reshape((N_CHIPS,) + w['even_w_out'].shape[1:]))])
    dx, grads['even_pre_g'] = _rms_bwd(x1, w['even_pre_g'], dh_e, 1.0, F32, tr, resid=dx)
    dx, fg[('a', 0)], pend = ffn_backward(dx, ffn_a0, 'a', 0, {'dact': scatter(pend[2:3]), 'dh': scatter(pend_even)})
    grad_x = dx.reshape(batch, seq, d)
    order = scatter(pend)
    arrived(order, _run_courier("scatter_chips", order[0]))

    reduced = []
    for n in BIG:
        layers = w[n].shape[0]
        buf = None
        for layer in range(layers):
            pt, r = received[(n, layer)]
            buf = _add_chips(pt, r, place, layer, layers, into=buf)
        reduced.append(buf)
    joined = _join_halves(reduced)
    for n, g in zip(BIG, joined):
        grads[n] = g.reshape(w[n].shape)

    for tag in ('a', 'b'):
        grads[f'ffn_{tag}_pre_g'] = jnp.concatenate([fg[(tag, 0)][0], fg[(tag, 1)][0]], axis=0)
        grads[f'ffn_{tag}_post_g'] = jnp.concatenate([fg[(tag, 0)][1], fg[(tag, 1)][1]], axis=0)
    grads['even_conv_w'] = g_conv_w[None]
    packed, rngs = _pack([grads[n] for n in SMALL])
    device = 2 * chip + lax.axis_index("c")
    slots = lax.dynamic_update_slice_in_dim(jnp.zeros((N_DEV,) + packed.shape, F32), packed[None], device, axis=0)
    summed = _add_devices(_gather_devices(slots))
    for n, rng in zip(SMALL, rngs):
        g = _unpack(summed, rng, grads[n].shape)
        if n in SMALL_SHARDED:
            width = w[n].shape[-1]
            g = lax.dynamic_slice_in_dim(g, chip * width, width, axis=g.ndim - 1)
        grads[n] = g

    delta, new_m, new_v = {}, {}, {}
    for n in BIG:
        delta[n], new_m[n], new_v[n] = _adamw(w[n], grads[n], given['m_' + n], given['v_' + n])
    packs = [_pack([src[n] for n in SMALL])[0][None] for src in
             (w, grads, {n: given['m_' + n] for n in SMALL}, {n: given['v_' + n] for n in SMALL})]
    _, rngs = _pack([w[n] for n in SMALL])
    small_out = _adamw(*packs)
    for n, rng in zip(SMALL, rngs):
        delta[n], new_m[n], new_v[n] = (_unpack(o[0], rng, w[n].shape) for o in small_out)

    return (loss, grad_x, *[grads[n] for n in WEIGHTS], *[delta[n] for n in WEIGHTS],
            *[new_m[n] for n in WEIGHTS], *[new_v[n] for n in WEIGHTS])
```

```python
import functools

import jax
import jax.numpy as jnp
from jax import lax
from jax.experimental import pallas as pl
from jax.experimental.pallas import tpu as pltpu

F32 = jnp.float32
BF16 = jnp.bfloat16
MXU_DTYPE = BF16
MESH = pl.DeviceIdType.MESH
VMEM_LIMIT_BYTES = 56 * 1024 * 1024
LANES = 128

D_MODEL = 2048
D_FF = 5632
EPS = 1e-6
HEADS = 8
V_HEAD = 128
QK_NOPE = 128
QK_ROPE = 64
Q_LORA = 512
KV_LORA = 512
ROPE_THETA = 10000.0
CONV_CH = 1024
CONV_WIDTH = 31
GROUPS = 8
CHUNK = 128
GM_WIDTH = 2048
ADAM_LR = 0.001
ADAM_B1 = 0.9
ADAM_B2 = 0.999
ADAM_EPS = 1e-08
ADAM_WD = 0.01
ADAM_STEP = 10

N_CHIPS = 4
N_DEV = 8
P_KV = Q_LORA
P_A = Q_LORA + KV_LORA
P_GATE = P_A + CONV_CH
P_KR = P_GATE + CONV_CH
P_WIDTH = P_KR + LANES
ROPE_HALF = QK_ROPE // 2

WEIGHTS = ['ffn_a_pre_g', 'ffn_a_post_g', 'ffn_a_w_gate', 'ffn_a_w_up', 'ffn_a_w_down', 'ffn_b_pre_g', 'ffn_b_post_g',
           'ffn_b_w_gate', 'ffn_b_w_up', 'ffn_b_w_down', 'even_pre_g', 'even_post_g', 'even_w_in', 'even_q_norm_g',
           'even_kv_norm_g', 'even_w_uq', 'even_w_ukv', 'even_conv_w', 'even_conv_b', 'even_conv_norm_g',
           'even_conv_norm_b', 'even_w_out', 'odd_pre_g', 'odd_post_g', 'odd_w_in', 'odd_v_norm_g', 'odd_v_norm_b',
           'odd_w_s', 'odd_b_s', 'odd_w_out']
BIG = ['ffn_a_w_gate', 'ffn_a_w_up', 'ffn_a_w_down', 'ffn_b_w_gate', 'ffn_b_w_up', 'ffn_b_w_down', 'even_w_in',
       'even_w_uq', 'even_w_ukv', 'even_w_out', 'odd_w_in', 'odd_w_out']
SMALL = [n for n in WEIGHTS if n not in BIG]
SMALL_SHARDED = ['even_conv_w', 'odd_pre_g', 'odd_post_g', 'odd_v_norm_g', 'odd_v_norm_b']


def _call(name, body, grid, in_specs, out_specs, out_shape, scratch=(), prefetch=0, aliases=None):
    params = pltpu.CompilerParams(dimension_semantics=("arbitrary",) * len(grid), vmem_limit_bytes=VMEM_LIMIT_BYTES)
    if prefetch:
        spec = pltpu.PrefetchScalarGridSpec(num_scalar_prefetch=prefetch, grid=grid, in_specs=in_specs,
                                            out_specs=out_specs, scratch_shapes=list(scratch))
        return pl.pallas_call(body, grid_spec=spec, out_shape=out_shape, compiler_params=params, name=name,
                              input_output_aliases=aliases or {})
    return pl.pallas_call(body, grid=grid, in_specs=in_specs, out_specs=out_specs, out_shape=out_shape,
                          scratch_shapes=list(scratch), compiler_params=params, name=name,
                          input_output_aliases=aliases or {})


def _spec(block, index_map):
    return pl.BlockSpec(block, index_map)


def _sds(shape, dtype):
    return jax.ShapeDtypeStruct(tuple(shape), dtype)


def _tile(n, want, mult=8):
    if n <= want:
        return n
    best = None
    for t in range(mult, want + 1, mult):
        if n % t == 0:
            best = t
    assert best is not None, (n, want, mult)
    return best


_DIMS = {"nn": (((1,), (0,)), ((), ())), "nt": (((1,), (1,)), ((), ())), "tn": (((0,), (0,)), ((), ()))}


def _dot(a, b, mode="nn"):
    return lax.dot_general(a.astype(MXU_DTYPE), b.astype(MXU_DTYPE), _DIMS[mode], preferred_element_type=F32)


def _matmul(name, mode, grid, groups, outs, acc_shape, extras=(), epilogue=None, courier=None):
    flat, specs = [], []
    for grp in groups:
        for a, a_spec, b, b_spec in grp:
            flat += [a, b]
            specs += [a_spec, b_spec]
    for e, e_spec in extras:
        flat.append(e)
        specs.append(e_spec)
    n_pairs = [len(g) for g in groups]
    one_step = grid[2] == 1
    n_in, n_ex, n_out, n_acc = 2 * sum(n_pairs), len(extras), len(outs), 0 if one_step else len(groups)
    last = tuple(g - 1 for g in grid)
    c_arrays = list(courier.arrays) if courier else []
    c_shapes = list(courier.out_shapes) if courier else []
    n_ci, n_co = len(c_arrays), len(c_shapes)
    any_spec = pl.BlockSpec(memory_space=pl.ANY)

    def body(*refs):
        ins, ex = refs[:n_in], refs[n_in:n_in + n_ex]
        pos = n_in + n_ex
        c_in = refs[pos:pos + n_ci]
        out_refs = refs[pos + n_ci:pos + n_ci + n_out]
        c_out = refs[pos + n_ci + n_out:pos + n_ci + n_out + n_co]
        accs = refs[pos + n_ci + n_out + n_co:pos + n_ci + n_out + n_co + n_acc]
        sems = refs[pos + n_ci + n_out + n_co + n_acc:]
        i, j, k = pl.program_id(0), pl.program_id(1), pl.program_id(2)

        if courier:
            @pl.when((i == 0) & (j == 0) & (k == 0))
            def _():
                courier.start(c_in, c_out, sems)

        def products():
            pos, totals = 0, []
            for n in n_pairs:
                total = None
                for _ in range(n):
                    d = _dot(ins[pos][...], ins[pos + 1][...], mode)
                    total = d if total is None else total + d
                    pos += 2
                totals.append(total)
            return totals

        def finish(vals):
            res = epilogue(vals, [e[...] for e in ex]) if epilogue else vals
            for o, r in zip(out_refs, res):
                o[...] = r.astype(o.dtype)

        if one_step:
            finish(products())
        else:
            @pl.when(k == 0)
            def _():
                for acc in accs:
                    acc[...] = jnp.zeros(acc.shape, F32)

            for acc, total in zip(accs, products()):
                acc[...] += total

            @pl.when(k == last[2])
            def _():
                finish([acc[...] for acc in accs])

        if courier:
            @pl.when((i == last[0]) & (j == last[1]) & (k == last[2]))
            def _():
                courier.finish(c_in, c_out, sems)

    aliases = {n_in + n_ex + t: n_out + t for t in range(n_ci)} if courier and courier.aliased else None
    scratch = [pltpu.VMEM(acc_shape, F32)] * n_acc
    scratch += [pltpu.SemaphoreType.DMA((n,)) for n in courier.sems] if courier else []
    res = _call(name, body, grid, specs + [any_spec] * n_ci, [o[2] for o in outs] + [any_spec] * n_co,
                [_sds(o[0], o[1]) for o in outs] + c_shapes, scratch=scratch, aliases=aliases)(*flat, *c_arrays)
    return list(res[:n_out]), list(res[n_out:])


def _rows(name, fn, rows, consts, outs, accs, tm):
    t_rows = rows[0][0].shape[0]
    grid = (t_rows // tm,)
    in_specs = [_spec((tm, bw), functools.partial(lambda i, cb: (i, cb), cb=cb)) for _, bw, cb in rows]
    in_specs += [_spec(c.shape, functools.partial(lambda i, nd: (0,) * nd, nd=c.ndim)) for c in consts]
    out_shape = [_sds((t_rows, w), dt) for w, dt in outs] + [_sds(s, F32) for s in accs]
    out_specs = [_spec((tm, w), lambda i: (i, 0)) for w, _ in outs]
    out_specs += [_spec(s, functools.partial(lambda i, nd: (0,) * nd, nd=len(s))) for s in accs]
    nr, nc, no = len(rows), len(consts), len(outs)

    def body(*refs):
        r = [ref[...] for ref in refs[:nr]]
        c = [ref[...] for ref in refs[nr:nr + nc]]
        o_refs, a_refs = refs[nr + nc:nr + nc + no], refs[nr + nc + no:]
        o_vals, a_vals = fn(r, c)
        for ref, v in zip(o_refs, o_vals):
            ref[...] = v.astype(ref.dtype)
        if a_refs:
            @pl.when(pl.program_id(0) == 0)
            def _():
                for ref in a_refs:
                    ref[...] = jnp.zeros(ref.shape, F32)

            for ref, v in zip(a_refs, a_vals):
                ref[...] += v

    return _call(name, body, grid, in_specs, out_specs, out_shape)(*[a for a, _, _ in rows], *consts)


def _rmsnorm(x, g):
    return x * lax.rsqrt(jnp.mean(x * x, axis=-1, keepdims=True) + EPS) * g


def _layernorm(x, g, b):
    mu = jnp.mean(x, axis=-1, keepdims=True)
    var = jnp.mean(jnp.square(x - mu), axis=-1, keepdims=True)
    return (x - mu) * lax.rsqrt(var + EPS) * g + b


def _swiglu_act(g, u):
    return jax.nn.silu(g) * u


def _rope(x, cos, sin_signed):
    return x * cos + pltpu.roll(x, 2 * ROPE_HALF, 1) * sin_signed


def _rope_transposed(dy, cos, sin_signed):
    return dy * cos - pltpu.roll(dy, 2 * ROPE_HALF, 1) * sin_signed


def _rms_fwd(x, g, tm):
    d = x.shape[1]
    return _rows("rms_fwd", lambda r, c: ([_rmsnorm(r[0], c[0])], []), [(x, d, 0)], [g], [(d, MXU_DTYPE)], [], tm)[0]


def _rms_residual(x, y, g, scale, tm):
    d = x.shape[1]
    return _rows("rms_residual", lambda r, c: ([r[0] + scale * _rmsnorm(r[1], c[0])], []),
                 [(x, d, 0), (y, d, 0)], [g], [(d, F32)], [], tm)[0]


def _rms_bwd(y, g, dout, scale, out_dtype, tm, resid=None):
    d = y.shape[1]

    def fn(r, c):
        _, vjp = jax.vjp(_rmsnorm, r[0], c[0])
        dy, dg = vjp(scale * r[1].astype(F32))
        if resid is not None:
            dy = dy + r[2]
        return [dy], [dg]

    rows = [(y, d, 0), (dout, d, 0)] + ([(resid, d, 0)] if resid is not None else [])
    return _rows("rms_bwd" if resid is None else "rms_bwd_resid", fn, rows, [g], [(d, out_dtype)], [(1, d)], tm)


def _loss_head(y, target, tm):
    d = y.shape[1]

    def fn(r, c):
        err = r[0] - r[1]
        sq = jnp.sum(jnp.sum(err * err, axis=1, keepdims=True), axis=0, keepdims=True)
        return [err * (1.0 / d)], [jnp.broadcast_to(sq, (1, LANES))]

    return _rows("loss_head", fn, [(y, d, 0), (target, d, 0)], [], [(d, F32)], [(1, LANES)], tm)


def _ffn_fwd(x, pre_g, post_g, wg, wu, wd, layer, tm, tr, couriers=None, wd_landed=None):
    t, d = x.shape
    fs = wg.shape[3]
    h = _rms_fwd(x, pre_g, tr)
    h_spec = _spec((tm, d), lambda j, i, k: (i, 0))
    w_spec = _spec((None, None, d, fs), lambda j, i, k: (j, layer, 0, 0))
    o_spec = _spec((tm, fs), lambda j, i, k: (i, j))
    couriers = couriers or {}
    carried = {}
    (g, u, a), carried["up"] = _matmul(
        "ffn_up", "nn", (N_CHIPS, t // tm, 1), [[(h, h_spec, wg, w_spec)], [(h, h_spec, wu, w_spec)]],
        [((t, N_CHIPS * fs), MXU_DTYPE, o_spec)] * 3, (tm, fs),
        epilogue=lambda accs, _: (accs[0], accs[1], _swiglu_act(accs[0], accs[1])), courier=couriers.get("up"))
    if wd_landed is not None:
        wd = carried["up"][wd_landed].reshape(wd.shape)
    (y,), carried["down"] = _matmul(
        "ffn_down", "nn", (t // tm, 1, N_CHIPS),
        [[(a, _spec((tm, fs), lambda i, j, k: (i, k)), wd, _spec((None, None, fs, d), lambda i, j, k: (k, layer, 0, 0)))]],
        [((t, d), F32, _spec((tm, d), lambda i, j, k: (i, 0)))], (tm, d), courier=couriers.get("down"))
    out = _rms_residual(x, y, post_g, 0.5, tr)
    return out, (x, h, g, u, a, y), carried


def _ffn_bwd(dout, saved, pre_g, post_g, wg, wu, wd, layer, tm, tr, couriers=None):
    x, h, g, u, a, y = saved
    t, d = x.shape
    fs = wg.shape[3]
    tk = _tile(d, 1024, LANES)
    tn = _tile(d, 1024, LANES)
    tkt = _tile(t, 512, 16)
    dy, dpost = _rms_bwd(y, post_g, dout, 0.5, MXU_DTYPE, tr)

    def act_bwd(accs, ex):
        _, vjp = jax.vjp(_swiglu_act, ex[0].astype(F32), ex[1].astype(F32))
        return vjp(accs[0])

    couriers = couriers or {}
    carried = {}
    gu_spec = _spec((tm, fs), lambda j, i, k: (i, j))
    (dg, du), carried["dact"] = _matmul(
        "ffn_dact", "nt", (N_CHIPS, t // tm, 1),
        [[(dy, _spec((tm, d), lambda j, i, k: (i, 0)), wd, _spec((None, None, fs, d), lambda j, i, k: (j, layer, 0, 0)))]],
        [((t, N_CHIPS * fs), MXU_DTYPE, gu_spec)] * 2, (tm, fs),
        extras=[(g, gu_spec), (u, gu_spec)], epilogue=act_bwd, courier=couriers.get("dact"))
    (dwd,), carried["dwdown"] = _matmul(
        "ffn_dwdown", "tn", (N_CHIPS, d // tn, t // tkt),
        [[(a, _spec((tkt, fs), lambda i, j, k: (k, i)), dy, _spec((tkt, tn), lambda i, j, k: (k, j)))]],
        [((N_CHIPS, fs, d), MXU_DTYPE, _spec((None, fs, tn), lambda i, j, k: (i, 0, j)))], (fs, tn),
        courier=couriers.get("dwdown"))
    da_spec = _spec((tm, fs), lambda i, j, k: (i, k))
    wt_spec = _spec((None, None, d, fs), lambda i, j, k: (k, layer, 0, 0))
    (dh,), carried["dh"] = _matmul(
        "ffn_dh", "nt", (t // tm, 1, N_CHIPS), [[(dg, da_spec, wg, wt_spec), (du, da_spec, wu, wt_spec)]],
        [((t, d), F32, _spec((tm, d), lambda i, j, k: (i, 0)))], (tm, d), courier=couriers.get("dh"))
    tmw = _tile(d, 512, LANES)
    h_spec = _spec((tkt, tmw), lambda i, j, k: (k, i))
    dgu_spec = _spec((tkt, fs), lambda i, j, k: (k, j))
    dw_spec = _spec((None, tmw, fs), lambda i, j, k: (j, i, 0))
    (dwg, dwu), carried["dwup"] = _matmul(
        "ffn_dwup", "tn", (d // tmw, N_CHIPS, t // tkt), [[(h, h_spec, dg, dgu_spec)], [(h, h_spec, du, dgu_spec)]],
        [((N_CHIPS, d, fs), MXU_DTYPE, dw_spec)] * 2, (tmw, fs), courier=couriers.get("dwup"))
    dx, dpre = _rms_bwd(x, pre_g, dh, 1.0, F32, tr, resid=dout)
    return dx, (dpre, dpost, dwg, dwu, dwd), carried


def _attn_scores(qn, qpe, kn, kpe, qi, tq, seq):
    s = (_dot(qn, kn, "nt") + _dot(qpe, kpe, "nt")) * ((QK_NOPE + QK_ROPE) ** -0.5)
    rows = qi * tq + lax.broadcasted_iota(jnp.int32, (tq, seq), 0)
    cols = lax.broadcasted_iota(jnp.int32, (tq, seq), 1)
    s = jnp.where(cols <= rows, s, -jnp.inf)
    e = jnp.exp(s - jnp.max(s, axis=1, keepdims=True))
    return e / jnp.sum(e, axis=1, keepdims=True)


def _attn_specs(nq, tq, seq):
    q_rows = lambda b, h, qi: b * nq + qi
    return [
        _spec((tq, LANES), lambda b, h, qi: (q_rows(b, h, qi), h)),
        _spec((tq, LANES), lambda b, h, qi: (q_rows(b, h, qi), HEADS + h)),
        _spec((seq, 2 * LANES), lambda b, h, qi: (b, h)),
        _spec((seq, LANES), lambda b, h, qi: (b, P_KR // LANES)),
        _spec((tq, LANES), lambda b, h, qi: (q_rows(b, h, qi), 0)),
        _spec((tq, LANES), lambda b, h, qi: (q_rows(b, h, qi), 0)),
        _spec((seq, LANES), lambda b, h, qi: (b, 0)),
        _spec((seq, LANES), lambda b, h, qi: (b, 0)),
    ]


def _attn_fwd(q_all, kv, p, cos, sin, batch, seq):
    t = q_all.shape[0]
    tq = _tile(seq, 256, 16)
    nq = seq // tq

    def body(qn_ref, qp_ref, kv_ref, kr_ref, cq, sq, ck, sk, o_ref):
        qpe = _rope(qp_ref[...], cq[...], sq[...])
        kpe = _rope(kr_ref[...], ck[...], sk[...])
        prob = _attn_scores(qn_ref[...], qpe, kv_ref[:, :LANES], kpe, pl.program_id(2), tq, seq)
        o_ref[...] = _dot(prob, kv_ref[:, LANES:]).astype(o_ref.dtype)

    return _call("attn_fwd", body, (batch, HEADS, nq), _attn_specs(nq, tq, seq),
                 _spec((tq, LANES), lambda b, h, qi: (b * nq + qi, h)), _sds((t, 2 * HEADS * V_HEAD), MXU_DTYPE)
                 )(q_all, q_all, kv, p, cos, sin, cos, sin)


def _attn_bwd(q_all, kv, p, cos, sin, d_cat, batch, seq):
    t = q_all.shape[0]
    tq = _tile(seq, 256, 16)
    nq = seq // tq

    def body(qn_ref, qp_ref, kv_ref, kr_ref, cq, sq, ck, sk, do_ref, dqn_ref, dqp_ref, dkv_ref, dkr_ref, dkpe_acc):
        h, qi = pl.program_id(1), pl.program_id(2)
        qn = qn_ref[...]
        qpe = _rope(qp_ref[...], cq[...], sq[...])
        kn, v = kv_ref[:, :LANES], kv_ref[:, LANES:]
        kpe = _rope(kr_ref[...], ck[...], sk[...])
        prob = _attn_scores(qn, qpe, kn, kpe, qi, tq, seq)
        do = do_ref[...]
        dprob = _dot(do, v, "nt")
        ds = prob * (dprob - jnp.sum(prob * dprob, axis=1, keepdims=True)) * ((QK_NOPE + QK_ROPE) ** -0.5)
        dqn_ref[...] = _dot(ds, kn)
        dqp_ref[...] = _rope_transposed(_dot(ds, kpe), cq[...], sq[...])

        @pl.when(qi == 0)
        def _():
            dkv_ref[...] = jnp.zeros(dkv_ref.shape, F32)

        dkv_ref[:, :LANES] += _dot(ds, qn, "tn")
        dkv_ref[:, LANES:] += _dot(prob, do, "tn")

        @pl.when((h == 0) & (qi == 0))
        def _():
            dkpe_acc[...] = jnp.zeros(dkpe_acc.shape, F32)

        dkpe_acc[...] += _dot(ds, qpe, "tn")

        @pl.when((h == HEADS - 1) & (qi == nq - 1))
        def _():
            dkr_ref[...] = _rope_transposed(dkpe_acc[...], ck[...], sk[...])

    q_out = _spec((tq, LANES), lambda b, h, qi: (b * nq + qi, h))
    return _call("attn_bwd", body, (batch, HEADS, nq),
                 _attn_specs(nq, tq, seq) + [_spec((tq, LANES), lambda b, h, qi: (b * nq + qi, h))],
                 [q_out, q_out, _spec((seq, 2 * LANES), lambda b, h, qi: (b, h)), _spec((seq, LANES), lambda b, h, qi: (b, 0))],
                 [_sds((t, HEADS * LANES), F32), _sds((t, HEADS * LANES), F32), _sds((t, HEADS * 2 * LANES), F32), _sds((t, LANES), F32)],
                 scratch=[pltpu.VMEM((seq, LANES), F32)])(q_all, q_all, kv, p, cos, sin, cos, sin, d_cat)


CONV_PAD = 32


def _conv_taps(w_ref, src_ref, first_row, n_rows, init, offset):
    acc = init
    for k in range(CONV_WIDTH):
        acc = acc + w_ref[k:k + 1, :] * src_ref[pl.ds(first_row + offset(k), n_rows), :]
    return acc


def _norm_act(conv, g, b):
    return jax.nn.silu(_layernorm(conv, g, b))


def _conv_fwd(p, cat, conv_w, conv_b, norm_g, norm_b, batch, seq):
    rc = _tile(seq, 256, 8)
    a_blk, gate_blk, out_blk = P_A // LANES, P_GATE // LANES, (HEADS * V_HEAD) // LANES

    def body(a_ref, gate_ref, w_ref, cb_ref, ng_ref, nb_ref, cat_in, o_ref, zp):
        del cat_in
        zp[pl.ds(0, CONV_PAD), :] = jnp.zeros((CONV_PAD, LANES), F32)
        zp[pl.ds(CONV_PAD, seq), :] = a_ref[...] * jax.nn.sigmoid(gate_ref[...])
        for r0 in range(0, seq, rc):
            conv = _conv_taps(w_ref, zp, r0, rc, jnp.broadcast_to(cb_ref[...], (rc, LANES)),
                              lambda k: CONV_PAD - (CONV_WIDTH - 1) + k)
            o_ref[pl.ds(r0, rc), :] = _norm_act(conv, ng_ref[...], nb_ref[...]).astype(o_ref.dtype)

    vec = _spec((1, LANES), lambda b, g: (0, g))
    return _call("conv_fwd", body, (batch, GROUPS),
                 [_spec((seq, LANES), lambda b, g: (b, a_blk + g)), _spec((seq, LANES), lambda b, g: (b, gate_blk + g)),
                  _spec((CONV_WIDTH, LANES), lambda b, g: (0, g)), vec, vec, vec, pl.BlockSpec(memory_space=pl.ANY)],
                 _spec((seq, LANES), lambda b, g: (b, out_blk + g)), _sds(cat.shape, cat.dtype),
                 scratch=[pltpu.VMEM((seq + CONV_PAD, LANES), F32)], aliases={6: 0}
                 )(p, p, conv_w, conv_b, norm_g, norm_b, cat)


def _conv_bwd(p, d_cat, conv_w, conv_b, norm_g, norm_b, batch, seq):
    t = p.shape[0]
    rc = _tile(seq, 256, 8)
    a_blk, gate_blk, out_blk = P_A // LANES, P_GATE // LANES, (HEADS * V_HEAD) // LANES

    def body(a_ref, gate_ref, w_ref, cb_ref, ng_ref, nb_ref, dc_ref, da_ref, dgate_ref, dw_ref, db_ref, dng_ref, dnb_ref,
             zp, dcp, buf):
        b = pl.program_id(1)
        a, sig = a_ref[...], jax.nn.sigmoid(gate_ref[...])
        zp[pl.ds(0, CONV_PAD), :] = jnp.zeros((CONV_PAD, LANES), F32)
        zp[pl.ds(CONV_PAD, seq), :] = a * sig
        for r0 in range(0, seq, rc):
            buf[pl.ds(r0, rc), :] = _conv_taps(w_ref, zp, r0, rc, jnp.broadcast_to(cb_ref[...], (rc, LANES)),
                                               lambda k: CONV_PAD - (CONV_WIDTH - 1) + k)
        _, vjp = jax.vjp(_norm_act, buf[...], ng_ref[...], nb_ref[...])
        dconv, dng, dnb = vjp(dc_ref[...].astype(F32))
        dcp[pl.ds(0, seq), :] = dconv
        dcp[pl.ds(seq, CONV_PAD), :] = jnp.zeros((CONV_PAD, LANES), F32)

        @pl.when(b == 0)
        def _():
            dw_ref[...] = jnp.zeros(dw_ref.shape, F32)
            db_ref[...] = jnp.zeros(db_ref.shape, F32)
            dng_ref[...] = jnp.zeros(dng_ref.shape, F32)
            dnb_ref[...] = jnp.zeros(dnb_ref.shape, F32)

        db_ref[...] += jnp.sum(dconv, axis=0, keepdims=True)
        dng_ref[...] += dng
        dnb_ref[...] += dnb
        for k in range(CONV_WIDTH):
            shifted = zp[pl.ds(CONV_PAD - (CONV_WIDTH - 1) + k, seq), :]
            dw_ref[k:k + 1, :] += jnp.sum(dconv * shifted, axis=0, keepdims=True)
        for r0 in range(0, seq, rc):
            buf[pl.ds(r0, rc), :] = _conv_taps(w_ref, dcp, r0, rc, jnp.zeros((rc, LANES), F32),
                                               lambda k: CONV_WIDTH - 1 - k)
        dz = buf[...]
        da_ref[...] = (dz * sig).astype(da_ref.dtype)
        dgate_ref[...] = (dz * a * sig * (1.0 - sig)).astype(dgate_ref.dtype)

    vec = _spec((1, LANES), lambda g, b: (0, g))
    row_out = _spec((seq, LANES), lambda g, b: (b, g))
    return _call("conv_bwd", body, (GROUPS, batch),
                 [_spec((seq, LANES), lambda g, b: (b, a_blk + g)), _spec((seq, LANES), lambda g, b: (b, gate_blk + g)),
                  _spec((CONV_WIDTH, LANES), lambda g, b: (0, g)), vec, vec, vec,
                  _spec((seq, LANES), lambda g, b: (b, out_blk + g))],
                 [row_out, row_out, _spec((CONV_WIDTH, LANES), lambda g, b: (0, g)), vec, vec, vec],
                 [_sds((t, CONV_CH), MXU_DTYPE), _sds((t, CONV_CH), MXU_DTYPE), _sds((CONV_WIDTH, CONV_CH), F32),
                  _sds((1, CONV_CH), F32), _sds((1, CONV_CH), F32), _sds((1, CONV_CH), F32)],
                 scratch=[pltpu.VMEM((seq + CONV_PAD, LANES), F32), pltpu.VMEM((seq + CONV_PAD, LANES), F32),
                          pltpu.VMEM((seq, LANES), F32)])(p, p, conv_w, conv_b, norm_g, norm_b, d_cat)


def _sgu_pre(hw, g, b):
    z = jax.nn.gelu(hw)
    return z[:, :GM_WIDTH], _layernorm(z[:, GM_WIDTH:], g, b)


def _causal(w):
    keep = lax.broadcasted_iota(jnp.int32, (CHUNK, CHUNK), 0) >= lax.broadcasted_iota(jnp.int32, (CHUNK, CHUNK), 1)
    return jnp.where(keep, w, 0.0)


def _sgu_fwd(hw, vg, vb, w_s, b_st):
    t = hw.shape[0]
    gw = GM_WIDTH // GROUPS

    def body(hw_ref, vg_ref, vb_ref, w_ref, b_ref, o_ref):
        u, v = _sgu_pre(hw_ref[...], vg_ref[...], vb_ref[...])
        for g in range(GROUPS):
            cols = slice(g * gw, (g + 1) * gw)
            s = _dot(_causal(w_ref[g]), v[:, cols]) + b_ref[:, g:g + 1]
            o_ref[:, cols] = (u[:, cols] * s).astype(o_ref.dtype)

    return _call("sgu_fwd", body, (t // CHUNK,),
                 [_spec((CHUNK, 2 * GM_WIDTH), lambda i: (i, 0)), _spec((1, GM_WIDTH), lambda i: (0, 0)),
                  _spec((1, GM_WIDTH), lambda i: (0, 0)), _spec((GROUPS, CHUNK, CHUNK), lambda i: (0, 0, 0)),
                  _spec((CHUNK, GROUPS), lambda i: (0, 0))],
                 _spec((CHUNK, GM_WIDTH), lambda i: (i, 0)), _sds((t, GM_WIDTH), MXU_DTYPE))(hw, vg, vb, w_s, b_st)


def _sgu_bwd(hw, vg, vb, w_s, b_st, dout):
    t = hw.shape[0]
    gw = GM_WIDTH // GROUPS

    def body(hw_ref, vg_ref, vb_ref, w_ref, b_ref, do_ref, dhw_ref, dvg_ref, dvb_ref, dw_ref, db_ref, du_buf, dv_buf):
        (u, v), vjp = jax.vjp(_sgu_pre, hw_ref[...], vg_ref[...], vb_ref[...])

        @pl.when(pl.program_id(0) == 0)
        def _():
            dvg_ref[...] = jnp.zeros(dvg_ref.shape, F32)
            dvb_ref[...] = jnp.zeros(dvb_ref.shape, F32)
            dw_ref[...] = jnp.zeros(dw_ref.shape, F32)
            db_ref[...] = jnp.zeros(db_ref.shape, F32)

        lane = lax.broadcasted_iota(jnp.int32, (CHUNK, LANES), 1)
        db = jnp.zeros((CHUNK, LANES), F32)
        for g in range(GROUPS):
            cols = slice(g * gw, (g + 1) * gw)
            w = _causal(w_ref[g])
            s = _dot(w, v[:, cols]) + b_ref[:, g:g + 1]
            do = do_ref[:, cols]
            ds = do * u[:, cols]
            du_buf[:, cols] = do * s
            dw_ref[g] += _causal(_dot(ds, v[:, cols], "nt"))
            dv_buf[:, cols] = _dot(w, ds, "tn")
            db = db + jnp.where(lane == g, jnp.sum(ds, axis=1, keepdims=True), 0.0)
        db_ref[...] += db
        dhw, dvg, dvb = vjp((du_buf[...], dv_buf[...]))
        dhw_ref[...] = dhw.astype(dhw_ref.dtype)
        dvg_ref[...] += dvg
        dvb_ref[...] += dvb

    vec = _spec((1, GM_WIDTH), lambda i: (0, 0))
    w_spec = _spec((GROUPS, CHUNK, CHUNK), lambda i: (0, 0, 0))
    return _call("sgu_bwd", body, (t // CHUNK,),
                 [_spec((CHUNK, 2 * GM_WIDTH), lambda i: (i, 0)), vec, vec, w_spec, _spec((CHUNK, GROUPS), lambda i: (0, 0)),
                  _spec((CHUNK, GM_WIDTH), lambda i: (i, 0))],
                 [_spec((CHUNK, 2 * GM_WIDTH), lambda i: (i, 0)), vec, vec, w_spec, _spec((CHUNK, LANES), lambda i: (0, 0))],
                 [_sds((t, 2 * GM_WIDTH), MXU_DTYPE), _sds((1, GM_WIDTH), F32), _sds((1, GM_WIDTH), F32),
                  _sds((GROUPS, CHUNK, CHUNK), F32), _sds((CHUNK, LANES), F32)],
                 scratch=[pltpu.VMEM((CHUNK, GM_WIDTH), F32), pltpu.VMEM((CHUNK, GM_WIDTH), F32)]
                 )(hw, vg, vb, w_s, b_st, dout)


def _mm_nn(name, a, b, out_dtype, tm, a_col0=0, tn_want=1024, tk_want=1024, courier=None):
    t = a.shape[0]
    kk, n = b.shape
    tk = _tile(kk, tk_want, LANES)
    tn = _tile(n, tn_want, LANES)
    k0 = a_col0 // tk
    assert a_col0 % tk == 0
    outs, carried = _matmul(
        name, "nn", (t // tm, n // tn, kk // tk),
        [[(a, _spec((tm, tk), lambda i, j, k: (i, k0 + k)), b, _spec((tk, tn), lambda i, j, k: (k, j)))]],
        [((t, n), out_dtype, _spec((tm, tn), lambda i, j, k: (i, j)))], (tm, tn), courier=courier)
    return (outs[0], carried) if courier else outs[0]


def _mm_nt(name, pairs, out_dtype, tm, tn_want=1024, tk_want=1024, courier=None):
    t, kk = pairs[0][0].shape
    n = pairs[0][1].shape[0]
    tk = _tile(kk, tk_want, LANES)
    tn = _tile(n, tn_want, 16)
    grp = []
    for a, b, b_col0 in pairs:
        assert b_col0 % tk == 0 and a.shape == (t, kk)
        grp.append((a, _spec((tm, tk), lambda i, j, k: (i, k)), b,
                    _spec((tn, tk), functools.partial(lambda i, j, k, k0: (j, k0 + k), k0=b_col0 // tk))))
    outs, carried = _matmul(name, "nt", (t // tm, n // tn, kk // tk), [grp],
                            [((t, n), out_dtype, _spec((tm, tn), lambda i, j, k: (i, j)))], (tm, tn), courier=courier)
    return (outs[0], carried) if courier else outs[0]


def _mm_tn(name, a, bs, out_dtype, a_col0=0, m_width=None, tm_want=512, tn_want=1024, courier=None):
    t = a.shape[0]
    m = m_width or a.shape[1]
    n = bs[0].shape[1]
    tmw = _tile(m, tm_want, LANES)
    tn = _tile(n, tn_want, LANES)
    tkt = _tile(t, 512, 16)
    assert a_col0 % tmw == 0
    i0 = a_col0 // tmw
    a_spec = _spec((tkt, tmw), lambda i, j, k: (k, i0 + i))
    groups = [[(a, a_spec, b, _spec((tkt, tn), lambda i, j, k: (k, j)))] for b in bs]
    outs, carried = _matmul(name, "tn", (m // tmw, n // tn, t // tkt), groups,
                            [((m, n), out_dtype, _spec((tmw, tn), lambda i, j, k: (i, j)))] * len(bs), (tmw, tn),
                            courier=courier)
    return (outs, carried) if courier else outs


_ANY = pl.BlockSpec(memory_space=pl.ANY)


def _comm_call(name, body, ins, out_shapes, sems, aliases=None):
    return pl.pallas_call(body, name=name, out_shape=out_shapes, in_specs=[_ANY] * len(ins),
                          out_specs=[_ANY] * len(out_shapes), input_output_aliases=aliases or {},
                          scratch_shapes=[pltpu.SemaphoreType.DMA((n,)) for n in sems])(*ins)


def _place():
    return lax.axis_index("x"), lax.axis_index("y"), lax.axis_index("c")


def _other_chips(x, y):
    return [(1 - x, y), (x, 1 - y), (1 - x, 1 - y)]


def _cast_into_slot(w, place, dtype):
    layers, rows, cols = w.shape
    half = rows // 2
    tr = _tile(half, 256, 16)
    nt = half // tr

    def body(s, w_ref, o_ref):
        del s
        o_ref[...] = w_ref[...].astype(o_ref.dtype)

    return _call("cast_into_slot", body, (layers, 2, nt),
                 [_spec((None, tr, cols), lambda l, h, i, s: (l, h * nt + i, 0))],
                 _spec((None, None, None, tr, cols), lambda l, h, i, s: (s[0], l, h, i, 0)),
                 _sds((N_CHIPS, layers, 2, half, cols), dtype), prefetch=1)(place, w)


class _GatherChips:
    aliased = True

    def __init__(self, bufs):
        self.arrays = list(bufs)
        self.out_shapes = [_sds(b.shape, b.dtype) for b in bufs]
        self.sems = [3 * len(bufs)] * 4

    def _sent(self, buf, sems):
        x, y, c = _place()
        me = 2 * x + y
        return [[pltpu.make_async_remote_copy(
            src_ref=buf[t].at[me, :, c], dst_ref=buf[t].at[me, :, c], send_sem=sems[0].at[3 * t + r],
            recv_sem=sems[1].at[3 * t + r], device_id=(px, py, c), device_id_type=MESH)
            for r, (px, py) in enumerate(_other_chips(x, y))] for t in range(len(buf))]

    def start(self, _, buf, sems):
        for row in self._sent(buf, sems):
            for cp in row:
                cp.start()

    def finish(self, _, buf, sems):
        x, y, c = _place()
        sent = self._sent(buf, sems)
        passed = []
        for t in range(len(buf)):
            for r, (px, py) in enumerate(_other_chips(x, y)):
                landed = buf[t].at[2 * px + py, :, c]
                sent[t][r].wait_recv()
                cp = pltpu.make_async_remote_copy(
                    src_ref=landed, dst_ref=landed, send_sem=sems[2].at[3 * t + r], recv_sem=sems[3].at[3 * t + r],
                    device_id=(x, y, 1 - c), device_id_type=MESH)
                cp.start()
                passed.append(cp)
        for cp in passed:
            cp.wait()
        for row in sent:
            for cp in row:
                cp.wait_send()


class _ScatterChips:
    aliased = False

    def __init__(self, partials):
        self.arrays = list(partials)
        self.out_shapes = [_sds((3,) + p.shape[1:], p.dtype) for p in partials]
        self.sems = [3 * len(partials)] * 2

    def _copies(self, src, dst, sems):
        x, y, c = _place()
        return [pltpu.make_async_remote_copy(
            src_ref=src[t].at[2 * px + py], dst_ref=dst[t].at[r], send_sem=sems[0].at[3 * t + r],
            recv_sem=sems[1].at[3 * t + r], device_id=(px, py, c), device_id_type=MESH)
            for t in range(len(src)) for r, (px, py) in enumerate(_other_chips(x, y))]

    def start(self, src, dst, sems):
        for cp in self._copies(src, dst, sems):
            cp.start()

    def finish(self, src, dst, sems):
        for cp in self._copies(src, dst, sems):
            cp.wait()


def _run_courier(name, courier):
    n_in, n_out = len(courier.arrays), len(courier.out_shapes)

    def body(*refs):
        src, dst, sems = refs[:n_in], refs[n_in:n_in + n_out], refs[n_in + n_out:]
        courier.start(src, dst, sems)
        courier.finish(src, dst, sems)

    return _comm_call(name, body, courier.arrays, courier.out_shapes, courier.sems,
                      aliases={t: t for t in range(n_in)} if courier.aliased else None)


def _swap_halves(grads):
    n = len(grads)

    def body(*refs):
        src, dst = refs[:n], refs[n:2 * n]
        send_sems, recv_sems = refs[2 * n:]
        x, y, c = _place()
        copies = [pltpu.make_async_remote_copy(
            src_ref=src[t].at[:, 1 - c], dst_ref=dst[t], send_sem=send_sems.at[t], recv_sem=recv_sems.at[t],
            device_id=(x, y, 1 - c), device_id_type=MESH) for t in range(n)]
        for cp in copies:
            cp.start()
        for cp in copies:
            cp.wait()

    return _comm_call("swap_halves", body, grads, [_sds((g.shape[0],) + g.shape[2:], g.dtype) for g in grads], [n, n])


def _join_halves(grads):
    n = len(grads)

    def body(*refs):
        buf = refs[n:2 * n]
        send_sems, recv_sems = refs[2 * n:]
        x, y, c = _place()
        copies = [pltpu.make_async_remote_copy(
            src_ref=buf[t].at[:, c], dst_ref=buf[t].at[:, c], send_sem=send_sems.at[t], recv_sem=recv_sems.at[t],
            device_id=(x, y, 1 - c), device_id_type=MESH) for t in range(n)]
        for cp in copies:
            cp.start()
        for cp in copies:
            cp.wait()

    return _comm_call("join_halves", body, grads, [_sds(g.shape, g.dtype) for g in grads], [n, n],
                      aliases={t: t for t in range(n)})


def _gather_devices(slots):
    def body(src, buf, send_sems, recv_sems):
        del src
        x, y, c = _place()
        me = 4 * x + 2 * y + c
        copies = []
        for r in range(1, N_DEV):
            fx, fy, fc = (r >> 2) & 1, (r >> 1) & 1, r & 1
            peer = (1 - x if fx else x, 1 - y if fy else y, 1 - c if fc else c)
            copies.append(pltpu.make_async_remote_copy(
                src_ref=buf.at[me], dst_ref=buf.at[me], send_sem=send_sems.at[r - 1], recv_sem=recv_sems.at[r - 1],
                device_id=peer, device_id_type=MESH))
        for cp in copies:
            cp.start()
        for cp in copies:
            cp.wait()

    return _comm_call("gather_devices", body, [slots], [_sds(slots.shape, slots.dtype)], [N_DEV - 1, N_DEV - 1],
                      aliases={0: 0})[0]


def _add_halves(grad, got, place):
    _, _, rows, cols = grad.shape
    tr = _tile(rows, 256, 16)

    def body(s, a, b, o):
        del s
        o[...] = (a[...].astype(F32) + b[...].astype(F32)).astype(o.dtype)

    return _call("add_halves", body, (N_CHIPS, rows // tr),
                 [_spec((None, None, tr, cols), lambda j, i, s: (j, s[1], i, 0)), _spec((None, tr, cols), lambda j, i, s: (j, i, 0))],
                 _spec((None, tr, cols), lambda j, i, s: (j, i, 0)), _sds((N_CHIPS, rows, cols), grad.dtype), prefetch=1
                 )(place, grad, got)


def _add_chips(partial, got, place, layer, layers, into=None):
    _, rows, cols = partial.shape
    tr = _tile(rows, 256, 16)

    def body(s, a, b, *rest):
        o = rest[-1]
        o[...] = a[...].astype(F32) + b[0].astype(F32) + b[1].astype(F32) + b[2].astype(F32)

    in_specs = [_spec((None, tr, cols), lambda i, s: (s[0], i, 0)), _spec((3, tr, cols), lambda i, s: (0, i, 0))]
    args = [place, partial, got]
    if into is not None:
        in_specs.append(pl.BlockSpec(memory_space=pl.ANY))
        args.append(into)
    return _call("add_chips", body, (rows // tr,), in_specs,
                 _spec((None, None, tr, cols), lambda i, s: (layer, s[1], i, 0)), _sds((layers, 2, rows, cols), F32),
                 prefetch=1, aliases={3: 0} if into is not None else None)(*args)


def _add_devices(got):
    _, rows, cols = got.shape
    tr = _tile(rows, 512, 8)

    def body(a, o):
        total = a[0]
        for d in range(1, N_DEV):
            total = total + a[d]
        o[...] = total

    return _call("add_devices", body, (rows // tr,), [_spec((N_DEV, tr, cols), lambda i: (0, i, 0))],
                 _spec((tr, cols), lambda i: (i, 0)), _sds((rows, cols), F32))(got)


def _adamw(w, g, m, v):
    layers, rows, cols = w.shape
    tr = _tile(rows, 256, 8)

    def body(w_ref, g_ref, m_ref, v_ref, d_ref, nm_ref, nv_ref):
        grad = g_ref[...]
        new_m = ADAM_B1 * m_ref[...] + (1.0 - ADAM_B1) * grad
        new_v = ADAM_B2 * v_ref[...] + (1.0 - ADAM_B2) * jnp.square(grad)
        m_hat = new_m / (1.0 - ADAM_B1 ** ADAM_STEP)
        v_hat = new_v / (1.0 - ADAM_B2 ** ADAM_STEP)
        d_ref[...] = -ADAM_LR * (m_hat / (jnp.sqrt(v_hat) + ADAM_EPS) + ADAM_WD * w_ref[...])
        nm_ref[...] = new_m
        nv_ref[...] = new_v

    blk = _spec((None, tr, cols), lambda l, i: (l, i, 0))
    return _call("adamw", body, (layers, rows // tr), [blk] * 4, [blk] * 3, [_sds(w.shape, F32)] * 3)(w, g, m, v)


def _rope_layout(w):
    z = jnp.zeros(w.shape[:-1] + (ROPE_HALF,), w.dtype)
    return jnp.concatenate([w[..., :ROPE_HALF], z, w[..., ROPE_HALF:], z], axis=-1)


def _rope_layout_inv(w):
    return jnp.concatenate([w[..., :ROPE_HALF], w[..., 2 * ROPE_HALF:3 * ROPE_HALF]], axis=-1)


def _cols_from_chips(g):
    return jnp.moveaxis(g, 0, 1).reshape(g.shape[1], N_CHIPS * g.shape[2])


def _cols_to_chips(w):
    return jnp.moveaxis(w.reshape(w.shape[0], N_CHIPS, w.shape[1] // N_CHIPS), 1, 0)


def _w_in_layout(w):
    off_kr = Q_LORA + KV_LORA
    return jnp.concatenate([w[:, :off_kr], w[:, off_kr + QK_ROPE:], _rope_layout(w[:, off_kr:off_kr + QK_ROPE])], axis=1)


def _w_in_layout_inv(dw):
    return jnp.concatenate([dw[:, :P_A], _rope_layout_inv(dw[:, P_KR:]), dw[:, P_A:P_KR]], axis=1)


def _w_uq_layout(w):
    w = w.reshape(Q_LORA, HEADS, QK_NOPE + QK_ROPE)
    return jnp.concatenate([w[..., :QK_NOPE].reshape(Q_LORA, HEADS * QK_NOPE),
                            _rope_layout(w[..., QK_NOPE:]).reshape(Q_LORA, HEADS * LANES)], axis=1)


def _w_uq_layout_inv(d_nope, d_rope):
    d_nope = d_nope.reshape(Q_LORA, HEADS, QK_NOPE)
    d_rope = _rope_layout_inv(d_rope.reshape(Q_LORA, HEADS, LANES))
    return jnp.concatenate([d_nope, d_rope], axis=-1).reshape(Q_LORA, HEADS * (QK_NOPE + QK_ROPE))


def _to_lanes(a):
    flat = a.reshape(-1)
    pad = (-flat.shape[0]) % LANES
    if pad:
        flat = jnp.concatenate([flat, jnp.zeros((pad,), flat.dtype)])
    return flat.reshape(-1, LANES)


PACK_ROWS = 64


def _pack(arrays):
    parts, ranges, row = [], [], 0
    for a in arrays:
        p = _to_lanes(a)
        parts.append(p)
        ranges.append((row, row + p.shape[0]))
        row += p.shape[0]
    pad = (-row) % PACK_ROWS
    if pad:
        parts.append(jnp.zeros((pad, LANES), F32))
    return jnp.concatenate(parts, axis=0), ranges


def _unpack(packed, rng, shape):
    n = 1
    for s in shape:
        n *= s
    return packed[rng[0]:rng[1]].reshape(-1)[:n].reshape(shape)


def kernel(x, positions, ffn_a_pre_g, ffn_a_post_g, ffn_a_w_gate, ffn_a_w_up, ffn_a_w_down, ffn_b_pre_g, ffn_b_post_g, ffn_b_w_gate, ffn_b_w_up, ffn_b_w_down, even_pre_g, even_post_g, even_w_in, even_q_norm_g, even_kv_norm_g, even_w_uq, even_w_ukv, even_conv_w, even_conv_b, even_conv_norm_g, even_conv_norm_b, even_w_out, odd_pre_g, odd_post_g, odd_w_in, odd_v_norm_g, odd_v_norm_b, odd_w_s, odd_b_s, odd_w_out, loss_target, m_ffn_a_pre_g, m_ffn_a_post_g, m_ffn_a_w_gate, m_ffn_a_w_up, m_ffn_a_w_down, m_ffn_b_pre_g, m_ffn_b_post_g, m_ffn_b_w_gate, m_ffn_b_w_up, m_ffn_b_w_down, m_even_pre_g, m_even_post_g, m_even_w_in, m_even_q_norm_g, m_even_kv_norm_g, m_even_w_uq, m_even_w_ukv, m_even_conv_w, m_even_conv_b, m_even_conv_norm_g, m_even_conv_norm_b, m_even_w_out, m_odd_pre_g, m_odd_post_g, m_odd_w_in, m_odd_v_norm_g, m_odd_v_norm_b, m_odd_w_s, m_odd_b_s, m_odd_w_out, v_ffn_a_pre_g, v_ffn_a_post_g, v_ffn_a_w_gate, v_ffn_a_w_up, v_ffn_a_w_down, v_ffn_b_pre_g, v_ffn_b_post_g, v_ffn_b_w_gate, v_ffn_b_w_up, v_ffn_b_w_down, v_even_pre_g, v_even_post_g, v_even_w_in, v_even_q_norm_g, v_even_kv_norm_g, v_even_w_uq, v_even_w_ukv, v_even_conv_w, v_even_conv_b, v_even_conv_norm_g, v_even_conv_norm_b, v_even_w_out, v_odd_pre_g, v_odd_post_g, v_odd_w_in, v_odd_v_norm_g, v_odd_v_norm_b, v_odd_w_s, v_odd_b_s, v_odd_w_out):
    given = dict(locals())
    w = {n: given[n] for n in WEIGHTS}
    batch, seq, d = x.shape
    t = batch * seq
    tm = _tile(t, 512, 16)
    tr = _tile(t, 256, 16)
    chip = 2 * lax.axis_index("x") + lax.axis_index("y")
    place = jnp.stack([chip, lax.axis_index("c")]).astype(jnp.int32)

    small_shard, small_shard_rng = _pack([w[n] for n in SMALL_SHARDED])
    wbuf = {(n, layer): _cast_into_slot(w[n][layer:layer + 1], place, MXU_DTYPE)
            for n in BIG for layer in range(w[n].shape[0])}
    wbuf['small'] = _cast_into_slot(small_shard[None], place, F32)

    def ffn_keys(tag, layer):
        return [(f'ffn_{tag}_w_{part}', layer) for part in ('gate', 'up', 'down')]

    def gather(keys):
        return _GatherChips([wbuf[k] for k in keys]), keys

    def landed(order, results):
        for k, r in zip(order[1], results):
            wbuf[k] = r

    def weight(key):
        n, _ = key
        return wbuf[key].reshape((N_CHIPS, 1) + w[n].shape[1:])

    first = gather(ffn_keys('a', 0)[:2] + ['small'])
    landed(first, _run_courier("gather_chips", first[0]))
    shards = wbuf['small'].reshape((N_CHIPS,) + small_shard.shape)
    full = dict(w)
    for n, rng in zip(SMALL_SHARDED, small_shard_rng):
        per_chip = [_unpack(shards[j], rng, w[n].shape) for j in range(N_CHIPS)]
        full[n] = jnp.concatenate(per_chip, axis=-1)

    inv_freq = ROPE_THETA ** (-jnp.arange(0, QK_ROPE, 2, dtype=F32) / QK_ROPE)
    ang = positions.astype(F32).reshape(t, 1) * inv_freq
    zeros = jnp.zeros((t, ROPE_HALF), F32)
    cos = jnp.concatenate([jnp.cos(ang), zeros, jnp.cos(ang), zeros], axis=1)
    sin = jnp.concatenate([-jnp.sin(ang), zeros, jnp.sin(ang), zeros], axis=1)

    def ffn_args(tag, layer):
        gate, up, down = (weight(k) for k in ffn_keys(tag, layer))
        return (w[f'ffn_{tag}_pre_g'][layer:layer + 1], w[f'ffn_{tag}_post_g'][layer:layer + 1], gate, up, down, 0, tm, tr)

    def ffn_forward(xin, tag, layer, up_keys, down_keys):
        orders = {'up': gather(ffn_keys(tag, layer)[2:] + up_keys)}
        if down_keys:
            orders['down'] = gather(down_keys)
        out, saved, carried = _ffn_fwd(xin, *ffn_args(tag, layer), couriers={k: o[0] for k, o in orders.items()}, wd_landed=0)
        for k, o in orders.items():
            landed(o, carried[k])
        return out, saved

    even_keys = [('even_w_in', 0), ('even_w_uq', 0), ('even_w_ukv', 0), ('even_w_out', 0)]
    odd_keys = [('odd_w_in', 0), ('odd_w_out', 0)]

    xs = x.reshape(t, d)
    x1, ffn_a0 = ffn_forward(xs, 'a', 0, even_keys, ffn_keys('b', 0)[:1])
    w_in = _w_in_layout(_cols_from_chips(weight(('even_w_in', 0))[:, 0]))
    w_uq = _w_uq_layout(_cols_from_chips(weight(('even_w_uq', 0))[:, 0]))
    w_ukv = _cols_from_chips(weight(('even_w_ukv', 0))[:, 0])
    w_out_e = weight(('even_w_out', 0))[:, 0].reshape(d, d)
    h_e = _rms_fwd(x1, w['even_pre_g'], tr)
    order = gather(ffn_keys('b', 0)[1:2])
    p, res = _mm_nn("mm_w_in", h_e, w_in, F32, tm, tn_want=640, courier=order[0])
    landed(order, res)
    lat = _rows("latent_norm", lambda r, c: ([jnp.concatenate([_rmsnorm(r[0], c[0]), _rmsnorm(r[1], c[1])], axis=1)], []),
                [(p, Q_LORA, 0), (p, KV_LORA, 1)], [w['even_q_norm_g'], w['even_kv_norm_g']], [(Q_LORA + KV_LORA, MXU_DTYPE)], [], tr)[0]
    q_all = _mm_nn("mm_w_uq", lat, w_uq, F32, tm, a_col0=0, tk_want=Q_LORA)
    kv = _mm_nn("mm_w_ukv", lat, w_ukv, MXU_DTYPE, tm, a_col0=Q_LORA, tk_want=KV_LORA)
    cat = _attn_fwd(q_all, kv, p, cos, sin, batch, seq)
    cat = _conv_fwd(p, cat, full['even_conv_w'][0], w['even_conv_b'], w['even_conv_norm_g'], w['even_conv_norm_b'], batch, seq)
    y_e = _mm_nn("mm_w_out", cat, w_out_e, F32, tm)
    x2 = _rms_residual(x1, y_e, w['even_post_g'], 1.0, tr)
    x3, ffn_b0 = ffn_forward(x2, 'b', 0, ffn_keys('a', 1)[:1], ffn_keys('a', 1)[1:2])
    x4, ffn_a1 = ffn_forward(x3, 'a', 1, odd_keys, ffn_keys('b', 1)[:1])
    w_out_o = weight(('odd_w_out', 0))[:, 0].reshape(GM_WIDTH, d)
    w_in_o = weight(('odd_w_in', 0))
    h_o = _rms_fwd(x4, full['odd_pre_g'], tr)
    ns = w_in_o.shape[3]
    tk_o = _tile(d, 1024, LANES)
    order = gather(ffn_keys('b', 1)[1:2])
    (hw,), res = _matmul("mm_w_in_odd", "nn", (t // tm, N_CHIPS, d // tk_o),
                         [[(h_o, _spec((tm, tk_o), lambda i, j, k: (i, k)), w_in_o, _spec((None, None, tk_o, ns), lambda i, j, k: (j, 0, k, 0)))]],
                         [((t, N_CHIPS * ns), F32, _spec((tm, ns), lambda i, j, k: (i, j)))], (tm, ns), courier=order[0])
    landed(order, res)
    b_st = w['odd_b_s'][0].T
    sg = _sgu_fwd(hw, full['odd_v_norm_g'], full['odd_v_norm_b'], w['odd_w_s'][0], b_st)
    y_o = _mm_nn("mm_w_out", sg, w_out_o, F32, tm)
    x5 = _rms_residual(x4, y_o, full['odd_post_g'], 1.0, tr)
    x6, ffn_b1 = ffn_forward(x5, 'b', 1, [], [])

    dy, sq_err = _loss_head(x6, loss_target.reshape(t, d), tr)
    loss = lax.psum(0.5 * sq_err[0, 0] / d, ("x", "y", "c"))

    grads = {}
    received = {}

    def core_sums(named):
        items = [g.reshape(N_CHIPS, 2, g.shape[1] // 2, g.shape[2]) for _, g in named]
        got = _swap_halves(items)
        return [(k, _add_halves(g, r, place)) for (k, _), g, r in zip(named, items, got)]

    def scatter(named):
        return _ScatterChips([pt for _, pt in named]), named

    def arrived(order, results):
        for (k, pt), r in zip(order[1], results):
            received[k] = (pt, r)

    def ffn_backward(dout, saved, tag, layer, orders):
        dxin, g, carried = _ffn_bwd(dout, saved, *ffn_args(tag, layer), couriers={k: o[0] for k, o in orders.items()})
        for k, o in orders.items():
            arrived(o, carried[k])
        return dxin, g, core_sums(list(zip(ffn_keys(tag, layer), g[2:])))

    fg = {}
    dx, fg[('b', 1)], pend = ffn_backward(dy, ffn_b1, 'b', 1, {})
    dy_o, grads['odd_post_g'] = _rms_bwd(y_o, full['odd_post_g'], dx, 1.0, MXU_DTYPE, tr)
    dsg = _mm_nt("mm_dsg", [(dy_o, w_out_o, 0)], F32, tm)
    (g_w_out_o,) = _mm_tn("mm_dw_out", sg, [dy_o], MXU_DTYPE)
    dhw, grads['odd_v_norm_g'], grads['odd_v_norm_b'], g_ws, g_bst = _sgu_bwd(
        hw, full['odd_v_norm_g'], full['odd_v_norm_b'], w['odd_w_s'][0], b_st, dsg)
    grads['odd_w_s'] = g_ws[None]
    grads['odd_b_s'] = g_bst[:, :GROUPS].T[None]
    tn_o = _tile(d, 1024, LANES)
    order = scatter(pend[0:1])
    (dh_o,), res = _matmul("mm_dh_odd", "nt", (t // tm, d // tn_o, N_CHIPS),
                           [[(dhw, _spec((tm, ns), lambda i, j, k: (i, k)), w_in_o, _spec((None, None, tn_o, ns), lambda i, j, k: (k, 0, j, 0)))]],
                           [((t, d), F32, _spec((tm, tn_o), lambda i, j, k: (i, j)))], (tm, tn_o), courier=order[0])
    arrived(order, res)
    tmw = _tile(d, 512, LANES)
    tkt = _tile(t, 512, 16)
    order = scatter(pend[1:2])
    (g_w_in_o,), res = _matmul("mm_dw_in_odd", "tn", (d // tmw, N_CHIPS, t // tkt),
                               [[(h_o, _spec((tkt, tmw), lambda i, j, k: (k, i)), dhw, _spec((tkt, ns), lambda i, j, k: (k, j)))]],
                               [((N_CHIPS, d, ns), MXU_DTYPE, _spec((None, tmw, ns), lambda i, j, k: (j, i, 0)))], (tmw, ns),
                               courier=order[0])
    arrived(order, res)
    pend_odd = core_sums([(('odd_w_in', 0), g_w_in_o),
                          (('odd_w_out', 0), g_w_out_o.reshape((N_CHIPS,) + w['odd_w_out'].shape[1:]))])
    dx, grads['odd_pre_g'] = _rms_bwd(x4, full['odd_pre_g'], dh_o, 1.0, F32, tr, resid=dx)
    dx, fg[('a', 1)], pend = ffn_backward(dx, ffn_a1, 'a', 1, {'dact': scatter(pend[2:3]), 'dwdown': scatter(pend_odd)})
    dx, fg[('b', 0)], pend = ffn_backward(dx, ffn_b0, 'b', 0, {'dact': scatter(pend[0:1]), 'dh': scatter(pend[1:2]),
                                                              'dwup': scatter(pend[2:3])})
    dy_e, grads['even_post_g'] = _rms_bwd(y_e, w['even_post_g'], dx, 1.0, MXU_DTYPE, tr)
    d_cat = _mm_nt("mm_dcat", [(dy_e, w_out_e, 0)], F32, tm)
    (g_w_out_e,) = _mm_tn("mm_dw_out", cat, [dy_e], MXU_DTYPE)
    dqn, dqp, dkv, dkr = _attn_bwd(q_all, kv, p, cos, sin, d_cat, batch, seq)
    hq = HEADS * LANES
    d_latq = _mm_nt("mm_dlat_q", [(dqn, w_uq, 0), (dqp, w_uq, hq)], F32, tm, tn_want=Q_LORA)
    d_latkv = _mm_nt("mm_dlat_kv", [(dkv, w_ukv, 0)], F32, tm, tn_want=KV_LORA)
    g_uq_n, g_uq_r = _mm_tn("mm_dw_uq", lat, [dqn, dqp], MXU_DTYPE, a_col0=0, m_width=Q_LORA)
    (g_ukv,) = _mm_tn("mm_dw_ukv", lat, [dkv], MXU_DTYPE, a_col0=Q_LORA, m_width=KV_LORA)

    def latent_bwd(r, c):
        _, vjp_q = jax.vjp(_rmsnorm, r[0], c[0])
        _, vjp_kv = jax.vjp(_rmsnorm, r[1], c[1])
        dq, dqg = vjp_q(r[2])
        dk, dkg = vjp_kv(r[3])
        return [jnp.concatenate([dq, dk], axis=1)], [dqg, dkg]

    d_lat, grads['even_q_norm_g'], grads['even_kv_norm_g'] = _rows(
        "latent_norm_bwd", latent_bwd, [(p, Q_LORA, 0), (p, KV_LORA, 1), (d_latq, Q_LORA, 0), (d_latkv, KV_LORA, 0)],
        [w['even_q_norm_g'], w['even_kv_norm_g']], [(Q_LORA + KV_LORA, MXU_DTYPE)], [(1, Q_LORA), (1, KV_LORA)], tr)
    da, dgate, g_conv_w, grads['even_conv_b'], grads['even_conv_norm_g'], grads['even_conv_norm_b'] = _conv_bwd(
        p, d_cat, full['even_conv_w'][0], w['even_conv_b'], w['even_conv_norm_g'], w['even_conv_norm_b'], batch, seq)
    dp = jnp.concatenate([d_lat, da, dgate, dkr.astype(MXU_DTYPE)], axis=1)
    order = scatter(pend[0:1])
    dh_e, res = _mm_nt("mm_dh_even", [(dp, w_in, 0)], F32, tm, tk_want=640, courier=order[0])
    arrived(order, res)
    order = scatter(pend[1:2])
    (g_w_in,), res = _mm_tn("mm_dw_in", h_e, [dp], MXU_DTYPE, tn_want=640, courier=order[0])
    arrived(order, res)
    pend_even = core_sums([(('even_w_in', 0), _cols_to_chips(_w_in_layout_inv(g_w_in))),
                           (('even_w_uq', 0), _cols_to_chips(_w_uq_layout_inv(g_uq_n, g_uq_r))),
                           (('even_w_ukv', 0), _cols_to_chips(g_ukv)),
                           (('even_w_out', 0), g_w_out_e.reshape((N_CHIPS,) + w['even_w_out'].shape[1:]))])
    dx, grads['even_pre_g'] = _rms_bwd(x1, w['even_pre_g'], dh_e, 1.0, F32, tr, resid=dx)
    dx, fg[('a', 0)], pend = ffn_backward(dx, ffn_a0, 'a', 0, {'dact': scatter(pend[2:3]), 'dh': scatter(pend_even)})
    grad_x = dx.reshape(batch, seq, d)
    order = scatter(pend)
    arrived(order, _run_courier("scatter_chips", order[0]))

    reduced = []
    for n in BIG:
        layers = w[n].shape[0]
        buf = None
        for layer in range(layers):
            pt, r = received[(n, layer)]
            buf = _add_chips(pt, r, place, layer, layers, into=buf)
        reduced.append(buf)
    joined = _join_halves(reduced)
    for n, g in zip(BIG, joined):
        grads[n] = g.reshape(w[n].shape)

    for tag in ('a', 'b'):
        grads[f'ffn_{tag}_pre_g'] = jnp.concatenate([fg[(tag, 0)][0], fg[(tag, 1)][0]], axis=0)
        grads[f'ffn_{tag}_post_g'] = jnp.concatenate([fg[(tag, 0)][1], fg[(tag, 1)][1]], axis=0)
    grads['even_conv_w'] = g_conv_w[None]
    packed, rngs = _pack([grads[n] for n in SMALL])
    device = 2 * chip + lax.axis_index("c")
    slots = lax.dynamic_update_slice_in_dim(jnp.zeros((N_DEV,) + packed.shape, F32), packed[None], device, axis=0)
    summed = _add_devices(_gather_devices(slots))
    for n, rng in zip(SMALL, rngs):
        g = _unpack(summed, rng, grads[n].shape)
        if n in SMALL_SHARDED:
            width = w[n].shape[-1]
            g = lax.dynamic_slice_in_dim(g, chip * width, width, axis=g.ndim - 1)
        grads[n] = g

    delta, new_m, new_v = {}, {}, {}
    for n in BIG:
        delta[n], new_m[n], new_v[n] = _adamw(w[n], grads[n], given['m_' + n], given['v_' + n])
    packs = [_pack([src[n] for n in SMALL])[0][None] for src in
             (w, grads, {n: given['m_' + n] for n in SMALL}, {n: given['v_' + n] for n in SMALL})]
    _, rngs = _pack([w[n] for n in SMALL])
    small_out = _adamw(*packs)
    for n, rng in zip(SMALL, rngs):
        delta[n], new_m[n], new_v[n] = (_unpack(o[0], rng, w[n].shape) for o in small_out)

    return (loss, grad_x, *[grads[n] for n in WEIGHTS], *[delta[n] for n in WEIGHTS],
            *[new_m[n] for n in WEIGHTS], *[new_v[n] for n in WEIGHTS])
```

```python
import functools

import jax
import jax.numpy as jnp
from jax import lax
from jax.experimental import pallas as pl
from jax.experimental.pallas import tpu as pltpu

F32 = jnp.float32
BF16 = jnp.bfloat16
MXU_DTYPE = BF16
MESH = pl.DeviceIdType.MESH
VMEM_LIMIT_BYTES = 56 * 1024 * 1024
LANES = 128
TN_ROWS = 2048

D_MODEL = 2048
D_FF = 5632
EPS = 1e-6
HEADS = 8
V_HEAD = 128
QK_NOPE = 128
QK_ROPE = 64
Q_LORA = 512
KV_LORA = 512
ROPE_THETA = 10000.0
CONV_CH = 1024
CONV_WIDTH = 31
GROUPS = 8
CHUNK = 128
GM_WIDTH = 2048
ADAM_LR = 0.001
ADAM_B1 = 0.9
ADAM_B2 = 0.999
ADAM_EPS = 1e-08
ADAM_WD = 0.01
ADAM_STEP = 10

N_CHIPS = 4
N_DEV = 8
P_KV = Q_LORA
P_A = Q_LORA + KV_LORA
P_GATE = P_A + CONV_CH
P_KR = P_GATE + CONV_CH
P_WIDTH = P_KR + LANES
ROPE_HALF = QK_ROPE // 2

WEIGHTS = ['ffn_a_pre_g', 'ffn_a_post_g', 'ffn_a_w_gate', 'ffn_a_w_up', 'ffn_a_w_down', 'ffn_b_pre_g', 'ffn_b_post_g',
           'ffn_b_w_gate', 'ffn_b_w_up', 'ffn_b_w_down', 'even_pre_g', 'even_post_g', 'even_w_in', 'even_q_norm_g',
           'even_kv_norm_g', 'even_w_uq', 'even_w_ukv', 'even_conv_w', 'even_conv_b', 'even_conv_norm_g',
           'even_conv_norm_b', 'even_w_out', 'odd_pre_g', 'odd_post_g', 'odd_w_in', 'odd_v_norm_g', 'odd_v_norm_b',
           'odd_w_s', 'odd_b_s', 'odd_w_out']
BIG = ['ffn_a_w_gate', 'ffn_a_w_up', 'ffn_a_w_down', 'ffn_b_w_gate', 'ffn_b_w_up', 'ffn_b_w_down', 'even_w_in',
       'even_w_uq', 'even_w_ukv', 'even_w_out', 'odd_w_in', 'odd_w_out']
SMALL = [n for n in WEIGHTS if n not in BIG]
SMALL_SHARDED = ['even_conv_w', 'odd_pre_g', 'odd_post_g', 'odd_v_norm_g', 'odd_v_norm_b']


def _call(name, body, grid, in_specs, out_specs, out_shape, scratch=(), prefetch=0, aliases=None):
    params = pltpu.CompilerParams(dimension_semantics=("arbitrary",) * len(grid), vmem_limit_bytes=VMEM_LIMIT_BYTES)
    if prefetch:
        spec = pltpu.PrefetchScalarGridSpec(num_scalar_prefetch=prefetch, grid=grid, in_specs=in_specs,
                                            out_specs=out_specs, scratch_shapes=list(scratch))
        return pl.pallas_call(body, grid_spec=spec, out_shape=out_shape, compiler_params=params, name=name,
                              input_output_aliases=aliases or {})
    return pl.pallas_call(body, grid=grid, in_specs=in_specs, out_specs=out_specs, out_shape=out_shape,
                          scratch_shapes=list(scratch), compiler_params=params, name=name,
                          input_output_aliases=aliases or {})


def _spec(block, index_map):
    return pl.BlockSpec(block, index_map)


def _sds(shape, dtype):
    return jax.ShapeDtypeStruct(tuple(shape), dtype)


def _tile(n, want, mult=8):
    if n <= want:
        return n
    best = None
    for t in range(mult, want + 1, mult):
        if n % t == 0:
            best = t
    assert best is not None, (n, want, mult)
    return best


_DIMS = {"nn": (((1,), (0,)), ((), ())), "nt": (((1,), (1,)), ((), ())), "tn": (((0,), (0,)), ((), ()))}


def _dot(a, b, mode="nn"):
    return lax.dot_general(a.astype(MXU_DTYPE), b.astype(MXU_DTYPE), _DIMS[mode], preferred_element_type=F32)


def _matmul(name, mode, grid, groups, outs, acc_shape, extras=(), epilogue=None, courier=None):
    flat, specs = [], []
    for grp in groups:
        for a, a_spec, b, b_spec in grp:
            flat += [a, b]
            specs += [a_spec, b_spec]
    for e, e_spec in extras:
        flat.append(e)
        specs.append(e_spec)
    n_pairs = [len(g) for g in groups]
    one_step = grid[2] == 1
    n_in, n_ex, n_out, n_acc = 2 * sum(n_pairs), len(extras), len(outs), 0 if one_step else len(groups)
    last = tuple(g - 1 for g in grid)
    c_arrays = list(courier.arrays) if courier else []
    c_shapes = list(courier.out_shapes) if courier else []
    n_ci, n_co = len(c_arrays), len(c_shapes)
    any_spec = pl.BlockSpec(memory_space=pl.ANY)

    def body(*refs):
        ins, ex = refs[:n_in], refs[n_in:n_in + n_ex]
        pos = n_in + n_ex
        c_in = refs[pos:pos + n_ci]
        out_refs = refs[pos + n_ci:pos + n_ci + n_out]
        c_out = refs[pos + n_ci + n_out:pos + n_ci + n_out + n_co]
        accs = refs[pos + n_ci + n_out + n_co:pos + n_ci + n_out + n_co + n_acc]
        sems = refs[pos + n_ci + n_out + n_co + n_acc:]
        i, j, k = pl.program_id(0), pl.program_id(1), pl.program_id(2)

        if courier:
            @pl.when((i == 0) & (j == 0) & (k == 0))
            def _():
                courier.start(c_in, c_out, sems)

        def products():
            pos, totals = 0, []
            for n in n_pairs:
                total = None
                for _ in range(n):
                    d = _dot(ins[pos][...], ins[pos + 1][...], mode)
                    total = d if total is None else total + d
                    pos += 2
                totals.append(total)
            return totals

        def finish(vals):
            res = epilogue(vals, [e[...] for e in ex]) if epilogue else vals
            for o, r in zip(out_refs, res):
                o[...] = r.astype(o.dtype)

        if one_step:
            finish(products())
        else:
            @pl.when(k == 0)
            def _():
                for acc in accs:
                    acc[...] = jnp.zeros(acc.shape, F32)

            for acc, total in zip(accs, products()):
                acc[...] += total

            @pl.when(k == last[2])
            def _():
                finish([acc[...] for acc in accs])

        if courier:
            @pl.when((i == last[0]) & (j == last[1]) & (k == last[2]))
            def _():
                courier.finish(c_in, c_out, sems)

    aliases = {n_in + n_ex + t: n_out + t for t in range(n_ci)} if courier and courier.aliased else None
    scratch = [pltpu.VMEM(acc_shape, F32)] * n_acc
    scratch += [pltpu.SemaphoreType.DMA((n,)) for n in courier.sems] if courier else []
    res = _call(name, body, grid, specs + [any_spec] * n_ci, [o[2] for o in outs] + [any_spec] * n_co,
                [_sds(o[0], o[1]) for o in outs] + c_shapes, scratch=scratch, aliases=aliases)(*flat, *c_arrays)
    return list(res[:n_out]), list(res[n_out:])


def _rows(name, fn, rows, consts, outs, accs, tm):
    t_rows = rows[0][0].shape[0]
    grid = (t_rows // tm,)
    in_specs = [_spec((tm, bw), functools.partial(lambda i, cb: (i, cb), cb=cb)) for _, bw, cb in rows]
    in_specs += [_spec(c.shape, functools.partial(lambda i, nd: (0,) * nd, nd=c.ndim)) for c in consts]
    out_shape = [_sds((t_rows, w), dt) for w, dt in outs] + [_sds(s, F32) for s in accs]
    out_specs = [_spec((tm, w), lambda i: (i, 0)) for w, _ in outs]
    out_specs += [_spec(s, functools.partial(lambda i, nd: (0,) * nd, nd=len(s))) for s in accs]
    nr, nc, no = len(rows), len(consts), len(outs)

    def body(*refs):
        r = [ref[...] for ref in refs[:nr]]
        c = [ref[...] for ref in refs[nr:nr + nc]]
        o_refs, a_refs = refs[nr + nc:nr + nc + no], refs[nr + nc + no:]
        o_vals, a_vals = fn(r, c)
        for ref, v in zip(o_refs, o_vals):
            ref[...] = v.astype(ref.dtype)
        if a_refs:
            @pl.when(pl.program_id(0) == 0)
            def _():
                for ref in a_refs:
                    ref[...] = jnp.zeros(ref.shape, F32)

            for ref, v in zip(a_refs, a_vals):
                ref[...] += v

    return _call(name, body, grid, in_specs, out_specs, out_shape)(*[a for a, _, _ in rows], *consts)


def _rmsnorm(x, g):
    return x * lax.rsqrt(jnp.mean(x * x, axis=-1, keepdims=True) + EPS) * g


def _layernorm(x, g, b):
    mu = jnp.mean(x, axis=-1, keepdims=True)
    var = jnp.mean(jnp.square(x - mu), axis=-1, keepdims=True)
    return (x - mu) * lax.rsqrt(var + EPS) * g + b


def _swiglu_act(g, u):
    return jax.nn.silu(g) * u


def _rope(x, cos, sin_signed):
    return x * cos + pltpu.roll(x, 2 * ROPE_HALF, 1) * sin_signed


def _rope_transposed(dy, cos, sin_signed):
    return dy * cos - pltpu.roll(dy, 2 * ROPE_HALF, 1) * sin_signed


def _rms_fwd(x, g, tm):
    d = x.shape[1]
    return _rows("rms_fwd", lambda r, c: ([_rmsnorm(r[0], c[0])], []), [(x, d, 0)], [g], [(d, MXU_DTYPE)], [], tm)[0]


def _rms_residual(x, y, g, scale, tm):
    d = x.shape[1]
    return _rows("rms_residual", lambda r, c: ([r[0] + scale * _rmsnorm(r[1], c[0])], []),
                 [(x, d, 0), (y, d, 0)], [g], [(d, F32)], [], tm)[0]


def _rms_bwd(y, g, dout, scale, out_dtype, tm, resid=None):
    d = y.shape[1]

    def fn(r, c):
        _, vjp = jax.vjp(_rmsnorm, r[0], c[0])
        dy, dg = vjp(scale * r[1].astype(F32))
        if resid is not None:
            dy = dy + r[2]
        return [dy], [dg]

    rows = [(y, d, 0), (dout, d, 0)] + ([(resid, d, 0)] if resid is not None else [])
    return _rows("rms_bwd" if resid is None else "rms_bwd_resid", fn, rows, [g], [(d, out_dtype)], [(1, d)], tm)


def _loss_head(y, target, tm):
    d = y.shape[1]

    def fn(r, c):
        err = r[0] - r[1]
        sq = jnp.sum(jnp.sum(err * err, axis=1, keepdims=True), axis=0, keepdims=True)
        return [err * (1.0 / d)], [jnp.broadcast_to(sq, (1, LANES))]

    return _rows("loss_head", fn, [(y, d, 0), (target, d, 0)], [], [(d, F32)], [(1, LANES)], tm)


def _ffn_fwd(x, pre_g, post_g, wg, wu, wd, layer, tm, tr, couriers=None, wd_landed=None):
    t, d = x.shape
    fs = wg.shape[3]
    h = _rms_fwd(x, pre_g, tr)
    h_spec = _spec((tm, d), lambda j, i, k: (i, 0))
    w_spec = _spec((None, None, d, fs), lambda j, i, k: (j, layer, 0, 0))
    o_spec = _spec((tm, fs), lambda j, i, k: (i, j))
    couriers = couriers or {}
    carried = {}
    (g, u, a), carried["up"] = _matmul(
        "ffn_up", "nn", (N_CHIPS, t // tm, 1), [[(h, h_spec, wg, w_spec)], [(h, h_spec, wu, w_spec)]],
        [((t, N_CHIPS * fs), MXU_DTYPE, o_spec)] * 3, (tm, fs),
        epilogue=lambda accs, _: (accs[0], accs[1], _swiglu_act(accs[0], accs[1])), courier=couriers.get("up"))
    if wd_landed is not None:
        wd = carried["up"][wd_landed].reshape(wd.shape)
    (y,), carried["down"] = _matmul(
        "ffn_down", "nn", (t // tm, 1, N_CHIPS),
        [[(a, _spec((tm, fs), lambda i, j, k: (i, k)), wd, _spec((None, None, fs, d), lambda i, j, k: (k, layer, 0, 0)))]],
        [((t, d), F32, _spec((tm, d), lambda i, j, k: (i, 0)))], (tm, d), courier=couriers.get("down"))
    out = _rms_residual(x, y, post_g, 0.5, tr)
    return out, (x, h, g, u, a, y), carried


def _ffn_bwd(dout, saved, pre_g, post_g, wg, wu, wd, layer, tm, tr, couriers=None, weights_first=False):
    x, h, g, u, a, y = saved
    t, d = x.shape
    fs = wg.shape[3]
    tn = _tile(d, 1024, LANES)
    tkt = _tile(t, TN_ROWS, 16)
    dy, dpost = _rms_bwd(y, post_g, dout, 0.5, MXU_DTYPE, tr)

    def act_bwd(accs, ex):
        _, vjp = jax.vjp(_swiglu_act, ex[0].astype(F32), ex[1].astype(F32))
        return vjp(accs[0])

    couriers = couriers or {}
    carried = {}
    gu_spec = _spec((tm, fs), lambda j, i, k: (i, j))
    (dg, du), carried["dact"] = _matmul(
        "ffn_dact", "nt", (N_CHIPS, t // tm, 1),
        [[(dy, _spec((tm, d), lambda j, i, k: (i, 0)), wd, _spec((None, None, fs, d), lambda j, i, k: (j, layer, 0, 0)))]],
        [((t, N_CHIPS * fs), MXU_DTYPE, gu_spec)] * 2, (tm, fs),
        extras=[(g, gu_spec), (u, gu_spec)], epilogue=act_bwd, courier=couriers.get("dact"))
    (dwd,), carried["dwdown"] = _matmul(
        "ffn_dwdown", "tn", (N_CHIPS, d // tn, t // tkt),
        [[(a, _spec((tkt, fs), lambda i, j, k: (k, i)), dy, _spec((tkt, tn), lambda i, j, k: (k, j)))]],
        [((N_CHIPS, fs, d), MXU_DTYPE, _spec((None, fs, tn), lambda i, j, k: (i, 0, j)))], (fs, tn),
        courier=couriers.get("dwdown"))
    def d_hidden(courier):
        da_spec = _spec((tm, fs), lambda i, j, k: (i, k))
        wt_spec = _spec((None, None, d, fs), lambda i, j, k: (k, layer, 0, 0))
        return _matmul("ffn_dh", "nt", (t // tm, 1, N_CHIPS), [[(dg, da_spec, wg, wt_spec), (du, da_spec, wu, wt_spec)]],
                       [((t, d), F32, _spec((tm, d), lambda i, j, k: (i, 0)))], (tm, d), courier=courier)

    def d_w_up(courier):
        tmw = _tile(d, 512, LANES)
        h_spec = _spec((tkt, tmw), lambda i, j, k: (k, i))
        dgu_spec = _spec((tkt, fs), lambda i, j, k: (k, j))
        dw_spec = _spec((None, tmw, fs), lambda i, j, k: (j, i, 0))
        return _matmul("ffn_dwup", "tn", (d // tmw, N_CHIPS, t // tkt), [[(h, h_spec, dg, dgu_spec)], [(h, h_spec, du, dgu_spec)]],
                       [((N_CHIPS, d, fs), MXU_DTYPE, dw_spec)] * 2, (tmw, fs), courier=courier)

    if weights_first:
        (dwg, dwu), carried["dwup"] = d_w_up(couriers["dwup"](dwd))
        (dh,), carried["dh"] = d_hidden(couriers["dh"](dwg, dwu))
    else:
        (dh,), carried["dh"] = d_hidden(couriers.get("dh"))
        (dwg, dwu), carried["dwup"] = d_w_up(couriers.get("dwup"))
    dx, dpre = _rms_bwd(x, pre_g, dh, 1.0, F32, tr, resid=dout)
    return dx, (dpre, dpost, dwg, dwu, dwd), carried


def _attn_scores(qn, qpe, kn, kpe, qi, tq, seq):
    s = (_dot(qn, kn, "nt") + _dot(qpe, kpe, "nt")) * ((QK_NOPE + QK_ROPE) ** -0.5)
    rows = qi * tq + lax.broadcasted_iota(jnp.int32, (tq, seq), 0)
    cols = lax.broadcasted_iota(jnp.int32, (tq, seq), 1)
    s = jnp.where(cols <= rows, s, -jnp.inf)
    e = jnp.exp(s - jnp.max(s, axis=1, keepdims=True))
    return e / jnp.sum(e, axis=1, keepdims=True)


KEY_EXTENTS = 4


def _for_key_extent(qi, tq, seq, attend):
    n = KEY_EXTENTS if seq % (KEY_EXTENTS * tq) == 0 else 1
    ext = seq // n
    mine = (qi * tq) // ext
    for e in range(n):
        pl.when(mine == e)(functools.partial(attend, (e + 1) * ext))


def _attn_specs(nq, tq, seq):
    q_rows = lambda b, h, qi: b * nq + qi
    return [
        _spec((tq, LANES), lambda b, h, qi: (q_rows(b, h, qi), h)),
        _spec((tq, LANES), lambda b, h, qi: (q_rows(b, h, qi), HEADS + h)),
        _spec((seq, 2 * LANES), lambda b, h, qi: (b, h)),
        _spec((seq, LANES), lambda b, h, qi: (b, P_KR // LANES)),
        _spec((tq, LANES), lambda b, h, qi: (q_rows(b, h, qi), 0)),
        _spec((tq, LANES), lambda b, h, qi: (q_rows(b, h, qi), 0)),
        _spec((seq, LANES), lambda b, h, qi: (b, 0)),
        _spec((seq, LANES), lambda b, h, qi: (b, 0)),
    ]


def _attn_fwd(q_all, kv, p, cos, sin, batch, seq):
    t = q_all.shape[0]
    tq = _tile(seq, 256, 16)
    nq = seq // tq

    def body(qn_ref, qp_ref, kv_ref, kr_ref, cq, sq, ck, sk, o_ref):
        qi = pl.program_id(2)
        qpe = _rope(qp_ref[...], cq[...], sq[...])

        def attend(keys):
            kpe = _rope(kr_ref[:keys, :], ck[:keys, :], sk[:keys, :])
            prob = _attn_scores(qn_ref[...], qpe, kv_ref[:keys, :LANES], kpe, qi, tq, keys)
            o_ref[...] = _dot(prob, kv_ref[:keys, LANES:]).astype(o_ref.dtype)

        _for_key_extent(qi, tq, seq, attend)

    return _call("attn_fwd", body, (batch, HEADS, nq), _attn_specs(nq, tq, seq),
                 _spec((tq, LANES), lambda b, h, qi: (b * nq + qi, h)), _sds((t, 2 * HEADS * V_HEAD), MXU_DTYPE)
                 )(q_all, q_all, kv, p, cos, sin, cos, sin)


def _attn_bwd(q_all, kv, p, cos, sin, d_cat, batch, seq):
    t = q_all.shape[0]
    tq = _tile(seq, 256, 16)
    nq = seq // tq

    def body(qn_ref, qp_ref, kv_ref, kr_ref, cq, sq, ck, sk, do_ref, dqn_ref, dqp_ref, dkv_ref, dkr_ref, dkpe_acc):
        h, qi = pl.program_id(1), pl.program_id(2)
        qn = qn_ref[...]
        qpe = _rope(qp_ref[...], cq[...], sq[...])
        do = do_ref[...]

        @pl.when(qi == 0)
        def _():
            dkv_ref[...] = jnp.zeros(dkv_ref.shape, F32)

        @pl.when((h == 0) & (qi == 0))
        def _():
            dkpe_acc[...] = jnp.zeros(dkpe_acc.shape, F32)

        def attend(keys):
            kn, v = kv_ref[:keys, :LANES], kv_ref[:keys, LANES:]
            kpe = _rope(kr_ref[:keys, :], ck[:keys, :], sk[:keys, :])
            prob = _attn_scores(qn, qpe, kn, kpe, qi, tq, keys)
            dprob = _dot(do, v, "nt")
            ds = prob * (dprob - jnp.sum(prob * dprob, axis=1, keepdims=True)) * ((QK_NOPE + QK_ROPE) ** -0.5)
            dqn_ref[...] = _dot(ds, kn)
            dqp_ref[...] = _rope_transposed(_dot(ds, kpe), cq[...], sq[...])
            dkv_ref[:keys, :LANES] += _dot(ds, qn, "tn")
            dkv_ref[:keys, LANES:] += _dot(prob, do, "tn")
            dkpe_acc[:keys, :] += _dot(ds, qpe, "tn")

        _for_key_extent(qi, tq, seq, attend)

        @pl.when((h == HEADS - 1) & (qi == nq - 1))
        def _():
            dkr_ref[...] = _rope_transposed(dkpe_acc[...], ck[...], sk[...])

    q_out = _spec((tq, LANES), lambda b, h, qi: (b * nq + qi, h))
    return _call("attn_bwd", body, (batch, HEADS, nq),
                 _attn_specs(nq, tq, seq) + [_spec((tq, LANES), lambda b, h, qi: (b * nq + qi, h))],
                 [q_out, q_out, _spec((seq, 2 * LANES), lambda b, h, qi: (b, h)), _spec((seq, LANES), lambda b, h, qi: (b, 0))],
                 [_sds((t, HEADS * LANES), F32), _sds((t, HEADS * LANES), F32), _sds((t, HEADS * 2 * LANES), F32), _sds((t, LANES), F32)],
                 scratch=[pltpu.VMEM((seq, LANES), F32)])(q_all, q_all, kv, p, cos, sin, cos, sin, d_cat)


CONV_PAD = 32


def _conv_taps(w_ref, src_ref, first_row, n_rows, init, offset):
    acc = init
    for k in range(CONV_WIDTH):
        acc = acc + w_ref[k:k + 1, :] * src_ref[pl.ds(first_row + offset(k), n_rows), :]
    return acc


def _norm_act(conv, g, b):
    return jax.nn.silu(_layernorm(conv, g, b))


def _conv_fwd(p, cat, conv_w, conv_b, norm_g, norm_b, batch, seq):
    rc = _tile(seq, 256, 8)
    a_blk, gate_blk, out_blk = P_A // LANES, P_GATE // LANES, (HEADS * V_HEAD) // LANES

    def body(a_ref, gate_ref, w_ref, cb_ref, ng_ref, nb_ref, cat_in, o_ref, zp):
        del cat_in
        zp[pl.ds(0, CONV_PAD), :] = jnp.zeros((CONV_PAD, LANES), F32)
        zp[pl.ds(CONV_PAD, seq), :] = a_ref[...] * jax.nn.sigmoid(gate_ref[...])
        for r0 in range(0, seq, rc):
            conv = _conv_taps(w_ref, zp, r0, rc, jnp.broadcast_to(cb_ref[...], (rc, LANES)),
                              lambda k: CONV_PAD - (CONV_WIDTH - 1) + k)
            o_ref[pl.ds(r0, rc), :] = _norm_act(conv, ng_ref[...], nb_ref[...]).astype(o_ref.dtype)

    vec = _spec((1, LANES), lambda b, g: (0, g))
    return _call("conv_fwd", body, (batch, GROUPS),
                 [_spec((seq, LANES), lambda b, g: (b, a_blk + g)), _spec((seq, LANES), lambda b, g: (b, gate_blk + g)),
                  _spec((CONV_WIDTH, LANES), lambda b, g: (0, g)), vec, vec, vec, pl.BlockSpec(memory_space=pl.ANY)],
                 _spec((seq, LANES), lambda b, g: (b, out_blk + g)), _sds(cat.shape, cat.dtype),
                 scratch=[pltpu.VMEM((seq + CONV_PAD, LANES), F32)], aliases={6: 0}
                 )(p, p, conv_w, conv_b, norm_g, norm_b, cat)


def _conv_bwd(p, d_cat, conv_w, conv_b, norm_g, norm_b, batch, seq):
    t = p.shape[0]
    rc = _tile(seq, 256, 8)
    a_blk, gate_blk, out_blk = P_A // LANES, P_GATE // LANES, (HEADS * V_HEAD) // LANES

    def body(a_ref, gate_ref, w_ref, cb_ref, ng_ref, nb_ref, dc_ref, da_ref, dgate_ref, dw_ref, db_ref, dng_ref, dnb_ref,
             zp, dcp, buf):
        b = pl.program_id(1)
        a, sig = a_ref[...], jax.nn.sigmoid(gate_ref[...])
        zp[pl.ds(0, CONV_PAD), :] = jnp.zeros((CONV_PAD, LANES), F32)
        zp[pl.ds(CONV_PAD, seq), :] = a * sig
        for r0 in range(0, seq, rc):
            buf[pl.ds(r0, rc), :] = _conv_taps(w_ref, zp, r0, rc, jnp.broadcast_to(cb_ref[...], (rc, LANES)),
                                               lambda k: CONV_PAD - (CONV_WIDTH - 1) + k)
        _, vjp = jax.vjp(_norm_act, buf[...], ng_ref[...], nb_ref[...])
        dconv, dng, dnb = vjp(dc_ref[...].astype(F32))
        dcp[pl.ds(0, seq), :] = dconv
        dcp[pl.ds(seq, CONV_PAD), :] = jnp.zeros((CONV_PAD, LANES), F32)

        @pl.when(b == 0)
        def _():
            dw_ref[...] = jnp.zeros(dw_ref.shape, F32)
            db_ref[...] = jnp.zeros(db_ref.shape, F32)
            dng_ref[...] = jnp.zeros(dng_ref.shape, F32)
            dnb_ref[...] = jnp.zeros(dnb_ref.shape, F32)

        db_ref[...] += jnp.sum(dconv, axis=0, keepdims=True)
        dng_ref[...] += dng
        dnb_ref[...] += dnb
        for k in range(CONV_WIDTH):
            shifted = zp[pl.ds(CONV_PAD - (CONV_WIDTH - 1) + k, seq), :]
            dw_ref[k:k + 1, :] += jnp.sum(dconv * shifted, axis=0, keepdims=True)
        for r0 in range(0, seq, rc):
            buf[pl.ds(r0, rc), :] = _conv_taps(w_ref, dcp, r0, rc, jnp.zeros((rc, LANES), F32),
                                               lambda k: CONV_WIDTH - 1 - k)
        dz = buf[...]
        da_ref[...] = (dz * sig).astype(da_ref.dtype)
        dgate_ref[...] = (dz * a * sig * (1.0 - sig)).astype(dgate_ref.dtype)

    vec = _spec((1, LANES), lambda g, b: (0, g))
    row_out = _spec((seq, LANES), lambda g, b: (b, g))
    return _call("conv_bwd", body, (GROUPS, batch),
                 [_spec((seq, LANES), lambda g, b: (b, a_blk + g)), _spec((seq, LANES), lambda g, b: (b, gate_blk + g)),
                  _spec((CONV_WIDTH, LANES), lambda g, b: (0, g)), vec, vec, vec,
                  _spec((seq, LANES), lambda g, b: (b, out_blk + g))],
                 [row_out, row_out, _spec((CONV_WIDTH, LANES), lambda g, b: (0, g)), vec, vec, vec],
                 [_sds((t, CONV_CH), MXU_DTYPE), _sds((t, CONV_CH), MXU_DTYPE), _sds((CONV_WIDTH, CONV_CH), F32),
                  _sds((1, CONV_CH), F32), _sds((1, CONV_CH), F32), _sds((1, CONV_CH), F32)],
                 scratch=[pltpu.VMEM((seq + CONV_PAD, LANES), F32), pltpu.VMEM((seq + CONV_PAD, LANES), F32),
                          pltpu.VMEM((seq, LANES), F32)])(p, p, conv_w, conv_b, norm_g, norm_b, d_cat)


def _sgu_pre(hw, g, b):
    z = jax.nn.gelu(hw)
    return z[:, :GM_WIDTH], _layernorm(z[:, GM_WIDTH:], g, b)


def _causal(w):
    keep = lax.broadcasted_iota(jnp.int32, (CHUNK, CHUNK), 0) >= lax.broadcasted_iota(jnp.int32, (CHUNK, CHUNK), 1)
    return jnp.where(keep, w, 0.0)


def _sgu_fwd(hw, vg, vb, w_s, b_st):
    t = hw.shape[0]
    gw = GM_WIDTH // GROUPS

    def body(hw_ref, vg_ref, vb_ref, w_ref, b_ref, o_ref):
        u, v = _sgu_pre(hw_ref[...], vg_ref[...], vb_ref[...])
        for g in range(GROUPS):
            cols = slice(g * gw, (g + 1) * gw)
            s = _dot(_causal(w_ref[g]), v[:, cols]) + b_ref[:, g:g + 1]
            o_ref[:, cols] = (u[:, cols] * s).astype(o_ref.dtype)

    return _call("sgu_fwd", body, (t // CHUNK,),
                 [_spec((CHUNK, 2 * GM_WIDTH), lambda i: (i, 0)), _spec((1, GM_WIDTH), lambda i: (0, 0)),
                  _spec((1, GM_WIDTH), lambda i: (0, 0)), _spec((GROUPS, CHUNK, CHUNK), lambda i: (0, 0, 0)),
                  _spec((CHUNK, GROUPS), lambda i: (0, 0))],
                 _spec((CHUNK, GM_WIDTH), lambda i: (i, 0)), _sds((t, GM_WIDTH), MXU_DTYPE))(hw, vg, vb, w_s, b_st)


def _sgu_bwd(hw, vg, vb, w_s, b_st, dout):
    t = hw.shape[0]
    gw = GM_WIDTH // GROUPS

    def body(hw_ref, vg_ref, vb_ref, w_ref, b_ref, do_ref, dhw_ref, dvg_ref, dvb_ref, dw_ref, db_ref, du_buf, dv_buf):
        (u, v), vjp = jax.vjp(_sgu_pre, hw_ref[...], vg_ref[...], vb_ref[...])

        @pl.when(pl.program_id(0) == 0)
        def _():
            dvg_ref[...] = jnp.zeros(dvg_ref.shape, F32)
            dvb_ref[...] = jnp.zeros(dvb_ref.shape, F32)
            dw_ref[...] = jnp.zeros(dw_ref.shape, F32)
            db_ref[...] = jnp.zeros(db_ref.shape, F32)

        lane = lax.broadcasted_iota(jnp.int32, (CHUNK, LANES), 1)
        db = jnp.zeros((CHUNK, LANES), F32)
        for g in range(GROUPS):
            cols = slice(g * gw, (g + 1) * gw)
            w = _causal(w_ref[g])
            s = _dot(w, v[:, cols]) + b_ref[:, g:g + 1]
            do = do_ref[:, cols]
            ds = do * u[:, cols]
            du_buf[:, cols] = do * s
            dw_ref[g] += _causal(_dot(ds, v[:, cols], "nt"))
            dv_buf[:, cols] = _dot(w, ds, "tn")
            db = db + jnp.where(lane == g, jnp.sum(ds, axis=1, keepdims=True), 0.0)
        db_ref[...] += db
        dhw, dvg, dvb = vjp((du_buf[...], dv_buf[...]))
        dhw_ref[...] = dhw.astype(dhw_ref.dtype)
        dvg_ref[...] += dvg
        dvb_ref[...] += dvb

    vec = _spec((1, GM_WIDTH), lambda i: (0, 0))
    w_spec = _spec((GROUPS, CHUNK, CHUNK), lambda i: (0, 0, 0))
    return _call("sgu_bwd", body, (t // CHUNK,),
                 [_spec((CHUNK, 2 * GM_WIDTH), lambda i: (i, 0)), vec, vec, w_spec, _spec((CHUNK, GROUPS), lambda i: (0, 0)),
                  _spec((CHUNK, GM_WIDTH), lambda i: (i, 0))],
                 [_spec((CHUNK, 2 * GM_WIDTH), lambda i: (i, 0)), vec, vec, w_spec, _spec((CHUNK, LANES), lambda i: (0, 0))],
                 [_sds((t, 2 * GM_WIDTH), MXU_DTYPE), _sds((1, GM_WIDTH), F32), _sds((1, GM_WIDTH), F32),
                  _sds((GROUPS, CHUNK, CHUNK), F32), _sds((CHUNK, LANES), F32)],
                 scratch=[pltpu.VMEM((CHUNK, GM_WIDTH), F32), pltpu.VMEM((CHUNK, GM_WIDTH), F32)]
                 )(hw, vg, vb, w_s, b_st, dout)


def _mm_nn(name, a, b, out_dtype, tm, a_col0=0, tn_want=1024, tk_want=1024, courier=None):
    t = a.shape[0]
    kk, n = b.shape
    tk = _tile(kk, tk_want, LANES)
    tn = _tile(n, tn_want, LANES)
    k0 = a_col0 // tk
    assert a_col0 % tk == 0
    outs, carried = _matmul(
        name, "nn", (t // tm, n // tn, kk // tk),
        [[(a, _spec((tm, tk), lambda i, j, k: (i, k0 + k)), b, _spec((tk, tn), lambda i, j, k: (k, j)))]],
        [((t, n), out_dtype, _spec((tm, tn), lambda i, j, k: (i, j)))], (tm, tn), courier=courier)
    return (outs[0], carried) if courier else outs[0]


def _mm_nt(name, pairs, out_dtype, tm, tn_want=1024, tk_want=1024, courier=None):
    t, kk = pairs[0][0].shape
    n = pairs[0][1].shape[0]
    tk = _tile(kk, tk_want, LANES)
    tn = _tile(n, tn_want, 16)
    grp = []
    for a, b, b_col0 in pairs:
        assert b_col0 % tk == 0 and a.shape == (t, kk)
        grp.append((a, _spec((tm, tk), lambda i, j, k: (i, k)), b,
                    _spec((tn, tk), functools.partial(lambda i, j, k, k0: (j, k0 + k), k0=b_col0 // tk))))
    outs, carried = _matmul(name, "nt", (t // tm, n // tn, kk // tk), [grp],
                            [((t, n), out_dtype, _spec((tm, tn), lambda i, j, k: (i, j)))], (tm, tn), courier=courier)
    return (outs[0], carried) if courier else outs[0]


def _mm_tn(name, a, bs, out_dtype, a_col0=0, m_width=None, tm_want=512, tn_want=1024, courier=None):
    t = a.shape[0]
    m = m_width or a.shape[1]
    n = bs[0].shape[1]
    tmw = _tile(m, tm_want, LANES)
    tn = _tile(n, tn_want, LANES)
    tkt = _tile(t, TN_ROWS, 16)
    assert a_col0 % tmw == 0
    i0 = a_col0 // tmw
    a_spec = _spec((tkt, tmw), lambda i, j, k: (k, i0 + i))
    groups = [[(a, a_spec, b, _spec((tkt, tn), lambda i, j, k: (k, j)))] for b in bs]
    outs, carried = _matmul(name, "tn", (m // tmw, n // tn, t // tkt), groups,
                            [((m, n), out_dtype, _spec((tmw, tn), lambda i, j, k: (i, j)))] * len(bs), (tmw, tn),
                            courier=courier)
    return (outs, carried) if courier else outs


_ANY = pl.BlockSpec(memory_space=pl.ANY)


def _comm_call(name, body, ins, out_shapes, sems, aliases=None):
    return pl.pallas_call(body, name=name, out_shape=out_shapes, in_specs=[_ANY] * len(ins),
                          out_specs=[_ANY] * len(out_shapes), input_output_aliases=aliases or {},
                          scratch_shapes=[pltpu.SemaphoreType.DMA((n,)) for n in sems])(*ins)


def _place():
    return lax.axis_index("x"), lax.axis_index("y"), lax.axis_index("c")


def _other_chips(x, y):
    return [(1 - x, y), (x, 1 - y), (1 - x, 1 - y)]


def _cast_into_slot(w, place, dtype):
    layers, rows, cols = w.shape
    half = rows // 2
    tr = _tile(half, 256, 16)
    nt = half // tr

    def body(s, w_ref, o_ref):
        del s
        o_ref[...] = w_ref[...].astype(o_ref.dtype)

    return _call("cast_into_slot", body, (layers, 2, nt),
                 [_spec((None, tr, cols), lambda l, h, i, s: (l, h * nt + i, 0))],
                 _spec((None, None, None, tr, cols), lambda l, h, i, s: (s[0], l, h, i, 0)),
                 _sds((N_CHIPS, layers, 2, half, cols), dtype), prefetch=1)(place, w)


class _GatherChips:
    aliased = True

    def __init__(self, bufs):
        self.arrays = list(bufs)
        self.out_shapes = [_sds(b.shape, b.dtype) for b in bufs]
        self.sems = [3 * len(bufs)] * 4

    def _sent(self, buf, sems):
        x, y, c = _place()
        me = 2 * x + y
        return [[pltpu.make_async_remote_copy(
            src_ref=buf[t].at[me, :, c], dst_ref=buf[t].at[me, :, c], send_sem=sems[0].at[3 * t + r],
            recv_sem=sems[1].at[3 * t + r], device_id=(px, py, c), device_id_type=MESH)
            for r, (px, py) in enumerate(_other_chips(x, y))] for t in range(len(buf))]

    def start(self, _, buf, sems):
        for row in self._sent(buf, sems):
            for cp in row:
                cp.start()

    def finish(self, _, buf, sems):
        x, y, c = _place()
        sent = self._sent(buf, sems)
        passed = []
        for t in range(len(buf)):
            for r, (px, py) in enumerate(_other_chips(x, y)):
                landed = buf[t].at[2 * px + py, :, c]
                sent[t][r].wait_recv()
                cp = pltpu.make_async_remote_copy(
                    src_ref=landed, dst_ref=landed, send_sem=sems[2].at[3 * t + r], recv_sem=sems[3].at[3 * t + r],
                    device_id=(x, y, 1 - c), device_id_type=MESH)
                cp.start()
                passed.append(cp)
        for cp in passed:
            cp.wait()
        for row in sent:
            for cp in row:
                cp.wait_send()


class _ScatterChips:
    aliased = False

    def __init__(self, partials):
        self.arrays = list(partials)
        self.out_shapes = [_sds((3,) + p.shape[1:], p.dtype) for p in partials]
        self.sems = [3 * len(partials)] * 2

    def _copies(self, src, dst, sems):
        x, y, c = _place()
        return [pltpu.make_async_remote_copy(
            src_ref=src[t].at[2 * px + py], dst_ref=dst[t].at[r], send_sem=sems[0].at[3 * t + r],
            recv_sem=sems[1].at[3 * t + r], device_id=(px, py, c), device_id_type=MESH)
            for t in range(len(src)) for r, (px, py) in enumerate(_other_chips(x, y))]

    def start(self, src, dst, sems):
        for cp in self._copies(src, dst, sems):
            cp.start()

    def finish(self, src, dst, sems):
        for cp in self._copies(src, dst, sems):
            cp.wait()


def _run_courier(name, courier):
    n_in, n_out = len(courier.arrays), len(courier.out_shapes)

    def body(*refs):
        src, dst, sems = refs[:n_in], refs[n_in:n_in + n_out], refs[n_in + n_out:]
        courier.start(src, dst, sems)
        courier.finish(src, dst, sems)

    return _comm_call(name, body, courier.arrays, courier.out_shapes, courier.sems,
                      aliases={t: t for t in range(n_in)} if courier.aliased else None)


def _swap_halves(grads):
    n = len(grads)

    def body(*refs):
        src, dst = refs[:n], refs[n:2 * n]
        send_sems, recv_sems = refs[2 * n:]
        x, y, c = _place()
        copies = [pltpu.make_async_remote_copy(
            src_ref=src[t].at[:, 1 - c], dst_ref=dst[t], send_sem=send_sems.at[t], recv_sem=recv_sems.at[t],
            device_id=(x, y, 1 - c), device_id_type=MESH) for t in range(n)]
        for cp in copies:
            cp.start()
        for cp in copies:
            cp.wait()

    return _comm_call("swap_halves", body, grads, [_sds((g.shape[0],) + g.shape[2:], g.dtype) for g in grads], [n, n])


def _join_halves(grads):
    n = len(grads)

    def body(*refs):
        buf = refs[n:2 * n]
        send_sems, recv_sems = refs[2 * n:]
        x, y, c = _place()
        copies = [pltpu.make_async_remote_copy(
            src_ref=buf[t].at[:, c], dst_ref=buf[t].at[:, c], send_sem=send_sems.at[t], recv_sem=recv_sems.at[t],
            device_id=(x, y, 1 - c), device_id_type=MESH) for t in range(n)]
        for cp in copies:
            cp.start()
        for cp in copies:
            cp.wait()

    return _comm_call("join_halves", body, grads, [_sds(g.shape, g.dtype) for g in grads], [n, n],
                      aliases={t: t for t in range(n)})


def _gather_devices(slots):
    def body(src, buf, send_sems, recv_sems):
        del src
        x, y, c = _place()
        me = 4 * x + 2 * y + c
        copies = []
        for r in range(1, N_DEV):
            fx, fy, fc = (r >> 2) & 1, (r >> 1) & 1, r & 1
            peer = (1 - x if fx else x, 1 - y if fy else y, 1 - c if fc else c)
            copies.append(pltpu.make_async_remote_copy(
                src_ref=buf.at[me], dst_ref=buf.at[me], send_sem=send_sems.at[r - 1], recv_sem=recv_sems.at[r - 1],
                device_id=peer, device_id_type=MESH))
        for cp in copies:
            cp.start()
        for cp in copies:
            cp.wait()

    return _comm_call("gather_devices", body, [slots], [_sds(slots.shape, slots.dtype)], [N_DEV - 1, N_DEV - 1],
                      aliases={0: 0})[0]


def _add_halves(grad, got, place):
    _, _, rows, cols = grad.shape
    tr = _tile(rows, 256, 16)

    def body(s, a, b, o):
        del s
        o[...] = (a[...].astype(F32) + b[...].astype(F32)).astype(o.dtype)

    return _call("add_halves", body, (N_CHIPS, rows // tr),
                 [_spec((None, None, tr, cols), lambda j, i, s: (j, s[1], i, 0)), _spec((None, tr, cols), lambda j, i, s: (j, i, 0))],
                 _spec((None, tr, cols), lambda j, i, s: (j, i, 0)), _sds((N_CHIPS, rows, cols), grad.dtype), prefetch=1
                 )(place, grad, got)


def _add_chips(partial, got, place, layer, layers, into=None):
    _, rows, cols = partial.shape
    tr = _tile(rows, 256, 16)

    def body(s, a, b, *rest):
        o = rest[-1]
        o[...] = a[...].astype(F32) + b[0].astype(F32) + b[1].astype(F32) + b[2].astype(F32)

    in_specs = [_spec((None, tr, cols), lambda i, s: (s[0], i, 0)), _spec((3, tr, cols), lambda i, s: (0, i, 0))]
    args = [place, partial, got]
    if into is not None:
        in_specs.append(pl.BlockSpec(memory_space=pl.ANY))
        args.append(into)
    return _call("add_chips", body, (rows // tr,), in_specs,
                 _spec((None, None, tr, cols), lambda i, s: (layer, s[1], i, 0)), _sds((layers, 2, rows, cols), F32),
                 prefetch=1, aliases={3: 0} if into is not None else None)(*args)


def _add_devices(got):
    _, rows, cols = got.shape
    tr = _tile(rows, 512, 8)

    def body(a, o):
        total = a[0]
        for d in range(1, N_DEV):
            total = total + a[d]
        o[...] = total

    return _call("add_devices", body, (rows // tr,), [_spec((N_DEV, tr, cols), lambda i: (0, i, 0))],
                 _spec((tr, cols), lambda i: (i, 0)), _sds((rows, cols), F32))(got)


def _adamw(w, g, m, v):
    layers, rows, cols = w.shape
    tr = _tile(rows, 256, 8)

    def body(w_ref, g_ref, m_ref, v_ref, d_ref, nm_ref, nv_ref):
        grad = g_ref[...]
        new_m = ADAM_B1 * m_ref[...] + (1.0 - ADAM_B1) * grad
        new_v = ADAM_B2 * v_ref[...] + (1.0 - ADAM_B2) * jnp.square(grad)
        m_hat = new_m / (1.0 - ADAM_B1 ** ADAM_STEP)
        v_hat = new_v / (1.0 - ADAM_B2 ** ADAM_STEP)
        d_ref[...] = -ADAM_LR * (m_hat / (jnp.sqrt(v_hat) + ADAM_EPS) + ADAM_WD * w_ref[...])
        nm_ref[...] = new_m
        nv_ref[...] = new_v

    blk = _spec((None, tr, cols), lambda l, i: (l, i, 0))
    return _call("adamw", body, (layers, rows // tr), [blk] * 4, [blk] * 3, [_sds(w.shape, F32)] * 3)(w, g, m, v)


def _rope_layout(w):
    z = jnp.zeros(w.shape[:-1] + (ROPE_HALF,), w.dtype)
    return jnp.concatenate([w[..., :ROPE_HALF], z, w[..., ROPE_HALF:], z], axis=-1)


def _rope_layout_inv(w):
    return jnp.concatenate([w[..., :ROPE_HALF], w[..., 2 * ROPE_HALF:3 * ROPE_HALF]], axis=-1)


def _cols_from_chips(g):
    return jnp.moveaxis(g, 0, 1).reshape(g.shape[1], N_CHIPS * g.shape[2])


def _cols_to_chips(w):
    return jnp.moveaxis(w.reshape(w.shape[0], N_CHIPS, w.shape[1] // N_CHIPS), 1, 0)


def _w_in_layout(w):
    off_kr = Q_LORA + KV_LORA
    return jnp.concatenate([w[:, :off_kr], w[:, off_kr + QK_ROPE:], _rope_layout(w[:, off_kr:off_kr + QK_ROPE])], axis=1)


def _w_in_layout_inv(dw):
    return jnp.concatenate([dw[:, :P_A], _rope_layout_inv(dw[:, P_KR:]), dw[:, P_A:P_KR]], axis=1)


def _w_uq_layout(w):
    w = w.reshape(Q_LORA, HEADS, QK_NOPE + QK_ROPE)
    return jnp.concatenate([w[..., :QK_NOPE].reshape(Q_LORA, HEADS * QK_NOPE),
                            _rope_layout(w[..., QK_NOPE:]).reshape(Q_LORA, HEADS * LANES)], axis=1)


def _w_uq_layout_inv(d_nope, d_rope):
    d_nope = d_nope.reshape(Q_LORA, HEADS, QK_NOPE)
    d_rope = _rope_layout_inv(d_rope.reshape(Q_LORA, HEADS, LANES))
    return jnp.concatenate([d_nope, d_rope], axis=-1).reshape(Q_LORA, HEADS * (QK_NOPE + QK_ROPE))


def _to_lanes(a):
    flat = a.reshape(-1)
    pad = (-flat.shape[0]) % LANES
    if pad:
        flat = jnp.concatenate([flat, jnp.zeros((pad,), flat.dtype)])
    return flat.reshape(-1, LANES)


PACK_ROWS = 64


def _pack(arrays):
    parts, ranges, row = [], [], 0
    for a in arrays:
        p = _to_lanes(a)
        parts.append(p)
        ranges.append((row, row + p.shape[0]))
        row += p.shape[0]
    pad = (-row) % PACK_ROWS
    if pad:
        parts.append(jnp.zeros((pad, LANES), F32))
    return jnp.concatenate(parts, axis=0), ranges


def _unpack(packed, rng, shape):
    n = 1
    for s in shape:
        n *= s
    return packed[rng[0]:rng[1]].reshape(-1)[:n].reshape(shape)


def kernel(x, positions, ffn_a_pre_g, ffn_a_post_g, ffn_a_w_gate, ffn_a_w_up, ffn_a_w_down, ffn_b_pre_g, ffn_b_post_g, ffn_b_w_gate, ffn_b_w_up, ffn_b_w_down, even_pre_g, even_post_g, even_w_in, even_q_norm_g, even_kv_norm_g, even_w_uq, even_w_ukv, even_conv_w, even_conv_b, even_conv_norm_g, even_conv_norm_b, even_w_out, odd_pre_g, odd_post_g, odd_w_in, odd_v_norm_g, odd_v_norm_b, odd_w_s, odd_b_s, odd_w_out, loss_target, m_ffn_a_pre_g, m_ffn_a_post_g, m_ffn_a_w_gate, m_ffn_a_w_up, m_ffn_a_w_down, m_ffn_b_pre_g, m_ffn_b_post_g, m_ffn_b_w_gate, m_ffn_b_w_up, m_ffn_b_w_down, m_even_pre_g, m_even_post_g, m_even_w_in, m_even_q_norm_g, m_even_kv_norm_g, m_even_w_uq, m_even_w_ukv, m_even_conv_w, m_even_conv_b, m_even_conv_norm_g, m_even_conv_norm_b, m_even_w_out, m_odd_pre_g, m_odd_post_g, m_odd_w_in, m_odd_v_norm_g, m_odd_v_norm_b, m_odd_w_s, m_odd_b_s, m_odd_w_out, v_ffn_a_pre_g, v_ffn_a_post_g, v_ffn_a_w_gate, v_ffn_a_w_up, v_ffn_a_w_down, v_ffn_b_pre_g, v_ffn_b_post_g, v_ffn_b_w_gate, v_ffn_b_w_up, v_ffn_b_w_down, v_even_pre_g, v_even_post_g, v_even_w_in, v_even_q_norm_g, v_even_kv_norm_g, v_even_w_uq, v_even_w_ukv, v_even_conv_w, v_even_conv_b, v_even_conv_norm_g, v_even_conv_norm_b, v_even_w_out, v_odd_pre_g, v_odd_post_g, v_odd_w_in, v_odd_v_norm_g, v_odd_v_norm_b, v_odd_w_s, v_odd_b_s, v_odd_w_out):
    given = dict(locals())
    w = {n: given[n] for n in WEIGHTS}
    batch, seq, d = x.shape
    t = batch * seq
    tm = _tile(t, 512, 16)
    tr = _tile(t, 256, 16)
    chip = 2 * lax.axis_index("x") + lax.axis_index("y")
    place = jnp.stack([chip, lax.axis_index("c")]).astype(jnp.int32)

    small_shard, small_shard_rng = _pack([w[n] for n in SMALL_SHARDED])
    wbuf = {(n, layer): _cast_into_slot(w[n][layer:layer + 1], place, MXU_DTYPE)
            for n in BIG for layer in range(w[n].shape[0])}
    wbuf['small'] = _cast_into_slot(small_shard[None], place, F32)

    def ffn_keys(tag, layer):
        return [(f'ffn_{tag}_w_{part}', layer) for part in ('gate', 'up', 'down')]

    def gather(keys):
        return _GatherChips([wbuf[k] for k in keys]), keys

    def landed(order, results):
        for k, r in zip(order[1], results):
            wbuf[k] = r

    def weight(key):
        n, _ = key
        return wbuf[key].reshape((N_CHIPS, 1) + w[n].shape[1:])

    first = gather(ffn_keys('a', 0)[:2] + ['small'])
    landed(first, _run_courier("gather_chips", first[0]))
    shards = wbuf['small'].reshape((N_CHIPS,) + small_shard.shape)
    full = dict(w)
    for n, rng in zip(SMALL_SHARDED, small_shard_rng):
        per_chip = [_unpack(shards[j], rng, w[n].shape) for j in range(N_CHIPS)]
        full[n] = jnp.concatenate(per_chip, axis=-1)

    inv_freq = ROPE_THETA ** (-jnp.arange(0, QK_ROPE, 2, dtype=F32) / QK_ROPE)
    ang = positions.astype(F32).reshape(t, 1) * inv_freq
    zeros = jnp.zeros((t, ROPE_HALF), F32)
    cos = jnp.concatenate([jnp.cos(ang), zeros, jnp.cos(ang), zeros], axis=1)
    sin = jnp.concatenate([-jnp.sin(ang), zeros, jnp.sin(ang), zeros], axis=1)

    def ffn_args(tag, layer):
        gate, up, down = (weight(k) for k in ffn_keys(tag, layer))
        return (w[f'ffn_{tag}_pre_g'][layer:layer + 1], w[f'ffn_{tag}_post_g'][layer:layer + 1], gate, up, down, 0, tm, tr)

    def ffn_forward(xin, tag, layer, up_keys, down_keys):
        orders = {'up': gather(ffn_keys(tag, layer)[2:] + up_keys)}
        if down_keys:
            orders['down'] = gather(down_keys)
        out, saved, carried = _ffn_fwd(xin, *ffn_args(tag, layer), couriers={k: o[0] for k, o in orders.items()}, wd_landed=0)
        for k, o in orders.items():
            landed(o, carried[k])
        return out, saved

    even_keys = [('even_w_in', 0), ('even_w_uq', 0), ('even_w_ukv', 0), ('even_w_out', 0)]
    odd_keys = [('odd_w_in', 0), ('odd_w_out', 0)]

    xs = x.reshape(t, d)
    x1, ffn_a0 = ffn_forward(xs, 'a', 0, even_keys, ffn_keys('b', 0)[:1])
    w_in = _w_in_layout(_cols_from_chips(weight(('even_w_in', 0))[:, 0]))
    w_uq = _w_uq_layout(_cols_from_chips(weight(('even_w_uq', 0))[:, 0]))
    w_ukv = _cols_from_chips(weight(('even_w_ukv', 0))[:, 0])
    w_out_e = weight(('even_w_out', 0))[:, 0].reshape(d, d)
    h_e = _rms_fwd(x1, w['even_pre_g'], tr)
    order = gather(ffn_keys('b', 0)[1:2])
    p, res = _mm_nn("mm_w_in", h_e, w_in, F32, tm, tn_want=640, courier=order[0])
    landed(order, res)
    lat = _rows("latent_norm", lambda r, c: ([jnp.concatenate([_rmsnorm(r[0], c[0]), _rmsnorm(r[1], c[1])], axis=1)], []),
                [(p, Q_LORA, 0), (p, KV_LORA, 1)], [w['even_q_norm_g'], w['even_kv_norm_g']], [(Q_LORA + KV_LORA, MXU_DTYPE)], [], tr)[0]
    q_all = _mm_nn("mm_w_uq", lat, w_uq, F32, tm, a_col0=0, tk_want=Q_LORA)
    kv = _mm_nn("mm_w_ukv", lat, w_ukv, MXU_DTYPE, tm, a_col0=Q_LORA, tk_want=KV_LORA)
    cat = _attn_fwd(q_all, kv, p, cos, sin, batch, seq)
    cat = _conv_fwd(p, cat, full['even_conv_w'][0], w['even_conv_b'], w['even_conv_norm_g'], w['even_conv_norm_b'], batch, seq)
    y_e = _mm_nn("mm_w_out", cat, w_out_e, F32, tm)
    x2 = _rms_residual(x1, y_e, w['even_post_g'], 1.0, tr)
    x3, ffn_b0 = ffn_forward(x2, 'b', 0, ffn_keys('a', 1)[:1], ffn_keys('a', 1)[1:2])
    x4, ffn_a1 = ffn_forward(x3, 'a', 1, odd_keys, ffn_keys('b', 1)[:1])
    w_out_o = weight(('odd_w_out', 0))[:, 0].reshape(GM_WIDTH, d)
    w_in_o = weight(('odd_w_in', 0))
    h_o = _rms_fwd(x4, full['odd_pre_g'], tr)
    ns = w_in_o.shape[3]
    tk_o = _tile(d, 1024, LANES)
    order = gather(ffn_keys('b', 1)[1:2])
    (hw,), res = _matmul("mm_w_in_odd", "nn", (t // tm, N_CHIPS, d // tk_o),
                         [[(h_o, _spec((tm, tk_o), lambda i, j, k: (i, k)), w_in_o, _spec((None, None, tk_o, ns), lambda i, j, k: (j, 0, k, 0)))]],
                         [((t, N_CHIPS * ns), F32, _spec((tm, ns), lambda i, j, k: (i, j)))], (tm, ns), courier=order[0])
    landed(order, res)
    b_st = w['odd_b_s'][0].T
    sg = _sgu_fwd(hw, full['odd_v_norm_g'], full['odd_v_norm_b'], w['odd_w_s'][0], b_st)
    y_o = _mm_nn("mm_w_out", sg, w_out_o, F32, tm)
    x5 = _rms_residual(x4, y_o, full['odd_post_g'], 1.0, tr)
    x6, ffn_b1 = ffn_forward(x5, 'b', 1, [], [])

    dy, sq_err = _loss_head(x6, loss_target.reshape(t, d), tr)
    loss = lax.psum(0.5 * sq_err[0, 0] / d, ("x", "y", "c"))

    grads = {}
    received = {}

    def core_sums(named):
        items = [g.reshape(N_CHIPS, 2, g.shape[1] // 2, g.shape[2]) for _, g in named]
        got = _swap_halves(items)
        return [(k, _add_halves(g, r, place)) for (k, _), g, r in zip(named, items, got)]

    def scatter(named):
        return _ScatterChips([pt for _, pt in named]), named

    def arrived(order, results):
        for (k, pt), r in zip(order[1], results):
            received[k] = (pt, r)

    def ffn_backward(dout, saved, tag, layer, orders):
        dxin, g, carried = _ffn_bwd(dout, saved, *ffn_args(tag, layer), couriers={k: o[0] for k, o in orders.items()})
        for k, o in orders.items():
            arrived(o, carried[k])
        return dxin, g, core_sums(list(zip(ffn_keys(tag, layer), g[2:])))

    fg = {}
    dx, fg[('b', 1)], pend = ffn_backward(dy, ffn_b1, 'b', 1, {})
    dy_o, grads['odd_post_g'] = _rms_bwd(y_o, full['odd_post_g'], dx, 1.0, MXU_DTYPE, tr)
    dsg = _mm_nt("mm_dsg", [(dy_o, w_out_o, 0)], F32, tm)
    (g_w_out_o,) = _mm_tn("mm_dw_out", sg, [dy_o], MXU_DTYPE)
    dhw, grads['odd_v_norm_g'], grads['odd_v_norm_b'], g_ws, g_bst = _sgu_bwd(
        hw, full['odd_v_norm_g'], full['odd_v_norm_b'], w['odd_w_s'][0], b_st, dsg)
    grads['odd_w_s'] = g_ws[None]
    grads['odd_b_s'] = g_bst[:, :GROUPS].T[None]
    tn_o = _tile(d, 1024, LANES)
    order = scatter(pend[0:1])
    (dh_o,), res = _matmul("mm_dh_odd", "nt", (t // tm, d // tn_o, N_CHIPS),
                           [[(dhw, _spec((tm, ns), lambda i, j, k: (i, k)), w_in_o, _spec((None, None, tn_o, ns), lambda i, j, k: (k, 0, j, 0)))]],
                           [((t, d), F32, _spec((tm, tn_o), lambda i, j, k: (i, j)))], (tm, tn_o), courier=order[0])
    arrived(order, res)
    tmw = _tile(d, 512, LANES)
    tkt = _tile(t, TN_ROWS, 16)
    order = scatter(pend[1:2])
    (g_w_in_o,), res = _matmul("mm_dw_in_odd", "tn", (d // tmw, N_CHIPS, t // tkt),
                               [[(h_o, _spec((tkt, tmw), lambda i, j, k: (k, i)), dhw, _spec((tkt, ns), lambda i, j, k: (k, j)))]],
                               [((N_CHIPS, d, ns), MXU_DTYPE, _spec((None, tmw, ns), lambda i, j, k: (j, i, 0)))], (tmw, ns),
                               courier=order[0])
    arrived(order, res)
    pend_odd = core_sums([(('odd_w_in', 0), g_w_in_o),
                          (('odd_w_out', 0), g_w_out_o.reshape((N_CHIPS,) + w['odd_w_out'].shape[1:]))])
    dx, grads['odd_pre_g'] = _rms_bwd(x4, full['odd_pre_g'], dh_o, 1.0, F32, tr, resid=dx)
    dx, fg[('a', 1)], pend = ffn_backward(dx, ffn_a1, 'a', 1, {'dact': scatter(pend[2:3]), 'dwdown': scatter(pend_odd)})
    dx, fg[('b', 0)], pend = ffn_backward(dx, ffn_b0, 'b', 0, {'dact': scatter(pend[0:1]), 'dh': scatter(pend[1:2]),
                                                              'dwup': scatter(pend[2:3])})
    dy_e, grads['even_post_g'] = _rms_bwd(y_e, w['even_post_g'], dx, 1.0, MXU_DTYPE, tr)
    d_cat = _mm_nt("mm_dcat", [(dy_e, w_out_e, 0)], F32, tm)
    (g_w_out_e,) = _mm_tn("mm_dw_out", cat, [dy_e], MXU_DTYPE)
    dqn, dqp, dkv, dkr = _attn_bwd(q_all, kv, p, cos, sin, d_cat, batch, seq)
    hq = HEADS * LANES
    d_latq = _mm_nt("mm_dlat_q", [(dqn, w_uq, 0), (dqp, w_uq, hq)], F32, tm, tn_want=Q_LORA)
    d_latkv = _mm_nt("mm_dlat_kv", [(dkv, w_ukv, 0)], F32, tm, tn_want=KV_LORA)
    g_uq_n, g_uq_r = _mm_tn("mm_dw_uq", lat, [dqn, dqp], MXU_DTYPE, a_col0=0, m_width=Q_LORA)
    (g_ukv,) = _mm_tn("mm_dw_ukv", lat, [dkv], MXU_DTYPE, a_col0=Q_LORA, m_width=KV_LORA)

    def latent_bwd(r, c):
        _, vjp_q = jax.vjp(_rmsnorm, r[0], c[0])
        _, vjp_kv = jax.vjp(_rmsnorm, r[1], c[1])
        dq, dqg = vjp_q(r[2])
        dk, dkg = vjp_kv(r[3])
        return [jnp.concatenate([dq, dk], axis=1)], [dqg, dkg]

    d_lat, grads['even_q_norm_g'], grads['even_kv_norm_g'] = _rows(
        "latent_norm_bwd", latent_bwd, [(p, Q_LORA, 0), (p, KV_LORA, 1), (d_latq, Q_LORA, 0), (d_latkv, KV_LORA, 0)],
        [w['even_q_norm_g'], w['even_kv_norm_g']], [(Q_LORA + KV_LORA, MXU_DTYPE)], [(1, Q_LORA), (1, KV_LORA)], tr)
    da, dgate, g_conv_w, grads['even_conv_b'], grads['even_conv_norm_g'], grads['even_conv_norm_b'] = _conv_bwd(
        p, d_cat, full['even_conv_w'][0], w['even_conv_b'], w['even_conv_norm_g'], w['even_conv_norm_b'], batch, seq)
    dp = jnp.concatenate([d_lat, da, dgate, dkr.astype(MXU_DTYPE)], axis=1)
    order = scatter(pend[0:1])
    dh_e, res = _mm_nt("mm_dh_even", [(dp, w_in, 0)], F32, tm, tk_want=640, courier=order[0])
    arrived(order, res)
    order = scatter(pend[1:2])
    (g_w_in,), res = _mm_tn("mm_dw_in", h_e, [dp], MXU_DTYPE, tn_want=640, courier=order[0])
    arrived(order, res)
    pend_even = core_sums([(('even_w_in', 0), _cols_to_chips(_w_in_layout_inv(g_w_in))),
                           (('even_w_uq', 0), _cols_to_chips(_w_uq_layout_inv(g_uq_n, g_uq_r))),
                           (('even_w_ukv', 0), _cols_to_chips(g_ukv)),
                           (('even_w_out', 0), g_w_out_e.reshape((N_CHIPS,) + w['even_w_out'].shape[1:]))])
    dx, grads['even_pre_g'] = _rms_bwd(x1, w['even_pre_g'], dh_e, 1.0, F32, tr, resid=dx)
    early = {'dact': scatter(pend[2:3]), 'dwdown': scatter(pend_even)}
    late = {}

    def after_w_down(dwd):
        late['dwup'] = scatter(core_sums([(ffn_keys('a', 0)[2], dwd)]))
        return late['dwup'][0]

    def after_w_up(dwg, dwu):
        late['dh'] = scatter(core_sums(list(zip(ffn_keys('a', 0)[:2], (dwg, dwu)))))
        return late['dh'][0]

    dx, fg[('a', 0)], carried = _ffn_bwd(
        dx, ffn_a0, *ffn_args('a', 0), weights_first=True,
        couriers={'dact': early['dact'][0], 'dwdown': early['dwdown'][0], 'dwup': after_w_down, 'dh': after_w_up})
    for k, o in {**early, **late}.items():
        arrived(o, carried[k])
    grad_x = dx.reshape(batch, seq, d)

    reduced = []
    for n in BIG:
        layers = w[n].shape[0]
        buf = None
        for layer in range(layers):
            pt, r = received[(n, layer)]
            buf = _add_chips(pt, r, place, layer, layers, into=buf)
        reduced.append(buf)
    joined = _join_halves(reduced)
    for n, g in zip(BIG, joined):
        grads[n] = g.reshape(w[n].shape)

    for tag in ('a', 'b'):
        grads[f'ffn_{tag}_pre_g'] = jnp.concatenate([fg[(tag, 0)][0], fg[(tag, 1)][0]], axis=0)
        grads[f'ffn_{tag}_post_g'] = jnp.concatenate([fg[(tag, 0)][1], fg[(tag, 1)][1]], axis=0)
    grads['even_conv_w'] = g_conv_w[None]
    packed, rngs = _pack([grads[n] for n in SMALL])
    device = 2 * chip + lax.axis_index("c")
    slots = lax.dynamic_update_slice_in_dim(jnp.zeros((N_DEV,) + packed.shape, F32), packed[None], device, axis=0)
    summed = _add_devices(_gather_devices(slots))
    for n, rng in zip(SMALL, rngs):
        g = _unpack(summed, rng, grads[n].shape)
        if n in SMALL_SHARDED:
            width = w[n].shape[-1]
            g = lax.dynamic_slice_in_dim(g, chip * width, width, axis=g.ndim - 1)
        grads[n] = g

    delta, new_m, new_v = {}, {}, {}
    for n in BIG:
        delta[n], new_m[n], new_v[n] = _adamw(w[n], grads[n], given['m_' + n], given['v_' + n])
    packs = [_pack([src[n] for n in SMALL])[0][None] for src in
             (w, grads, {n: given['m_' + n] for n in SMALL}, {n: given['v_' + n] for n in SMALL})]
    _, rngs = _pack([w[n] for n in SMALL])
    small_out = _adamw(*packs)
    for n, rng in zip(SMALL, rngs):
        delta[n], new_m[n], new_v[n] = (_unpack(o[0], rng, w[n].shape) for o in small_out)

    return (loss, grad_x, *[grads[n] for n in WEIGHTS], *[delta[n] for n in WEIGHTS],
            *[new_m[n] for n in WEIGHTS], *[new_v[n] for n in WEIGHTS])
```

```python
import functools

import jax
import jax.numpy as jnp
from jax import lax
from jax.experimental import pallas as pl
from jax.experimental.pallas import tpu as pltpu

F32 = jnp.float32
BF16 = jnp.bfloat16
MXU_DTYPE = BF16
MESH = pl.DeviceIdType.MESH
VMEM_LIMIT_BYTES = 56 * 1024 * 1024
LANES = 128
TN_ROWS = 2048

D_MODEL = 2048
D_FF = 5632
EPS = 1e-6
HEADS = 8
V_HEAD = 128
QK_NOPE = 128
QK_ROPE = 64
Q_LORA = 512
KV_LORA = 512
ROPE_THETA = 10000.0
CONV_CH = 1024
CONV_WIDTH = 31
GROUPS = 8
CHUNK = 128
GM_WIDTH = 2048
ADAM_LR = 0.001
ADAM_B1 = 0.9
ADAM_B2 = 0.999
ADAM_EPS = 1e-08
ADAM_WD = 0.01
ADAM_STEP = 10

N_CHIPS = 4
N_DEV = 8
P_KV = Q_LORA
P_A = Q_LORA + KV_LORA
P_GATE = P_A + CONV_CH
P_KR = P_GATE + CONV_CH
P_WIDTH = P_KR + LANES
ROPE_HALF = QK_ROPE // 2

WEIGHTS = ['ffn_a_pre_g', 'ffn_a_post_g', 'ffn_a_w_gate', 'ffn_a_w_up', 'ffn_a_w_down', 'ffn_b_pre_g', 'ffn_b_post_g',
           'ffn_b_w_gate', 'ffn_b_w_up', 'ffn_b_w_down', 'even_pre_g', 'even_post_g', 'even_w_in', 'even_q_norm_g',
           'even_kv_norm_g', 'even_w_uq', 'even_w_ukv', 'even_conv_w', 'even_conv_b', 'even_conv_norm_g',
           'even_conv_norm_b', 'even_w_out', 'odd_pre_g', 'odd_post_g', 'odd_w_in', 'odd_v_norm_g', 'odd_v_norm_b',
           'odd_w_s', 'odd_b_s', 'odd_w_out']
BIG = ['ffn_a_w_gate', 'ffn_a_w_up', 'ffn_a_w_down', 'ffn_b_w_gate', 'ffn_b_w_up', 'ffn_b_w_down', 'even_w_in',
       'even_w_uq', 'even_w_ukv', 'even_w_out', 'odd_w_in', 'odd_w_out']
SMALL = [n for n in WEIGHTS if n not in BIG]
SMALL_SHARDED = ['even_conv_w', 'odd_pre_g', 'odd_post_g', 'odd_v_norm_g', 'odd_v_norm_b']


def _call(name, body, grid, in_specs, out_specs, out_shape, scratch=(), prefetch=0, aliases=None):
    params = pltpu.CompilerParams(dimension_semantics=("arbitrary",) * len(grid), vmem_limit_bytes=VMEM_LIMIT_BYTES)
    if prefetch:
        spec = pltpu.PrefetchScalarGridSpec(num_scalar_prefetch=prefetch, grid=grid, in_specs=in_specs,
                                            out_specs=out_specs, scratch_shapes=list(scratch))
        return pl.pallas_call(body, grid_spec=spec, out_shape=out_shape, compiler_params=params, name=name,
                              input_output_aliases=aliases or {})
    return pl.pallas_call(body, grid=grid, in_specs=in_specs, out_specs=out_specs, out_shape=out_shape,
                          scratch_shapes=list(scratch), compiler_params=params, name=name,
                          input_output_aliases=aliases or {})


def _spec(block, index_map):
    return pl.BlockSpec(block, index_map)


def _sds(shape, dtype):
    return jax.ShapeDtypeStruct(tuple(shape), dtype)


def _tile(n, want, mult=8):
    if n <= want:
        return n
    best = None
    for t in range(mult, want + 1, mult):
        if n % t == 0:
            best = t
    assert best is not None, (n, want, mult)
    return best


_DIMS = {"nn": (((1,), (0,)), ((), ())), "nt": (((1,), (1,)), ((), ())), "tn": (((0,), (0,)), ((), ()))}


def _dot(a, b, mode="nn"):
    return lax.dot_general(a.astype(MXU_DTYPE), b.astype(MXU_DTYPE), _DIMS[mode], preferred_element_type=F32)


def _matmul(name, mode, grid, groups, outs, acc_shape, extras=(), epilogue=None, courier=None, sum_outs=0):
    flat, specs = [], []
    for grp in groups:
        for a, a_spec, b, b_spec in grp:
            flat += [a, b]
            specs += [a_spec, b_spec]
    for e, e_spec in extras:
        flat.append(e)
        specs.append(e_spec)
    n_pairs = [len(g) for g in groups]
    one_step = grid[2] == 1
    n_in, n_ex, n_out, n_acc = 2 * sum(n_pairs), len(extras), len(outs), 0 if one_step else len(groups)
    last = tuple(g - 1 for g in grid)
    c_arrays = list(courier.arrays) if courier else []
    c_shapes = list(courier.out_shapes) if courier else []
    n_ci, n_co = len(c_arrays), len(c_shapes)
    any_spec = pl.BlockSpec(memory_space=pl.ANY)

    def body(*refs):
        ins, ex = refs[:n_in], refs[n_in:n_in + n_ex]
        pos = n_in + n_ex
        c_in = refs[pos:pos + n_ci]
        out_refs = refs[pos + n_ci:pos + n_ci + n_out]
        c_out = refs[pos + n_ci + n_out:pos + n_ci + n_out + n_co]
        accs = refs[pos + n_ci + n_out + n_co:pos + n_ci + n_out + n_co + n_acc]
        sems = refs[pos + n_ci + n_out + n_co + n_acc:]
        i, j, k = pl.program_id(0), pl.program_id(1), pl.program_id(2)

        if courier:
            @pl.when((i == 0) & (j == 0) & (k == 0))
            def _():
                courier.start(c_in, c_out, sems)

        def products():
            pos, totals = 0, []
            for n in n_pairs:
                total = None
                for _ in range(n):
                    d = _dot(ins[pos][...], ins[pos + 1][...], mode)
                    total = d if total is None else total + d
                    pos += 2
                totals.append(total)
            return totals

        summed = out_refs[n_out - sum_outs:] if sum_outs else ()
        if summed:
            @pl.when((i == 0) & (j == 0) & (k == 0))
            def _():
                for o in summed:
                    o[...] = jnp.zeros(o.shape, F32)

        def finish(vals):
            res = epilogue(vals, [e[...] for e in ex]) if epilogue else vals
            for idx, (o, r) in enumerate(zip(out_refs, res)):
                if idx >= n_out - sum_outs:
                    o[...] += r
                else:
                    o[...] = r.astype(o.dtype)

        if one_step:
            finish(products())
        else:
            @pl.when(k == 0)
            def _():
                for acc in accs:
                    acc[...] = jnp.zeros(acc.shape, F32)

            for acc, total in zip(accs, products()):
                acc[...] += total

            @pl.when(k == last[2])
            def _():
                finish([acc[...] for acc in accs])

        if courier:
            @pl.when((i == last[0]) & (j == last[1]) & (k == last[2]))
            def _():
                courier.finish(c_in, c_out, sems)

    aliases = {n_in + n_ex + t: n_out + t for t in range(n_ci)} if courier and courier.aliased else None
    scratch = [pltpu.VMEM(acc_shape, F32)] * n_acc
    scratch += [pltpu.SemaphoreType.DMA((n,)) for n in courier.sems] if courier else []
    res = _call(name, body, grid, specs + [any_spec] * n_ci, [o[2] for o in outs] + [any_spec] * n_co,
                [_sds(o[0], o[1]) for o in outs] + c_shapes, scratch=scratch, aliases=aliases)(*flat, *c_arrays)
    return list(res[:n_out]), list(res[n_out:])


def _rows(name, fn, rows, consts, outs, accs, tm):
    t_rows = rows[0][0].shape[0]
    grid = (t_rows // tm,)
    in_specs = [_spec((tm, bw), functools.partial(lambda i, cb: (i, cb), cb=cb)) for _, bw, cb in rows]
    in_specs += [_spec(c.shape, functools.partial(lambda i, nd: (0,) * nd, nd=c.ndim)) for c in consts]
    out_shape = [_sds((t_rows, w), dt) for w, dt in outs] + [_sds(s, F32) for s in accs]
    out_specs = [_spec((tm, w), lambda i: (i, 0)) for w, _ in outs]
    out_specs += [_spec(s, functools.partial(lambda i, nd: (0,) * nd, nd=len(s))) for s in accs]
    nr, nc, no = len(rows), len(consts), len(outs)

    def body(*refs):
        r = [ref[...] for ref in refs[:nr]]
        c = [ref[...] for ref in refs[nr:nr + nc]]
        o_refs, a_refs = refs[nr + nc:nr + nc + no], refs[nr + nc + no:]
        o_vals, a_vals = fn(r, c)
        for ref, v in zip(o_refs, o_vals):
            ref[...] = v.astype(ref.dtype)
        if a_refs:
            @pl.when(pl.program_id(0) == 0)
            def _():
                for ref in a_refs:
                    ref[...] = jnp.zeros(ref.shape, F32)

            for ref, v in zip(a_refs, a_vals):
                ref[...] += v

    return _call(name, body, grid, in_specs, out_specs, out_shape)(*[a for a, _, _ in rows], *consts)


def _rmsnorm(x, g):
    return x * lax.rsqrt(jnp.mean(x * x, axis=-1, keepdims=True) + EPS) * g


def _layernorm(x, g, b):
    mu = jnp.mean(x, axis=-1, keepdims=True)
    var = jnp.mean(jnp.square(x - mu), axis=-1, keepdims=True)
    return (x - mu) * lax.rsqrt(var + EPS) * g + b


def _swiglu_act(g, u):
    return jax.nn.silu(g) * u


def _rope(x, cos, sin_signed):
    return x * cos + pltpu.roll(x, 2 * ROPE_HALF, 1) * sin_signed


def _rope_transposed(dy, cos, sin_signed):
    return dy * cos - pltpu.roll(dy, 2 * ROPE_HALF, 1) * sin_signed


def _rms_fwd(x, g, tm):
    d = x.shape[1]
    return _rows("rms_fwd", lambda r, c: ([_rmsnorm(r[0], c[0])], []), [(x, d, 0)], [g], [(d, MXU_DTYPE)], [], tm)[0]


def _rms_residual(x, y, g, scale, tm):
    d = x.shape[1]
    return _rows("rms_residual", lambda r, c: ([r[0] + scale * _rmsnorm(r[1], c[0])], []),
                 [(x, d, 0), (y, d, 0)], [g], [(d, F32)], [], tm)[0]


def _rms_bwd(y, g, dout, scale, out_dtype, tm, resid=None):
    d = y.shape[1]

    def fn(r, c):
        _, vjp = jax.vjp(_rmsnorm, r[0], c[0])
        dy, dg = vjp(scale * r[1].astype(F32))
        if resid is not None:
            dy = dy + r[2]
        return [dy], [dg]

    rows = [(y, d, 0), (dout, d, 0)] + ([(resid, d, 0)] if resid is not None else [])
    return _rows("rms_bwd" if resid is None else "rms_bwd_resid", fn, rows, [g], [(d, out_dtype)], [(1, d)], tm)


def _loss_head(y, target, tm):
    d = y.shape[1]

    def fn(r, c):
        err = r[0] - r[1]
        sq = jnp.sum(jnp.sum(err * err, axis=1, keepdims=True), axis=0, keepdims=True)
        return [err * (1.0 / d)], [jnp.broadcast_to(sq, (1, LANES))]

    return _rows("loss_head", fn, [(y, d, 0), (target, d, 0)], [], [(d, F32)], [(1, LANES)], tm)


def _ffn_fwd(x, pre_g, post_g, wg, wu, wd, layer, tm, tr, couriers=None, wd_landed=None, h=None, next_g=None):
    t, d = x.shape
    fs = wg.shape[3]
    if h is None:
        h = _rms_fwd(x, pre_g, tr)
    h_spec = _spec((tm, d), lambda j, i, k: (i, 0))
    w_spec = _spec((None, None, d, fs), lambda j, i, k: (j, layer, 0, 0))
    o_spec = _spec((tm, fs), lambda j, i, k: (i, j))
    couriers = couriers or {}
    carried = {}
    (g, u, a), carried["up"] = _matmul(
        "ffn_up", "nn", (N_CHIPS, t // tm, 1), [[(h, h_spec, wg, w_spec)], [(h, h_spec, wu, w_spec)]],
        [((t, N_CHIPS * fs), MXU_DTYPE, o_spec)] * 3, (tm, fs),
        epilogue=lambda accs, _: (accs[0], accs[1], _swiglu_act(accs[0], accs[1])), courier=couriers.get("up"))
    if wd_landed is not None:
        wd = carried["up"][wd_landed].reshape(wd.shape)
    row_spec = _spec((tm, d), lambda i, j, k: (i, 0))
    vec_spec = _spec((1, d), lambda i, j, k: (0, 0))

    def post(accs, ex):
        out = ex[0] + 0.5 * _rmsnorm(accs[0], ex[1])
        return (accs[0], out) + ((_rmsnorm(out, ex[2]),) if next_g is not None else ())

    res, carried["down"] = _matmul(
        "ffn_down", "nn", (t // tm, 1, N_CHIPS),
        [[(a, _spec((tm, fs), lambda i, j, k: (i, k)), wd, _spec((None, None, fs, d), lambda i, j, k: (k, layer, 0, 0)))]],
        [((t, d), F32, row_spec)] * 2 + ([((t, d), MXU_DTYPE, row_spec)] if next_g is not None else []), (tm, d),
        extras=[(x, row_spec), (post_g, vec_spec)] + ([(next_g, vec_spec)] if next_g is not None else []),
        epilogue=post, courier=couriers.get("down"))
    y, out = res[0], res[1]
    return out, (x, h, g, u, a, y), carried, (res[2] if next_g is not None else None)


def _ffn_bwd(dout, saved, pre_g, post_g, wg, wu, wd, layer, tm, tr, couriers=None, weights_first=False):
    x, h, g, u, a, y = saved
    t, d = x.shape
    fs = wg.shape[3]
    tn = _tile(d, 1024, LANES)
    tkt = _tile(t, TN_ROWS, 16)
    dy, dpost = _rms_bwd(y, post_g, dout, 0.5, MXU_DTYPE, tr)

    def act_bwd(accs, ex):
        _, vjp = jax.vjp(_swiglu_act, ex[0].astype(F32), ex[1].astype(F32))
        return vjp(accs[0])

    couriers = couriers or {}
    carried = {}
    gu_spec = _spec((tm, fs), lambda j, i, k: (i, j))
    (dg, du), carried["dact"] = _matmul(
        "ffn_dact", "nt", (N_CHIPS, t // tm, 1),
        [[(dy, _spec((tm, d), lambda j, i, k: (i, 0)), wd, _spec((None, None, fs, d), lambda j, i, k: (j, layer, 0, 0)))]],
        [((t, N_CHIPS * fs), MXU_DTYPE, gu_spec)] * 2, (tm, fs),
        extras=[(g, gu_spec), (u, gu_spec)], epilogue=act_bwd, courier=couriers.get("dact"))
    (dwd,), carried["dwdown"] = _matmul(
        "ffn_dwdown", "tn", (N_CHIPS, d // tn, t // tkt),
        [[(a, _spec((tkt, fs), lambda i, j, k: (k, i)), dy, _spec((tkt, tn), lambda i, j, k: (k, j)))]],
        [((N_CHIPS, fs, d), MXU_DTYPE, _spec((None, fs, tn), lambda i, j, k: (i, 0, j)))], (fs, tn),
        courier=couriers.get("dwdown"))
    def d_hidden(courier):
        da_spec = _spec((tm, fs), lambda i, j, k: (i, k))
        wt_spec = _spec((None, None, d, fs), lambda i, j, k: (k, layer, 0, 0))
        return _matmul("ffn_dh", "nt", (t // tm, 1, N_CHIPS), [[(dg, da_spec, wg, wt_spec), (du, da_spec, wu, wt_spec)]],
                       [((t, d), F32, _spec((tm, d), lambda i, j, k: (i, 0)))], (tm, d), courier=courier)

    def d_w_up(courier):
        tmw = _tile(d, 512, LANES)
        h_spec = _spec((tkt, tmw), lambda i, j, k: (k, i))
        dgu_spec = _spec((tkt, fs), lambda i, j, k: (k, j))
        dw_spec = _spec((None, tmw, fs), lambda i, j, k: (j, i, 0))
        return _matmul("ffn_dwup", "tn", (d // tmw, N_CHIPS, t // tkt), [[(h, h_spec, dg, dgu_spec)], [(h, h_spec, du, dgu_spec)]],
                       [((N_CHIPS, d, fs), MXU_DTYPE, dw_spec)] * 2, (tmw, fs), courier=courier)

    if weights_first:
        (dwg, dwu), carried["dwup"] = d_w_up(couriers["dwup"](dwd))
        (dh,), carried["dh"] = d_hidden(couriers["dh"](dwg, dwu))
    else:
        (dh,), carried["dh"] = d_hidden(couriers.get("dh"))
        (dwg, dwu), carried["dwup"] = d_w_up(couriers.get("dwup"))
    dx, dpre = _rms_bwd(x, pre_g, dh, 1.0, F32, tr, resid=dout)
    return dx, (dpre, dpost, dwg, dwu, dwd), carried


def _attn_scores(qn, qpe, kn, kpe, qi, tq, seq):
    s = (_dot(qn, kn, "nt") + _dot(qpe, kpe, "nt")) * ((QK_NOPE + QK_ROPE) ** -0.5)
    rows = qi * tq + lax.broadcasted_iota(jnp.int32, (tq, seq), 0)
    cols = lax.broadcasted_iota(jnp.int32, (tq, seq), 1)
    s = jnp.where(cols <= rows, s, -jnp.inf)
    e = jnp.exp(s - jnp.max(s, axis=1, keepdims=True))
    return e / jnp.sum(e, axis=1, keepdims=True)


KEY_EXTENTS = 4


def _for_key_extent(qi, tq, seq, attend):
    n = KEY_EXTENTS if seq % (KEY_EXTENTS * tq) == 0 else 1
    ext = seq // n
    mine = (qi * tq) // ext
    for e in range(n):
        pl.when(mine == e)(functools.partial(attend, (e + 1) * ext))


def _attn_specs(nq, tq, seq):
    q_rows = lambda b, h, qi: b * nq + qi
    return [
        _spec((tq, LANES), lambda b, h, qi: (q_rows(b, h, qi), h)),
        _spec((tq, LANES), lambda b, h, qi: (q_rows(b, h, qi), HEADS + h)),
        _spec((seq, 2 * LANES), lambda b, h, qi: (b, h)),
        _spec((seq, LANES), lambda b, h, qi: (b, P_KR // LANES)),
        _spec((tq, LANES), lambda b, h, qi: (q_rows(b, h, qi), 0)),
        _spec((tq, LANES), lambda b, h, qi: (q_rows(b, h, qi), 0)),
        _spec((seq, LANES), lambda b, h, qi: (b, 0)),
        _spec((seq, LANES), lambda b, h, qi: (b, 0)),
    ]


def _attn_fwd(q_all, kv, p, cos, sin, batch, seq):
    t = q_all.shape[0]
    tq = _tile(seq, 256, 16)
    nq = seq // tq

    def body(qn_ref, qp_ref, kv_ref, kr_ref, cq, sq, ck, sk, o_ref):
        qi = pl.program_id(2)
        qpe = _rope(qp_ref[...], cq[...], sq[...])

        def attend(keys):
            kpe = _rope(kr_ref[:keys, :], ck[:keys, :], sk[:keys, :])
            prob = _attn_scores(qn_ref[...], qpe, kv_ref[:keys, :LANES], kpe, qi, tq, keys)
            o_ref[...] = _dot(prob, kv_ref[:keys, LANES:]).astype(o_ref.dtype)

        _for_key_extent(qi, tq, seq, attend)

    return _call("attn_fwd", body, (batch, HEADS, nq), _attn_specs(nq, tq, seq),
                 _spec((tq, LANES), lambda b, h, qi: (b * nq + qi, h)), _sds((t, 2 * HEADS * V_HEAD), MXU_DTYPE)
                 )(q_all, q_all, kv, p, cos, sin, cos, sin)


def _attn_bwd(q_all, kv, p, cos, sin, d_cat, batch, seq):
    t = q_all.shape[0]
    tq = _tile(seq, 256, 16)
    nq = seq // tq

    def body(qn_ref, qp_ref, kv_ref, kr_ref, cq, sq, ck, sk, do_ref, dqn_ref, dqp_ref, dkv_ref, dkr_ref, dkpe_acc):
        h, qi = pl.program_id(1), pl.program_id(2)
        qn = qn_ref[...]
        qpe = _rope(qp_ref[...], cq[...], sq[...])
        do = do_ref[...]

        @pl.when(qi == 0)
        def _():
            dkv_ref[...] = jnp.zeros(dkv_ref.shape, F32)

        @pl.when((h == 0) & (qi == 0))
        def _():
            dkpe_acc[...] = jnp.zeros(dkpe_acc.shape, F32)

        def attend(keys):
            kn, v = kv_ref[:keys, :LANES], kv_ref[:keys, LANES:]
            kpe = _rope(kr_ref[:keys, :], ck[:keys, :], sk[:keys, :])
            prob = _attn_scores(qn, qpe, kn, kpe, qi, tq, keys)
            dprob = _dot(do, v, "nt")
            ds = prob * (dprob - jnp.sum(prob * dprob, axis=1, keepdims=True)) * ((QK_NOPE + QK_ROPE) ** -0.5)
            dqn_ref[...] = _dot(ds, kn)
            dqp_ref[...] = _rope_transposed(_dot(ds, kpe), cq[...], sq[...])
            dkv_ref[:keys, :LANES] += _dot(ds, qn, "tn")
            dkv_ref[:keys, LANES:] += _dot(prob, do, "tn")
            dkpe_acc[:keys, :] += _dot(ds, qpe, "tn")

        _for_key_extent(qi, tq, seq, attend)

        @pl.when((h == HEADS - 1) & (qi == nq - 1))
        def _():
            dkr_ref[...] = _rope_transposed(dkpe_acc[...], ck[...], sk[...])

    q_out = _spec((tq, LANES), lambda b, h, qi: (b * nq + qi, h))
    return _call("attn_bwd", body, (batch, HEADS, nq),
                 _attn_specs(nq, tq, seq) + [_spec((tq, LANES), lambda b, h, qi: (b * nq + qi, h))],
                 [q_out, q_out, _spec((seq, 2 * LANES), lambda b, h, qi: (b, h)), _spec((seq, LANES), lambda b, h, qi: (b, 0))],
                 [_sds((t, HEADS * LANES), F32), _sds((t, HEADS * LANES), F32), _sds((t, HEADS * 2 * LANES), F32), _sds((t, LANES), F32)],
                 scratch=[pltpu.VMEM((seq, LANES), F32)])(q_all, q_all, kv, p, cos, sin, cos, sin, d_cat)


CONV_PAD = 32


def _conv_taps(w_ref, src_ref, first_row, n_rows, init, offset):
    acc = init
    for k in range(CONV_WIDTH):
        acc = acc + w_ref[k:k + 1, :] * src_ref[pl.ds(first_row + offset(k), n_rows), :]
    return acc


def _norm_act(conv, g, b):
    return jax.nn.silu(_layernorm(conv, g, b))


def _conv_fwd(p, cat, conv_w, conv_b, norm_g, norm_b, batch, seq):
    rc = _tile(seq, 256, 8)
    a_blk, gate_blk, out_blk = P_A // LANES, P_GATE // LANES, (HEADS * V_HEAD) // LANES

    def body(a_ref, gate_ref, w_ref, cb_ref, ng_ref, nb_ref, cat_in, o_ref, zp):
        del cat_in
        zp[pl.ds(0, CONV_PAD), :] = jnp.zeros((CONV_PAD, LANES), F32)
        zp[pl.ds(CONV_PAD, seq), :] = a_ref[...] * jax.nn.sigmoid(gate_ref[...])
        for r0 in range(0, seq, rc):
            conv = _conv_taps(w_ref, zp, r0, rc, jnp.broadcast_to(cb_ref[...], (rc, LANES)),
                              lambda k: CONV_PAD - (CONV_WIDTH - 1) + k)
            o_ref[pl.ds(r0, rc), :] = _norm_act(conv, ng_ref[...], nb_ref[...]).astype(o_ref.dtype)

    vec = _spec((1, LANES), lambda b, g: (0, g))
    return _call("conv_fwd", body, (batch, GROUPS),
                 [_spec((seq, LANES), lambda b, g: (b, a_blk + g)), _spec((seq, LANES), lambda b, g: (b, gate_blk + g)),
                  _spec((CONV_WIDTH, LANES), lambda b, g: (0, g)), vec, vec, vec, pl.BlockSpec(memory_space=pl.ANY)],
                 _spec((seq, LANES), lambda b, g: (b, out_blk + g)), _sds(cat.shape, cat.dtype),
                 scratch=[pltpu.VMEM((seq + CONV_PAD, LANES), F32)], aliases={6: 0}
                 )(p, p, conv_w, conv_b, norm_g, norm_b, cat)


def _conv_bwd(p, d_cat, conv_w, conv_b, norm_g, norm_b, batch, seq):
    t = p.shape[0]
    rc = _tile(seq, 256, 8)
    a_blk, gate_blk, out_blk = P_A // LANES, P_GATE // LANES, (HEADS * V_HEAD) // LANES

    def body(a_ref, gate_ref, w_ref, cb_ref, ng_ref, nb_ref, dc_ref, da_ref, dgate_ref, dw_ref, db_ref, dng_ref, dnb_ref,
             zp, dcp, buf):
        b = pl.program_id(1)
        a, sig = a_ref[...], jax.nn.sigmoid(gate_ref[...])
        zp[pl.ds(0, CONV_PAD), :] = jnp.zeros((CONV_PAD, LANES), F32)
        zp[pl.ds(CONV_PAD, seq), :] = a * sig
        for r0 in range(0, seq, rc):
            buf[pl.ds(r0, rc), :] = _conv_taps(w_ref, zp, r0, rc, jnp.broadcast_to(cb_ref[...], (rc, LANES)),
                                               lambda k: CONV_PAD - (CONV_WIDTH - 1) + k)
        _, vjp = jax.vjp(_norm_act, buf[...], ng_ref[...], nb_ref[...])
        dconv, dng, dnb = vjp(dc_ref[...].astype(F32))
        dcp[pl.ds(0, seq), :] = dconv
        dcp[pl.ds(seq, CONV_PAD), :] = jnp.zeros((CONV_PAD, LANES), F32)

        @pl.when(b == 0)
        def _():
            dw_ref[...] = jnp.zeros(dw_ref.shape, F32)
            db_ref[...] = jnp.zeros(db_ref.shape, F32)
            dng_ref[...] = jnp.zeros(dng_ref.shape, F32)
            dnb_ref[...] = jnp.zeros(dnb_ref.shape, F32)

        db_ref[...] += jnp.sum(dconv, axis=0, keepdims=True)
        dng_ref[...] += dng
        dnb_ref[...] += dnb
        for k in range(CONV_WIDTH):
            shifted = zp[pl.ds(CONV_PAD - (CONV_WIDTH - 1) + k, seq), :]
            dw_ref[k:k + 1, :] += jnp.sum(dconv * shifted, axis=0, keepdims=True)
        for r0 in range(0, seq, rc):
            buf[pl.ds(r0, rc), :] = _conv_taps(w_ref, dcp, r0, rc, jnp.zeros((rc, LANES), F32),
                                               lambda k: CONV_WIDTH - 1 - k)
        dz = buf[...]
        da_ref[...] = (dz * sig).astype(da_ref.dtype)
        dgate_ref[...] = (dz * a * sig * (1.0 - sig)).astype(dgate_ref.dtype)

    vec = _spec((1, LANES), lambda g, b: (0, g))
    row_out = _spec((seq, LANES), lambda g, b: (b, g))
    return _call("conv_bwd", body, (GROUPS, batch),
                 [_spec((seq, LANES), lambda g, b: (b, a_blk + g)), _spec((seq, LANES), lambda g, b: (b, gate_blk + g)),
                  _spec((CONV_WIDTH, LANES), lambda g, b: (0, g)), vec, vec, vec,
                  _spec((seq, LANES), lambda g, b: (b, out_blk + g))],
                 [row_out, row_out, _spec((CONV_WIDTH, LANES), lambda g, b: (0, g)), vec, vec, vec],
                 [_sds((t, CONV_CH), MXU_DTYPE), _sds((t, CONV_CH), MXU_DTYPE), _sds((CONV_WIDTH, CONV_CH), F32),
                  _sds((1, CONV_CH), F32), _sds((1, CONV_CH), F32), _sds((1, CONV_CH), F32)],
                 scratch=[pltpu.VMEM((seq + CONV_PAD, LANES), F32), pltpu.VMEM((seq + CONV_PAD, LANES), F32),
                          pltpu.VMEM((seq, LANES), F32)])(p, p, conv_w, conv_b, norm_g, norm_b, d_cat)


def _sgu_pre(hw, g, b):
    z = jax.nn.gelu(hw)
    return z[:, :GM_WIDTH], _layernorm(z[:, GM_WIDTH:], g, b)


def _causal(w):
    keep = lax.broadcasted_iota(jnp.int32, (CHUNK, CHUNK), 0) >= lax.broadcasted_iota(jnp.int32, (CHUNK, CHUNK), 1)
    return jnp.where(keep, w, 0.0)


def _sgu_fwd(hw, vg, vb, w_s, b_st):
    t = hw.shape[0]
    gw = GM_WIDTH // GROUPS

    def body(hw_ref, vg_ref, vb_ref, w_ref, b_ref, o_ref):
        u, v = _sgu_pre(hw_ref[...], vg_ref[...], vb_ref[...])
        for g in range(GROUPS):
            cols = slice(g * gw, (g + 1) * gw)
            s = _dot(_causal(w_ref[g]), v[:, cols]) + b_ref[:, g:g + 1]
            o_ref[:, cols] = (u[:, cols] * s).astype(o_ref.dtype)

    return _call("sgu_fwd", body, (t // CHUNK,),
                 [_spec((CHUNK, 2 * GM_WIDTH), lambda i: (i, 0)), _spec((1, GM_WIDTH), lambda i: (0, 0)),
                  _spec((1, GM_WIDTH), lambda i: (0, 0)), _spec((GROUPS, CHUNK, CHUNK), lambda i: (0, 0, 0)),
                  _spec((CHUNK, GROUPS), lambda i: (0, 0))],
                 _spec((CHUNK, GM_WIDTH), lambda i: (i, 0)), _sds((t, GM_WIDTH), MXU_DTYPE))(hw, vg, vb, w_s, b_st)


def _sgu_bwd(hw, vg, vb, w_s, b_st, dout):
    t = hw.shape[0]
    gw = GM_WIDTH // GROUPS

    def body(hw_ref, vg_ref, vb_ref, w_ref, b_ref, do_ref, dhw_ref, dvg_ref, dvb_ref, dw_ref, db_ref, du_buf, dv_buf):
        (u, v), vjp = jax.vjp(_sgu_pre, hw_ref[...], vg_ref[...], vb_ref[...])

        @pl.when(pl.program_id(0) == 0)
        def _():
            dvg_ref[...] = jnp.zeros(dvg_ref.shape, F32)
            dvb_ref[...] = jnp.zeros(dvb_ref.shape, F32)
            dw_ref[...] = jnp.zeros(dw_ref.shape, F32)
            db_ref[...] = jnp.zeros(db_ref.shape, F32)

        lane = lax.broadcasted_iota(jnp.int32, (CHUNK, LANES), 1)
        db = jnp.zeros((CHUNK, LANES), F32)
        for g in range(GROUPS):
            cols = slice(g * gw, (g + 1) * gw)
            w = _causal(w_ref[g])
            s = _dot(w, v[:, cols]) + b_ref[:, g:g + 1]
            do = do_ref[:, cols]
            ds = do * u[:, cols]
            du_buf[:, cols] = do * s
            dw_ref[g] += _causal(_dot(ds, v[:, cols], "nt"))
            dv_buf[:, cols] = _dot(w, ds, "tn")
            db = db + jnp.where(lane == g, jnp.sum(ds, axis=1, keepdims=True), 0.0)
        db_ref[...] += db
        dhw, dvg, dvb = vjp((du_buf[...], dv_buf[...]))
        dhw_ref[...] = dhw.astype(dhw_ref.dtype)
        dvg_ref[...] += dvg
        dvb_ref[...] += dvb

    vec = _spec((1, GM_WIDTH), lambda i: (0, 0))
    w_spec = _spec((GROUPS, CHUNK, CHUNK), lambda i: (0, 0, 0))
    return _call("sgu_bwd", body, (t // CHUNK,),
                 [_spec((CHUNK, 2 * GM_WIDTH), lambda i: (i, 0)), vec, vec, w_spec, _spec((CHUNK, GROUPS), lambda i: (0, 0)),
                  _spec((CHUNK, GM_WIDTH), lambda i: (i, 0))],
                 [_spec((CHUNK, 2 * GM_WIDTH), lambda i: (i, 0)), vec, vec, w_spec, _spec((CHUNK, LANES), lambda i: (0, 0))],
                 [_sds((t, 2 * GM_WIDTH), MXU_DTYPE), _sds((1, GM_WIDTH), F32), _sds((1, GM_WIDTH), F32),
                  _sds((GROUPS, CHUNK, CHUNK), F32), _sds((CHUNK, LANES), F32)],
                 scratch=[pltpu.VMEM((CHUNK, GM_WIDTH), F32), pltpu.VMEM((CHUNK, GM_WIDTH), F32)]
                 )(hw, vg, vb, w_s, b_st, dout)


def _mm_nn(name, a, b, out_dtype, tm, a_col0=0, tn_want=1024, tk_want=1024, courier=None):
    t = a.shape[0]
    kk, n = b.shape
    tk = _tile(kk, tk_want, LANES)
    tn = _tile(n, tn_want, LANES)
    k0 = a_col0 // tk
    assert a_col0 % tk == 0
    outs, carried = _matmul(
        name, "nn", (t // tm, n // tn, kk // tk),
        [[(a, _spec((tm, tk), lambda i, j, k: (i, k0 + k)), b, _spec((tk, tn), lambda i, j, k: (k, j)))]],
        [((t, n), out_dtype, _spec((tm, tn), lambda i, j, k: (i, j)))], (tm, tn), courier=courier)
    return (outs[0], carried) if courier else outs[0]


def _mm_nt(name, pairs, out_dtype, tm, tn_want=1024, tk_want=1024, courier=None):
    t, kk = pairs[0][0].shape
    n = pairs[0][1].shape[0]
    tk = _tile(kk, tk_want, LANES)
    tn = _tile(n, tn_want, 16)
    grp = []
    for a, b, b_col0 in pairs:
        assert b_col0 % tk == 0 and a.shape == (t, kk)
        grp.append((a, _spec((tm, tk), lambda i, j, k: (i, k)), b,
                    _spec((tn, tk), functools.partial(lambda i, j, k, k0: (j, k0 + k), k0=b_col0 // tk))))
    outs, carried = _matmul(name, "nt", (t // tm, n // tn, kk // tk), [grp],
                            [((t, n), out_dtype, _spec((tm, tn), lambda i, j, k: (i, j)))], (tm, tn), courier=courier)
    return (outs[0], carried) if courier else outs[0]


def _mm_tn(name, a, bs, out_dtype, a_col0=0, m_width=None, tm_want=512, tn_want=1024, courier=None):
    t = a.shape[0]
    m = m_width or a.shape[1]
    n = bs[0].shape[1]
    tmw = _tile(m, tm_want, LANES)
    tn = _tile(n, tn_want, LANES)
    tkt = _tile(t, TN_ROWS, 16)
    assert a_col0 % tmw == 0
    i0 = a_col0 // tmw
    a_spec = _spec((tkt, tmw), lambda i, j, k: (k, i0 + i))
    groups = [[(a, a_spec, b, _spec((tkt, tn), lambda i, j, k: (k, j)))] for b in bs]
    outs, carried = _matmul(name, "tn", (m // tmw, n // tn, t // tkt), groups,
                            [((m, n), out_dtype, _spec((tmw, tn), lambda i, j, k: (i, j)))] * len(bs), (tmw, tn),
                            courier=courier)
    return (outs, carried) if courier else outs


_ANY = pl.BlockSpec(memory_space=pl.ANY)


def _comm_call(name, body, ins, out_shapes, sems, aliases=None):
    return pl.pallas_call(body, name=name, out_shape=out_shapes, in_specs=[_ANY] * len(ins),
                          out_specs=[_ANY] * len(out_shapes), input_output_aliases=aliases or {},
                          scratch_shapes=[pltpu.SemaphoreType.DMA((n,)) for n in sems])(*ins)


def _place():
    return lax.axis_index("x"), lax.axis_index("y"), lax.axis_index("c")


def _other_chips(x, y):
    return [(1 - x, y), (x, 1 - y), (1 - x, 1 - y)]


def _cast_into_slot(w, place, dtype):
    layers, rows, cols = w.shape
    half = rows // 2
    tr = _tile(half, 256, 16)
    nt = half // tr

    def body(s, w_ref, o_ref):
        del s
        o_ref[...] = w_ref[...].astype(o_ref.dtype)

    return _call("cast_into_slot", body, (layers, 2, nt),
                 [_spec((None, tr, cols), lambda l, h, i, s: (l, h * nt + i, 0))],
                 _spec((None, None, None, tr, cols), lambda l, h, i, s: (s[0], l, h, i, 0)),
                 _sds((N_CHIPS, layers, 2, half, cols), dtype), prefetch=1)(place, w)


class _GatherChips:
    aliased = True

    def __init__(self, bufs):
        self.arrays = list(bufs)
        self.out_shapes = [_sds(b.shape, b.dtype) for b in bufs]
        self.sems = [3 * len(bufs)] * 4

    def _sent(self, buf, sems):
        x, y, c = _place()
        me = 2 * x + y
        return [[pltpu.make_async_remote_copy(
            src_ref=buf[t].at[me, :, c], dst_ref=buf[t].at[me, :, c], send_sem=sems[0].at[3 * t + r],
            recv_sem=sems[1].at[3 * t + r], device_id=(px, py, c), device_id_type=MESH)
            for r, (px, py) in enumerate(_other_chips(x, y))] for t in range(len(buf))]

    def start(self, _, buf, sems):
        for row in self._sent(buf, sems):
            for cp in row:
                cp.start()

    def finish(self, _, buf, sems):
        x, y, c = _place()
        sent = self._sent(buf, sems)
        passed = []
        for t in range(len(buf)):
            for r, (px, py) in enumerate(_other_chips(x, y)):
                landed = buf[t].at[2 * px + py, :, c]
                sent[t][r].wait_recv()
                cp = pltpu.make_async_remote_copy(
                    src_ref=landed, dst_ref=landed, send_sem=sems[2].at[3 * t + r], recv_sem=sems[3].at[3 * t + r],
                    device_id=(x, y, 1 - c), device_id_type=MESH)
                cp.start()
                passed.append(cp)
        for cp in passed:
            cp.wait()
        for row in sent:
            for cp in row:
                cp.wait_send()


class _ScatterChips:
    aliased = False

    def __init__(self, partials):
        self.arrays = list(partials)
        self.out_shapes = [_sds((3,) + p.shape[1:], p.dtype) for p in partials]
        self.sems = [3 * len(partials)] * 2

    def _copies(self, src, dst, sems):
        x, y, c = _place()
        return [pltpu.make_async_remote_copy(
            src_ref=src[t].at[2 * px + py], dst_ref=dst[t].at[r], send_sem=sems[0].at[3 * t + r],
            recv_sem=sems[1].at[3 * t + r], device_id=(px, py, c), device_id_type=MESH)
            for t in range(len(src)) for r, (px, py) in enumerate(_other_chips(x, y))]

    def start(self, src, dst, sems):
        for cp in self._copies(src, dst, sems):
            cp.start()

    def finish(self, src, dst, sems):
        for cp in self._copies(src, dst, sems):
            cp.wait()


def _run_courier(name, courier):
    n_in, n_out = len(courier.arrays), len(courier.out_shapes)

    def body(*refs):
        src, dst, sems = refs[:n_in], refs[n_in:n_in + n_out], refs[n_in + n_out:]
        courier.start(src, dst, sems)
        courier.finish(src, dst, sems)

    return _comm_call(name, body, courier.arrays, courier.out_shapes, courier.sems,
                      aliases={t: t for t in range(n_in)} if courier.aliased else None)


def _swap_halves(grads):
    n = len(grads)

    def body(*refs):
        src, dst = refs[:n], refs[n:2 * n]
        send_sems, recv_sems = refs[2 * n:]
        x, y, c = _place()
        copies = [pltpu.make_async_remote_copy(
            src_ref=src[t].at[:, 1 - c], dst_ref=dst[t], send_sem=send_sems.at[t], recv_sem=recv_sems.at[t],
            device_id=(x, y, 1 - c), device_id_type=MESH) for t in range(n)]
        for cp in copies:
            cp.start()
        for cp in copies:
            cp.wait()

    return _comm_call("swap_halves", body, grads, [_sds((g.shape[0],) + g.shape[2:], g.dtype) for g in grads], [n, n])


def _join_halves(grads):
    n = len(grads)

    def body(*refs):
        buf = refs[n:2 * n]
        send_sems, recv_sems = refs[2 * n:]
        x, y, c = _place()
        copies = [pltpu.make_async_remote_copy(
            src_ref=buf[t].at[:, c], dst_ref=buf[t].at[:, c], send_sem=send_sems.at[t], recv_sem=recv_sems.at[t],
            device_id=(x, y, 1 - c), device_id_type=MESH) for t in range(n)]
        for cp in copies:
            cp.start()
        for cp in copies:
            cp.wait()

    return _comm_call("join_halves", body, grads, [_sds(g.shape, g.dtype) for g in grads], [n, n],
                      aliases={t: t for t in range(n)})


def _gather_devices(slots):
    def body(src, buf, send_sems, recv_sems):
        del src
        x, y, c = _place()
        me = 4 * x + 2 * y + c
        copies = []
        for r in range(1, N_DEV):
            fx, fy, fc = (r >> 2) & 1, (r >> 1) & 1, r & 1
            peer = (1 - x if fx else x, 1 - y if fy else y, 1 - c if fc else c)
            copies.append(pltpu.make_async_remote_copy(
                src_ref=buf.at[me], dst_ref=buf.at[me], send_sem=send_sems.at[r - 1], recv_sem=recv_sems.at[r - 1],
                device_id=peer, device_id_type=MESH))
        for cp in copies:
            cp.start()
        for cp in copies:
            cp.wait()

    return _comm_call("gather_devices", body, [slots], [_sds(slots.shape, slots.dtype)], [N_DEV - 1, N_DEV - 1],
                      aliases={0: 0})[0]


def _add_halves(grad, got, place):
    _, _, rows, cols = grad.shape
    tr = _tile(rows, 256, 16)

    def body(s, a, b, o):
        del s
        o[...] = (a[...].astype(F32) + b[...].astype(F32)).astype(o.dtype)

    return _call("add_halves", body, (N_CHIPS, rows // tr),
                 [_spec((None, None, tr, cols), lambda j, i, s: (j, s[1], i, 0)), _spec((None, tr, cols), lambda j, i, s: (j, i, 0))],
                 _spec((None, tr, cols), lambda j, i, s: (j, i, 0)), _sds((N_CHIPS, rows, cols), grad.dtype), prefetch=1
                 )(place, grad, got)


def _add_chips(partial, got, place, layer, layers, into=None):
    _, rows, cols = partial.shape
    tr = _tile(rows, 256, 16)

    def body(s, a, b, *rest):
        o = rest[-1]
        o[...] = a[...].astype(F32) + b[0].astype(F32) + b[1].astype(F32) + b[2].astype(F32)

    in_specs = [_spec((None, tr, cols), lambda i, s: (s[0], i, 0)), _spec((3, tr, cols), lambda i, s: (0, i, 0))]
    args = [place, partial, got]
    if into is not None:
        in_specs.append(pl.BlockSpec(memory_space=pl.ANY))
        args.append(into)
    return _call("add_chips", body, (rows // tr,), in_specs,
                 _spec((None, None, tr, cols), lambda i, s: (layer, s[1], i, 0)), _sds((layers, 2, rows, cols), F32),
                 prefetch=1, aliases={3: 0} if into is not None else None)(*args)


def _add_devices(got):
    _, rows, cols = got.shape
    tr = _tile(rows, 512, 8)

    def body(a, o):
        total = a[0]
        for d in range(1, N_DEV):
            total = total + a[d]
        o[...] = total

    return _call("add_devices", body, (rows // tr,), [_spec((N_DEV, tr, cols), lambda i: (0, i, 0))],
                 _spec((tr, cols), lambda i: (i, 0)), _sds((rows, cols), F32))(got)


def _adamw(w, g, m, v):
    layers, rows, cols = w.shape
    tr = _tile(rows, 256, 8)

    def body(w_ref, g_ref, m_ref, v_ref, d_ref, nm_ref, nv_ref):
        grad = g_ref[...]
        new_m = ADAM_B1 * m_ref[...] + (1.0 - ADAM_B1) * grad
        new_v = ADAM_B2 * v_ref[...] + (1.0 - ADAM_B2) * jnp.square(grad)
        m_hat = new_m / (1.0 - ADAM_B1 ** ADAM_STEP)
        v_hat = new_v / (1.0 - ADAM_B2 ** ADAM_STEP)
        d_ref[...] = -ADAM_LR * (m_hat / (jnp.sqrt(v_hat) + ADAM_EPS) + ADAM_WD * w_ref[...])
        nm_ref[...] = new_m
        nv_ref[...] = new_v

    blk = _spec((None, tr, cols), lambda l, i: (l, i, 0))
    return _call("adamw", body, (layers, rows // tr), [blk] * 4, [blk] * 3, [_sds(w.shape, F32)] * 3)(w, g, m, v)


def _rope_layout(w):
    z = jnp.zeros(w.shape[:-1] + (ROPE_HALF,), w.dtype)
    return jnp.concatenate([w[..., :ROPE_HALF], z, w[..., ROPE_HALF:], z], axis=-1)


def _rope_layout_inv(w):
    return jnp.concatenate([w[..., :ROPE_HALF], w[..., 2 * ROPE_HALF:3 * ROPE_HALF]], axis=-1)


def _cols_from_chips(g):
    return jnp.moveaxis(g, 0, 1).reshape(g.shape[1], N_CHIPS * g.shape[2])


def _cols_to_chips(w):
    return jnp.moveaxis(w.reshape(w.shape[0], N_CHIPS, w.shape[1] // N_CHIPS), 1, 0)


def _w_in_layout(w):
    off_kr = Q_LORA + KV_LORA
    return jnp.concatenate([w[:, :off_kr], w[:, off_kr + QK_ROPE:], _rope_layout(w[:, off_kr:off_kr + QK_ROPE])], axis=1)


def _w_in_layout_inv(dw):
    return jnp.concatenate([dw[:, :P_A], _rope_layout_inv(dw[:, P_KR:]), dw[:, P_A:P_KR]], axis=1)


def _w_uq_layout(w):
    w = w.reshape(Q_LORA, HEADS, QK_NOPE + QK_ROPE)
    return jnp.concatenate([w[..., :QK_NOPE].reshape(Q_LORA, HEADS * QK_NOPE),
                            _rope_layout(w[..., QK_NOPE:]).reshape(Q_LORA, HEADS * LANES)], axis=1)


def _w_uq_layout_inv(d_nope, d_rope):
    d_nope = d_nope.reshape(Q_LORA, HEADS, QK_NOPE)
    d_rope = _rope_layout_inv(d_rope.reshape(Q_LORA, HEADS, LANES))
    return jnp.concatenate([d_nope, d_rope], axis=-1).reshape(Q_LORA, HEADS * (QK_NOPE + QK_ROPE))


def _to_lanes(a):
    flat = a.reshape(-1)
    pad = (-flat.shape[0]) % LANES
    if pad:
        flat = jnp.concatenate([flat, jnp.zeros((pad,), flat.dtype)])
    return flat.reshape(-1, LANES)


PACK_ROWS = 64


def _pack(arrays):
    parts, ranges, row = [], [], 0
    for a in arrays:
        p = _to_lanes(a)
        parts.append(p)
        ranges.append((row, row + p.shape[0]))
        row += p.shape[0]
    pad = (-row) % PACK_ROWS
    if pad:
        parts.append(jnp.zeros((pad, LANES), F32))
    return jnp.concatenate(parts, axis=0), ranges


def _unpack(packed, rng, shape):
    n = 1
    for s in shape:
        n *= s
    return packed[rng[0]:rng[1]].reshape(-1)[:n].reshape(shape)


def kernel(x, positions, ffn_a_pre_g, ffn_a_post_g, ffn_a_w_gate, ffn_a_w_up, ffn_a_w_down, ffn_b_pre_g, ffn_b_post_g, ffn_b_w_gate, ffn_b_w_up, ffn_b_w_down, even_pre_g, even_post_g, even_w_in, even_q_norm_g, even_kv_norm_g, even_w_uq, even_w_ukv, even_conv_w, even_conv_b, even_conv_norm_g, even_conv_norm_b, even_w_out, odd_pre_g, odd_post_g, odd_w_in, odd_v_norm_g, odd_v_norm_b, odd_w_s, odd_b_s, odd_w_out, loss_target, m_ffn_a_pre_g, m_ffn_a_post_g, m_ffn_a_w_gate, m_ffn_a_w_up, m_ffn_a_w_down, m_ffn_b_pre_g, m_ffn_b_post_g, m_ffn_b_w_gate, m_ffn_b_w_up, m_ffn_b_w_down, m_even_pre_g, m_even_post_g, m_even_w_in, m_even_q_norm_g, m_even_kv_norm_g, m_even_w_uq, m_even_w_ukv, m_even_conv_w, m_even_conv_b, m_even_conv_norm_g, m_even_conv_norm_b, m_even_w_out, m_odd_pre_g, m_odd_post_g, m_odd_w_in, m_odd_v_norm_g, m_odd_v_norm_b, m_odd_w_s, m_odd_b_s, m_odd_w_out, v_ffn_a_pre_g, v_ffn_a_post_g, v_ffn_a_w_gate, v_ffn_a_w_up, v_ffn_a_w_down, v_ffn_b_pre_g, v_ffn_b_post_g, v_ffn_b_w_gate, v_ffn_b_w_up, v_ffn_b_w_down, v_even_pre_g, v_even_post_g, v_even_w_in, v_even_q_norm_g, v_even_kv_norm_g, v_even_w_uq, v_even_w_ukv, v_even_conv_w, v_even_conv_b, v_even_conv_norm_g, v_even_conv_norm_b, v_even_w_out, v_odd_pre_g, v_odd_post_g, v_odd_w_in, v_odd_v_norm_g, v_odd_v_norm_b, v_odd_w_s, v_odd_b_s, v_odd_w_out):
    given = dict(locals())
    w = {n: given[n] for n in WEIGHTS}
    batch, seq, d = x.shape
    t = batch * seq
    tm = _tile(t, 512, 16)
    tr = _tile(t, 256, 16)
    chip = 2 * lax.axis_index("x") + lax.axis_index("y")
    place = jnp.stack([chip, lax.axis_index("c")]).astype(jnp.int32)

    small_shard, small_shard_rng = _pack([w[n] for n in SMALL_SHARDED])
    wbuf = {(n, layer): _cast_into_slot(w[n][layer:layer + 1], place, MXU_DTYPE)
            for n in BIG for layer in range(w[n].shape[0])}
    wbuf['small'] = _cast_into_slot(small_shard[None], place, F32)

    def ffn_keys(tag, layer):
        return [(f'ffn_{tag}_w_{part}', layer) for part in ('gate', 'up', 'down')]

    def gather(keys):
        return _GatherChips([wbuf[k] for k in keys]), keys

    def landed(order, results):
        for k, r in zip(order[1], results):
            wbuf[k] = r

    def weight(key):
        n, _ = key
        return wbuf[key].reshape((N_CHIPS, 1) + w[n].shape[1:])

    first = gather(ffn_keys('a', 0)[:2] + ['small'])
    landed(first, _run_courier("gather_chips", first[0]))
    shards = wbuf['small'].reshape((N_CHIPS,) + small_shard.shape)
    full = dict(w)
    for n, rng in zip(SMALL_SHARDED, small_shard_rng):
        per_chip = [_unpack(shards[j], rng, w[n].shape) for j in range(N_CHIPS)]
        full[n] = jnp.concatenate(per_chip, axis=-1)

    inv_freq = ROPE_THETA ** (-jnp.arange(0, QK_ROPE, 2, dtype=F32) / QK_ROPE)
    ang = positions.astype(F32).reshape(t, 1) * inv_freq
    zeros = jnp.zeros((t, ROPE_HALF), F32)
    cos = jnp.concatenate([jnp.cos(ang), zeros, jnp.cos(ang), zeros], axis=1)
    sin = jnp.concatenate([-jnp.sin(ang), zeros, jnp.sin(ang), zeros], axis=1)

    def ffn_args(tag, layer):
        gate, up, down = (weight(k) for k in ffn_keys(tag, layer))
        return (w[f'ffn_{tag}_pre_g'][layer:layer + 1], w[f'ffn_{tag}_post_g'][layer:layer + 1], gate, up, down, 0, tm, tr)

    def ffn_forward(xin, tag, layer, up_keys, down_keys, h=None, next_g=None):
        orders = {'up': gather(ffn_keys(tag, layer)[2:] + up_keys)}
        if down_keys:
            orders['down'] = gather(down_keys)
        out, saved, carried, h_next = _ffn_fwd(xin, *ffn_args(tag, layer), couriers={k: o[0] for k, o in orders.items()},
                                               wd_landed=0, h=h, next_g=next_g)
        for k, o in orders.items():
            landed(o, carried[k])
        return out, saved, h_next

    even_keys = [('even_w_in', 0), ('even_w_uq', 0), ('even_w_ukv', 0), ('even_w_out', 0)]
    odd_keys = [('odd_w_in', 0), ('odd_w_out', 0)]

    xs = x.reshape(t, d)
    x1, ffn_a0, h_e = ffn_forward(xs, 'a', 0, even_keys, ffn_keys('b', 0)[:1], next_g=w['even_pre_g'])
    w_in = _w_in_layout(_cols_from_chips(weight(('even_w_in', 0))[:, 0]))
    w_uq = _w_uq_layout(_cols_from_chips(weight(('even_w_uq', 0))[:, 0]))
    w_ukv = _cols_from_chips(weight(('even_w_ukv', 0))[:, 0])
    w_out_e = weight(('even_w_out', 0))[:, 0].reshape(d, d)
    order = gather(ffn_keys('b', 0)[1:2])
    p, res = _mm_nn("mm_w_in", h_e, w_in, F32, tm, tn_want=640, courier=order[0])
    landed(order, res)
    lat = _rows("latent_norm", lambda r, c: ([jnp.concatenate([_rmsnorm(r[0], c[0]), _rmsnorm(r[1], c[1])], axis=1)], []),
                [(p, Q_LORA, 0), (p, KV_LORA, 1)], [w['even_q_norm_g'], w['even_kv_norm_g']], [(Q_LORA + KV_LORA, MXU_DTYPE)], [], tr)[0]
    q_all = _mm_nn("mm_w_uq", lat, w_uq, F32, tm, a_col0=0, tk_want=Q_LORA)
    kv = _mm_nn("mm_w_ukv", lat, w_ukv, MXU_DTYPE, tm, a_col0=Q_LORA, tk_want=KV_LORA)
    cat = _attn_fwd(q_all, kv, p, cos, sin, batch, seq)
    cat = _conv_fwd(p, cat, full['even_conv_w'][0], w['even_conv_b'], w['even_conv_norm_g'], w['even_conv_norm_b'], batch, seq)
    y_e = _mm_nn("mm_w_out", cat, w_out_e, F32, tm)
    x2 = _rms_residual(x1, y_e, w['even_post_g'], 1.0, tr)
    x3, ffn_b0, h_a1 = ffn_forward(x2, 'b', 0, ffn_keys('a', 1)[:1], ffn_keys('a', 1)[1:2], next_g=w['ffn_a_pre_g'][1:2])
    x4, ffn_a1, h_o = ffn_forward(x3, 'a', 1, odd_keys, ffn_keys('b', 1)[:1], h=h_a1, next_g=full['odd_pre_g'])
    w_out_o = weight(('odd_w_out', 0))[:, 0].reshape(GM_WIDTH, d)
    w_in_o = weight(('odd_w_in', 0))
    ns = w_in_o.shape[3]
    tk_o = _tile(d, 1024, LANES)
    order = gather(ffn_keys('b', 1)[1:2])
    (hw,), res = _matmul("mm_w_in_odd", "nn", (t // tm, N_CHIPS, d // tk_o),
                         [[(h_o, _spec((tm, tk_o), lambda i, j, k: (i, k)), w_in_o, _spec((None, None, tk_o, ns), lambda i, j, k: (j, 0, k, 0)))]],
                         [((t, N_CHIPS * ns), F32, _spec((tm, ns), lambda i, j, k: (i, j)))], (tm, ns), courier=order[0])
    landed(order, res)
    b_st = w['odd_b_s'][0].T
    sg = _sgu_fwd(hw, full['odd_v_norm_g'], full['odd_v_norm_b'], w['odd_w_s'][0], b_st)
    y_o = _mm_nn("mm_w_out", sg, w_out_o, F32, tm)
    x5 = _rms_residual(x4, y_o, full['odd_post_g'], 1.0, tr)
    x6, ffn_b1, _ = ffn_forward(x5, 'b', 1, [], [])

    dy, sq_err = _loss_head(x6, loss_target.reshape(t, d), tr)
    loss = lax.psum(0.5 * sq_err[0, 0] / d, ("x", "y", "c"))

    grads = {}
    received = {}

    def core_sums(named):
        items = [g.reshape(N_CHIPS, 2, g.shape[1] // 2, g.shape[2]) for _, g in named]
        got = _swap_halves(items)
        return [(k, _add_halves(g, r, place)) for (k, _), g, r in zip(named, items, got)]

    def scatter(named):
        return _ScatterChips([pt for _, pt in named]), named

    def arrived(order, results):
        for (k, pt), r in zip(order[1], results):
            received[k] = (pt, r)

    def ffn_backward(dout, saved, tag, layer, orders):
        dxin, g, carried = _ffn_bwd(dout, saved, *ffn_args(tag, layer), couriers={k: o[0] for k, o in orders.items()})
        for k, o in orders.items():
            arrived(o, carried[k])
        return dxin, g, core_sums(list(zip(ffn_keys(tag, layer), g[2:])))

    fg = {}
    dx, fg[('b', 1)], pend = ffn_backward(dy, ffn_b1, 'b', 1, {})
    dy_o, grads['odd_post_g'] = _rms_bwd(y_o, full['odd_post_g'], dx, 1.0, MXU_DTYPE, tr)
    dsg = _mm_nt("mm_dsg", [(dy_o, w_out_o, 0)], F32, tm)
    (g_w_out_o,) = _mm_tn("mm_dw_out", sg, [dy_o], MXU_DTYPE)
    dhw, grads['odd_v_norm_g'], grads['odd_v_norm_b'], g_ws, g_bst = _sgu_bwd(
        hw, full['odd_v_norm_g'], full['odd_v_norm_b'], w['odd_w_s'][0], b_st, dsg)
    grads['odd_w_s'] = g_ws[None]
    grads['odd_b_s'] = g_bst[:, :GROUPS].T[None]
    tn_o = _tile(d, 1024, LANES)
    order = scatter(pend[0:1])
    (dh_o,), res = _matmul("mm_dh_odd", "nt", (t // tm, d // tn_o, N_CHIPS),
                           [[(dhw, _spec((tm, ns), lambda i, j, k: (i, k)), w_in_o, _spec((None, None, tn_o, ns), lambda i, j, k: (k, 0, j, 0)))]],
                           [((t, d), F32, _spec((tm, tn_o), lambda i, j, k: (i, j)))], (tm, tn_o), courier=order[0])
    arrived(order, res)
    tmw = _tile(d, 512, LANES)
    tkt = _tile(t, TN_ROWS, 16)
    order = scatter(pend[1:2])
    (g_w_in_o,), res = _matmul("mm_dw_in_odd", "tn", (d // tmw, N_CHIPS, t // tkt),
                               [[(h_o, _spec((tkt, tmw), lambda i, j, k: (k, i)), dhw, _spec((tkt, ns), lambda i, j, k: (k, j)))]],
                               [((N_CHIPS, d, ns), MXU_DTYPE, _spec((None, tmw, ns), lambda i, j, k: (j, i, 0)))], (tmw, ns),
                               courier=order[0])
    arrived(order, res)
    pend_odd = core_sums([(('odd_w_in', 0), g_w_in_o),
                          (('odd_w_out', 0), g_w_out_o.reshape((N_CHIPS,) + w['odd_w_out'].shape[1:]))])
    dx, grads['odd_pre_g'] = _rms_bwd(x4, full['odd_pre_g'], dh_o, 1.0, F32, tr, resid=dx)
    dx, fg[('a', 1)], pend = ffn_backward(dx, ffn_a1, 'a', 1, {'dact': scatter(pend[2:3]), 'dwdown': scatter(pend_odd)})
    dx, fg[('b', 0)], pend = ffn_backward(dx, ffn_b0, 'b', 0, {'dact': scatter(pend[0:1]), 'dh': scatter(pend[1:2]),
                                                              'dwup': scatter(pend[2:3])})
    dy_e, grads['even_post_g'] = _rms_bwd(y_e, w['even_post_g'], dx, 1.0, MXU_DTYPE, tr)
    d_cat = _mm_nt("mm_dcat", [(dy_e, w_out_e, 0)], F32, tm)
    (g_w_out_e,) = _mm_tn("mm_dw_out", cat, [dy_e], MXU_DTYPE)
    dqn, dqp, dkv, dkr = _attn_bwd(q_all, kv, p, cos, sin, d_cat, batch, seq)
    hq = HEADS * LANES
    d_latq = _mm_nt("mm_dlat_q", [(dqn, w_uq, 0), (dqp, w_uq, hq)], F32, tm, tn_want=Q_LORA)
    d_latkv = _mm_nt("mm_dlat_kv", [(dkv, w_ukv, 0)], F32, tm, tn_want=KV_LORA)
    g_uq_n, g_uq_r = _mm_tn("mm_dw_uq", lat, [dqn, dqp], MXU_DTYPE, a_col0=0, m_width=Q_LORA)
    (g_ukv,) = _mm_tn("mm_dw_ukv", lat, [dkv], MXU_DTYPE, a_col0=Q_LORA, m_width=KV_LORA)

    def latent_bwd(r, c):
        _, vjp_q = jax.vjp(_rmsnorm, r[0], c[0])
        _, vjp_kv = jax.vjp(_rmsnorm, r[1], c[1])
        dq, dqg = vjp_q(r[2])
        dk, dkg = vjp_kv(r[3])
        return [jnp.concatenate([dq, dk], axis=1)], [dqg, dkg]

    d_lat, grads['even_q_norm_g'], grads['even_kv_norm_g'] = _rows(
        "latent_norm_bwd", latent_bwd, [(p, Q_LORA, 0), (p, KV_LORA, 1), (d_latq, Q_LORA, 0), (d_latkv, KV_LORA, 0)],
        [w['even_q_norm_g'], w['even_kv_norm_g']], [(Q_LORA + KV_LORA, MXU_DTYPE)], [(1, Q_LORA), (1, KV_LORA)], tr)
    da, dgate, g_conv_w, grads['even_conv_b'], grads['even_conv_norm_g'], grads['even_conv_norm_b'] = _conv_bwd(
        p, d_cat, full['even_conv_w'][0], w['even_conv_b'], w['even_conv_norm_g'], w['even_conv_norm_b'], batch, seq)
    dp = jnp.concatenate([d_lat, da, dgate, dkr.astype(MXU_DTYPE)], axis=1)
    order = scatter(pend[0:1])
    dh_e, res = _mm_nt("mm_dh_even", [(dp, w_in, 0)], F32, tm, tk_want=640, courier=order[0])
    arrived(order, res)
    order = scatter(pend[1:2])
    (g_w_in,), res = _mm_tn("mm_dw_in", h_e, [dp], MXU_DTYPE, tn_want=640, courier=order[0])
    arrived(order, res)
    pend_even = core_sums([(('even_w_in', 0), _cols_to_chips(_w_in_layout_inv(g_w_in))),
                           (('even_w_uq', 0), _cols_to_chips(_w_uq_layout_inv(g_uq_n, g_uq_r))),
                           (('even_w_ukv', 0), _cols_to_chips(g_ukv)),
                           (('even_w_out', 0), g_w_out_e.reshape((N_CHIPS,) + w['even_w_out'].shape[1:]))])
    dx, grads['even_pre_g'] = _rms_bwd(x1, w['even_pre_g'], dh_e, 1.0, F32, tr, resid=dx)
    early = {'dact': scatter(pend[2:3]), 'dwdown': scatter(pend_even)}
    late = {}

    def after_w_down(dwd):
        late['dwup'] = scatter(core_sums([(ffn_keys('a', 0)[2], dwd)]))
        return late['dwup'][0]

    def after_w_up(dwg, dwu):
        late['dh'] = scatter(core_sums(list(zip(ffn_keys('a', 0)[:2], (dwg, dwu)))))
        return late['dh'][0]

    dx, fg[('a', 0)], carried = _ffn_bwd(
        dx, ffn_a0, *ffn_args('a', 0), weights_first=True,
        couriers={'dact': early['dact'][0], 'dwdown': early['dwdown'][0], 'dwup': after_w_down, 'dh': after_w_up})
    for k, o in {**early, **late}.items():
        arrived(o, carried[k])
    grad_x = dx.reshape(batch, seq, d)

    reduced = []
    for n in BIG:
        layers = w[n].shape[0]
        buf = None
        for layer in range(layers):
            pt, r = received[(n, layer)]
            buf = _add_chips(pt, r, place, layer, layers, into=buf)
        reduced.append(buf)
    joined = _join_halves(reduced)
    for n, g in zip(BIG, joined):
        grads[n] = g.reshape(w[n].shape)

    for tag in ('a', 'b'):
        grads[f'ffn_{tag}_pre_g'] = jnp.concatenate([fg[(tag, 0)][0], fg[(tag, 1)][0]], axis=0)
        grads[f'ffn_{tag}_post_g'] = jnp.concatenate([fg[(tag, 0)][1], fg[(tag, 1)][1]], axis=0)
    grads['even_conv_w'] = g_conv_w[None]
    packed, rngs = _pack([grads[n] for n in SMALL])
    device = 2 * chip + lax.axis_index("c")
    slots = lax.dynamic_update_slice_in_dim(jnp.zeros((N_DEV,) + packed.shape, F32), packed[None], device, axis=0)
    summed = _add_devices(_gather_devices(slots))
    for n, rng in zip(SMALL, rngs):
        g = _unpack(summed, rng, grads[n].shape)
        if n in SMALL_SHARDED:
            width = w[n].shape[-1]
            g = lax.dynamic_slice_in_dim(g, chip * width, width, axis=g.ndim - 1)
        grads[n] = g

    delta, new_m, new_v = {}, {}, {}
    for n in BIG:
        delta[n], new_m[n], new_v[n] = _adamw(w[n], grads[n], given['m_' + n], given['v_' + n])
    packs = [_pack([src[n] for n in SMALL])[0][None] for src in
             (w, grads, {n: given['m_' + n] for n in SMALL}, {n: given['v_' + n] for n in SMALL})]
    _, rngs = _pack([w[n] for n in SMALL])
    small_out = _adamw(*packs)
    for n, rng in zip(SMALL, rngs):
        delta[n], new_m[n], new_v[n] = (_unpack(o[0], rng, w[n].shape) for o in small_out)

    return (loss, grad_x, *[grads[n] for n in WEIGHTS], *[delta[n] for n in WEIGHTS],
            *[new_m[n] for n in WEIGHTS], *[new_v[n] for n in WEIGHTS])
```

```python
import functools

import jax
import jax.numpy as jnp
from jax import lax
from jax.experimental import pallas as pl
from jax.experimental.pallas import tpu as pltpu

F32 = jnp.float32
BF16 = jnp.bfloat16
MXU_DTYPE = BF16
MESH = pl.DeviceIdType.MESH
VMEM_LIMIT_BYTES = 56 * 1024 * 1024
LANES = 128
TN_ROWS = 2048

D_MODEL = 2048
D_FF = 5632
EPS = 1e-6
HEADS = 8
V_HEAD = 128
QK_NOPE = 128
QK_ROPE = 64
Q_LORA = 512
KV_LORA = 512
ROPE_THETA = 10000.0
CONV_CH = 1024
CONV_WIDTH = 31
GROUPS = 8
CHUNK = 128
GM_WIDTH = 2048
ADAM_LR = 0.001
ADAM_B1 = 0.9
ADAM_B2 = 0.999
ADAM_EPS = 1e-08
ADAM_WD = 0.01
ADAM_STEP = 10

N_CHIPS = 4
N_DEV = 8
P_KV = Q_LORA
P_A = Q_LORA + KV_LORA
P_GATE = P_A + CONV_CH
P_KR = P_GATE + CONV_CH
P_WIDTH = P_KR + LANES
ROPE_HALF = QK_ROPE // 2

WEIGHTS = ['ffn_a_pre_g', 'ffn_a_post_g', 'ffn_a_w_gate', 'ffn_a_w_up', 'ffn_a_w_down', 'ffn_b_pre_g', 'ffn_b_post_g',
           'ffn_b_w_gate', 'ffn_b_w_up', 'ffn_b_w_down', 'even_pre_g', 'even_post_g', 'even_w_in', 'even_q_norm_g',
           'even_kv_norm_g', 'even_w_uq', 'even_w_ukv', 'even_conv_w', 'even_conv_b', 'even_conv_norm_g',
           'even_conv_norm_b', 'even_w_out', 'odd_pre_g', 'odd_post_g', 'odd_w_in', 'odd_v_norm_g', 'odd_v_norm_b',
           'odd_w_s', 'odd_b_s', 'odd_w_out']
BIG = ['ffn_a_w_gate', 'ffn_a_w_up', 'ffn_a_w_down', 'ffn_b_w_gate', 'ffn_b_w_up', 'ffn_b_w_down', 'even_w_in',
       'even_w_uq', 'even_w_ukv', 'even_w_out', 'odd_w_in', 'odd_w_out']
SMALL = [n for n in WEIGHTS if n not in BIG]
SMALL_SHARDED = ['even_conv_w', 'odd_pre_g', 'odd_post_g', 'odd_v_norm_g', 'odd_v_norm_b']


def _call(name, body, grid, in_specs, out_specs, out_shape, scratch=(), prefetch=0, aliases=None):
    params = pltpu.CompilerParams(dimension_semantics=("arbitrary",) * len(grid), vmem_limit_bytes=VMEM_LIMIT_BYTES)
    if prefetch:
        spec = pltpu.PrefetchScalarGridSpec(num_scalar_prefetch=prefetch, grid=grid, in_specs=in_specs,
                                            out_specs=out_specs, scratch_shapes=list(scratch))
        return pl.pallas_call(body, grid_spec=spec, out_shape=out_shape, compiler_params=params, name=name,
                              input_output_aliases=aliases or {})
    return pl.pallas_call(body, grid=grid, in_specs=in_specs, out_specs=out_specs, out_shape=out_shape,
                          scratch_shapes=list(scratch), compiler_params=params, name=name,
                          input_output_aliases=aliases or {})


def _spec(block, index_map):
    return pl.BlockSpec(block, index_map)


def _sds(shape, dtype):
    return jax.ShapeDtypeStruct(tuple(shape), dtype)


def _tile(n, want, mult=8):
    if n <= want:
        return n
    best = None
    for t in range(mult, want + 1, mult):
        if n % t == 0:
            best = t
    assert best is not None, (n, want, mult)
    return best


_DIMS = {"nn": (((1,), (0,)), ((), ())), "nt": (((1,), (1,)), ((), ())), "tn": (((0,), (0,)), ((), ()))}


def _dot(a, b, mode="nn"):
    return lax.dot_general(a.astype(MXU_DTYPE), b.astype(MXU_DTYPE), _DIMS[mode], preferred_element_type=F32)


def _matmul(name, mode, grid, groups, outs, acc_shape, extras=(), epilogue=None, courier=None, sum_outs=0):
    flat, specs = [], []
    for grp in groups:
        for a, a_spec, b, b_spec in grp:
            flat += [a, b]
            specs += [a_spec, b_spec]
    for e, e_spec in extras:
        flat.append(e)
        specs.append(e_spec)
    n_pairs = [len(g) for g in groups]
    one_step = grid[2] == 1
    n_in, n_ex, n_out, n_acc = 2 * sum(n_pairs), len(extras), len(outs), 0 if one_step else len(groups)
    last = tuple(g - 1 for g in grid)
    c_arrays = list(courier.arrays) if courier else []
    c_shapes = list(courier.out_shapes) if courier else []
    n_ci, n_co = len(c_arrays), len(c_shapes)
    any_spec = pl.BlockSpec(memory_space=pl.ANY)

    def body(*refs):
        ins, ex = refs[:n_in], refs[n_in:n_in + n_ex]
        pos = n_in + n_ex
        c_in = refs[pos:pos + n_ci]
        out_refs = refs[pos + n_ci:pos + n_ci + n_out]
        c_out = refs[pos + n_ci + n_out:pos + n_ci + n_out + n_co]
        accs = refs[pos + n_ci + n_out + n_co:pos + n_ci + n_out + n_co + n_acc]
        sems = refs[pos + n_ci + n_out + n_co + n_acc:]
        i, j, k = pl.program_id(0), pl.program_id(1), pl.program_id(2)

        if courier:
            @pl.when((i == 0) & (j == 0) & (k == 0))
            def _():
                courier.start(c_in, c_out, sems)

        def products():
            pos, totals = 0, []
            for n in n_pairs:
                total = None
                for _ in range(n):
                    d = _dot(ins[pos][...], ins[pos + 1][...], mode)
                    total = d if total is None else total + d
                    pos += 2
                totals.append(total)
            return totals

        summed = out_refs[n_out - sum_outs:] if sum_outs else ()
        if summed:
            @pl.when((i == 0) & (j == 0) & (k == 0))
            def _():
                for o in summed:
                    o[...] = jnp.zeros(o.shape, F32)

        def finish(vals):
            res = epilogue(vals, [e[...] for e in ex]) if epilogue else vals
            for idx, (o, r) in enumerate(zip(out_refs, res)):
                if idx >= n_out - sum_outs:
                    o[...] += r
                else:
                    o[...] = r.astype(o.dtype)

        if one_step:
            finish(products())
        else:
            @pl.when(k == 0)
            def _():
                for acc in accs:
                    acc[...] = jnp.zeros(acc.shape, F32)

            for acc, total in zip(accs, products()):
                acc[...] += total

            @pl.when(k == last[2])
            def _():
                finish([acc[...] for acc in accs])

        if courier:
            @pl.when((i == last[0]) & (j == last[1]) & (k == last[2]))
            def _():
                courier.finish(c_in, c_out, sems)

    aliases = {n_in + n_ex + t: n_out + t for t in range(n_ci)} if courier and courier.aliased else None
    scratch = [pltpu.VMEM(acc_shape, F32)] * n_acc
    scratch += [pltpu.SemaphoreType.DMA((n,)) for n in courier.sems] if courier else []
    res = _call(name, body, grid, specs + [any_spec] * n_ci, [o[2] for o in outs] + [any_spec] * n_co,
                [_sds(o[0], o[1]) for o in outs] + c_shapes, scratch=scratch, aliases=aliases)(*flat, *c_arrays)
    return list(res[:n_out]), list(res[n_out:])


def _rows(name, fn, rows, consts, outs, accs, tm):
    t_rows = rows[0][0].shape[0]
    grid = (t_rows // tm,)
    in_specs = [_spec((tm, bw), functools.partial(lambda i, cb: (i, cb), cb=cb)) for _, bw, cb in rows]
    in_specs += [_spec(c.shape, functools.partial(lambda i, nd: (0,) * nd, nd=c.ndim)) for c in consts]
    out_shape = [_sds((t_rows, w), dt) for w, dt in outs] + [_sds(s, F32) for s in accs]
    out_specs = [_spec((tm, w), lambda i: (i, 0)) for w, _ in outs]
    out_specs += [_spec(s, functools.partial(lambda i, nd: (0,) * nd, nd=len(s))) for s in accs]
    nr, nc, no = len(rows), len(consts), len(outs)

    def body(*refs):
        r = [ref[...] for ref in refs[:nr]]
        c = [ref[...] for ref in refs[nr:nr + nc]]
        o_refs, a_refs = refs[nr + nc:nr + nc + no], refs[nr + nc + no:]
        o_vals, a_vals = fn(r, c)
        for ref, v in zip(o_refs, o_vals):
            ref[...] = v.astype(ref.dtype)
        if a_refs:
            @pl.when(pl.program_id(0) == 0)
            def _():
                for ref in a_refs:
                    ref[...] = jnp.zeros(ref.shape, F32)

            for ref, v in zip(a_refs, a_vals):
                ref[...] += v

    return _call(name, body, grid, in_specs, out_specs, out_shape)(*[a for a, _, _ in rows], *consts)


def _rmsnorm(x, g):
    return x * lax.rsqrt(jnp.mean(x * x, axis=-1, keepdims=True) + EPS) * g


def _layernorm(x, g, b):
    mu = jnp.mean(x, axis=-1, keepdims=True)
    var = jnp.mean(jnp.square(x - mu), axis=-1, keepdims=True)
    return (x - mu) * lax.rsqrt(var + EPS) * g + b


def _swiglu_act(g, u):
    return jax.nn.silu(g) * u


def _rope(x, cos, sin_signed):
    return x * cos + pltpu.roll(x, 2 * ROPE_HALF, 1) * sin_signed


def _rope_transposed(dy, cos, sin_signed):
    return dy * cos - pltpu.roll(dy, 2 * ROPE_HALF, 1) * sin_signed


def _rms_fwd(x, g, tm):
    d = x.shape[1]
    return _rows("rms_fwd", lambda r, c: ([_rmsnorm(r[0], c[0])], []), [(x, d, 0)], [g], [(d, MXU_DTYPE)], [], tm)[0]


def _rms_residual(x, y, g, scale, tm):
    d = x.shape[1]
    return _rows("rms_residual", lambda r, c: ([r[0] + scale * _rmsnorm(r[1], c[0])], []),
                 [(x, d, 0), (y, d, 0)], [g], [(d, F32)], [], tm)[0]


def _rms_bwd(y, g, dout, scale, out_dtype, tm, resid=None):
    d = y.shape[1]

    def fn(r, c):
        _, vjp = jax.vjp(_rmsnorm, r[0], c[0])
        dy, dg = vjp(scale * r[1].astype(F32))
        if resid is not None:
            dy = dy + r[2]
        return [dy], [dg]

    rows = [(y, d, 0), (dout, d, 0)] + ([(resid, d, 0)] if resid is not None else [])
    return _rows("rms_bwd" if resid is None else "rms_bwd_resid", fn, rows, [g], [(d, out_dtype)], [(1, d)], tm)


def _loss_head(y, target, tm):
    d = y.shape[1]

    def fn(r, c):
        err = r[0] - r[1]
        sq = jnp.sum(jnp.sum(err * err, axis=1, keepdims=True), axis=0, keepdims=True)
        return [err * (1.0 / d)], [jnp.broadcast_to(sq, (1, LANES))]

    return _rows("loss_head", fn, [(y, d, 0), (target, d, 0)], [], [(d, F32)], [(1, LANES)], tm)


def _ffn_fwd(x, pre_g, post_g, wg, wu, wd, layer, tm, tr, couriers=None, wd_landed=None, h=None, next_g=None):
    t, d = x.shape
    fs = wg.shape[3]
    if h is None:
        h = _rms_fwd(x, pre_g, tr)
    h_spec = _spec((tm, d), lambda j, i, k: (i, 0))
    w_spec = _spec((None, None, d, fs), lambda j, i, k: (j, layer, 0, 0))
    o_spec = _spec((tm, fs), lambda j, i, k: (i, j))
    couriers = couriers or {}
    carried = {}
    (g, u, a), carried["up"] = _matmul(
        "ffn_up", "nn", (N_CHIPS, t // tm, 1), [[(h, h_spec, wg, w_spec)], [(h, h_spec, wu, w_spec)]],
        [((t, N_CHIPS * fs), MXU_DTYPE, o_spec)] * 3, (tm, fs),
        epilogue=lambda accs, _: (accs[0], accs[1], _swiglu_act(accs[0], accs[1])), courier=couriers.get("up"))
    if wd_landed is not None:
        wd = carried["up"][wd_landed].reshape(wd.shape)
    row_spec = _spec((tm, d), lambda i, j, k: (i, 0))
    vec_spec = _spec((1, d), lambda i, j, k: (0, 0))

    def post(accs, ex):
        out = ex[0] + 0.5 * _rmsnorm(accs[0], ex[1])
        return (accs[0], out) + ((_rmsnorm(out, ex[2]),) if next_g is not None else ())

    res, carried["down"] = _matmul(
        "ffn_down", "nn", (t // tm, 1, N_CHIPS),
        [[(a, _spec((tm, fs), lambda i, j, k: (i, k)), wd, _spec((None, None, fs, d), lambda i, j, k: (k, layer, 0, 0)))]],
        [((t, d), F32, row_spec)] * 2 + ([((t, d), MXU_DTYPE, row_spec)] if next_g is not None else []), (tm, d),
        extras=[(x, row_spec), (post_g, vec_spec)] + ([(next_g, vec_spec)] if next_g is not None else []),
        epilogue=post, courier=couriers.get("down"))
    y, out = res[0], res[1]
    return out, (x, h, g, u, a, y), carried, (res[2] if next_g is not None else None)


def _ffn_bwd(dout, saved, pre_g, post_g, wg, wu, wd, layer, tm, tr, couriers=None, weights_first=False):
    x, h, g, u, a, y = saved
    t, d = x.shape
    fs = wg.shape[3]
    tn = _tile(d, 1024, LANES)
    tkt = _tile(t, TN_ROWS, 16)
    dy, dpost = _rms_bwd(y, post_g, dout, 0.5, MXU_DTYPE, tr)

    def act_bwd(accs, ex):
        _, vjp = jax.vjp(_swiglu_act, ex[0].astype(F32), ex[1].astype(F32))
        return vjp(accs[0])

    couriers = couriers or {}
    carried = {}
    known = {"carried": carried}

    def late(name):
        c = couriers.get(name)
        return c(known) if callable(c) else c

    gu_spec = _spec((tm, fs), lambda j, i, k: (i, j))
    (dg, du), carried["dact"] = _matmul(
        "ffn_dact", "nt", (N_CHIPS, t // tm, 1),
        [[(dy, _spec((tm, d), lambda j, i, k: (i, 0)), wd, _spec((None, None, fs, d), lambda j, i, k: (j, layer, 0, 0)))]],
        [((t, N_CHIPS * fs), MXU_DTYPE, gu_spec)] * 2, (tm, fs),
        extras=[(g, gu_spec), (u, gu_spec)], epilogue=act_bwd, courier=couriers.get("dact"))
    (dwd,), carried["dwdown"] = _matmul(
        "ffn_dwdown", "tn", (N_CHIPS, d // tn, t // tkt),
        [[(a, _spec((tkt, fs), lambda i, j, k: (k, i)), dy, _spec((tkt, tn), lambda i, j, k: (k, j)))]],
        [((N_CHIPS, fs, d), MXU_DTYPE, _spec((None, fs, tn), lambda i, j, k: (i, 0, j)))], (fs, tn),
        courier=late("dwdown"))

    def d_hidden(courier):
        da_spec = _spec((tm, fs), lambda i, j, k: (i, k))
        wt_spec = _spec((None, None, d, fs), lambda i, j, k: (k, layer, 0, 0))
        return _matmul("ffn_dh", "nt", (t // tm, 1, N_CHIPS), [[(dg, da_spec, wg, wt_spec), (du, da_spec, wu, wt_spec)]],
                       [((t, d), F32, _spec((tm, d), lambda i, j, k: (i, 0)))], (tm, d), courier=courier)

    def d_w_up(courier):
        tmw = _tile(d, 512, LANES)
        h_spec = _spec((tkt, tmw), lambda i, j, k: (k, i))
        dgu_spec = _spec((tkt, fs), lambda i, j, k: (k, j))
        dw_spec = _spec((None, tmw, fs), lambda i, j, k: (j, i, 0))
        return _matmul("ffn_dwup", "tn", (d // tmw, N_CHIPS, t // tkt), [[(h, h_spec, dg, dgu_spec)], [(h, h_spec, du, dgu_spec)]],
                       [((N_CHIPS, d, fs), MXU_DTYPE, dw_spec)] * 2, (tmw, fs), courier=courier)

    known["dwd"] = dwd
    if weights_first:
        (dwg, dwu), carried["dwup"] = d_w_up(late("dwup"))
        known["dwg"], known["dwu"] = dwg, dwu
        (dh,), carried["dh"] = d_hidden(late("dh"))
    else:
        (dh,), carried["dh"] = d_hidden(late("dh"))
        (dwg, dwu), carried["dwup"] = d_w_up(late("dwup"))
    dx, dpre = _rms_bwd(x, pre_g, dh, 1.0, F32, tr, resid=dout)
    return dx, (dpre, dpost, dwg, dwu, dwd), carried


def _attn_scores(qn, qpe, kn, kpe, qi, tq, seq):
    s = (_dot(qn, kn, "nt") + _dot(qpe, kpe, "nt")) * ((QK_NOPE + QK_ROPE) ** -0.5)
    rows = qi * tq + lax.broadcasted_iota(jnp.int32, (tq, seq), 0)
    cols = lax.broadcasted_iota(jnp.int32, (tq, seq), 1)
    s = jnp.where(cols <= rows, s, -jnp.inf)
    e = jnp.exp(s - jnp.max(s, axis=1, keepdims=True))
    return e / jnp.sum(e, axis=1, keepdims=True)


KEY_EXTENTS = 4


def _for_key_extent(qi, tq, seq, attend):
    n = KEY_EXTENTS if seq % (KEY_EXTENTS * tq) == 0 else 1
    ext = seq // n
    mine = (qi * tq) // ext
    for e in range(n):
        pl.when(mine == e)(functools.partial(attend, (e + 1) * ext))


def _attn_specs(nq, tq, seq):
    q_rows = lambda b, h, qi: b * nq + qi
    return [
        _spec((tq, LANES), lambda b, h, qi: (q_rows(b, h, qi), h)),
        _spec((tq, LANES), lambda b, h, qi: (q_rows(b, h, qi), HEADS + h)),
        _spec((seq, 2 * LANES), lambda b, h, qi: (b, h)),
        _spec((seq, LANES), lambda b, h, qi: (b, P_KR // LANES)),
        _spec((tq, LANES), lambda b, h, qi: (q_rows(b, h, qi), 0)),
        _spec((tq, LANES), lambda b, h, qi: (q_rows(b, h, qi), 0)),
        _spec((seq, LANES), lambda b, h, qi: (b, 0)),
        _spec((seq, LANES), lambda b, h, qi: (b, 0)),
    ]


def _attn_fwd(q_all, kv, p, cos, sin, batch, seq):
    t = q_all.shape[0]
    tq = _tile(seq, 256, 16)
    nq = seq // tq

    def body(qn_ref, qp_ref, kv_ref, kr_ref, cq, sq, ck, sk, o_ref):
        qi = pl.program_id(2)
        qpe = _rope(qp_ref[...], cq[...], sq[...])

        def attend(keys):
            kpe = _rope(kr_ref[:keys, :], ck[:keys, :], sk[:keys, :])
            prob = _attn_scores(qn_ref[...], qpe, kv_ref[:keys, :LANES], kpe, qi, tq, keys)
            o_ref[...] = _dot(prob, kv_ref[:keys, LANES:]).astype(o_ref.dtype)

        _for_key_extent(qi, tq, seq, attend)

    return _call("attn_fwd", body, (batch, HEADS, nq), _attn_specs(nq, tq, seq),
                 _spec((tq, LANES), lambda b, h, qi: (b * nq + qi, h)), _sds((t, 2 * HEADS * V_HEAD), MXU_DTYPE)
                 )(q_all, q_all, kv, p, cos, sin, cos, sin)


def _attn_bwd(q_all, kv, p, cos, sin, d_cat, batch, seq):
    t = q_all.shape[0]
    tq = _tile(seq, 256, 16)
    nq = seq // tq

    def body(qn_ref, qp_ref, kv_ref, kr_ref, cq, sq, ck, sk, do_ref, dqn_ref, dqp_ref, dkv_ref, dkr_ref, dkpe_acc):
        h, qi = pl.program_id(1), pl.program_id(2)
        qn = qn_ref[...]
        qpe = _rope(qp_ref[...], cq[...], sq[...])
        do = do_ref[...]

        @pl.when(qi == 0)
        def _():
            dkv_ref[...] = jnp.zeros(dkv_ref.shape, F32)

        @pl.when((h == 0) & (qi == 0))
        def _():
            dkpe_acc[...] = jnp.zeros(dkpe_acc.shape, F32)

        def attend(keys):
            kn, v = kv_ref[:keys, :LANES], kv_ref[:keys, LANES:]
            kpe = _rope(kr_ref[:keys, :], ck[:keys, :], sk[:keys, :])
            prob = _attn_scores(qn, qpe, kn, kpe, qi, tq, keys)
            dprob = _dot(do, v, "nt")
            ds = prob * (dprob - jnp.sum(prob * dprob, axis=1, keepdims=True)) * ((QK_NOPE + QK_ROPE) ** -0.5)
            dqn_ref[...] = _dot(ds, kn)
            dqp_ref[...] = _rope_transposed(_dot(ds, kpe), cq[...], sq[...])
            dkv_ref[:keys, :LANES] += _dot(ds, qn, "tn")
            dkv_ref[:keys, LANES:] += _dot(prob, do, "tn")
            dkpe_acc[:keys, :] += _dot(ds, qpe, "tn")

        _for_key_extent(qi, tq, seq, attend)

        @pl.when((h == HEADS - 1) & (qi == nq - 1))
        def _():
            dkr_ref[...] = _rope_transposed(dkpe_acc[...], ck[...], sk[...])

    q_out = _spec((tq, LANES), lambda b, h, qi: (b * nq + qi, h))
    return _call("attn_bwd", body, (batch, HEADS, nq),
                 _attn_specs(nq, tq, seq) + [_spec((tq, LANES), lambda b, h, qi: (b * nq + qi, h))],
                 [q_out, q_out, _spec((seq, 2 * LANES), lambda b, h, qi: (b, h)), _spec((seq, LANES), lambda b, h, qi: (b, 0))],
                 [_sds((t, HEADS * LANES), F32), _sds((t, HEADS * LANES), F32), _sds((t, HEADS * 2 * LANES), F32), _sds((t, LANES), F32)],
                 scratch=[pltpu.VMEM((seq, LANES), F32)])(q_all, q_all, kv, p, cos, sin, cos, sin, d_cat)


CONV_PAD = 32


def _conv_taps(w_ref, src_ref, first_row, n_rows, init, offset):
    acc = init
    for k in range(CONV_WIDTH):
        acc = acc + w_ref[k:k + 1, :] * src_ref[pl.ds(first_row + offset(k), n_rows), :]
    return acc


def _norm_act(conv, g, b):
    return jax.nn.silu(_layernorm(conv, g, b))


def _conv_fwd(p, cat, conv_w, conv_b, norm_g, norm_b, batch, seq):
    rc = _tile(seq, 256, 8)
    a_blk, gate_blk, out_blk = P_A // LANES, P_GATE // LANES, (HEADS * V_HEAD) // LANES

    def body(a_ref, gate_ref, w_ref, cb_ref, ng_ref, nb_ref, cat_in, o_ref, zp):
        del cat_in
        zp[pl.ds(0, CONV_PAD), :] = jnp.zeros((CONV_PAD, LANES), F32)
        zp[pl.ds(CONV_PAD, seq), :] = a_ref[...] * jax.nn.sigmoid(gate_ref[...])
        for r0 in range(0, seq, rc):
            conv = _conv_taps(w_ref, zp, r0, rc, jnp.broadcast_to(cb_ref[...], (rc, LANES)),
                              lambda k: CONV_PAD - (CONV_WIDTH - 1) + k)
            o_ref[pl.ds(r0, rc), :] = _norm_act(conv, ng_ref[...], nb_ref[...]).astype(o_ref.dtype)

    vec = _spec((1, LANES), lambda b, g: (0, g))
    return _call("conv_fwd", body, (batch, GROUPS),
                 [_spec((seq, LANES), lambda b, g: (b, a_blk + g)), _spec((seq, LANES), lambda b, g: (b, gate_blk + g)),
                  _spec((CONV_WIDTH, LANES), lambda b, g: (0, g)), vec, vec, vec, pl.BlockSpec(memory_space=pl.ANY)],
                 _spec((seq, LANES), lambda b, g: (b, out_blk + g)), _sds(cat.shape, cat.dtype),
                 scratch=[pltpu.VMEM((seq + CONV_PAD, LANES), F32)], aliases={6: 0}
                 )(p, p, conv_w, conv_b, norm_g, norm_b, cat)


def _conv_bwd(p, d_cat, conv_w, conv_b, norm_g, norm_b, batch, seq):
    t = p.shape[0]
    rc = _tile(seq, 256, 8)
    a_blk, gate_blk, out_blk = P_A // LANES, P_GATE // LANES, (HEADS * V_HEAD) // LANES

    def body(a_ref, gate_ref, w_ref, cb_ref, ng_ref, nb_ref, dc_ref, da_ref, dgate_ref, dw_ref, db_ref, dng_ref, dnb_ref,
             zp, dcp, buf):
        b = pl.program_id(1)
        a, sig = a_ref[...], jax.nn.sigmoid(gate_ref[...])
        zp[pl.ds(0, CONV_PAD), :] = jnp.zeros((CONV_PAD, LANES), F32)
        zp[pl.ds(CONV_PAD, seq), :] = a * sig
        for r0 in range(0, seq, rc):
            buf[pl.ds(r0, rc), :] = _conv_taps(w_ref, zp, r0, rc, jnp.broadcast_to(cb_ref[...], (rc, LANES)),
                                               lambda k: CONV_PAD - (CONV_WIDTH - 1) + k)
        _, vjp = jax.vjp(_norm_act, buf[...], ng_ref[...], nb_ref[...])
        dconv, dng, dnb = vjp(dc_ref[...].astype(F32))
        dcp[pl.ds(0, seq), :] = dconv
        dcp[pl.ds(seq, CONV_PAD), :] = jnp.zeros((CONV_PAD, LANES), F32)

        @pl.when(b == 0)
        def _():
            dw_ref[...] = jnp.zeros(dw_ref.shape, F32)
            db_ref[...] = jnp.zeros(db_ref.shape, F32)
            dng_ref[...] = jnp.zeros(dng_ref.shape, F32)
            dnb_ref[...] = jnp.zeros(dnb_ref.shape, F32)

        db_ref[...] += jnp.sum(dconv, axis=0, keepdims=True)
        dng_ref[...] += dng
        dnb_ref[...] += dnb
        for k in range(CONV_WIDTH):
            shifted = zp[pl.ds(CONV_PAD - (CONV_WIDTH - 1) + k, seq), :]
            dw_ref[k:k + 1, :] += jnp.sum(dconv * shifted, axis=0, keepdims=True)
        for r0 in range(0, seq, rc):
            buf[pl.ds(r0, rc), :] = _conv_taps(w_ref, dcp, r0, rc, jnp.zeros((rc, LANES), F32),
                                               lambda k: CONV_WIDTH - 1 - k)
        dz = buf[...]
        da_ref[...] = (dz * sig).astype(da_ref.dtype)
        dgate_ref[...] = (dz * a * sig * (1.0 - sig)).astype(dgate_ref.dtype)

    vec = _spec((1, LANES), lambda g, b: (0, g))
    row_out = _spec((seq, LANES), lambda g, b: (b, g))
    return _call("conv_bwd", body, (GROUPS, batch),
                 [_spec((seq, LANES), lambda g, b: (b, a_blk + g)), _spec((seq, LANES), lambda g, b: (b, gate_blk + g)),
                  _spec((CONV_WIDTH, LANES), lambda g, b: (0, g)), vec, vec, vec,
                  _spec((seq, LANES), lambda g, b: (b, out_blk + g))],
                 [row_out, row_out, _spec((CONV_WIDTH, LANES), lambda g, b: (0, g)), vec, vec, vec],
                 [_sds((t, CONV_CH), MXU_DTYPE), _sds((t, CONV_CH), MXU_DTYPE), _sds((CONV_WIDTH, CONV_CH), F32),
                  _sds((1, CONV_CH), F32), _sds((1, CONV_CH), F32), _sds((1, CONV_CH), F32)],
                 scratch=[pltpu.VMEM((seq + CONV_PAD, LANES), F32), pltpu.VMEM((seq + CONV_PAD, LANES), F32),
                          pltpu.VMEM((seq, LANES), F32)])(p, p, conv_w, conv_b, norm_g, norm_b, d_cat)


def _sgu_pre(hw, g, b):
    z = jax.nn.gelu(hw)
    return z[:, :GM_WIDTH], _layernorm(z[:, GM_WIDTH:], g, b)


def _causal(w):
    keep = lax.broadcasted_iota(jnp.int32, (CHUNK, CHUNK), 0) >= lax.broadcasted_iota(jnp.int32, (CHUNK, CHUNK), 1)
    return jnp.where(keep, w, 0.0)


def _sgu_fwd(hw, vg, vb, w_s, b_st):
    t = hw.shape[0]
    gw = GM_WIDTH // GROUPS

    def body(hw_ref, vg_ref, vb_ref, w_ref, b_ref, o_ref):
        u, v = _sgu_pre(hw_ref[...], vg_ref[...], vb_ref[...])
        for g in range(GROUPS):
            cols = slice(g * gw, (g + 1) * gw)
            s = _dot(_causal(w_ref[g]), v[:, cols]) + b_ref[:, g:g + 1]
            o_ref[:, cols] = (u[:, cols] * s).astype(o_ref.dtype)

    return _call("sgu_fwd", body, (t // CHUNK,),
                 [_spec((CHUNK, 2 * GM_WIDTH), lambda i: (i, 0)), _spec((1, GM_WIDTH), lambda i: (0, 0)),
                  _spec((1, GM_WIDTH), lambda i: (0, 0)), _spec((GROUPS, CHUNK, CHUNK), lambda i: (0, 0, 0)),
                  _spec((CHUNK, GROUPS), lambda i: (0, 0))],
                 _spec((CHUNK, GM_WIDTH), lambda i: (i, 0)), _sds((t, GM_WIDTH), MXU_DTYPE))(hw, vg, vb, w_s, b_st)


def _sgu_bwd(hw, vg, vb, w_s, b_st, dout):
    t = hw.shape[0]
    gw = GM_WIDTH // GROUPS

    def body(hw_ref, vg_ref, vb_ref, w_ref, b_ref, do_ref, dhw_ref, dvg_ref, dvb_ref, dw_ref, db_ref, du_buf, dv_buf):
        (u, v), vjp = jax.vjp(_sgu_pre, hw_ref[...], vg_ref[...], vb_ref[...])

        @pl.when(pl.program_id(0) == 0)
        def _():
            dvg_ref[...] = jnp.zeros(dvg_ref.shape, F32)
            dvb_ref[...] = jnp.zeros(dvb_ref.shape, F32)
            dw_ref[...] = jnp.zeros(dw_ref.shape, F32)
            db_ref[...] = jnp.zeros(db_ref.shape, F32)

        lane = lax.broadcasted_iota(jnp.int32, (CHUNK, LANES), 1)
        db = jnp.zeros((CHUNK, LANES), F32)
        for g in range(GROUPS):
            cols = slice(g * gw, (g + 1) * gw)
            w = _causal(w_ref[g])
            s = _dot(w, v[:, cols]) + b_ref[:, g:g + 1]
            do = do_ref[:, cols]
            ds = do * u[:, cols]
            du_buf[:, cols] = do * s
            dw_ref[g] += _causal(_dot(ds, v[:, cols], "nt"))
            dv_buf[:, cols] = _dot(w, ds, "tn")
            db = db + jnp.where(lane == g, jnp.sum(ds, axis=1, keepdims=True), 0.0)
        db_ref[...] += db
        dhw, dvg, dvb = vjp((du_buf[...], dv_buf[...]))
        dhw_ref[...] = dhw.astype(dhw_ref.dtype)
        dvg_ref[...] += dvg
        dvb_ref[...] += dvb

    vec = _spec((1, GM_WIDTH), lambda i: (0, 0))
    w_spec = _spec((GROUPS, CHUNK, CHUNK), lambda i: (0, 0, 0))
    return _call("sgu_bwd", body, (t // CHUNK,),
                 [_spec((CHUNK, 2 * GM_WIDTH), lambda i: (i, 0)), vec, vec, w_spec, _spec((CHUNK, GROUPS), lambda i: (0, 0)),
                  _spec((CHUNK, GM_WIDTH), lambda i: (i, 0))],
                 [_spec((CHUNK, 2 * GM_WIDTH), lambda i: (i, 0)), vec, vec, w_spec, _spec((CHUNK, LANES), lambda i: (0, 0))],
                 [_sds((t, 2 * GM_WIDTH), MXU_DTYPE), _sds((1, GM_WIDTH), F32), _sds((1, GM_WIDTH), F32),
                  _sds((GROUPS, CHUNK, CHUNK), F32), _sds((CHUNK, LANES), F32)],
                 scratch=[pltpu.VMEM((CHUNK, GM_WIDTH), F32), pltpu.VMEM((CHUNK, GM_WIDTH), F32)]
                 )(hw, vg, vb, w_s, b_st, dout)


def _mm_nn(name, a, b, out_dtype, tm, a_col0=0, tn_want=1024, tk_want=1024, courier=None):
    t = a.shape[0]
    kk, n = b.shape
    tk = _tile(kk, tk_want, LANES)
    tn = _tile(n, tn_want, LANES)
    k0 = a_col0 // tk
    assert a_col0 % tk == 0
    outs, carried = _matmul(
        name, "nn", (t // tm, n // tn, kk // tk),
        [[(a, _spec((tm, tk), lambda i, j, k: (i, k0 + k)), b, _spec((tk, tn), lambda i, j, k: (k, j)))]],
        [((t, n), out_dtype, _spec((tm, tn), lambda i, j, k: (i, j)))], (tm, tn), courier=courier)
    return (outs[0], carried) if courier else outs[0]


def _mm_nt(name, pairs, out_dtype, tm, tn_want=1024, tk_want=1024, courier=None):
    t, kk = pairs[0][0].shape
    n = pairs[0][1].shape[0]
    tk = _tile(kk, tk_want, LANES)
    tn = _tile(n, tn_want, 16)
    grp = []
    for a, b, b_col0 in pairs:
        assert b_col0 % tk == 0 and a.shape == (t, kk)
        grp.append((a, _spec((tm, tk), lambda i, j, k: (i, k)), b,
                    _spec((tn, tk), functools.partial(lambda i, j, k, k0: (j, k0 + k), k0=b_col0 // tk))))
    outs, carried = _matmul(name, "nt", (t // tm, n // tn, kk // tk), [grp],
                            [((t, n), out_dtype, _spec((tm, tn), lambda i, j, k: (i, j)))], (tm, tn), courier=courier)
    return (outs[0], carried) if courier else outs[0]


def _mm_tn(name, a, bs, out_dtype, a_col0=0, m_width=None, tm_want=512, tn_want=1024, courier=None):
    t = a.shape[0]
    m = m_width or a.shape[1]
    n = bs[0].shape[1]
    tmw = _tile(m, tm_want, LANES)
    tn = _tile(n, tn_want, LANES)
    tkt = _tile(t, TN_ROWS, 16)
    assert a_col0 % tmw == 0
    i0 = a_col0 // tmw
    a_spec = _spec((tkt, tmw), lambda i, j, k: (k, i0 + i))
    groups = [[(a, a_spec, b, _spec((tkt, tn), lambda i, j, k: (k, j)))] for b in bs]
    outs, carried = _matmul(name, "tn", (m // tmw, n // tn, t // tkt), groups,
                            [((m, n), out_dtype, _spec((tmw, tn), lambda i, j, k: (i, j)))] * len(bs), (tmw, tn),
                            courier=courier)
    return (outs, carried) if courier else outs


_ANY = pl.BlockSpec(memory_space=pl.ANY)


def _comm_call(name, body, ins, out_shapes, sems, aliases=None):
    return pl.pallas_call(body, name=name, out_shape=out_shapes, in_specs=[_ANY] * len(ins),
                          out_specs=[_ANY] * len(out_shapes), input_output_aliases=aliases or {},
                          scratch_shapes=[pltpu.SemaphoreType.DMA((n,)) for n in sems])(*ins)


def _place():
    return lax.axis_index("x"), lax.axis_index("y"), lax.axis_index("c")


def _other_chips(x, y):
    return [(1 - x, y), (x, 1 - y), (1 - x, 1 - y)]


def _cast_into_slot(w, place, dtype):
    layers, rows, cols = w.shape
    half = rows // 2
    tr = _tile(half, 256, 16)
    nt = half // tr

    def body(s, w_ref, o_ref):
        del s
        o_ref[...] = w_ref[...].astype(o_ref.dtype)

    return _call("cast_into_slot", body, (layers, 2, nt),
                 [_spec((None, tr, cols), lambda l, h, i, s: (l, h * nt + i, 0))],
                 _spec((None, None, None, tr, cols), lambda l, h, i, s: (s[0], l, h, i, 0)),
                 _sds((N_CHIPS, layers, 2, half, cols), dtype), prefetch=1)(place, w)


class _GatherChips:
    aliased = True

    def __init__(self, bufs):
        self.arrays = list(bufs)
        self.out_shapes = [_sds(b.shape, b.dtype) for b in bufs]
        self.sems = [3 * len(bufs)] * 4

    def _sent(self, buf, sems):
        x, y, c = _place()
        me = 2 * x + y
        return [[pltpu.make_async_remote_copy(
            src_ref=buf[t].at[me, :, c], dst_ref=buf[t].at[me, :, c], send_sem=sems[0].at[3 * t + r],
            recv_sem=sems[1].at[3 * t + r], device_id=(px, py, c), device_id_type=MESH)
            for r, (px, py) in enumerate(_other_chips(x, y))] for t in range(len(buf))]

    def start(self, _, buf, sems):
        for row in self._sent(buf, sems):
            for cp in row:
                cp.start()

    def finish(self, _, buf, sems):
        x, y, c = _place()
        sent = self._sent(buf, sems)
        passed = []
        for t in range(len(buf)):
            for r, (px, py) in enumerate(_other_chips(x, y)):
                landed = buf[t].at[2 * px + py, :, c]
                sent[t][r].wait_recv()
                cp = pltpu.make_async_remote_copy(
                    src_ref=landed, dst_ref=landed, send_sem=sems[2].at[3 * t + r], recv_sem=sems[3].at[3 * t + r],
                    device_id=(x, y, 1 - c), device_id_type=MESH)
                cp.start()
                passed.append(cp)
        for cp in passed:
            cp.wait()
        for row in sent:
            for cp in row:
                cp.wait_send()


class _ScatterChips:
    aliased = False

    def __init__(self, partials):
        self.arrays = list(partials)
        self.out_shapes = [_sds((3,) + p.shape[1:], p.dtype) for p in partials]
        self.sems = [3 * len(partials)] * 2

    def _copies(self, src, dst, sems):
        x, y, c = _place()
        return [pltpu.make_async_remote_copy(
            src_ref=src[t].at[2 * px + py], dst_ref=dst[t].at[r], send_sem=sems[0].at[3 * t + r],
            recv_sem=sems[1].at[3 * t + r], device_id=(px, py, c), device_id_type=MESH)
            for t in range(len(src)) for r, (px, py) in enumerate(_other_chips(x, y))]

    def start(self, src, dst, sems):
        for cp in self._copies(src, dst, sems):
            cp.start()

    def finish(self, src, dst, sems):
        for cp in self._copies(src, dst, sems):
            cp.wait()


def _run_courier(name, courier):
    n_in, n_out = len(courier.arrays), len(courier.out_shapes)

    def body(*refs):
        src, dst, sems = refs[:n_in], refs[n_in:n_in + n_out], refs[n_in + n_out:]
        courier.start(src, dst, sems)
        courier.finish(src, dst, sems)

    return _comm_call(name, body, courier.arrays, courier.out_shapes, courier.sems,
                      aliases={t: t for t in range(n_in)} if courier.aliased else None)


class _SwapHalves:
    aliased = False

    def __init__(self, grads):
        self.arrays = list(grads)
        self.out_shapes = [_sds((g.shape[0],) + g.shape[2:], g.dtype) for g in grads]
        self.sems = [len(grads)] * 2

    def _copies(self, src, dst, sems):
        x, y, c = _place()
        return [pltpu.make_async_remote_copy(
            src_ref=src[t].at[:, 1 - c], dst_ref=dst[t], send_sem=sems[0].at[t], recv_sem=sems[1].at[t],
            device_id=(x, y, 1 - c), device_id_type=MESH) for t in range(len(src))]

    def start(self, src, dst, sems):
        for cp in self._copies(src, dst, sems):
            cp.start()

    def finish(self, src, dst, sems):
        for cp in self._copies(src, dst, sems):
            cp.wait()


class _Both:
    aliased = False

    def __init__(self, first, second):
        assert not first.aliased and not second.aliased
        self.parts = (first, second)
        self.arrays = first.arrays + second.arrays
        self.out_shapes = first.out_shapes + second.out_shapes
        self.sems = first.sems + second.sems

    def _split(self, src, dst, sems):
        a = self.parts[0]
        na, no, ns = len(a.arrays), len(a.out_shapes), len(a.sems)
        return (src[:na], dst[:no], sems[:ns]), (src[na:], dst[no:], sems[ns:])

    def start(self, src, dst, sems):
        for part, args in zip(self.parts, self._split(src, dst, sems)):
            part.start(*args)

    def finish(self, src, dst, sems):
        for part, args in zip(self.parts, self._split(src, dst, sems)):
            part.finish(*args)


def _join_halves(grads):
    n = len(grads)

    def body(*refs):
        buf = refs[n:2 * n]
        send_sems, recv_sems = refs[2 * n:]
        x, y, c = _place()
        copies = [pltpu.make_async_remote_copy(
            src_ref=buf[t].at[:, c], dst_ref=buf[t].at[:, c], send_sem=send_sems.at[t], recv_sem=recv_sems.at[t],
            device_id=(x, y, 1 - c), device_id_type=MESH) for t in range(n)]
        for cp in copies:
            cp.start()
        for cp in copies:
            cp.wait()

    return _comm_call("join_halves", body, grads, [_sds(g.shape, g.dtype) for g in grads], [n, n],
                      aliases={t: t for t in range(n)})


def _gather_devices(slots):
    def body(src, buf, send_sems, recv_sems):
        del src
        x, y, c = _place()
        me = 4 * x + 2 * y + c
        copies = []
        for r in range(1, N_DEV):
            fx, fy, fc = (r >> 2) & 1, (r >> 1) & 1, r & 1
            peer = (1 - x if fx else x, 1 - y if fy else y, 1 - c if fc else c)
            copies.append(pltpu.make_async_remote_copy(
                src_ref=buf.at[me], dst_ref=buf.at[me], send_sem=send_sems.at[r - 1], recv_sem=recv_sems.at[r - 1],
                device_id=peer, device_id_type=MESH))
        for cp in copies:
            cp.start()
        for cp in copies:
            cp.wait()

    return _comm_call("gather_devices", body, [slots], [_sds(slots.shape, slots.dtype)], [N_DEV - 1, N_DEV - 1],
                      aliases={0: 0})[0]


def _add_halves(grad, got, place):
    _, _, rows, cols = grad.shape
    tr = _tile(rows, 256, 16)

    def body(s, a, b, o):
        del s
        o[...] = (a[...].astype(F32) + b[...].astype(F32)).astype(o.dtype)

    return _call("add_halves", body, (N_CHIPS, rows // tr),
                 [_spec((None, None, tr, cols), lambda j, i, s: (j, s[1], i, 0)), _spec((None, tr, cols), lambda j, i, s: (j, i, 0))],
                 _spec((None, tr, cols), lambda j, i, s: (j, i, 0)), _sds((N_CHIPS, rows, cols), grad.dtype), prefetch=1
                 )(place, grad, got)


def _add_chips(partial, got, place, layer, layers, into=None):
    _, rows, cols = partial.shape
    tr = _tile(rows, 256, 16)

    def body(s, a, b, *rest):
        o = rest[-1]
        o[...] = a[...].astype(F32) + b[0].astype(F32) + b[1].astype(F32) + b[2].astype(F32)

    in_specs = [_spec((None, tr, cols), lambda i, s: (s[0], i, 0)), _spec((3, tr, cols), lambda i, s: (0, i, 0))]
    args = [place, partial, got]
    if into is not None:
        in_specs.append(pl.BlockSpec(memory_space=pl.ANY))
        args.append(into)
    return _call("add_chips", body, (rows // tr,), in_specs,
                 _spec((None, None, tr, cols), lambda i, s: (layer, s[1], i, 0)), _sds((layers, 2, rows, cols), F32),
                 prefetch=1, aliases={3: 0} if into is not None else None)(*args)


def _add_devices(got):
    _, rows, cols = got.shape
    tr = _tile(rows, 512, 8)

    def body(a, o):
        total = a[0]
        for d in range(1, N_DEV):
            total = total + a[d]
        o[...] = total

    return _call("add_devices", body, (rows // tr,), [_spec((N_DEV, tr, cols), lambda i: (0, i, 0))],
                 _spec((tr, cols), lambda i: (i, 0)), _sds((rows, cols), F32))(got)


def _adamw(w, g, m, v):
    layers, rows, cols = w.shape
    tr = _tile(rows, 256, 8)

    def body(w_ref, g_ref, m_ref, v_ref, d_ref, nm_ref, nv_ref):
        grad = g_ref[...]
        new_m = ADAM_B1 * m_ref[...] + (1.0 - ADAM_B1) * grad
        new_v = ADAM_B2 * v_ref[...] + (1.0 - ADAM_B2) * jnp.square(grad)
        m_hat = new_m / (1.0 - ADAM_B1 ** ADAM_STEP)
        v_hat = new_v / (1.0 - ADAM_B2 ** ADAM_STEP)
        d_ref[...] = -ADAM_LR * (m_hat / (jnp.sqrt(v_hat) + ADAM_EPS) + ADAM_WD * w_ref[...])
        nm_ref[...] = new_m
        nv_ref[...] = new_v

    blk = _spec((None, tr, cols), lambda l, i: (l, i, 0))
    return _call("adamw", body, (layers, rows // tr), [blk] * 4, [blk] * 3, [_sds(w.shape, F32)] * 3)(w, g, m, v)


def _rope_layout(w):
    z = jnp.zeros(w.shape[:-1] + (ROPE_HALF,), w.dtype)
    return jnp.concatenate([w[..., :ROPE_HALF], z, w[..., ROPE_HALF:], z], axis=-1)


def _rope_layout_inv(w):
    return jnp.concatenate([w[..., :ROPE_HALF], w[..., 2 * ROPE_HALF:3 * ROPE_HALF]], axis=-1)


def _cols_from_chips(g):
    return jnp.moveaxis(g, 0, 1).reshape(g.shape[1], N_CHIPS * g.shape[2])


def _cols_to_chips(w):
    return jnp.moveaxis(w.reshape(w.shape[0], N_CHIPS, w.shape[1] // N_CHIPS), 1, 0)


def _w_in_layout(w):
    off_kr = Q_LORA + KV_LORA
    return jnp.concatenate([w[:, :off_kr], w[:, off_kr + QK_ROPE:], _rope_layout(w[:, off_kr:off_kr + QK_ROPE])], axis=1)


def _w_in_layout_inv(dw):
    return jnp.concatenate([dw[:, :P_A], _rope_layout_inv(dw[:, P_KR:]), dw[:, P_A:P_KR]], axis=1)


def _w_uq_layout(w):
    w = w.reshape(Q_LORA, HEADS, QK_NOPE + QK_ROPE)
    return jnp.concatenate([w[..., :QK_NOPE].reshape(Q_LORA, HEADS * QK_NOPE),
                            _rope_layout(w[..., QK_NOPE:]).reshape(Q_LORA, HEADS * LANES)], axis=1)


def _w_uq_layout_inv(d_nope, d_rope):
    d_nope = d_nope.reshape(Q_LORA, HEADS, QK_NOPE)
    d_rope = _rope_layout_inv(d_rope.reshape(Q_LORA, HEADS, LANES))
    return jnp.concatenate([d_nope, d_rope], axis=-1).reshape(Q_LORA, HEADS * (QK_NOPE + QK_ROPE))


def _to_lanes(a):
    flat = a.reshape(-1)
    pad = (-flat.shape[0]) % LANES
    if pad:
        flat = jnp.concatenate([flat, jnp.zeros((pad,), flat.dtype)])
    return flat.reshape(-1, LANES)


PACK_ROWS = 64


def _pack(arrays):
    parts, ranges, row = [], [], 0
    for a in arrays:
        p = _to_lanes(a)
        parts.append(p)
        ranges.append((row, row + p.shape[0]))
        row += p.shape[0]
    pad = (-row) % PACK_ROWS
    if pad:
        parts.append(jnp.zeros((pad, LANES), F32))
    return jnp.concatenate(parts, axis=0), ranges


def _unpack(packed, rng, shape):
    n = 1
    for s in shape:
        n *= s
    return packed[rng[0]:rng[1]].reshape(-1)[:n].reshape(shape)


def kernel(x, positions, ffn_a_pre_g, ffn_a_post_g, ffn_a_w_gate, ffn_a_w_up, ffn_a_w_down, ffn_b_pre_g, ffn_b_post_g, ffn_b_w_gate, ffn_b_w_up, ffn_b_w_down, even_pre_g, even_post_g, even_w_in, even_q_norm_g, even_kv_norm_g, even_w_uq, even_w_ukv, even_conv_w, even_conv_b, even_conv_norm_g, even_conv_norm_b, even_w_out, odd_pre_g, odd_post_g, odd_w_in, odd_v_norm_g, odd_v_norm_b, odd_w_s, odd_b_s, odd_w_out, loss_target, m_ffn_a_pre_g, m_ffn_a_post_g, m_ffn_a_w_gate, m_ffn_a_w_up, m_ffn_a_w_down, m_ffn_b_pre_g, m_ffn_b_post_g, m_ffn_b_w_gate, m_ffn_b_w_up, m_ffn_b_w_down, m_even_pre_g, m_even_post_g, m_even_w_in, m_even_q_norm_g, m_even_kv_norm_g, m_even_w_uq, m_even_w_ukv, m_even_conv_w, m_even_conv_b, m_even_conv_norm_g, m_even_conv_norm_b, m_even_w_out, m_odd_pre_g, m_odd_post_g, m_odd_w_in, m_odd_v_norm_g, m_odd_v_norm_b, m_odd_w_s, m_odd_b_s, m_odd_w_out, v_ffn_a_pre_g, v_ffn_a_post_g, v_ffn_a_w_gate, v_ffn_a_w_up, v_ffn_a_w_down, v_ffn_b_pre_g, v_ffn_b_post_g, v_ffn_b_w_gate, v_ffn_b_w_up, v_ffn_b_w_down, v_even_pre_g, v_even_post_g, v_even_w_in, v_even_q_norm_g, v_even_kv_norm_g, v_even_w_uq, v_even_w_ukv, v_even_conv_w, v_even_conv_b, v_even_conv_norm_g, v_even_conv_norm_b, v_even_w_out, v_odd_pre_g, v_odd_post_g, v_odd_w_in, v_odd_v_norm_g, v_odd_v_norm_b, v_odd_w_s, v_odd_b_s, v_odd_w_out):
    given = dict(locals())
    w = {n: given[n] for n in WEIGHTS}
    batch, seq, d = x.shape
    t = batch * seq
    tm = _tile(t, 512, 16)
    tr = _tile(t, 256, 16)
    chip = 2 * lax.axis_index("x") + lax.axis_index("y")
    place = jnp.stack([chip, lax.axis_index("c")]).astype(jnp.int32)

    small_shard, small_shard_rng = _pack([w[n] for n in SMALL_SHARDED])
    wbuf = {(n, layer): _cast_into_slot(w[n][layer:layer + 1], place, MXU_DTYPE)
            for n in BIG for layer in range(w[n].shape[0])}
    wbuf['small'] = _cast_into_slot(small_shard[None], place, F32)

    def ffn_keys(tag, layer):
        return [(f'ffn_{tag}_w_{part}', layer) for part in ('gate', 'up', 'down')]

    def gather(keys):
        return _GatherChips([wbuf[k] for k in keys]), keys

    def landed(order, results):
        for k, r in zip(order[1], results):
            wbuf[k] = r

    def weight(key):
        n, _ = key
        return wbuf[key].reshape((N_CHIPS, 1) + w[n].shape[1:])

    first = gather(ffn_keys('a', 0)[:2] + ['small'])
    landed(first, _run_courier("gather_chips", first[0]))
    shards = wbuf['small'].reshape((N_CHIPS,) + small_shard.shape)
    full = dict(w)
    for n, rng in zip(SMALL_SHARDED, small_shard_rng):
        per_chip = [_unpack(shards[j], rng, w[n].shape) for j in range(N_CHIPS)]
        full[n] = jnp.concatenate(per_chip, axis=-1)

    inv_freq = ROPE_THETA ** (-jnp.arange(0, QK_ROPE, 2, dtype=F32) / QK_ROPE)
    ang = positions.astype(F32).reshape(t, 1) * inv_freq
    zeros = jnp.zeros((t, ROPE_HALF), F32)
    cos = jnp.concatenate([jnp.cos(ang), zeros, jnp.cos(ang), zeros], axis=1)
    sin = jnp.concatenate([-jnp.sin(ang), zeros, jnp.sin(ang), zeros], axis=1)

    def ffn_args(tag, layer):
        gate, up, down = (weight(k) for k in ffn_keys(tag, layer))
        return (w[f'ffn_{tag}_pre_g'][layer:layer + 1], w[f'ffn_{tag}_post_g'][layer:layer + 1], gate, up, down, 0, tm, tr)

    def ffn_forward(xin, tag, layer, up_keys, down_keys, h=None, next_g=None):
        orders = {'up': gather(ffn_keys(tag, layer)[2:] + up_keys)}
        if down_keys:
            orders['down'] = gather(down_keys)
        out, saved, carried, h_next = _ffn_fwd(xin, *ffn_args(tag, layer), couriers={k: o[0] for k, o in orders.items()},
                                               wd_landed=0, h=h, next_g=next_g)
        for k, o in orders.items():
            landed(o, carried[k])
        return out, saved, h_next

    even_keys = [('even_w_in', 0), ('even_w_uq', 0), ('even_w_ukv', 0), ('even_w_out', 0)]
    odd_keys = [('odd_w_in', 0), ('odd_w_out', 0)]

    xs = x.reshape(t, d)
    x1, ffn_a0, h_e = ffn_forward(xs, 'a', 0, even_keys, ffn_keys('b', 0)[:1], next_g=w['even_pre_g'])
    w_in = _w_in_layout(_cols_from_chips(weight(('even_w_in', 0))[:, 0]))
    w_uq = _w_uq_layout(_cols_from_chips(weight(('even_w_uq', 0))[:, 0]))
    w_ukv = _cols_from_chips(weight(('even_w_ukv', 0))[:, 0])
    w_out_e = weight(('even_w_out', 0))[:, 0].reshape(d, d)
    order = gather(ffn_keys('b', 0)[1:2])
    p, res = _mm_nn("mm_w_in", h_e, w_in, F32, tm, tn_want=640, courier=order[0])
    landed(order, res)
    lat = _rows("latent_norm", lambda r, c: ([jnp.concatenate([_rmsnorm(r[0], c[0]), _rmsnorm(r[1], c[1])], axis=1)], []),
                [(p, Q_LORA, 0), (p, KV_LORA, 1)], [w['even_q_norm_g'], w['even_kv_norm_g']], [(Q_LORA + KV_LORA, MXU_DTYPE)], [], tr)[0]
    q_all = _mm_nn("mm_w_uq", lat, w_uq, F32, tm, a_col0=0, tk_want=Q_LORA)
    kv = _mm_nn("mm_w_ukv", lat, w_ukv, MXU_DTYPE, tm, a_col0=Q_LORA, tk_want=KV_LORA)
    cat = _attn_fwd(q_all, kv, p, cos, sin, batch, seq)
    cat = _conv_fwd(p, cat, full['even_conv_w'][0], w['even_conv_b'], w['even_conv_norm_g'], w['even_conv_norm_b'], batch, seq)
    y_e = _mm_nn("mm_w_out", cat, w_out_e, F32, tm)
    x2 = _rms_residual(x1, y_e, w['even_post_g'], 1.0, tr)
    x3, ffn_b0, h_a1 = ffn_forward(x2, 'b', 0, ffn_keys('a', 1)[:1], ffn_keys('a', 1)[1:2], next_g=w['ffn_a_pre_g'][1:2])
    x4, ffn_a1, h_o = ffn_forward(x3, 'a', 1, odd_keys, ffn_keys('b', 1)[:1], h=h_a1, next_g=full['odd_pre_g'])
    w_out_o = weight(('odd_w_out', 0))[:, 0].reshape(GM_WIDTH, d)
    w_in_o = weight(('odd_w_in', 0))
    ns = w_in_o.shape[3]
    tk_o = _tile(d, 1024, LANES)
    order = gather(ffn_keys('b', 1)[1:2])
    (hw,), res = _matmul("mm_w_in_odd", "nn", (t // tm, N_CHIPS, d // tk_o),
                         [[(h_o, _spec((tm, tk_o), lambda i, j, k: (i, k)), w_in_o, _spec((None, None, tk_o, ns), lambda i, j, k: (j, 0, k, 0)))]],
                         [((t, N_CHIPS * ns), F32, _spec((tm, ns), lambda i, j, k: (i, j)))], (tm, ns), courier=order[0])
    landed(order, res)
    b_st = w['odd_b_s'][0].T
    sg = _sgu_fwd(hw, full['odd_v_norm_g'], full['odd_v_norm_b'], w['odd_w_s'][0], b_st)
    y_o = _mm_nn("mm_w_out", sg, w_out_o, F32, tm)
    x5 = _rms_residual(x4, y_o, full['odd_post_g'], 1.0, tr)
    x6, ffn_b1, _ = ffn_forward(x5, 'b', 1, [], [])

    dy, sq_err = _loss_head(x6, loss_target.reshape(t, d), tr)
    loss = lax.psum(0.5 * sq_err[0, 0] / d, ("x", "y", "c"))

    grads = {}
    received = {}

    def swap(named):
        items = [(k, g.reshape(N_CHIPS, 2, g.shape[1] // 2, g.shape[2])) for k, g in named]
        return _SwapHalves([g for _, g in items]), items

    def core_sums_of(order, results):
        return [(k, _add_halves(g, r, place)) for (k, g), r in zip(order[1], results)]

    def core_sums(named):
        order = swap(named)
        return core_sums_of(order, _run_courier("swap_halves", order[0]))

    def scatter(named):
        return _ScatterChips([pt for _, pt in named]), named

    def arrived(order, results):
        for (k, pt), r in zip(order[1], results):
            received[k] = (pt, r)

    def ffn_grads(tag, layer):
        return list(zip(ffn_keys(tag, layer), fg[(tag, layer)][2:]))

    fg = {}
    made = {}
    dx, fg[('b', 1)], _ = _ffn_bwd(dy, ffn_b1, *ffn_args('b', 1))
    swap_b1 = swap(ffn_grads('b', 1))
    dy_o, grads['odd_post_g'] = _rms_bwd(y_o, full['odd_post_g'], dx, 1.0, MXU_DTYPE, tr)
    dsg, res = _mm_nt("mm_dsg", [(dy_o, w_out_o, 0)], F32, tm, courier=swap_b1[0])
    pend = core_sums_of(swap_b1, res)
    (g_w_out_o,) = _mm_tn("mm_dw_out", sg, [dy_o], MXU_DTYPE)
    dhw, grads['odd_v_norm_g'], grads['odd_v_norm_b'], g_ws, g_bst = _sgu_bwd(
        hw, full['odd_v_norm_g'], full['odd_v_norm_b'], w['odd_w_s'][0], b_st, dsg)
    grads['odd_w_s'] = g_ws[None]
    grads['odd_b_s'] = g_bst[:, :GROUPS].T[None]
    tn_o = _tile(d, 1024, LANES)
    order = scatter(pend[0:1])
    (dh_o,), res = _matmul("mm_dh_odd", "nt", (t // tm, d // tn_o, N_CHIPS),
                           [[(dhw, _spec((tm, ns), lambda i, j, k: (i, k)), w_in_o, _spec((None, None, tn_o, ns), lambda i, j, k: (k, 0, j, 0)))]],
                           [((t, d), F32, _spec((tm, tn_o), lambda i, j, k: (i, j)))], (tm, tn_o), courier=order[0])
    arrived(order, res)
    tmw = _tile(d, 512, LANES)
    tkt = _tile(t, TN_ROWS, 16)
    order = scatter(pend[1:2])
    (g_w_in_o,), res = _matmul("mm_dw_in_odd", "tn", (d // tmw, N_CHIPS, t // tkt),
                               [[(h_o, _spec((tkt, tmw), lambda i, j, k: (k, i)), dhw, _spec((tkt, ns), lambda i, j, k: (k, j)))]],
                               [((N_CHIPS, d, ns), MXU_DTYPE, _spec((None, tmw, ns), lambda i, j, k: (j, i, 0)))], (tmw, ns),
                               courier=order[0])
    arrived(order, res)
    swap_odd = swap([(('odd_w_in', 0), g_w_in_o),
                     (('odd_w_out', 0), g_w_out_o.reshape((N_CHIPS,) + w['odd_w_out'].shape[1:]))])
    dx, grads['odd_pre_g'] = _rms_bwd(x4, full['odd_pre_g'], dh_o, 1.0, F32, tr, resid=dx)

    def scatter_after_dact(name, first, swapped):
        def make(known):
            res = known["carried"]["dact"]
            arrived(first, res[:1])
            made[name] = scatter(core_sums_of(swapped, res[1:]))
            return made[name][0]
        return make

    first = scatter(pend[2:3])
    dx, fg[('a', 1)], carried = _ffn_bwd(dx, ffn_a1, *ffn_args('a', 1), couriers={
        'dact': _Both(first[0], swap_odd[0]), 'dwdown': scatter_after_dact('odd', first, swap_odd)})
    arrived(made['odd'], carried['dwdown'])
    swap_a1 = swap(ffn_grads('a', 1))

    def a1_scatter(idx):
        def make(known):
            if 'a1' not in made:
                made['a1'] = core_sums_of(swap_a1, known["carried"]["dact"])
            made[('a1', idx)] = scatter(made['a1'][idx:idx + 1])
            return made[('a1', idx)][0]
        return make

    dx, fg[('b', 0)], carried = _ffn_bwd(dx, ffn_b0, *ffn_args('b', 0), couriers={
        'dact': swap_a1[0], 'dwdown': a1_scatter(0), 'dh': a1_scatter(1), 'dwup': a1_scatter(2)})
    for idx, name in enumerate(('dwdown', 'dh', 'dwup')):
        arrived(made[('a1', idx)], carried[name])
    swap_b0 = swap(ffn_grads('b', 0))
    dy_e, grads['even_post_g'] = _rms_bwd(y_e, w['even_post_g'], dx, 1.0, MXU_DTYPE, tr)
    d_cat, res = _mm_nt("mm_dcat", [(dy_e, w_out_e, 0)], F32, tm, courier=swap_b0[0])
    pend = core_sums_of(swap_b0, res)
    (g_w_out_e,) = _mm_tn("mm_dw_out", cat, [dy_e], MXU_DTYPE)
    dqn, dqp, dkv, dkr = _attn_bwd(q_all, kv, p, cos, sin, d_cat, batch, seq)
    hq = HEADS * LANES
    d_latq = _mm_nt("mm_dlat_q", [(dqn, w_uq, 0), (dqp, w_uq, hq)], F32, tm, tn_want=Q_LORA)
    d_latkv = _mm_nt("mm_dlat_kv", [(dkv, w_ukv, 0)], F32, tm, tn_want=KV_LORA)
    g_uq_n, g_uq_r = _mm_tn("mm_dw_uq", lat, [dqn, dqp], MXU_DTYPE, a_col0=0, m_width=Q_LORA)
    (g_ukv,) = _mm_tn("mm_dw_ukv", lat, [dkv], MXU_DTYPE, a_col0=Q_LORA, m_width=KV_LORA)

    def latent_bwd(r, c):
        _, vjp_q = jax.vjp(_rmsnorm, r[0], c[0])
        _, vjp_kv = jax.vjp(_rmsnorm, r[1], c[1])
        dq, dqg = vjp_q(r[2])
        dk, dkg = vjp_kv(r[3])
        return [jnp.concatenate([dq, dk], axis=1)], [dqg, dkg]

    d_lat, grads['even_q_norm_g'], grads['even_kv_norm_g'] = _rows(
        "latent_norm_bwd", latent_bwd, [(p, Q_LORA, 0), (p, KV_LORA, 1), (d_latq, Q_LORA, 0), (d_latkv, KV_LORA, 0)],
        [w['even_q_norm_g'], w['even_kv_norm_g']], [(Q_LORA + KV_LORA, MXU_DTYPE)], [(1, Q_LORA), (1, KV_LORA)], tr)
    da, dgate, g_conv_w, grads['even_conv_b'], grads['even_conv_norm_g'], grads['even_conv_norm_b'] = _conv_bwd(
        p, d_cat, full['even_conv_w'][0], w['even_conv_b'], w['even_conv_norm_g'], w['even_conv_norm_b'], batch, seq)
    dp = jnp.concatenate([d_lat, da, dgate, dkr.astype(MXU_DTYPE)], axis=1)
    order = scatter(pend[0:1])
    dh_e, res = _mm_nt("mm_dh_even", [(dp, w_in, 0)], F32, tm, tk_want=640, courier=order[0])
    arrived(order, res)
    order = scatter(pend[1:2])
    (g_w_in,), res = _mm_tn("mm_dw_in", h_e, [dp], MXU_DTYPE, tn_want=640, courier=order[0])
    arrived(order, res)
    swap_even = swap([(('even_w_in', 0), _cols_to_chips(_w_in_layout_inv(g_w_in))),
                      (('even_w_uq', 0), _cols_to_chips(_w_uq_layout_inv(g_uq_n, g_uq_r))),
                      (('even_w_ukv', 0), _cols_to_chips(g_ukv)),
                      (('even_w_out', 0), g_w_out_e.reshape((N_CHIPS,) + w['even_w_out'].shape[1:]))])
    dx, grads['even_pre_g'] = _rms_bwd(x1, w['even_pre_g'], dh_e, 1.0, F32, tr, resid=dx)
    first = scatter(pend[2:3])

    def after_w_down(known):
        made['a0_down'] = scatter(core_sums([(ffn_keys('a', 0)[2], known["dwd"])]))
        return made['a0_down'][0]

    def after_w_up(known):
        made['a0_up'] = scatter(core_sums(list(zip(ffn_keys('a', 0)[:2], (known["dwg"], known["dwu"])))))
        return made['a0_up'][0]

    dx, fg[('a', 0)], carried = _ffn_bwd(dx, ffn_a0, *ffn_args('a', 0), weights_first=True, couriers={
        'dact': _Both(first[0], swap_even[0]), 'dwdown': scatter_after_dact('even', first, swap_even),
        'dwup': after_w_down, 'dh': after_w_up})
    arrived(made['even'], carried['dwdown'])
    arrived(made['a0_down'], carried['dwup'])
    arrived(made['a0_up'], carried['dh'])
    grad_x = dx.reshape(batch, seq, d)

    reduced = []
    for n in BIG:
        layers = w[n].shape[0]
        buf = None
        for layer in range(layers):
            pt, r = received[(n, layer)]
            buf = _add_chips(pt, r, place, layer, layers, into=buf)
        reduced.append(buf)
    joined = _join_halves(reduced)
    for n, g in zip(BIG, joined):
        grads[n] = g.reshape(w[n].shape)

    for tag in ('a', 'b'):
        grads[f'ffn_{tag}_pre_g'] = jnp.concatenate([fg[(tag, 0)][0], fg[(tag, 1)][0]], axis=0)
        grads[f'ffn_{tag}_post_g'] = jnp.concatenate([fg[(tag, 0)][1], fg[(tag, 1)][1]], axis=0)
    grads['even_conv_w'] = g_conv_w[None]
    packed, rngs = _pack([grads[n] for n in SMALL])
    device = 2 * chip + lax.axis_index("c")
    slots = lax.dynamic_update_slice_in_dim(jnp.zeros((N_DEV,) + packed.shape, F32), packed[None], device, axis=0)
    summed = _add_devices(_gather_devices(slots))
    for n, rng in zip(SMALL, rngs):
        g = _unpack(summed, rng, grads[n].shape)
        if n in SMALL_SHARDED:
            width = w[n].shape[-1]
            g = lax.dynamic_slice_in_dim(g, chip * width, width, axis=g.ndim - 1)
        grads[n] = g

    delta, new_m, new_v = {}, {}, {}
    for n in BIG:
        delta[n], new_m[n], new_v[n] = _adamw(w[n], grads[n], given['m_' + n], given['v_' + n])
    packs = [_pack([src[n] for n in SMALL])[0][None] for src in
             (w, grads, {n: given['m_' + n] for n in SMALL}, {n: given['v_' + n] for n in SMALL})]
    _, rngs = _pack([w[n] for n in SMALL])
    small_out = _adamw(*packs)
    for n, rng in zip(SMALL, rngs):
        delta[n], new_m[n], new_v[n] = (_unpack(o[0], rng, w[n].shape) for o in small_out)

    return (loss, grad_x, *[grads[n] for n in WEIGHTS], *[delta[n] for n in WEIGHTS],
            *[new_m[n] for n in WEIGHTS], *[new_v[n] for n in WEIGHTS])
```

```python
import functools

import jax
import jax.numpy as jnp
from jax import lax
from jax.experimental import pallas as pl
from jax.experimental.pallas import tpu as pltpu

F32 = jnp.float32
BF16 = jnp.bfloat16
MXU_DTYPE = BF16
MESH = pl.DeviceIdType.MESH
VMEM_LIMIT_BYTES = 56 * 1024 * 1024
LANES = 128
TN_ROWS = 2048
STREAM_TILE_ELEMS = 512 * 1408

D_MODEL = 2048
D_FF = 5632
EPS = 1e-6
HEADS = 8
V_HEAD = 128
QK_NOPE = 128
QK_ROPE = 64
Q_LORA = 512
KV_LORA = 512
ROPE_THETA = 10000.0
CONV_CH = 1024
CONV_WIDTH = 31
GROUPS = 8
CHUNK = 128
GM_WIDTH = 2048
ADAM_LR = 0.001
ADAM_B1 = 0.9
ADAM_B2 = 0.999
ADAM_EPS = 1e-08
ADAM_WD = 0.01
ADAM_STEP = 10

N_CHIPS = 4
N_DEV = 8
P_KV = Q_LORA
P_A = Q_LORA + KV_LORA
P_GATE = P_A + CONV_CH
P_KR = P_GATE + CONV_CH
P_WIDTH = P_KR + LANES
ROPE_HALF = QK_ROPE // 2

WEIGHTS = ['ffn_a_pre_g', 'ffn_a_post_g', 'ffn_a_w_gate', 'ffn_a_w_up', 'ffn_a_w_down', 'ffn_b_pre_g', 'ffn_b_post_g',
           'ffn_b_w_gate', 'ffn_b_w_up', 'ffn_b_w_down', 'even_pre_g', 'even_post_g', 'even_w_in', 'even_q_norm_g',
           'even_kv_norm_g', 'even_w_uq', 'even_w_ukv', 'even_conv_w', 'even_conv_b', 'even_conv_norm_g',
           'even_conv_norm_b', 'even_w_out', 'odd_pre_g', 'odd_post_g', 'odd_w_in', 'odd_v_norm_g', 'odd_v_norm_b',
           'odd_w_s', 'odd_b_s', 'odd_w_out']
BIG = ['ffn_a_w_gate', 'ffn_a_w_up', 'ffn_a_w_down', 'ffn_b_w_gate', 'ffn_b_w_up', 'ffn_b_w_down', 'even_w_in',
       'even_w_uq', 'even_w_ukv', 'even_w_out', 'odd_w_in', 'odd_w_out']
SMALL = [n for n in WEIGHTS if n not in BIG]
SMALL_SHARDED = ['even_conv_w', 'odd_pre_g', 'odd_post_g', 'odd_v_norm_g', 'odd_v_norm_b']


def _call(name, body, grid, in_specs, out_specs, out_shape, scratch=(), prefetch=0, aliases=None):
    params = pltpu.CompilerParams(dimension_semantics=("arbitrary",) * len(grid), vmem_limit_bytes=VMEM_LIMIT_BYTES)
    if prefetch:
        spec = pltpu.PrefetchScalarGridSpec(num_scalar_prefetch=prefetch, grid=grid, in_specs=in_specs,
                                            out_specs=out_specs, scratch_shapes=list(scratch))
        return pl.pallas_call(body, grid_spec=spec, out_shape=out_shape, compiler_params=params, name=name,
                              input_output_aliases=aliases or {})
    return pl.pallas_call(body, grid=grid, in_specs=in_specs, out_specs=out_specs, out_shape=out_shape,
                          scratch_shapes=list(scratch), compiler_params=params, name=name,
                          input_output_aliases=aliases or {})


def _spec(block, index_map):
    return pl.BlockSpec(block, index_map)


def _sds(shape, dtype):
    return jax.ShapeDtypeStruct(tuple(shape), dtype)


def _tile(n, want, mult=8):
    if n <= want:
        return n
    best = None
    for t in range(mult, want + 1, mult):
        if n % t == 0:
            best = t
    assert best is not None, (n, want, mult)
    return best


_DIMS = {"nn": (((1,), (0,)), ((), ())), "nt": (((1,), (1,)), ((), ())), "tn": (((0,), (0,)), ((), ()))}


def _dot(a, b, mode="nn"):
    return lax.dot_general(a.astype(MXU_DTYPE), b.astype(MXU_DTYPE), _DIMS[mode], preferred_element_type=F32)


def _matmul(name, mode, grid, groups, outs, acc_shape, extras=(), epilogue=None, courier=None, sum_outs=0):
    flat, specs = [], []
    for grp in groups:
        for a, a_spec, b, b_spec in grp:
            flat += [a, b]
            specs += [a_spec, b_spec]
    for e, e_spec in extras:
        flat.append(e)
        specs.append(e_spec)
    n_pairs = [len(g) for g in groups]
    one_step = grid[2] == 1
    n_in, n_ex, n_out, n_acc = 2 * sum(n_pairs), len(extras), len(outs), 0 if one_step else len(groups)
    last = tuple(g - 1 for g in grid)
    c_arrays = list(courier.arrays) if courier else []
    c_shapes = list(courier.out_shapes) if courier else []
    n_ci, n_co = len(c_arrays), len(c_shapes)
    any_spec = pl.BlockSpec(memory_space=pl.ANY)

    def body(*refs):
        ins, ex = refs[:n_in], refs[n_in:n_in + n_ex]
        pos = n_in + n_ex
        c_in = refs[pos:pos + n_ci]
        out_refs = refs[pos + n_ci:pos + n_ci + n_out]
        c_out = refs[pos + n_ci + n_out:pos + n_ci + n_out + n_co]
        accs = refs[pos + n_ci + n_out + n_co:pos + n_ci + n_out + n_co + n_acc]
        sems = refs[pos + n_ci + n_out + n_co + n_acc:]
        i, j, k = pl.program_id(0), pl.program_id(1), pl.program_id(2)

        if courier:
            @pl.when((i == 0) & (j == 0) & (k == 0))
            def _():
                courier.start(c_in, c_out, sems)

        def products():
            pos, totals = 0, []
            for n in n_pairs:
                total = None
                for _ in range(n):
                    d = _dot(ins[pos][...], ins[pos + 1][...], mode)
                    total = d if total is None else total + d
                    pos += 2
                totals.append(total)
            return totals

        summed = out_refs[n_out - sum_outs:] if sum_outs else ()
        if summed:
            @pl.when((i == 0) & (j == 0) & (k == 0))
            def _():
                for o in summed:
                    o[...] = jnp.zeros(o.shape, F32)

        def finish(vals):
            res = epilogue(vals, [e[...] for e in ex]) if epilogue else vals
            for idx, (o, r) in enumerate(zip(out_refs, res)):
                if idx >= n_out - sum_outs:
                    o[...] += r
                else:
                    o[...] = r.astype(o.dtype)

        if one_step:
            finish(products())
        else:
            @pl.when(k == 0)
            def _():
                for acc in accs:
                    acc[...] = jnp.zeros(acc.shape, F32)

            for acc, total in zip(accs, products()):
                acc[...] += total

            @pl.when(k == last[2])
            def _():
                finish([acc[...] for acc in accs])

        if courier:
            @pl.when((i == last[0]) & (j == last[1]) & (k == last[2]))
            def _():
                courier.finish(c_in, c_out, sems)

    aliases = {n_in + n_ex + t: n_out + t for t in range(n_ci)} if courier and courier.aliased else None
    scratch = [pltpu.VMEM(acc_shape, F32)] * n_acc
    scratch += [pltpu.SemaphoreType.DMA((n,)) for n in courier.sems] if courier else []
    res = _call(name, body, grid, specs + [any_spec] * n_ci, [o[2] for o in outs] + [any_spec] * n_co,
                [_sds(o[0], o[1]) for o in outs] + c_shapes, scratch=scratch, aliases=aliases)(*flat, *c_arrays)
    return list(res[:n_out]), list(res[n_out:])


def _rows(name, fn, rows, consts, outs, accs, tm):
    t_rows = rows[0][0].shape[0]
    grid = (t_rows // tm,)
    in_specs = [_spec((tm, bw), functools.partial(lambda i, cb: (i, cb), cb=cb)) for _, bw, cb in rows]
    in_specs += [_spec(c.shape, functools.partial(lambda i, nd: (0,) * nd, nd=c.ndim)) for c in consts]
    out_shape = [_sds((t_rows, w), dt) for w, dt in outs] + [_sds(s, F32) for s in accs]
    out_specs = [_spec((tm, w), lambda i: (i, 0)) for w, _ in outs]
    out_specs += [_spec(s, functools.partial(lambda i, nd: (0,) * nd, nd=len(s))) for s in accs]
    nr, nc, no = len(rows), len(consts), len(outs)

    def body(*refs):
        r = [ref[...] for ref in refs[:nr]]
        c = [ref[...] for ref in refs[nr:nr + nc]]
        o_refs, a_refs = refs[nr + nc:nr + nc + no], refs[nr + nc + no:]
        o_vals, a_vals = fn(r, c)
        for ref, v in zip(o_refs, o_vals):
            ref[...] = v.astype(ref.dtype)
        if a_refs:
            @pl.when(pl.program_id(0) == 0)
            def _():
                for ref in a_refs:
                    ref[...] = jnp.zeros(ref.shape, F32)

            for ref, v in zip(a_refs, a_vals):
                ref[...] += v

    return _call(name, body, grid, in_specs, out_specs, out_shape)(*[a for a, _, _ in rows], *consts)


def _rmsnorm(x, g):
    return x * lax.rsqrt(jnp.mean(x * x, axis=-1, keepdims=True) + EPS) * g


def _layernorm(x, g, b):
    mu = jnp.mean(x, axis=-1, keepdims=True)
    var = jnp.mean(jnp.square(x - mu), axis=-1, keepdims=True)
    return (x - mu) * lax.rsqrt(var + EPS) * g + b


def _swiglu_act(g, u):
    return jax.nn.silu(g) * u


def _rope(x, cos, sin_signed):
    return x * cos + pltpu.roll(x, 2 * ROPE_HALF, 1) * sin_signed


def _rope_transposed(dy, cos, sin_signed):
    return dy * cos - pltpu.roll(dy, 2 * ROPE_HALF, 1) * sin_signed


def _rms_fwd(x, g, tm):
    d = x.shape[1]
    return _rows("rms_fwd", lambda r, c: ([_rmsnorm(r[0], c[0])], []), [(x, d, 0)], [g], [(d, MXU_DTYPE)], [], tm)[0]


def _rms_residual(x, y, g, scale, tm):
    d = x.shape[1]
    return _rows("rms_residual", lambda r, c: ([r[0] + scale * _rmsnorm(r[1], c[0])], []),
                 [(x, d, 0), (y, d, 0)], [g], [(d, F32)], [], tm)[0]


def _rms_bwd(y, g, dout, scale, out_dtype, tm, resid=None):
    d = y.shape[1]

    def fn(r, c):
        _, vjp = jax.vjp(_rmsnorm, r[0], c[0])
        dy, dg = vjp(scale * r[1].astype(F32))
        if resid is not None:
            dy = dy + r[2]
        return [dy], [dg]

    rows = [(y, d, 0), (dout, d, 0)] + ([(resid, d, 0)] if resid is not None else [])
    return _rows("rms_bwd" if resid is None else "rms_bwd_resid", fn, rows, [g], [(d, out_dtype)], [(1, d)], tm)


def _loss_head(y, target, tm):
    d = y.shape[1]

    def fn(r, c):
        err = r[0] - r[1]
        sq = jnp.sum(jnp.sum(err * err, axis=1, keepdims=True), axis=0, keepdims=True)
        return [err * (1.0 / d)], [jnp.broadcast_to(sq, (1, LANES))]

    return _rows("loss_head", fn, [(y, d, 0), (target, d, 0)], [], [(d, F32)], [(1, LANES)], tm)


def _ffn_fwd(x, pre_g, post_g, wg, wu, wd, layer, tm, tr, couriers=None, wd_landed=None, h=None, next_g=None):
    t, d = x.shape
    fs = wg.shape[3]
    if h is None:
        h = _rms_fwd(x, pre_g, tr)
    h_spec = _spec((tm, d), lambda j, i, k: (i, 0))
    w_spec = _spec((None, None, d, fs), lambda j, i, k: (j, layer, 0, 0))
    o_spec = _spec((tm, fs), lambda j, i, k: (i, j))
    couriers = couriers or {}
    carried = {}
    (g, u, a), carried["up"] = _matmul(
        "ffn_up", "nn", (N_CHIPS, t // tm, 1), [[(h, h_spec, wg, w_spec)], [(h, h_spec, wu, w_spec)]],
        [((t, N_CHIPS * fs), MXU_DTYPE, o_spec)] * 3, (tm, fs),
        epilogue=lambda accs, _: (accs[0], accs[1], _swiglu_act(accs[0], accs[1])), courier=couriers.get("up"))
    if wd_landed is not None:
        wd = carried["up"][wd_landed].reshape(wd.shape)
    row_spec = _spec((tm, d), lambda i, j, k: (i, 0))
    vec_spec = _spec((1, d), lambda i, j, k: (0, 0))

    def post(accs, ex):
        out = ex[0] + 0.5 * _rmsnorm(accs[0], ex[1])
        return (accs[0], out) + ((_rmsnorm(out, ex[2]),) if next_g is not None else ())

    res, carried["down"] = _matmul(
        "ffn_down", "nn", (t // tm, 1, N_CHIPS),
        [[(a, _spec((tm, fs), lambda i, j, k: (i, k)), wd, _spec((None, None, fs, d), lambda i, j, k: (k, layer, 0, 0)))]],
        [((t, d), F32, row_spec)] * 2 + ([((t, d), MXU_DTYPE, row_spec)] if next_g is not None else []), (tm, d),
        extras=[(x, row_spec), (post_g, vec_spec)] + ([(next_g, vec_spec)] if next_g is not None else []),
        epilogue=post, courier=couriers.get("down"))
    y, out = res[0], res[1]
    return out, (x, h, g, u, a, y), carried, (res[2] if next_g is not None else None)


def _ffn_bwd(dout, saved, pre_g, post_g, wg, wu, wd, layer, tm, tr, couriers=None, weights_first=False):
    x, h, g, u, a, y = saved
    t, d = x.shape
    fs = wg.shape[3]
    tn = _tile(d, 1024, LANES)
    tkt = _tile(t, TN_ROWS, 16)
    dy, dpost = _rms_bwd(y, post_g, dout, 0.5, MXU_DTYPE, tr)

    def act_bwd(accs, ex):
        _, vjp = jax.vjp(_swiglu_act, ex[0].astype(F32), ex[1].astype(F32))
        return vjp(accs[0])

    couriers = couriers or {}
    carried = {}
    known = {"carried": carried}

    def late(name):
        c = couriers.get(name)
        return c(known) if callable(c) else c

    gu_spec = _spec((tm, fs), lambda j, i, k: (i, j))
    (dg, du), carried["dact"] = _matmul(
        "ffn_dact", "nt", (N_CHIPS, t // tm, 1),
        [[(dy, _spec((tm, d), lambda j, i, k: (i, 0)), wd, _spec((None, None, fs, d), lambda j, i, k: (j, layer, 0, 0)))]],
        [((t, N_CHIPS * fs), MXU_DTYPE, gu_spec)] * 2, (tm, fs),
        extras=[(g, gu_spec), (u, gu_spec)], epilogue=act_bwd, courier=couriers.get("dact"))
    (dwd,), carried["dwdown"] = _matmul(
        "ffn_dwdown", "tn", (N_CHIPS, d // tn, t // tkt),
        [[(a, _spec((tkt, fs), lambda i, j, k: (k, i)), dy, _spec((tkt, tn), lambda i, j, k: (k, j)))]],
        [((N_CHIPS, fs, d), MXU_DTYPE, _spec((None, fs, tn), lambda i, j, k: (i, 0, j)))], (fs, tn),
        courier=late("dwdown"))

    def d_hidden(courier):
        da_spec = _spec((tm, fs), lambda i, j, k: (i, k))
        wt_spec = _spec((None, None, d, fs), lambda i, j, k: (k, layer, 0, 0))
        return _matmul("ffn_dh", "nt", (t // tm, 1, N_CHIPS), [[(dg, da_spec, wg, wt_spec), (du, da_spec, wu, wt_spec)]],
                       [((t, d), F32, _spec((tm, d), lambda i, j, k: (i, 0)))], (tm, d), courier=courier)

    def d_w_up(courier):
        tmw = _tile(d, 512, LANES)
        h_spec = _spec((tkt, tmw), lambda i, j, k: (k, i))
        dgu_spec = _spec((tkt, fs), lambda i, j, k: (k, j))
        dw_spec = _spec((None, tmw, fs), lambda i, j, k: (j, i, 0))
        return _matmul("ffn_dwup", "tn", (d // tmw, N_CHIPS, t // tkt), [[(h, h_spec, dg, dgu_spec)], [(h, h_spec, du, dgu_spec)]],
                       [((N_CHIPS, d, fs), MXU_DTYPE, dw_spec)] * 2, (tmw, fs), courier=courier)

    known["dwd"] = dwd
    if weights_first:
        (dwg, dwu), carried["dwup"] = d_w_up(late("dwup"))
        known["dwg"], known["dwu"] = dwg, dwu
        (dh,), carried["dh"] = d_hidden(late("dh"))
    else:
        (dh,), carried["dh"] = d_hidden(late("dh"))
        (dwg, dwu), carried["dwup"] = d_w_up(late("dwup"))
    dx, dpre = _rms_bwd(x, pre_g, dh, 1.0, F32, tr, resid=dout)
    return dx, (dpre, dpost, dwg, dwu, dwd), carried


def _attn_scores(qn, qpe, kn, kpe, qi, tq, seq):
    s = (_dot(qn, kn, "nt") + _dot(qpe, kpe, "nt")) * ((QK_NOPE + QK_ROPE) ** -0.5)
    rows = qi * tq + lax.broadcasted_iota(jnp.int32, (tq, seq), 0)
    cols = lax.broadcasted_iota(jnp.int32, (tq, seq), 1)
    s = jnp.where(cols <= rows, s, -jnp.inf)
    e = jnp.exp(s - jnp.max(s, axis=1, keepdims=True))
    return e / jnp.sum(e, axis=1, keepdims=True)


KEY_EXTENTS = 4


def _for_key_extent(qi, tq, seq, attend):
    n = KEY_EXTENTS if seq % (KEY_EXTENTS * tq) == 0 else 1
    ext = seq // n
    mine = (qi * tq) // ext
    for e in range(n):
        pl.when(mine == e)(functools.partial(attend, (e + 1) * ext))


def _attn_specs(nq, tq, seq):
    q_rows = lambda b, h, qi: b * nq + qi
    return [
        _spec((tq, LANES), lambda b, h, qi: (q_rows(b, h, qi), h)),
        _spec((tq, LANES), lambda b, h, qi: (q_rows(b, h, qi), HEADS + h)),
        _spec((seq, 2 * LANES), lambda b, h, qi: (b, h)),
        _spec((seq, LANES), lambda b, h, qi: (b, P_KR // LANES)),
        _spec((tq, LANES), lambda b, h, qi: (q_rows(b, h, qi), 0)),
        _spec((tq, LANES), lambda b, h, qi: (q_rows(b, h, qi), 0)),
        _spec((seq, LANES), lambda b, h, qi: (b, 0)),
        _spec((seq, LANES), lambda b, h, qi: (b, 0)),
    ]


def _attn_fwd(q_all, kv, p, cos, sin, batch, seq):
    t = q_all.shape[0]
    tq = _tile(seq, 256, 16)
    nq = seq // tq

    def body(qn_ref, qp_ref, kv_ref, kr_ref, cq, sq, ck, sk, o_ref):
        qi = pl.program_id(2)
        qpe = _rope(qp_ref[...], cq[...], sq[...])

        def attend(keys):
            kpe = _rope(kr_ref[:keys, :], ck[:keys, :], sk[:keys, :])
            prob = _attn_scores(qn_ref[...], qpe, kv_ref[:keys, :LANES], kpe, qi, tq, keys)
            o_ref[...] = _dot(prob, kv_ref[:keys, LANES:]).astype(o_ref.dtype)

        _for_key_extent(qi, tq, seq, attend)

    return _call("attn_fwd", body, (batch, HEADS, nq), _attn_specs(nq, tq, seq),
                 _spec((tq, LANES), lambda b, h, qi: (b * nq + qi, h)), _sds((t, 2 * HEADS * V_HEAD), MXU_DTYPE)
                 )(q_all, q_all, kv, p, cos, sin, cos, sin)


def _attn_bwd(q_all, kv, p, cos, sin, d_cat, batch, seq):
    t = q_all.shape[0]
    tq = _tile(seq, 256, 16)
    nq = seq // tq

    def body(qn_ref, qp_ref, kv_ref, kr_ref, cq, sq, ck, sk, do_ref, dqn_ref, dqp_ref, dkv_ref, dkr_ref, dkpe_acc):
        h, qi = pl.program_id(1), pl.program_id(2)
        qn = qn_ref[...]
        qpe = _rope(qp_ref[...], cq[...], sq[...])
        do = do_ref[...]

        @pl.when(qi == 0)
        def _():
            dkv_ref[...] = jnp.zeros(dkv_ref.shape, F32)

        @pl.when((h == 0) & (qi == 0))
        def _():
            dkpe_acc[...] = jnp.zeros(dkpe_acc.shape, F32)

        def attend(keys):
            kn, v = kv_ref[:keys, :LANES], kv_ref[:keys, LANES:]
            kpe = _rope(kr_ref[:keys, :], ck[:keys, :], sk[:keys, :])
            prob = _attn_scores(qn, qpe, kn, kpe, qi, tq, keys)
            dprob = _dot(do, v, "nt")
            ds = prob * (dprob - jnp.sum(prob * dprob, axis=1, keepdims=True)) * ((QK_NOPE + QK_ROPE) ** -0.5)
            dqn_ref[...] = _dot(ds, kn)
            dqp_ref[...] = _rope_transposed(_dot(ds, kpe), cq[...], sq[...])
            dkv_ref[:keys, :LANES] += _dot(ds, qn, "tn")
            dkv_ref[:keys, LANES:] += _dot(prob, do, "tn")
            dkpe_acc[:keys, :] += _dot(ds, qpe, "tn")

        _for_key_extent(qi, tq, seq, attend)

        @pl.when((h == HEADS - 1) & (qi == nq - 1))
        def _():
            dkr_ref[...] = _rope_transposed(dkpe_acc[...], ck[...], sk[...])

    q_out = _spec((tq, LANES), lambda b, h, qi: (b * nq + qi, h))
    return _call("attn_bwd", body, (batch, HEADS, nq),
                 _attn_specs(nq, tq, seq) + [_spec((tq, LANES), lambda b, h, qi: (b * nq + qi, h))],
                 [q_out, q_out, _spec((seq, 2 * LANES), lambda b, h, qi: (b, h)), _spec((seq, LANES), lambda b, h, qi: (b, 0))],
                 [_sds((t, HEADS * LANES), F32), _sds((t, HEADS * LANES), F32), _sds((t, HEADS * 2 * LANES), F32), _sds((t, LANES), F32)],
                 scratch=[pltpu.VMEM((seq, LANES), F32)])(q_all, q_all, kv, p, cos, sin, cos, sin, d_cat)


CONV_PAD = 32


def _conv_taps(w_ref, src_ref, first_row, n_rows, init, offset):
    acc = init
    for k in range(CONV_WIDTH):
        acc = acc + w_ref[k:k + 1, :] * src_ref[pl.ds(first_row + offset(k), n_rows), :]
    return acc


def _norm_act(conv, g, b):
    return jax.nn.silu(_layernorm(conv, g, b))


def _conv_fwd(p, cat, conv_w, conv_b, norm_g, norm_b, batch, seq):
    rc = _tile(seq, 256, 8)
    a_blk, gate_blk, out_blk = P_A // LANES, P_GATE // LANES, (HEADS * V_HEAD) // LANES

    def body(a_ref, gate_ref, w_ref, cb_ref, ng_ref, nb_ref, cat_in, o_ref, zp):
        del cat_in
        zp[pl.ds(0, CONV_PAD), :] = jnp.zeros((CONV_PAD, LANES), F32)
        zp[pl.ds(CONV_PAD, seq), :] = a_ref[...] * jax.nn.sigmoid(gate_ref[...])
        for r0 in range(0, seq, rc):
            conv = _conv_taps(w_ref, zp, r0, rc, jnp.broadcast_to(cb_ref[...], (rc, LANES)),
                              lambda k: CONV_PAD - (CONV_WIDTH - 1) + k)
            o_ref[pl.ds(r0, rc), :] = _norm_act(conv, ng_ref[...], nb_ref[...]).astype(o_ref.dtype)

    vec = _spec((1, LANES), lambda b, g: (0, g))
    return _call("conv_fwd", body, (batch, GROUPS),
                 [_spec((seq, LANES), lambda b, g: (b, a_blk + g)), _spec((seq, LANES), lambda b, g: (b, gate_blk + g)),
                  _spec((CONV_WIDTH, LANES), lambda b, g: (0, g)), vec, vec, vec, pl.BlockSpec(memory_space=pl.ANY)],
                 _spec((seq, LANES), lambda b, g: (b, out_blk + g)), _sds(cat.shape, cat.dtype),
                 scratch=[pltpu.VMEM((seq + CONV_PAD, LANES), F32)], aliases={6: 0}
                 )(p, p, conv_w, conv_b, norm_g, norm_b, cat)


def _conv_bwd(p, d_cat, conv_w, conv_b, norm_g, norm_b, batch, seq):
    t = p.shape[0]
    rc = _tile(seq, 256, 8)
    a_blk, gate_blk, out_blk = P_A // LANES, P_GATE // LANES, (HEADS * V_HEAD) // LANES

    def body(a_ref, gate_ref, w_ref, cb_ref, ng_ref, nb_ref, dc_ref, da_ref, dgate_ref, dw_ref, db_ref, dng_ref, dnb_ref,
             zp, dcp, buf):
        b = pl.program_id(1)
        a, sig = a_ref[...], jax.nn.sigmoid(gate_ref[...])
        zp[pl.ds(0, CONV_PAD), :] = jnp.zeros((CONV_PAD, LANES), F32)
        zp[pl.ds(CONV_PAD, seq), :] = a * sig
        for r0 in range(0, seq, rc):
            buf[pl.ds(r0, rc), :] = _conv_taps(w_ref, zp, r0, rc, jnp.broadcast_to(cb_ref[...], (rc, LANES)),
                                               lambda k: CONV_PAD - (CONV_WIDTH - 1) + k)
        _, vjp = jax.vjp(_norm_act, buf[...], ng_ref[...], nb_ref[...])
        dconv, dng, dnb = vjp(dc_ref[...].astype(F32))
        dcp[pl.ds(0, seq), :] = dconv
        dcp[pl.ds(seq, CONV_PAD), :] = jnp.zeros((CONV_PAD, LANES), F32)

        @pl.when(b == 0)
        def _():
            dw_ref[...] = jnp.zeros(dw_ref.shape, F32)
            db_ref[...] = jnp.zeros(db_ref.shape, F32)
            dng_ref[...] = jnp.zeros(dng_ref.shape, F32)
            dnb_ref[...] = jnp.zeros(dnb_ref.shape, F32)

        db_ref[...] += jnp.sum(dconv, axis=0, keepdims=True)
        dng_ref[...] += dng
        dnb_ref[...] += dnb
        for k in range(CONV_WIDTH):
            shifted = zp[pl.ds(CONV_PAD - (CONV_WIDTH - 1) + k, seq), :]
            dw_ref[k:k + 1, :] += jnp.sum(dconv * shifted, axis=0, keepdims=True)
        for r0 in range(0, seq, rc):
            buf[pl.ds(r0, rc), :] = _conv_taps(w_ref, dcp, r0, rc, jnp.zeros((rc, LANES), F32),
                                               lambda k: CONV_WIDTH - 1 - k)
        dz = buf[...]
        da_ref[...] = (dz * sig).astype(da_ref.dtype)
        dgate_ref[...] = (dz * a * sig * (1.0 - sig)).astype(dgate_ref.dtype)

    vec = _spec((1, LANES), lambda g, b: (0, g))
    row_out = _spec((seq, LANES), lambda g, b: (b, g))
    return _call("conv_bwd", body, (GROUPS, batch),
                 [_spec((seq, LANES), lambda g, b: (b, a_blk + g)), _spec((seq, LANES), lambda g, b: (b, gate_blk + g)),
                  _spec((CONV_WIDTH, LANES), lambda g, b: (0, g)), vec, vec, vec,
                  _spec((seq, LANES), lambda g, b: (b, out_blk + g))],
                 [row_out, row_out, _spec((CONV_WIDTH, LANES), lambda g, b: (0, g)), vec, vec, vec],
                 [_sds((t, CONV_CH), MXU_DTYPE), _sds((t, CONV_CH), MXU_DTYPE), _sds((CONV_WIDTH, CONV_CH), F32),
                  _sds((1, CONV_CH), F32), _sds((1, CONV_CH), F32), _sds((1, CONV_CH), F32)],
                 scratch=[pltpu.VMEM((seq + CONV_PAD, LANES), F32), pltpu.VMEM((seq + CONV_PAD, LANES), F32),
                          pltpu.VMEM((seq, LANES), F32)])(p, p, conv_w, conv_b, norm_g, norm_b, d_cat)


def _sgu_pre(hw, g, b):
    z = jax.nn.gelu(hw)
    return z[:, :GM_WIDTH], _layernorm(z[:, GM_WIDTH:], g, b)


def _causal(w):
    keep = lax.broadcasted_iota(jnp.int32, (CHUNK, CHUNK), 0) >= lax.broadcasted_iota(jnp.int32, (CHUNK, CHUNK), 1)
    return jnp.where(keep, w, 0.0)


def _sgu_fwd(hw, vg, vb, w_s, b_st):
    t = hw.shape[0]
    gw = GM_WIDTH // GROUPS

    def body(hw_ref, vg_ref, vb_ref, w_ref, b_ref, o_ref):
        u, v = _sgu_pre(hw_ref[...], vg_ref[...], vb_ref[...])
        for g in range(GROUPS):
            cols = slice(g * gw, (g + 1) * gw)
            s = _dot(_causal(w_ref[g]), v[:, cols]) + b_ref[:, g:g + 1]
            o_ref[:, cols] = (u[:, cols] * s).astype(o_ref.dtype)

    return _call("sgu_fwd", body, (t // CHUNK,),
                 [_spec((CHUNK, 2 * GM_WIDTH), lambda i: (i, 0)), _spec((1, GM_WIDTH), lambda i: (0, 0)),
                  _spec((1, GM_WIDTH), lambda i: (0, 0)), _spec((GROUPS, CHUNK, CHUNK), lambda i: (0, 0, 0)),
                  _spec((CHUNK, GROUPS), lambda i: (0, 0))],
                 _spec((CHUNK, GM_WIDTH), lambda i: (i, 0)), _sds((t, GM_WIDTH), MXU_DTYPE))(hw, vg, vb, w_s, b_st)


def _sgu_bwd(hw, vg, vb, w_s, b_st, dout):
    t = hw.shape[0]
    gw = GM_WIDTH // GROUPS

    def body(hw_ref, vg_ref, vb_ref, w_ref, b_ref, do_ref, dhw_ref, dvg_ref, dvb_ref, dw_ref, db_ref, du_buf, dv_buf):
        (u, v), vjp = jax.vjp(_sgu_pre, hw_ref[...], vg_ref[...], vb_ref[...])

        @pl.when(pl.program_id(0) == 0)
        def _():
            dvg_ref[...] = jnp.zeros(dvg_ref.shape, F32)
            dvb_ref[...] = jnp.zeros(dvb_ref.shape, F32)
            dw_ref[...] = jnp.zeros(dw_ref.shape, F32)
            db_ref[...] = jnp.zeros(db_ref.shape, F32)

        lane = lax.broadcasted_iota(jnp.int32, (CHUNK, LANES), 1)
        db = jnp.zeros((CHUNK, LANES), F32)
        for g in range(GROUPS):
            cols = slice(g * gw, (g + 1) * gw)
            w = _causal(w_ref[g])
            s = _dot(w, v[:, cols]) + b_ref[:, g:g + 1]
            do = do_ref[:, cols]
            ds = do * u[:, cols]
            du_buf[:, cols] = do * s
            dw_ref[g] += _causal(_dot(ds, v[:, cols], "nt"))
            dv_buf[:, cols] = _dot(w, ds, "tn")
            db = db + jnp.where(lane == g, jnp.sum(ds, axis=1, keepdims=True), 0.0)
        db_ref[...] += db
        dhw, dvg, dvb = vjp((du_buf[...], dv_buf[...]))
        dhw_ref[...] = dhw.astype(dhw_ref.dtype)
        dvg_ref[...] += dvg
        dvb_ref[...] += dvb

    vec = _spec((1, GM_WIDTH), lambda i: (0, 0))
    w_spec = _spec((GROUPS, CHUNK, CHUNK), lambda i: (0, 0, 0))
    return _call("sgu_bwd", body, (t // CHUNK,),
                 [_spec((CHUNK, 2 * GM_WIDTH), lambda i: (i, 0)), vec, vec, w_spec, _spec((CHUNK, GROUPS), lambda i: (0, 0)),
                  _spec((CHUNK, GM_WIDTH), lambda i: (i, 0))],
                 [_spec((CHUNK, 2 * GM_WIDTH), lambda i: (i, 0)), vec, vec, w_spec, _spec((CHUNK, LANES), lambda i: (0, 0))],
                 [_sds((t, 2 * GM_WIDTH), MXU_DTYPE), _sds((1, GM_WIDTH), F32), _sds((1, GM_WIDTH), F32),
                  _sds((GROUPS, CHUNK, CHUNK), F32), _sds((CHUNK, LANES), F32)],
                 scratch=[pltpu.VMEM((CHUNK, GM_WIDTH), F32), pltpu.VMEM((CHUNK, GM_WIDTH), F32)]
                 )(hw, vg, vb, w_s, b_st, dout)


def _mm_nn(name, a, b, out_dtype, tm, a_col0=0, tn_want=1024, tk_want=1024, courier=None):
    t = a.shape[0]
    kk, n = b.shape
    tk = _tile(kk, tk_want, LANES)
    tn = _tile(n, tn_want, LANES)
    k0 = a_col0 // tk
    assert a_col0 % tk == 0
    outs, carried = _matmul(
        name, "nn", (t // tm, n // tn, kk // tk),
        [[(a, _spec((tm, tk), lambda i, j, k: (i, k0 + k)), b, _spec((tk, tn), lambda i, j, k: (k, j)))]],
        [((t, n), out_dtype, _spec((tm, tn), lambda i, j, k: (i, j)))], (tm, tn), courier=courier)
    return (outs[0], carried) if courier else outs[0]


def _mm_nt(name, pairs, out_dtype, tm, tn_want=1024, tk_want=1024, courier=None):
    t, kk = pairs[0][0].shape
    n = pairs[0][1].shape[0]
    tk = _tile(kk, tk_want, LANES)
    tn = _tile(n, tn_want, 16)
    grp = []
    for a, b, b_col0 in pairs:
        assert b_col0 % tk == 0 and a.shape == (t, kk)
        grp.append((a, _spec((tm, tk), lambda i, j, k: (i, k)), b,
                    _spec((tn, tk), functools.partial(lambda i, j, k, k0: (j, k0 + k), k0=b_col0 // tk))))
    outs, carried = _matmul(name, "nt", (t // tm, n // tn, kk // tk), [grp],
                            [((t, n), out_dtype, _spec((tm, tn), lambda i, j, k: (i, j)))], (tm, tn), courier=courier)
    return (outs[0], carried) if courier else outs[0]


def _mm_tn(name, a, bs, out_dtype, a_col0=0, m_width=None, tm_want=512, tn_want=1024, courier=None):
    t = a.shape[0]
    m = m_width or a.shape[1]
    n = bs[0].shape[1]
    tmw = _tile(m, tm_want, LANES)
    tn = _tile(n, tn_want, LANES)
    tkt = _tile(t, TN_ROWS, 16)
    assert a_col0 % tmw == 0
    i0 = a_col0 // tmw
    a_spec = _spec((tkt, tmw), lambda i, j, k: (k, i0 + i))
    groups = [[(a, a_spec, b, _spec((tkt, tn), lambda i, j, k: (k, j)))] for b in bs]
    outs, carried = _matmul(name, "tn", (m // tmw, n // tn, t // tkt), groups,
                            [((m, n), out_dtype, _spec((tmw, tn), lambda i, j, k: (i, j)))] * len(bs), (tmw, tn),
                            courier=courier)
    return (outs, carried) if courier else outs


_ANY = pl.BlockSpec(memory_space=pl.ANY)


def _comm_call(name, body, ins, out_shapes, sems, aliases=None):
    return pl.pallas_call(body, name=name, out_shape=out_shapes, in_specs=[_ANY] * len(ins),
                          out_specs=[_ANY] * len(out_shapes), input_output_aliases=aliases or {},
                          scratch_shapes=[pltpu.SemaphoreType.DMA((n,)) for n in sems])(*ins)


def _place():
    return lax.axis_index("x"), lax.axis_index("y"), lax.axis_index("c")


def _other_chips(x, y):
    return [(1 - x, y), (x, 1 - y), (1 - x, 1 - y)]


def _cast_into_slot(w, place, dtype):
    layers, rows, cols = w.shape
    half = rows // 2
    tr = _tile(half, 2 * STREAM_TILE_ELEMS // cols, 16)
    nt = half // tr

    def body(s, w_ref, o_ref):
        del s
        o_ref[...] = w_ref[...].astype(o_ref.dtype)

    return _call("cast_into_slot", body, (layers, 2, nt),
                 [_spec((None, tr, cols), lambda l, h, i, s: (l, h * nt + i, 0))],
                 _spec((None, None, None, tr, cols), lambda l, h, i, s: (s[0], l, h, i, 0)),
                 _sds((N_CHIPS, layers, 2, half, cols), dtype), prefetch=1)(place, w)


class _GatherChips:
    aliased = True

    def __init__(self, bufs):
        self.arrays = list(bufs)
        self.out_shapes = [_sds(b.shape, b.dtype) for b in bufs]
        self.sems = [3 * len(bufs)] * 4

    def _sent(self, buf, sems):
        x, y, c = _place()
        me = 2 * x + y
        return [[pltpu.make_async_remote_copy(
            src_ref=buf[t].at[me, :, c], dst_ref=buf[t].at[me, :, c], send_sem=sems[0].at[3 * t + r],
            recv_sem=sems[1].at[3 * t + r], device_id=(px, py, c), device_id_type=MESH)
            for r, (px, py) in enumerate(_other_chips(x, y))] for t in range(len(buf))]

    def start(self, _, buf, sems):
        for row in self._sent(buf, sems):
            for cp in row:
                cp.start()

    def finish(self, _, buf, sems):
        x, y, c = _place()
        sent = self._sent(buf, sems)
        passed = []
        for t in range(len(buf)):
            for r, (px, py) in enumerate(_other_chips(x, y)):
                landed = buf[t].at[2 * px + py, :, c]
                sent[t][r].wait_recv()
                cp = pltpu.make_async_remote_copy(
                    src_ref=landed, dst_ref=landed, send_sem=sems[2].at[3 * t + r], recv_sem=sems[3].at[3 * t + r],
                    device_id=(x, y, 1 - c), device_id_type=MESH)
                cp.start()
                passed.append(cp)
        for cp in passed:
            cp.wait()
        for row in sent:
            for cp in row:
                cp.wait_send()


class _ScatterChips:
    aliased = False

    def __init__(self, partials):
        self.arrays = list(partials)
        self.out_shapes = [_sds((3,) + p.shape[1:], p.dtype) for p in partials]
        self.sems = [3 * len(partials)] * 2

    def _copies(self, src, dst, sems):
        x, y, c = _place()
        return [pltpu.make_async_remote_copy(
            src_ref=src[t].at[2 * px + py], dst_ref=dst[t].at[r], send_sem=sems[0].at[3 * t + r],
            recv_sem=sems[1].at[3 * t + r], device_id=(px, py, c), device_id_type=MESH)
            for t in range(len(src)) for r, (px, py) in enumerate(_other_chips(x, y))]

    def start(self, src, dst, sems):
        for cp in self._copies(src, dst, sems):
            cp.start()

    def finish(self, src, dst, sems):
        for cp in self._copies(src, dst, sems):
            cp.wait()


def _run_courier(name, courier):
    n_in, n_out = len(courier.arrays), len(courier.out_shapes)

    def body(*refs):
        src, dst, sems = refs[:n_in], refs[n_in:n_in + n_out], refs[n_in + n_out:]
        courier.start(src, dst, sems)
        courier.finish(src, dst, sems)

    return _comm_call(name, body, courier.arrays, courier.out_shapes, courier.sems,
                      aliases={t: t for t in range(n_in)} if courier.aliased else None)


class _SwapHalves:
    aliased = False

    def __init__(self, grads):
        self.arrays = list(grads)
        self.out_shapes = [_sds((g.shape[0],) + g.shape[2:], g.dtype) for g in grads]
        self.sems = [len(grads)] * 2

    def _copies(self, src, dst, sems):
        x, y, c = _place()
        return [pltpu.make_async_remote_copy(
            src_ref=src[t].at[:, 1 - c], dst_ref=dst[t], send_sem=sems[0].at[t], recv_sem=sems[1].at[t],
            device_id=(x, y, 1 - c), device_id_type=MESH) for t in range(len(src))]

    def start(self, src, dst, sems):
        for cp in self._copies(src, dst, sems):
            cp.start()

    def finish(self, src, dst, sems):
        for cp in self._copies(src, dst, sems):
            cp.wait()


class _Both:
    aliased = False

    def __init__(self, first, second):
        assert not first.aliased and not second.aliased
        self.parts = (first, second)
        self.arrays = first.arrays + second.arrays
        self.out_shapes = first.out_shapes + second.out_shapes
        self.sems = first.sems + second.sems

    def _split(self, src, dst, sems):
        a = self.parts[0]
        na, no, ns = len(a.arrays), len(a.out_shapes), len(a.sems)
        return (src[:na], dst[:no], sems[:ns]), (src[na:], dst[no:], sems[ns:])

    def start(self, src, dst, sems):
        for part, args in zip(self.parts, self._split(src, dst, sems)):
            part.start(*args)

    def finish(self, src, dst, sems):
        for part, args in zip(self.parts, self._split(src, dst, sems)):
            part.finish(*args)


def _join_halves(grads):
    n = len(grads)

    def body(*refs):
        buf = refs[n:2 * n]
        send_sems, recv_sems = refs[2 * n:]
        x, y, c = _place()
        copies = [pltpu.make_async_remote_copy(
            src_ref=buf[t].at[:, c], dst_ref=buf[t].at[:, c], send_sem=send_sems.at[t], recv_sem=recv_sems.at[t],
            device_id=(x, y, 1 - c), device_id_type=MESH) for t in range(n)]
        for cp in copies:
            cp.start()
        for cp in copies:
            cp.wait()

    return _comm_call("join_halves", body, grads, [_sds(g.shape, g.dtype) for g in grads], [n, n],
                      aliases={t: t for t in range(n)})


def _gather_devices(slots):
    def body(src, buf, send_sems, recv_sems):
        del src
        x, y, c = _place()
        me = 4 * x + 2 * y + c
        copies = []
        for r in range(1, N_DEV):
            fx, fy, fc = (r >> 2) & 1, (r >> 1) & 1, r & 1
            peer = (1 - x if fx else x, 1 - y if fy else y, 1 - c if fc else c)
            copies.append(pltpu.make_async_remote_copy(
                src_ref=buf.at[me], dst_ref=buf.at[me], send_sem=send_sems.at[r - 1], recv_sem=recv_sems.at[r - 1],
                device_id=peer, device_id_type=MESH))
        for cp in copies:
            cp.start()
        for cp in copies:
            cp.wait()

    return _comm_call("gather_devices", body, [slots], [_sds(slots.shape, slots.dtype)], [N_DEV - 1, N_DEV - 1],
                      aliases={0: 0})[0]


def _add_halves(grad, got, place):
    _, _, rows, cols = grad.shape
    tr = _tile(rows, 2 * STREAM_TILE_ELEMS // cols, 16)

    def body(s, a, b, o):
        del s
        o[...] = (a[...].astype(F32) + b[...].astype(F32)).astype(o.dtype)

    return _call("add_halves", body, (N_CHIPS, rows // tr),
                 [_spec((None, None, tr, cols), lambda j, i, s: (j, s[1], i, 0)), _spec((None, tr, cols), lambda j, i, s: (j, i, 0))],
                 _spec((None, tr, cols), lambda j, i, s: (j, i, 0)), _sds((N_CHIPS, rows, cols), grad.dtype), prefetch=1
                 )(place, grad, got)


def _add_chips(partial, got, place, layer, layers, into=None):
    _, rows, cols = partial.shape
    tr = _tile(rows, STREAM_TILE_ELEMS // cols, 16)

    def body(s, a, b, *rest):
        o = rest[-1]
        o[...] = a[...].astype(F32) + b[0].astype(F32) + b[1].astype(F32) + b[2].astype(F32)

    in_specs = [_spec((None, tr, cols), lambda i, s: (s[0], i, 0)), _spec((3, tr, cols), lambda i, s: (0, i, 0))]
    args = [place, partial, got]
    if into is not None:
        in_specs.append(pl.BlockSpec(memory_space=pl.ANY))
        args.append(into)
    return _call("add_chips", body, (rows // tr,), in_specs,
                 _spec((None, None, tr, cols), lambda i, s: (layer, s[1], i, 0)), _sds((layers, 2, rows, cols), F32),
                 prefetch=1, aliases={3: 0} if into is not None else None)(*args)


def _add_devices(got):
    _, rows, cols = got.shape
    tr = _tile(rows, 512, 8)

    def body(a, o):
        total = a[0]
        for d in range(1, N_DEV):
            total = total + a[d]
        o[...] = total

    return _call("add_devices", body, (rows // tr,), [_spec((N_DEV, tr, cols), lambda i: (0, i, 0))],
                 _spec((tr, cols), lambda i: (i, 0)), _sds((rows, cols), F32))(got)


def _adamw(w, g, m, v):
    layers, rows, cols = w.shape
    tr = _tile(rows, STREAM_TILE_ELEMS // cols, 8)

    def body(w_ref, g_ref, m_ref, v_ref, d_ref, nm_ref, nv_ref):
        grad = g_ref[...]
        new_m = ADAM_B1 * m_ref[...] + (1.0 - ADAM_B1) * grad
        new_v = ADAM_B2 * v_ref[...] + (1.0 - ADAM_B2) * jnp.square(grad)
        m_hat = new_m / (1.0 - ADAM_B1 ** ADAM_STEP)
        v_hat = new_v / (1.0 - ADAM_B2 ** ADAM_STEP)
        d_ref[...] = -ADAM_LR * (m_hat / (jnp.sqrt(v_hat) + ADAM_EPS) + ADAM_WD * w_ref[...])
        nm_ref[...] = new_m
        nv_ref[...] = new_v

    blk = _spec((None, tr, cols), lambda l, i: (l, i, 0))
    return _call("adamw", body, (layers, rows // tr), [blk] * 4, [blk] * 3, [_sds(w.shape, F32)] * 3)(w, g, m, v)


def _rope_layout(w):
    z = jnp.zeros(w.shape[:-1] + (ROPE_HALF,), w.dtype)
    return jnp.concatenate([w[..., :ROPE_HALF], z, w[..., ROPE_HALF:], z], axis=-1)


def _rope_layout_inv(w):
    return jnp.concatenate([w[..., :ROPE_HALF], w[..., 2 * ROPE_HALF:3 * ROPE_HALF]], axis=-1)


def _cols_from_chips(g):
    return jnp.moveaxis(g, 0, 1).reshape(g.shape[1], N_CHIPS * g.shape[2])


def _cols_to_chips(w):
    return jnp.moveaxis(w.reshape(w.shape[0], N_CHIPS, w.shape[1] // N_CHIPS), 1, 0)


def _w_in_layout(w):
    off_kr = Q_LORA + KV_LORA
    return jnp.concatenate([w[:, :off_kr], w[:, off_kr + QK_ROPE:], _rope_layout(w[:, off_kr:off_kr + QK_ROPE])], axis=1)


def _w_in_layout_inv(dw):
    return jnp.concatenate([dw[:, :P_A], _rope_layout_inv(dw[:, P_KR:]), dw[:, P_A:P_KR]], axis=1)


def _w_uq_layout(w):
    w = w.reshape(Q_LORA, HEADS, QK_NOPE + QK_ROPE)
    return jnp.concatenate([w[..., :QK_NOPE].reshape(Q_LORA, HEADS * QK_NOPE),
                            _rope_layout(w[..., QK_NOPE:]).reshape(Q_LORA, HEADS * LANES)], axis=1)


def _w_uq_layout_inv(d_nope, d_rope):
    d_nope = d_nope.reshape(Q_LORA, HEADS, QK_NOPE)
    d_rope = _rope_layout_inv(d_rope.reshape(Q_LORA, HEADS, LANES))
    return jnp.concatenate([d_nope, d_rope], axis=-1).reshape(Q_LORA, HEADS * (QK_NOPE + QK_ROPE))


def _to_lanes(a):
    flat = a.reshape(-1)
    pad = (-flat.shape[0]) % LANES
    if pad:
        flat = jnp.concatenate([flat, jnp.zeros((pad,), flat.dtype)])
    return flat.reshape(-1, LANES)


PACK_ROWS = 64


def _pack(arrays):
    parts, ranges, row = [], [], 0
    for a in arrays:
        p = _to_lanes(a)
        parts.append(p)
        ranges.append((row, row + p.shape[0]))
        row += p.shape[0]
    pad = (-row) % PACK_ROWS
    if pad:
        parts.append(jnp.zeros((pad, LANES), F32))
    return jnp.concatenate(parts, axis=0), ranges


def _unpack(packed, rng, shape):
    n = 1
    for s in shape:
        n *= s
    return packed[rng[0]:rng[1]].reshape(-1)[:n].reshape(shape)


def kernel(x, positions, ffn_a_pre_g, ffn_a_post_g, ffn_a_w_gate, ffn_a_w_up, ffn_a_w_down, ffn_b_pre_g, ffn_b_post_g, ffn_b_w_gate, ffn_b_w_up, ffn_b_w_down, even_pre_g, even_post_g, even_w_in, even_q_norm_g, even_kv_norm_g, even_w_uq, even_w_ukv, even_conv_w, even_conv_b, even_conv_norm_g, even_conv_norm_b, even_w_out, odd_pre_g, odd_post_g, odd_w_in, odd_v_norm_g, odd_v_norm_b, odd_w_s, odd_b_s, odd_w_out, loss_target, m_ffn_a_pre_g, m_ffn_a_post_g, m_ffn_a_w_gate, m_ffn_a_w_up, m_ffn_a_w_down, m_ffn_b_pre_g, m_ffn_b_post_g, m_ffn_b_w_gate, m_ffn_b_w_up, m_ffn_b_w_down, m_even_pre_g, m_even_post_g, m_even_w_in, m_even_q_norm_g, m_even_kv_norm_g, m_even_w_uq, m_even_w_ukv, m_even_conv_w, m_even_conv_b, m_even_conv_norm_g, m_even_conv_norm_b, m_even_w_out, m_odd_pre_g, m_odd_post_g, m_odd_w_in, m_odd_v_norm_g, m_odd_v_norm_b, m_odd_w_s, m_odd_b_s, m_odd_w_out, v_ffn_a_pre_g, v_ffn_a_post_g, v_ffn_a_w_gate, v_ffn_a_w_up, v_ffn_a_w_down, v_ffn_b_pre_g, v_ffn_b_post_g, v_ffn_b_w_gate, v_ffn_b_w_up, v_ffn_b_w_down, v_even_pre_g, v_even_post_g, v_even_w_in, v_even_q_norm_g, v_even_kv_norm_g, v_even_w_uq, v_even_w_ukv, v_even_conv_w, v_even_conv_b, v_even_conv_norm_g, v_even_conv_norm_b, v_even_w_out, v_odd_pre_g, v_odd_post_g, v_odd_w_in, v_odd_v_norm_g, v_odd_v_norm_b, v_odd_w_s, v_odd_b_s, v_odd_w_out):
    given = dict(locals())
    w = {n: given[n] for n in WEIGHTS}
    batch, seq, d = x.shape
    t = batch * seq
    tm = _tile(t, 512, 16)
    tr = _tile(t, 512, 16)
    chip = 2 * lax.axis_index("x") + lax.axis_index("y")
    place = jnp.stack([chip, lax.axis_index("c")]).astype(jnp.int32)

    small_shard, small_shard_rng = _pack([w[n] for n in SMALL_SHARDED])
    wbuf = {(n, layer): _cast_into_slot(w[n][layer:layer + 1], place, MXU_DTYPE)
            for n in BIG for layer in range(w[n].shape[0])}
    wbuf['small'] = _cast_into_slot(small_shard[None], place, F32)

    def ffn_keys(tag, layer):
        return [(f'ffn_{tag}_w_{part}', layer) for part in ('gate', 'up', 'down')]

    def gather(keys):
        return _GatherChips([wbuf[k] for k in keys]), keys

    def landed(order, results):
        for k, r in zip(order[1], results):
            wbuf[k] = r

    def weight(key):
        n, _ = key
        return wbuf[key].reshape((N_CHIPS, 1) + w[n].shape[1:])

    first = gather(ffn_keys('a', 0)[:2] + ['small'])
    landed(first, _run_courier("gather_chips", first[0]))
    shards = wbuf['small'].reshape((N_CHIPS,) + small_shard.shape)
    full = dict(w)
    for n, rng in zip(SMALL_SHARDED, small_shard_rng):
        per_chip = [_unpack(shards[j], rng, w[n].shape) for j in range(N_CHIPS)]
        full[n] = jnp.concatenate(per_chip, axis=-1)

    inv_freq = ROPE_THETA ** (-jnp.arange(0, QK_ROPE, 2, dtype=F32) / QK_ROPE)
    ang = positions.astype(F32).reshape(t, 1) * inv_freq
    zeros = jnp.zeros((t, ROPE_HALF), F32)
    cos = jnp.concatenate([jnp.cos(ang), zeros, jnp.cos(ang), zeros], axis=1)
    sin = jnp.concatenate([-jnp.sin(ang), zeros, jnp.sin(ang), zeros], axis=1)

    def ffn_args(tag, layer):
        gate, up, down = (weight(k) for k in ffn_keys(tag, layer))
        return (w[f'ffn_{tag}_pre_g'][layer:layer + 1], w[f'ffn_{tag}_post_g'][layer:layer + 1], gate, up, down, 0, tm, tr)

    def ffn_forward(xin, tag, layer, up_keys, down_keys, h=None, next_g=None):
        orders = {'up': gather(ffn_keys(tag, layer)[2:] + up_keys)}
        if down_keys:
            orders['down'] = gather(down_keys)
        out, saved, carried, h_next = _ffn_fwd(xin, *ffn_args(tag, layer), couriers={k: o[0] for k, o in orders.items()},
                                               wd_landed=0, h=h, next_g=next_g)
        for k, o in orders.items():
            landed(o, carried[k])
        return out, saved, h_next

    even_keys = [('even_w_in', 0), ('even_w_uq', 0), ('even_w_ukv', 0), ('even_w_out', 0)]
    odd_keys = [('odd_w_in', 0), ('odd_w_out', 0)]

    xs = x.reshape(t, d)
    x1, ffn_a0, h_e = ffn_forward(xs, 'a', 0, even_keys, ffn_keys('b', 0)[:1], next_g=w['even_pre_g'])
    w_in = _w_in_layout(_cols_from_chips(weight(('even_w_in', 0))[:, 0]))
    w_uq = _w_uq_layout(_cols_from_chips(weight(('even_w_uq', 0))[:, 0]))
    w_ukv = _cols_from_chips(weight(('even_w_ukv', 0))[:, 0])
    w_out_e = weight(('even_w_out', 0))[:, 0].reshape(d, d)
    order = gather(ffn_keys('b', 0)[1:2])
    p, res = _mm_nn("mm_w_in", h_e, w_in, F32, tm, tn_want=640, courier=order[0])
    landed(order, res)
    lat = _rows("latent_norm", lambda r, c: ([jnp.concatenate([_rmsnorm(r[0], c[0]), _rmsnorm(r[1], c[1])], axis=1)], []),
                [(p, Q_LORA, 0), (p, KV_LORA, 1)], [w['even_q_norm_g'], w['even_kv_norm_g']], [(Q_LORA + KV_LORA, MXU_DTYPE)], [], tr)[0]
    q_all = _mm_nn("mm_w_uq", lat, w_uq, F32, tm, a_col0=0, tk_want=Q_LORA)
    kv = _mm_nn("mm_w_ukv", lat, w_ukv, MXU_DTYPE, tm, a_col0=Q_LORA, tk_want=KV_LORA)
    cat = _attn_fwd(q_all, kv, p, cos, sin, batch, seq)
    cat = _conv_fwd(p, cat, full['even_conv_w'][0], w['even_conv_b'], w['even_conv_norm_g'], w['even_conv_norm_b'], batch, seq)
    y_e = _mm_nn("mm_w_out", cat, w_out_e, F32, tm)
    x2 = _rms_residual(x1, y_e, w['even_post_g'], 1.0, tr)
    x3, ffn_b0, h_a1 = ffn_forward(x2, 'b', 0, ffn_keys('a', 1)[:1], ffn_keys('a', 1)[1:2], next_g=w['ffn_a_pre_g'][1:2])
    x4, ffn_a1, h_o = ffn_forward(x3, 'a', 1, odd_keys, ffn_keys('b', 1)[:1], h=h_a1, next_g=full['odd_pre_g'])
    w_out_o = weight(('odd_w_out', 0))[:, 0].reshape(GM_WIDTH, d)
    w_in_o = weight(('odd_w_in', 0))
    ns = w_in_o.shape[3]
    tk_o = _tile(d, 1024, LANES)
    order = gather(ffn_keys('b', 1)[1:2])
    (hw,), res = _matmul("mm_w_in_odd", "nn", (t // tm, N_CHIPS, d // tk_o),
                         [[(h_o, _spec((tm, tk_o), lambda i, j, k: (i, k)), w_in_o, _spec((None, None, tk_o, ns), lambda i, j, k: (j, 0, k, 0)))]],
                         [((t, N_CHIPS * ns), F32, _spec((tm, ns), lambda i, j, k: (i, j)))], (tm, ns), courier=order[0])
    landed(order, res)
    b_st = w['odd_b_s'][0].T
    sg = _sgu_fwd(hw, full['odd_v_norm_g'], full['odd_v_norm_b'], w['odd_w_s'][0], b_st)
    y_o = _mm_nn("mm_w_out", sg, w_out_o, F32, tm)
    x5 = _rms_residual(x4, y_o, full['odd_post_g'], 1.0, tr)
    x6, ffn_b1, _ = ffn_forward(x5, 'b', 1, [], [])

    dy, sq_err = _loss_head(x6, loss_target.reshape(t, d), tr)
    loss = lax.psum(0.5 * sq_err[0, 0] / d, ("x", "y", "c"))

    grads = {}
    received = {}

    def swap(named):
        items = [(k, g.reshape(N_CHIPS, 2, g.shape[1] // 2, g.shape[2])) for k, g in named]
        return _SwapHalves([g for _, g in items]), items

    def core_sums_of(order, results):
        return [(k, _add_halves(g, r, place)) for (k, g), r in zip(order[1], results)]

    def core_sums(named):
        order = swap(named)
        return core_sums_of(order, _run_courier("swap_halves", order[0]))

    def scatter(named):
        return _ScatterChips([pt for _, pt in named]), named

    def arrived(order, results):
        for (k, pt), r in zip(order[1], results):
            received[k] = (pt, r)

    def ffn_grads(tag, layer):
        return list(zip(ffn_keys(tag, layer), fg[(tag, layer)][2:]))

    fg = {}
    made = {}
    dx, fg[('b', 1)], _ = _ffn_bwd(dy, ffn_b1, *ffn_args('b', 1))
    swap_b1 = swap(ffn_grads('b', 1))
    dy_o, grads['odd_post_g'] = _rms_bwd(y_o, full['odd_post_g'], dx, 1.0, MXU_DTYPE, tr)
    dsg, res = _mm_nt("mm_dsg", [(dy_o, w_out_o, 0)], F32, tm, courier=swap_b1[0])
    pend = core_sums_of(swap_b1, res)
    (g_w_out_o,) = _mm_tn("mm_dw_out", sg, [dy_o], MXU_DTYPE)
    dhw, grads['odd_v_norm_g'], grads['odd_v_norm_b'], g_ws, g_bst = _sgu_bwd(
        hw, full['odd_v_norm_g'], full['odd_v_norm_b'], w['odd_w_s'][0], b_st, dsg)
    grads['odd_w_s'] = g_ws[None]
    grads['odd_b_s'] = g_bst[:, :GROUPS].T[None]
    tn_o = _tile(d, 1024, LANES)
    order = scatter(pend[0:1])
    (dh_o,), res = _matmul("mm_dh_odd", "nt", (t // tm, d // tn_o, N_CHIPS),
                           [[(dhw, _spec((tm, ns), lambda i, j, k: (i, k)), w_in_o, _spec((None, None, tn_o, ns), lambda i, j, k: (k, 0, j, 0)))]],
                           [((t, d), F32, _spec((tm, tn_o), lambda i, j, k: (i, j)))], (tm, tn_o), courier=order[0])
    arrived(order, res)
    tmw = _tile(d, 512, LANES)
    tkt = _tile(t, TN_ROWS, 16)
    order = scatter(pend[1:2])
    (g_w_in_o,), res = _matmul("mm_dw_in_odd", "tn", (d // tmw, N_CHIPS, t // tkt),
                               [[(h_o, _spec((tkt, tmw), lambda i, j, k: (k, i)), dhw, _spec((tkt, ns), lambda i, j, k: (k, j)))]],
                               [((N_CHIPS, d, ns), MXU_DTYPE, _spec((None, tmw, ns), lambda i, j, k: (j, i, 0)))], (tmw, ns),
                               courier=order[0])
    arrived(order, res)
    swap_odd = swap([(('odd_w_in', 0), g_w_in_o),
                     (('odd_w_out', 0), g_w_out_o.reshape((N_CHIPS,) + w['odd_w_out'].shape[1:]))])
    dx, grads['odd_pre_g'] = _rms_bwd(x4, full['odd_pre_g'], dh_o, 1.0, F32, tr, resid=dx)

    def scatter_after_dact(name, first, swapped):
        def make(known):
            res = known["carried"]["dact"]
            arrived(first, res[:1])
            made[name] = scatter(core_sums_of(swapped, res[1:]))
            return made[name][0]
        return make

    first = scatter(pend[2:3])
    dx, fg[('a', 1)], carried = _ffn_bwd(dx, ffn_a1, *ffn_args('a', 1), couriers={
        'dact': _Both(first[0], swap_odd[0]), 'dwdown': scatter_after_dact('odd', first, swap_odd)})
    arrived(made['odd'], carried['dwdown'])
    swap_a1 = swap(ffn_grads('a', 1))

    def a1_scatter(idx):
        def make(known):
            if 'a1' not in made:
                made['a1'] = core_sums_of(swap_a1, known["carried"]["dact"])
            made[('a1', idx)] = scatter(made['a1'][idx:idx + 1])
            return made[('a1', idx)][0]
        return make

    dx, fg[('b', 0)], carried = _ffn_bwd(dx, ffn_b0, *ffn_args('b', 0), couriers={
        'dact': swap_a1[0], 'dwdown': a1_scatter(0), 'dh': a1_scatter(1), 'dwup': a1_scatter(2)})
    for idx, name in enumerate(('dwdown', 'dh', 'dwup')):
        arrived(made[('a1', idx)], carried[name])
    swap_b0 = swap(ffn_grads('b', 0))
    dy_e, grads['even_post_g'] = _rms_bwd(y_e, w['even_post_g'], dx, 1.0, MXU_DTYPE, tr)
    d_cat, res = _mm_nt("mm_dcat", [(dy_e, w_out_e, 0)], F32, tm, courier=swap_b0[0])
    pend = core_sums_of(swap_b0, res)
    (g_w_out_e,) = _mm_tn("mm_dw_out", cat, [dy_e], MXU_DTYPE)
    dqn, dqp, dkv, dkr = _attn_bwd(q_all, kv, p, cos, sin, d_cat, batch, seq)
    hq = HEADS * LANES
    d_latq = _mm_nt("mm_dlat_q", [(dqn, w_uq, 0), (dqp, w_uq, hq)], F32, tm, tn_want=Q_LORA)
    d_latkv = _mm_nt("mm_dlat_kv", [(dkv, w_ukv, 0)], F32, tm, tn_want=KV_LORA)
    g_uq_n, g_uq_r = _mm_tn("mm_dw_uq", lat, [dqn, dqp], MXU_DTYPE, a_col0=0, m_width=Q_LORA)
    (g_ukv,) = _mm_tn("mm_dw_ukv", lat, [dkv], MXU_DTYPE, a_col0=Q_LORA, m_width=KV_LORA)

    def latent_bwd(r, c):
        _, vjp_q = jax.vjp(_rmsnorm, r[0], c[0])
        _, vjp_kv = jax.vjp(_rmsnorm, r[1], c[1])
        dq, dqg = vjp_q(r[2])
        dk, dkg = vjp_kv(r[3])
        return [jnp.concatenate([dq, dk], axis=1)], [dqg, dkg]

    d_lat, grads['even_q_norm_g'], grads['even_kv_norm_g'] = _rows(
        "latent_norm_bwd", latent_bwd, [(p, Q_LORA, 0), (p, KV_LORA, 1), (d_latq, Q_LORA, 0), (d_latkv, KV_LORA, 0)],
        [w['even_q_norm_g'], w['even_kv_norm_g']], [(Q_LORA + KV_LORA, MXU_DTYPE)], [(1, Q_LORA), (1, KV_LORA)], tr)
    da, dgate, g_conv_w, grads['even_conv_b'], grads['even_conv_norm_g'], grads['even_conv_norm_b'] = _conv_bwd(
        p, d_cat, full['even_conv_w'][0], w['even_conv_b'], w['even_conv_norm_g'], w['even_conv_norm_b'], batch, seq)
    dp = jnp.concatenate([d_lat, da, dgate, dkr.astype(MXU_DTYPE)], axis=1)
    order = scatter(pend[0:1])
    dh_e, res = _mm_nt("mm_dh_even", [(dp, w_in, 0)], F32, tm, tk_want=640, courier=order[0])
    arrived(order, res)
    order = scatter(pend[1:2])
    (g_w_in,), res = _mm_tn("mm_dw_in", h_e, [dp], MXU_DTYPE, tn_want=640, courier=order[0])
    arrived(order, res)
    swap_even = swap([(('even_w_in', 0), _cols_to_chips(_w_in_layout_inv(g_w_in))),
                      (('even_w_uq', 0), _cols_to_chips(_w_uq_layout_inv(g_uq_n, g_uq_r))),
                      (('even_w_ukv', 0), _cols_to_chips(g_ukv)),
                      (('even_w_out', 0), g_w_out_e.reshape((N_CHIPS,) + w['even_w_out'].shape[1:]))])
    dx, grads['even_pre_g'] = _rms_bwd(x1, w['even_pre_g'], dh_e, 1.0, F32, tr, resid=dx)
    first = scatter(pend[2:3])

    def after_w_down(known):
        made['a0_down'] = scatter(core_sums([(ffn_keys('a', 0)[2], known["dwd"])]))
        return made['a0_down'][0]

    def after_w_up(known):
        made['a0_up'] = scatter(core_sums(list(zip(ffn_keys('a', 0)[:2], (known["dwg"], known["dwu"])))))
        return made['a0_up'][0]

    dx, fg[('a', 0)], carried = _ffn_bwd(dx, ffn_a0, *ffn_args('a', 0), weights_first=True, couriers={
        'dact': _Both(first[0], swap_even[0]), 'dwdown': scatter_after_dact('even', first, swap_even),
        'dwup': after_w_down, 'dh': after_w_up})
    arrived(made['even'], carried['dwdown'])
    arrived(made['a0_down'], carried['dwup'])
    arrived(made['a0_up'], carried['dh'])
    grad_x = dx.reshape(batch, seq, d)

    reduced = []
    for n in BIG:
        layers = w[n].shape[0]
        buf = None
        for layer in range(layers):
            pt, r = received[(n, layer)]
            buf = _add_chips(pt, r, place, layer, layers, into=buf)
        reduced.append(buf)
    joined = _join_halves(reduced)
    for n, g in zip(BIG, joined):
        grads[n] = g.reshape(w[n].shape)

    for tag in ('a', 'b'):
        grads[f'ffn_{tag}_pre_g'] = jnp.concatenate([fg[(tag, 0)][0], fg[(tag, 1)][0]], axis=0)
        grads[f'ffn_{tag}_post_g'] = jnp.concatenate([fg[(tag, 0)][1], fg[(tag, 1)][1]], axis=0)
    grads['even_conv_w'] = g_conv_w[None]
    packed, rngs = _pack([grads[n] for n in SMALL])
    device = 2 * chip + lax.axis_index("c")
    slots = lax.dynamic_update_slice_in_dim(jnp.zeros((N_DEV,) + packed.shape, F32), packed[None], device, axis=0)
    summed = _add_devices(_gather_devices(slots))
    for n, rng in zip(SMALL, rngs):
        g = _unpack(summed, rng, grads[n].shape)
        if n in SMALL_SHARDED:
            width = w[n].shape[-1]
            g = lax.dynamic_slice_in_dim(g, chip * width, width, axis=g.ndim - 1)
        grads[n] = g

    delta, new_m, new_v = {}, {}, {}
    for n in BIG:
        delta[n], new_m[n], new_v[n] = _adamw(w[n], grads[n], given['m_' + n], given['v_' + n])
    packs = [_pack([src[n] for n in SMALL])[0][None] for src in
             (w, grads, {n: given['m_' + n] for n in SMALL}, {n: given['v_' + n] for n in SMALL})]
    _, rngs = _pack([w[n] for n in SMALL])
    small_out = _adamw(*packs)
    for n, rng in zip(SMALL, rngs):
        delta[n], new_m[n], new_v[n] = (_unpack(o[0], rng, w[n].shape) for o in small_out)

    return (loss, grad_x, *[grads[n] for n in WEIGHTS], *[delta[n] for n in WEIGHTS],
            *[new_m[n] for n in WEIGHTS], *[new_v[n] for n in WEIGHTS])
```

```python
import functools

import jax
import jax.numpy as jnp
from jax import lax
from jax.experimental import pallas as pl
from jax.experimental.pallas import tpu as pltpu

F32 = jnp.float32
BF16 = jnp.bfloat16
MXU_DTYPE = BF16
MESH = pl.DeviceIdType.MESH
VMEM_LIMIT_BYTES = 56 * 1024 * 1024
LANES = 128
TN_ROWS = 2048
STREAM_TILE_ELEMS = 512 * 1408

D_MODEL = 2048
D_FF = 5632
EPS = 1e-6
HEADS = 8
V_HEAD = 128
QK_NOPE = 128
QK_ROPE = 64
Q_LORA = 512
KV_LORA = 512
ROPE_THETA = 10000.0
CONV_CH = 1024
CONV_WIDTH = 31
GROUPS = 8
CHUNK = 128
GM_WIDTH = 2048
ADAM_LR = 0.001
ADAM_B1 = 0.9
ADAM_B2 = 0.999
ADAM_EPS = 1e-08
ADAM_WD = 0.01
ADAM_STEP = 10

N_CHIPS = 4
N_DEV = 8
P_KV = Q_LORA
P_A = Q_LORA + KV_LORA
P_GATE = P_A + CONV_CH
P_KR = P_GATE + CONV_CH
P_WIDTH = P_KR + LANES
ROPE_HALF = QK_ROPE // 2

WEIGHTS = ['ffn_a_pre_g', 'ffn_a_post_g', 'ffn_a_w_gate', 'ffn_a_w_up', 'ffn_a_w_down', 'ffn_b_pre_g', 'ffn_b_post_g',
           'ffn_b_w_gate', 'ffn_b_w_up', 'ffn_b_w_down', 'even_pre_g', 'even_post_g', 'even_w_in', 'even_q_norm_g',
           'even_kv_norm_g', 'even_w_uq', 'even_w_ukv', 'even_conv_w', 'even_conv_b', 'even_conv_norm_g',
           'even_conv_norm_b', 'even_w_out', 'odd_pre_g', 'odd_post_g', 'odd_w_in', 'odd_v_norm_g', 'odd_v_norm_b',
           'odd_w_s', 'odd_b_s', 'odd_w_out']
BIG = ['ffn_a_w_gate', 'ffn_a_w_up', 'ffn_a_w_down', 'ffn_b_w_gate', 'ffn_b_w_up', 'ffn_b_w_down', 'even_w_in',
       'even_w_uq', 'even_w_ukv', 'even_w_out', 'odd_w_in', 'odd_w_out']
SMALL = [n for n in WEIGHTS if n not in BIG]
SMALL_SHARDED = ['even_conv_w', 'odd_pre_g', 'odd_post_g', 'odd_v_norm_g', 'odd_v_norm_b']


def _call(name, body, grid, in_specs, out_specs, out_shape, scratch=(), prefetch=0, aliases=None):
    params = pltpu.CompilerParams(dimension_semantics=("arbitrary",) * len(grid), vmem_limit_bytes=VMEM_LIMIT_BYTES)
    if prefetch:
        spec = pltpu.PrefetchScalarGridSpec(num_scalar_prefetch=prefetch, grid=grid, in_specs=in_specs,
                                            out_specs=out_specs, scratch_shapes=list(scratch))
        return pl.pallas_call(body, grid_spec=spec, out_shape=out_shape, compiler_params=params, name=name,
                              input_output_aliases=aliases or {})
    return pl.pallas_call(body, grid=grid, in_specs=in_specs, out_specs=out_specs, out_shape=out_shape,
                          scratch_shapes=list(scratch), compiler_params=params, name=name,
                          input_output_aliases=aliases or {})


def _spec(block, index_map):
    return pl.BlockSpec(block, index_map)


def _sds(shape, dtype):
    return jax.ShapeDtypeStruct(tuple(shape), dtype)


def _tile(n, want, mult=8):
    if n <= want:
        return n
    best = None
    for t in range(mult, want + 1, mult):
        if n % t == 0:
            best = t
    assert best is not None, (n, want, mult)
    return best


_DIMS = {"nn": (((1,), (0,)), ((), ())), "nt": (((1,), (1,)), ((), ())), "tn": (((0,), (0,)), ((), ()))}


def _dot(a, b, mode="nn"):
    return lax.dot_general(a.astype(MXU_DTYPE), b.astype(MXU_DTYPE), _DIMS[mode], preferred_element_type=F32)


def _matmul(name, mode, grid, groups, outs, acc_shape, extras=(), epilogue=None, courier=None, sum_outs=0):
    flat, specs = [], []
    for grp in groups:
        for a, a_spec, b, b_spec in grp:
            flat += [a, b]
            specs += [a_spec, b_spec]
    for e, e_spec in extras:
        flat.append(e)
        specs.append(e_spec)
    n_pairs = [len(g) for g in groups]
    one_step = grid[2] == 1
    n_in, n_ex, n_out, n_acc = 2 * sum(n_pairs), len(extras), len(outs), 0 if one_step else len(groups)
    last = tuple(g - 1 for g in grid)
    c_arrays = list(courier.arrays) if courier else []
    c_shapes = list(courier.out_shapes) if courier else []
    n_ci, n_co = len(c_arrays), len(c_shapes)
    any_spec = pl.BlockSpec(memory_space=pl.ANY)

    def body(*refs):
        ins, ex = refs[:n_in], refs[n_in:n_in + n_ex]
        pos = n_in + n_ex
        c_in = refs[pos:pos + n_ci]
        out_refs = refs[pos + n_ci:pos + n_ci + n_out]
        c_out = refs[pos + n_ci + n_out:pos + n_ci + n_out + n_co]
        accs = refs[pos + n_ci + n_out + n_co:pos + n_ci + n_out + n_co + n_acc]
        sems = refs[pos + n_ci + n_out + n_co + n_acc:]
        i, j, k = pl.program_id(0), pl.program_id(1), pl.program_id(2)

        if courier:
            @pl.when((i == 0) & (j == 0) & (k == 0))
            def _():
                courier.start(c_in, c_out, sems)

        def products():
            pos, totals = 0, []
            for n in n_pairs:
                total = None
                for _ in range(n):
                    d = _dot(ins[pos][...], ins[pos + 1][...], mode)
                    total = d if total is None else total + d
                    pos += 2
                totals.append(total)
            return totals

        summed = out_refs[n_out - sum_outs:] if sum_outs else ()
        if summed:
            @pl.when((i == 0) & (j == 0) & (k == 0))
            def _():
                for o in summed:
                    o[...] = jnp.zeros(o.shape, F32)

        def finish(vals):
            res = epilogue(vals, [e[...] for e in ex]) if epilogue else vals
            for idx, (o, r) in enumerate(zip(out_refs, res)):
                if idx >= n_out - sum_outs:
                    o[...] += r
                else:
                    o[...] = r.astype(o.dtype)

        if one_step:
            finish(products())
        else:
            @pl.when(k == 0)
            def _():
                for acc in accs:
                    acc[...] = jnp.zeros(acc.shape, F32)

            for acc, total in zip(accs, products()):
                acc[...] += total

            @pl.when(k == last[2])
            def _():
                finish([acc[...] for acc in accs])

        if courier:
            @pl.when((i == last[0]) & (j == last[1]) & (k == last[2]))
            def _():
                courier.finish(c_in, c_out, sems)

    aliases = {n_in + n_ex + t: n_out + t for t in range(n_ci)} if courier and courier.aliased else None
    scratch = [pltpu.VMEM(acc_shape, F32)] * n_acc
    scratch += [pltpu.SemaphoreType.DMA((n,)) for n in courier.sems] if courier else []
    res = _call(name, body, grid, specs + [any_spec] * n_ci, [o[2] for o in outs] + [any_spec] * n_co,
                [_sds(o[0], o[1]) for o in outs] + c_shapes, scratch=scratch, aliases=aliases)(*flat, *c_arrays)
    return list(res[:n_out]), list(res[n_out:])


def _rows(name, fn, rows, consts, outs, accs, tm):
    t_rows = rows[0][0].shape[0]
    grid = (t_rows // tm,)
    in_specs = [_spec((tm, bw), functools.partial(lambda i, cb: (i, cb), cb=cb)) for _, bw, cb in rows]
    in_specs += [_spec(c.shape, functools.partial(lambda i, nd: (0,) * nd, nd=c.ndim)) for c in consts]
    out_shape = [_sds((t_rows, w), dt) for w, dt in outs] + [_sds(s, F32) for s in accs]
    out_specs = [_spec((tm, w), lambda i: (i, 0)) for w, _ in outs]
    out_specs += [_spec(s, functools.partial(lambda i, nd: (0,) * nd, nd=len(s))) for s in accs]
    nr, nc, no = len(rows), len(consts), len(outs)

    def body(*refs):
        r = [ref[...] for ref in refs[:nr]]
        c = [ref[...] for ref in refs[nr:nr + nc]]
        o_refs, a_refs = refs[nr + nc:nr + nc + no], refs[nr + nc + no:]
        o_vals, a_vals = fn(r, c)
        for ref, v in zip(o_refs, o_vals):
            ref[...] = v.astype(ref.dtype)
        if a_refs:
            @pl.when(pl.program_id(0) == 0)
            def _():
                for ref in a_refs:
                    ref[...] = jnp.zeros(ref.shape, F32)

            for ref, v in zip(a_refs, a_vals):
                ref[...] += v

    return _call(name, body, grid, in_specs, out_specs, out_shape)(*[a for a, _, _ in rows], *consts)


def _rmsnorm(x, g):
    return x * lax.rsqrt(jnp.mean(x * x, axis=-1, keepdims=True) + EPS) * g


def _layernorm(x, g, b):
    mu = jnp.mean(x, axis=-1, keepdims=True)
    var = jnp.mean(jnp.square(x - mu), axis=-1, keepdims=True)
    return (x - mu) * lax.rsqrt(var + EPS) * g + b


def _swiglu_act(g, u):
    return jax.nn.silu(g) * u


def _rope(x, cos, sin_signed):
    return x * cos + pltpu.roll(x, 2 * ROPE_HALF, 1) * sin_signed


def _rope_transposed(dy, cos, sin_signed):
    return dy * cos - pltpu.roll(dy, 2 * ROPE_HALF, 1) * sin_signed


def _rms_fwd(x, g, tm):
    d = x.shape[1]
    return _rows("rms_fwd", lambda r, c: ([_rmsnorm(r[0], c[0])], []), [(x, d, 0)], [g], [(d, MXU_DTYPE)], [], tm)[0]


def _rms_residual(x, y, g, scale, tm):
    d = x.shape[1]
    return _rows("rms_residual", lambda r, c: ([r[0] + scale * _rmsnorm(r[1], c[0])], []),
                 [(x, d, 0), (y, d, 0)], [g], [(d, F32)], [], tm)[0]


def _rms_bwd(y, g, dout, scale, out_dtype, tm, resid=None):
    d = y.shape[1]

    def fn(r, c):
        _, vjp = jax.vjp(_rmsnorm, r[0], c[0])
        dy, dg = vjp(scale * r[1].astype(F32))
        if resid is not None:
            dy = dy + r[2]
        return [dy], [dg]

    rows = [(y, d, 0), (dout, d, 0)] + ([(resid, d, 0)] if resid is not None else [])
    return _rows("rms_bwd" if resid is None else "rms_bwd_resid", fn, rows, [g], [(d, out_dtype)], [(1, d)], tm)


def _loss_head(y, target, tm):
    d = y.shape[1]

    def fn(r, c):
        err = r[0] - r[1]
        sq = jnp.sum(jnp.sum(err * err, axis=1, keepdims=True), axis=0, keepdims=True)
        return [err * (1.0 / d)], [jnp.broadcast_to(sq, (1, LANES))]

    return _rows("loss_head", fn, [(y, d, 0), (target, d, 0)], [], [(d, F32)], [(1, LANES)], tm)


def _ffn_fwd(x, pre_g, post_g, wg, wu, wd, layer, tm, tr, couriers=None, wd_landed=None, h=None, next_g=None):
    t, d = x.shape
    fs = wg.shape[3]
    if h is None:
        h = _rms_fwd(x, pre_g, tr)
    h_spec = _spec((tm, d), lambda j, i, k: (i, 0))
    w_spec = _spec((None, None, d, fs), lambda j, i, k: (j, layer, 0, 0))
    o_spec = _spec((tm, fs), lambda j, i, k: (i, j))
    couriers = couriers or {}
    carried = {}
    (g, u, a), carried["up"] = _matmul(
        "ffn_up", "nn", (N_CHIPS, t // tm, 1), [[(h, h_spec, wg, w_spec)], [(h, h_spec, wu, w_spec)]],
        [((t, N_CHIPS * fs), MXU_DTYPE, o_spec)] * 3, (tm, fs),
        epilogue=lambda accs, _: (accs[0], accs[1], _swiglu_act(accs[0], accs[1])), courier=couriers.get("up"))
    if wd_landed is not None:
        wd = carried["up"][wd_landed].reshape(wd.shape)
    row_spec = _spec((tm, d), lambda i, j, k: (i, 0))
    vec_spec = _spec((1, d), lambda i, j, k: (0, 0))

    def post(accs, ex):
        out = ex[0] + 0.5 * _rmsnorm(accs[0], ex[1])
        return (accs[0], out) + ((_rmsnorm(out, ex[2]),) if next_g is not None else ())

    res, carried["down"] = _matmul(
        "ffn_down", "nn", (t // tm, 1, N_CHIPS),
        [[(a, _spec((tm, fs), lambda i, j, k: (i, k)), wd, _spec((None, None, fs, d), lambda i, j, k: (k, layer, 0, 0)))]],
        [((t, d), F32, row_spec)] * 2 + ([((t, d), MXU_DTYPE, row_spec)] if next_g is not None else []), (tm, d),
        extras=[(x, row_spec), (post_g, vec_spec)] + ([(next_g, vec_spec)] if next_g is not None else []),
        epilogue=post, courier=couriers.get("down"))
    y, out = res[0], res[1]
    return out, (x, h, g, u, a, y), carried, (res[2] if next_g is not None else None)


def _ffn_bwd(dout, saved, pre_g, post_g, wg, wu, wd, layer, tm, tr, couriers=None, weights_first=False):
    x, h, g, u, a, y = saved
    t, d = x.shape
    fs = wg.shape[3]
    tn = _tile(d, 1024, LANES)
    tkt = _tile(t, TN_ROWS, 16)
    dy, dpost = _rms_bwd(y, post_g, dout, 0.5, MXU_DTYPE, tr)

    def act_bwd(accs, ex):
        _, vjp = jax.vjp(_swiglu_act, ex[0].astype(F32), ex[1].astype(F32))
        return vjp(accs[0])

    couriers = couriers or {}
    carried = {}
    known = {"carried": carried}

    def late(name):
        c = couriers.get(name)
        return c(known) if callable(c) else c

    gu_spec = _spec((tm, fs), lambda j, i, k: (i, j))
    (dg, du), carried["dact"] = _matmul(
        "ffn_dact", "nt", (N_CHIPS, t // tm, 1),
        [[(dy, _spec((tm, d), lambda j, i, k: (i, 0)), wd, _spec((None, None, fs, d), lambda j, i, k: (j, layer, 0, 0)))]],
        [((t, N_CHIPS * fs), MXU_DTYPE, gu_spec)] * 2, (tm, fs),
        extras=[(g, gu_spec), (u, gu_spec)], epilogue=act_bwd, courier=couriers.get("dact"))
    (dwd,), carried["dwdown"] = _matmul(
        "ffn_dwdown", "tn", (N_CHIPS, d // tn, t // tkt),
        [[(a, _spec((tkt, fs), lambda i, j, k: (k, i)), dy, _spec((tkt, tn), lambda i, j, k: (k, j)))]],
        [((N_CHIPS, fs, d), MXU_DTYPE, _spec((None, fs, tn), lambda i, j, k: (i, 0, j)))], (fs, tn),
        courier=late("dwdown"))

    def d_hidden(courier):
        da_spec = _spec((tm, fs), lambda i, j, k: (i, k))
        wt_spec = _spec((None, None, d, fs), lambda i, j, k: (k, layer, 0, 0))
        return _matmul("ffn_dh", "nt", (t // tm, 1, N_CHIPS), [[(dg, da_spec, wg, wt_spec), (du, da_spec, wu, wt_spec)]],
                       [((t, d), F32, _spec((tm, d), lambda i, j, k: (i, 0)))], (tm, d), courier=courier)

    def d_w_up(courier):
        tmw = _tile(d, 512, LANES)
        h_spec = _spec((tkt, tmw), lambda i, j, k: (k, i))
        dgu_spec = _spec((tkt, fs), lambda i, j, k: (k, j))
        dw_spec = _spec((None, tmw, fs), lambda i, j, k: (j, i, 0))
        return _matmul("ffn_dwup", "tn", (d // tmw, N_CHIPS, t // tkt), [[(h, h_spec, dg, dgu_spec)], [(h, h_spec, du, dgu_spec)]],
                       [((N_CHIPS, d, fs), MXU_DTYPE, dw_spec)] * 2, (tmw, fs), courier=courier)

    known["dwd"] = dwd
    if weights_first:
        (dwg, dwu), carried["dwup"] = d_w_up(late("dwup"))
        known["dwg"], known["dwu"] = dwg, dwu
        (dh,), carried["dh"] = d_hidden(late("dh"))
    else:
        (dh,), carried["dh"] = d_hidden(late("dh"))
        (dwg, dwu), carried["dwup"] = d_w_up(late("dwup"))
    dx, dpre = _rms_bwd(x, pre_g, dh, 1.0, F32, tr, resid=dout)
    return dx, (dpre, dpost, dwg, dwu, dwd), carried


def _attn_scores(qn, qpe, kn, kpe, qi, tq, seq):
    s = (_dot(qn, kn, "nt") + _dot(qpe, kpe, "nt")) * ((QK_NOPE + QK_ROPE) ** -0.5)
    rows = qi * tq + lax.broadcasted_iota(jnp.int32, (tq, seq), 0)
    cols = lax.broadcasted_iota(jnp.int32, (tq, seq), 1)
    s = jnp.where(cols <= rows, s, -jnp.inf)
    e = jnp.exp(s - jnp.max(s, axis=1, keepdims=True))
    return e / jnp.sum(e, axis=1, keepdims=True)


KEY_EXTENTS = 8


def _for_key_extent(qi, tq, seq, attend):
    n = KEY_EXTENTS if seq % (KEY_EXTENTS * tq) == 0 else 1
    ext = seq // n
    mine = (qi * tq) // ext
    for e in range(n):
        pl.when(mine == e)(functools.partial(attend, (e + 1) * ext))


def _attn_specs(nq, tq, seq):
    q_rows = lambda b, h, qi: b * nq + qi
    return [
        _spec((tq, LANES), lambda b, h, qi: (q_rows(b, h, qi), h)),
        _spec((tq, LANES), lambda b, h, qi: (q_rows(b, h, qi), HEADS + h)),
        _spec((seq, 2 * LANES), lambda b, h, qi: (b, h)),
        _spec((seq, LANES), lambda b, h, qi: (b, P_KR // LANES)),
        _spec((tq, LANES), lambda b, h, qi: (q_rows(b, h, qi), 0)),
        _spec((tq, LANES), lambda b, h, qi: (q_rows(b, h, qi), 0)),
        _spec((seq, LANES), lambda b, h, qi: (b, 0)),
        _spec((seq, LANES), lambda b, h, qi: (b, 0)),
    ]


def _attn_fwd(q_all, kv, p, cos, sin, batch, seq):
    t = q_all.shape[0]
    tq = _tile(seq, 256, 16)
    nq = seq // tq

    def body(qn_ref, qp_ref, kv_ref, kr_ref, cq, sq, ck, sk, o_ref):
        qi = pl.program_id(2)
        qpe = _rope(qp_ref[...], cq[...], sq[...])

        def attend(keys):
            kpe = _rope(kr_ref[:keys, :], ck[:keys, :], sk[:keys, :])
            prob = _attn_scores(qn_ref[...], qpe, kv_ref[:keys, :LANES], kpe, qi, tq, keys)
            o_ref[...] = _dot(prob, kv_ref[:keys, LANES:]).astype(o_ref.dtype)

        _for_key_extent(qi, tq, seq, attend)

    return _call("attn_fwd", body, (batch, HEADS, nq), _attn_specs(nq, tq, seq),
                 _spec((tq, LANES), lambda b, h, qi: (b * nq + qi, h)), _sds((t, 2 * HEADS * V_HEAD), MXU_DTYPE)
                 )(q_all, q_all, kv, p, cos, sin, cos, sin)


def _attn_bwd(q_all, kv, p, cos, sin, d_cat, batch, seq):
    t = q_all.shape[0]
    tq = _tile(seq, 256, 16)
    nq = seq // tq

    def body(qn_ref, qp_ref, kv_ref, kr_ref, cq, sq, ck, sk, do_ref, dqn_ref, dqp_ref, dkv_ref, dkr_ref, dkpe_acc):
        h, qi = pl.program_id(1), pl.program_id(2)
        qn = qn_ref[...]
        qpe = _rope(qp_ref[...], cq[...], sq[...])
        do = do_ref[...]

        @pl.when(qi == 0)
        def _():
            dkv_ref[...] = jnp.zeros(dkv_ref.shape, F32)

        @pl.when((h == 0) & (qi == 0))
        def _():
            dkpe_acc[...] = jnp.zeros(dkpe_acc.shape, F32)

        def attend(keys):
            kn, v = kv_ref[:keys, :LANES], kv_ref[:keys, LANES:]
            kpe = _rope(kr_ref[:keys, :], ck[:keys, :], sk[:keys, :])
            prob = _attn_scores(qn, qpe, kn, kpe, qi, tq, keys)
            dprob = _dot(do, v, "nt")
            ds = prob * (dprob - jnp.sum(prob * dprob, axis=1, keepdims=True)) * ((QK_NOPE + QK_ROPE) ** -0.5)
            dqn_ref[...] = _dot(ds, kn)
            dqp_ref[...] = _rope_transposed(_dot(ds, kpe), cq[...], sq[...])
            dkv_ref[:keys, :LANES] += _dot(ds, qn, "tn")
            dkv_ref[:keys, LANES:] += _dot(prob, do, "tn")
            dkpe_acc[:keys, :] += _dot(ds, qpe, "tn")

        _for_key_extent(qi, tq, seq, attend)

        @pl.when((h == HEADS - 1) & (qi == nq - 1))
        def _():
            dkr_ref[...] = _rope_transposed(dkpe_acc[...], ck[...], sk[...])

    q_out = _spec((tq, LANES), lambda b, h, qi: (b * nq + qi, h))
    return _call("attn_bwd", body, (batch, HEADS, nq),
                 _attn_specs(nq, tq, seq) + [_spec((tq, LANES), lambda b, h, qi: (b * nq + qi, h))],
                 [q_out, q_out, _spec((seq, 2 * LANES), lambda b, h, qi: (b, h)), _spec((seq, LANES), lambda b, h, qi: (b, 0))],
                 [_sds((t, HEADS * LANES), F32), _sds((t, HEADS * LANES), F32), _sds((t, HEADS * 2 * LANES), F32), _sds((t, LANES), F32)],
                 scratch=[pltpu.VMEM((seq, LANES), F32)])(q_all, q_all, kv, p, cos, sin, cos, sin, d_cat)


CONV_PAD = 32


def _conv_taps(w_ref, src_ref, first_row, n_rows, init, offset):
    acc = init
    for k in range(CONV_WIDTH):
        acc = acc + w_ref[k:k + 1, :] * src_ref[pl.ds(first_row + offset(k), n_rows), :]
    return acc


def _norm_act(conv, g, b):
    return jax.nn.silu(_layernorm(conv, g, b))


def _conv_fwd(p, cat, conv_w, conv_b, norm_g, norm_b, batch, seq):
    rc = _tile(seq, 256, 8)
    a_blk, gate_blk, out_blk = P_A // LANES, P_GATE // LANES, (HEADS * V_HEAD) // LANES

    def body(a_ref, gate_ref, w_ref, cb_ref, ng_ref, nb_ref, cat_in, o_ref, zp):
        del cat_in
        zp[pl.ds(0, CONV_PAD), :] = jnp.zeros((CONV_PAD, LANES), F32)
        zp[pl.ds(CONV_PAD, seq), :] = a_ref[...] * jax.nn.sigmoid(gate_ref[...])
        for r0 in range(0, seq, rc):
            conv = _conv_taps(w_ref, zp, r0, rc, jnp.broadcast_to(cb_ref[...], (rc, LANES)),
                              lambda k: CONV_PAD - (CONV_WIDTH - 1) + k)
            o_ref[pl.ds(r0, rc), :] = _norm_act(conv, ng_ref[...], nb_ref[...]).astype(o_ref.dtype)

    vec = _spec((1, LANES), lambda b, g: (0, g))
    return _call("conv_fwd", body, (batch, GROUPS),
                 [_spec((seq, LANES), lambda b, g: (b, a_blk + g)), _spec((seq, LANES), lambda b, g: (b, gate_blk + g)),
                  _spec((CONV_WIDTH, LANES), lambda b, g: (0, g)), vec, vec, vec, pl.BlockSpec(memory_space=pl.ANY)],
                 _spec((seq, LANES), lambda b, g: (b, out_blk + g)), _sds(cat.shape, cat.dtype),
                 scratch=[pltpu.VMEM((seq + CONV_PAD, LANES), F32)], aliases={6: 0}
                 )(p, p, conv_w, conv_b, norm_g, norm_b, cat)


def _conv_bwd(p, d_cat, conv_w, conv_b, norm_g, norm_b, batch, seq):
    t = p.shape[0]
    rc = _tile(seq, 256, 8)
    a_blk, gate_blk, out_blk = P_A // LANES, P_GATE // LANES, (HEADS * V_HEAD) // LANES

    def body(a_ref, gate_ref, w_ref, cb_ref, ng_ref, nb_ref, dc_ref, da_ref, dgate_ref, dw_ref, db_ref, dng_ref, dnb_ref,
             zp, dcp, buf):
        b = pl.program_id(1)
        a, sig = a_ref[...], jax.nn.sigmoid(gate_ref[...])
        zp[pl.ds(0, CONV_PAD), :] = jnp.zeros((CONV_PAD, LANES), F32)
        zp[pl.ds(CONV_PAD, seq), :] = a * sig
        for r0 in range(0, seq, rc):
            buf[pl.ds(r0, rc), :] = _conv_taps(w_ref, zp, r0, rc, jnp.broadcast_to(cb_ref[...], (rc, LANES)),
                                               lambda k: CONV_PAD - (CONV_WIDTH - 1) + k)
        _, vjp = jax.vjp(_norm_act, buf[...], ng_ref[...], nb_ref[...])
        dconv, dng, dnb = vjp(dc_ref[...].astype(F32))
        dcp[pl.ds(0, seq), :] = dconv
        dcp[pl.ds(seq, CONV_PAD), :] = jnp.zeros((CONV_PAD, LANES), F32)

        @pl.when(b == 0)
        def _():
            dw_ref[...] = jnp.zeros(dw_ref.shape, F32)
            db_ref[...] = jnp.zeros(db_ref.shape, F32)
            dng_ref[...] = jnp.zeros(dng_ref.shape, F32)
            dnb_ref[...] = jnp.zeros(dnb_ref.shape, F32)

        db_ref[...] += jnp.sum(dconv, axis=0, keepdims=True)
        dng_ref[...] += dng
        dnb_ref[...] += dnb
        for k in range(CONV_WIDTH):
            shifted = zp[pl.ds(CONV_PAD - (CONV_WIDTH - 1) + k, seq), :]
            dw_ref[k:k + 1, :] += jnp.sum(dconv * shifted, axis=0, keepdims=True)
        for r0 in range(0, seq, rc):
            buf[pl.ds(r0, rc), :] = _conv_taps(w_ref, dcp, r0, rc, jnp.zeros((rc, LANES), F32),
                                               lambda k: CONV_WIDTH - 1 - k)
        dz = buf[...]
        da_ref[...] = (dz * sig).astype(da_ref.dtype)
        dgate_ref[...] = (dz * a * sig * (1.0 - sig)).astype(dgate_ref.dtype)

    vec = _spec((1, LANES), lambda g, b: (0, g))
    row_out = _spec((seq, LANES), lambda g, b: (b, g))
    return _call("conv_bwd", body, (GROUPS, batch),
                 [_spec((seq, LANES), lambda g, b: (b, a_blk + g)), _spec((seq, LANES), lambda g, b: (b, gate_blk + g)),
                  _spec((CONV_WIDTH, LANES), lambda g, b: (0, g)), vec, vec, vec,
                  _spec((seq, LANES), lambda g, b: (b, out_blk + g))],
                 [row_out, row_out, _spec((CONV_WIDTH, LANES), lambda g, b: (0, g)), vec, vec, vec],
                 [_sds((t, CONV_CH), MXU_DTYPE), _sds((t, CONV_CH), MXU_DTYPE), _sds((CONV_WIDTH, CONV_CH), F32),
                  _sds((1, CONV_CH), F32), _sds((1, CONV_CH), F32), _sds((1, CONV_CH), F32)],
                 scratch=[pltpu.VMEM((seq + CONV_PAD, LANES), F32), pltpu.VMEM((seq + CONV_PAD, LANES), F32),
                          pltpu.VMEM((seq, LANES), F32)])(p, p, conv_w, conv_b, norm_g, norm_b, d_cat)


def _sgu_pre(hw, g, b):
    z = jax.nn.gelu(hw)
    return z[:, :GM_WIDTH], _layernorm(z[:, GM_WIDTH:], g, b)


def _causal(w):
    keep = lax.broadcasted_iota(jnp.int32, (CHUNK, CHUNK), 0) >= lax.broadcasted_iota(jnp.int32, (CHUNK, CHUNK), 1)
    return jnp.where(keep, w, 0.0)


def _sgu_fwd(hw, vg, vb, w_s, b_st):
    t = hw.shape[0]
    gw = GM_WIDTH // GROUPS

    def body(hw_ref, vg_ref, vb_ref, w_ref, b_ref, o_ref):
        u, v = _sgu_pre(hw_ref[...], vg_ref[...], vb_ref[...])
        for g in range(GROUPS):
            cols = slice(g * gw, (g + 1) * gw)
            s = _dot(_causal(w_ref[g]), v[:, cols]) + b_ref[:, g:g + 1]
            o_ref[:, cols] = (u[:, cols] * s).astype(o_ref.dtype)

    return _call("sgu_fwd", body, (t // CHUNK,),
                 [_spec((CHUNK, 2 * GM_WIDTH), lambda i: (i, 0)), _spec((1, GM_WIDTH), lambda i: (0, 0)),
                  _spec((1, GM_WIDTH), lambda i: (0, 0)), _spec((GROUPS, CHUNK, CHUNK), lambda i: (0, 0, 0)),
                  _spec((CHUNK, GROUPS), lambda i: (0, 0))],
                 _spec((CHUNK, GM_WIDTH), lambda i: (i, 0)), _sds((t, GM_WIDTH), MXU_DTYPE))(hw, vg, vb, w_s, b_st)


def _sgu_bwd(hw, vg, vb, w_s, b_st, dout):
    t = hw.shape[0]
    gw = GM_WIDTH // GROUPS

    def body(hw_ref, vg_ref, vb_ref, w_ref, b_ref, do_ref, dhw_ref, dvg_ref, dvb_ref, dw_ref, db_ref, du_buf, dv_buf):
        (u, v), vjp = jax.vjp(_sgu_pre, hw_ref[...], vg_ref[...], vb_ref[...])

        @pl.when(pl.program_id(0) == 0)
        def _():
            dvg_ref[...] = jnp.zeros(dvg_ref.shape, F32)
            dvb_ref[...] = jnp.zeros(dvb_ref.shape, F32)
            dw_ref[...] = jnp.zeros(dw_ref.shape, F32)
            db_ref[...] = jnp.zeros(db_ref.shape, F32)

        lane = lax.broadcasted_iota(jnp.int32, (CHUNK, LANES), 1)
        db = jnp.zeros((CHUNK, LANES), F32)
        for g in range(GROUPS):
            cols = slice(g * gw, (g + 1) * gw)
            w = _causal(w_ref[g])
            s = _dot(w, v[:, cols]) + b_ref[:, g:g + 1]
            do = do_ref[:, cols]
            ds = do * u[:, cols]
            du_buf[:, cols] = do * s
            dw_ref[g] += _causal(_dot(ds, v[:, cols], "nt"))
            dv_buf[:, cols] = _dot(w, ds, "tn")
            db = db + jnp.where(lane == g, jnp.sum(ds, axis=1, keepdims=True), 0.0)
        db_ref[...] += db
        dhw, dvg, dvb = vjp((du_buf[...], dv_buf[...]))
        dhw_ref[...] = dhw.astype(dhw_ref.dtype)
        dvg_ref[...] += dvg
        dvb_ref[...] += dvb

    vec = _spec((1, GM_WIDTH), lambda i: (0, 0))
    w_spec = _spec((GROUPS, CHUNK, CHUNK), lambda i: (0, 0, 0))
    return _call("sgu_bwd", body, (t // CHUNK,),
                 [_spec((CHUNK, 2 * GM_WIDTH), lambda i: (i, 0)), vec, vec, w_spec, _spec((CHUNK, GROUPS), lambda i: (0, 0)),
                  _spec((CHUNK, GM_WIDTH), lambda i: (i, 0))],
                 [_spec((CHUNK, 2 * GM_WIDTH), lambda i: (i, 0)), vec, vec, w_spec, _spec((CHUNK, LANES), lambda i: (0, 0))],
                 [_sds((t, 2 * GM_WIDTH), MXU_DTYPE), _sds((1, GM_WIDTH), F32), _sds((1, GM_WIDTH), F32),
                  _sds((GROUPS, CHUNK, CHUNK), F32), _sds((CHUNK, LANES), F32)],
                 scratch=[pltpu.VMEM((CHUNK, GM_WIDTH), F32), pltpu.VMEM((CHUNK, GM_WIDTH), F32)]
                 )(hw, vg, vb, w_s, b_st, dout)


def _mm_nn(name, a, b, out_dtype, tm, a_col0=0, tn_want=1024, tk_want=1024, courier=None):
    t = a.shape[0]
    kk, n = b.shape
    tk = _tile(kk, tk_want, LANES)
    tn = _tile(n, tn_want, LANES)
    k0 = a_col0 // tk
    assert a_col0 % tk == 0
    outs, carried = _matmul(
        name, "nn", (t // tm, n // tn, kk // tk),
        [[(a, _spec((tm, tk), lambda i, j, k: (i, k0 + k)), b, _spec((tk, tn), lambda i, j, k: (k, j)))]],
        [((t, n), out_dtype, _spec((tm, tn), lambda i, j, k: (i, j)))], (tm, tn), courier=courier)
    return (outs[0], carried) if courier else outs[0]


def _mm_nt(name, pairs, out_dtype, tm, tn_want=1024, tk_want=1024, courier=None):
    t, kk = pairs[0][0].shape
    n = pairs[0][1].shape[0]
    tk = _tile(kk, tk_want, LANES)
    tn = _tile(n, tn_want, 16)
    grp = []
    for a, b, b_col0 in pairs:
        assert b_col0 % tk == 0 and a.shape == (t, kk)
        grp.append((a, _spec((tm, tk), lambda i, j, k: (i, k)), b,
                    _spec((tn, tk), functools.partial(lambda i, j, k, k0: (j, k0 + k), k0=b_col0 // tk))))
    outs, carried = _matmul(name, "nt", (t // tm, n // tn, kk // tk), [grp],
                            [((t, n), out_dtype, _spec((tm, tn), lambda i, j, k: (i, j)))], (tm, tn), courier=courier)
    return (outs[0], carried) if courier else outs[0]


def _mm_tn(name, a, bs, out_dtype, a_col0=0, m_width=None, tm_want=512, tn_want=1024, courier=None):
    t = a.shape[0]
    m = m_width or a.shape[1]
    n = bs[0].shape[1]
    tmw = _tile(m, tm_want, LANES)
    tn = _tile(n, tn_want, LANES)
    tkt = _tile(t, TN_ROWS, 16)
    assert a_col0 % tmw == 0
    i0 = a_col0 // tmw
    a_spec = _spec((tkt, tmw), lambda i, j, k: (k, i0 + i))
    groups = [[(a, a_spec, b, _spec((tkt, tn), lambda i, j, k: (k, j)))] for b in bs]
    outs, carried = _matmul(name, "tn", (m // tmw, n // tn, t // tkt), groups,
                            [((m, n), out_dtype, _spec((tmw, tn), lambda i, j, k: (i, j)))] * len(bs), (tmw, tn),
                            courier=courier)
    return (outs, carried) if courier else outs


_ANY = pl.BlockSpec(memory_space=pl.ANY)


def _comm_call(name, body, ins, out_shapes, sems, aliases=None):
    return pl.pallas_call(body, name=name, out_shape=out_shapes, in_specs=[_ANY] * len(ins),
                          out_specs=[_ANY] * len(out_shapes), input_output_aliases=aliases or {},
                          scratch_shapes=[pltpu.SemaphoreType.DMA((n,)) for n in sems])(*ins)


def _place():
    return lax.axis_index("x"), lax.axis_index("y"), lax.axis_index("c")


def _other_chips(x, y):
    return [(1 - x, y), (x, 1 - y), (1 - x, 1 - y)]


def _cast_into_slot(w, place, dtype):
    layers, rows, cols = w.shape
    half = rows // 2
    tr = _tile(half, 2 * STREAM_TILE_ELEMS // cols, 16)
    nt = half // tr

    def body(s, w_ref, o_ref):
        del s
        o_ref[...] = w_ref[...].astype(o_ref.dtype)

    return _call("cast_into_slot", body, (layers, 2, nt),
                 [_spec((None, tr, cols), lambda l, h, i, s: (l, h * nt + i, 0))],
                 _spec((None, None, None, tr, cols), lambda l, h, i, s: (s[0], l, h, i, 0)),
                 _sds((N_CHIPS, layers, 2, half, cols), dtype), prefetch=1)(place, w)


class _GatherChips:
    aliased = True

    def __init__(self, bufs):
        self.arrays = list(bufs)
        self.out_shapes = [_sds(b.shape, b.dtype) for b in bufs]
        self.sems = [3 * len(bufs)] * 4

    def _sent(self, buf, sems):
        x, y, c = _place()
        me = 2 * x + y
        return [[pltpu.make_async_remote_copy(
            src_ref=buf[t].at[me, :, c], dst_ref=buf[t].at[me, :, c], send_sem=sems[0].at[3 * t + r],
            recv_sem=sems[1].at[3 * t + r], device_id=(px, py, c), device_id_type=MESH)
            for r, (px, py) in enumerate(_other_chips(x, y))] for t in range(len(buf))]

    def start(self, _, buf, sems):
        for row in self._sent(buf, sems):
            for cp in row:
                cp.start()

    def finish(self, _, buf, sems):
        x, y, c = _place()
        sent = self._sent(buf, sems)
        passed = []
        for t in range(len(buf)):
            for r, (px, py) in enumerate(_other_chips(x, y)):
                landed = buf[t].at[2 * px + py, :, c]
                sent[t][r].wait_recv()
                cp = pltpu.make_async_remote_copy(
                    src_ref=landed, dst_ref=landed, send_sem=sems[2].at[3 * t + r], recv_sem=sems[3].at[3 * t + r],
                    device_id=(x, y, 1 - c), device_id_type=MESH)
                cp.start()
                passed.append(cp)
        for cp in passed:
            cp.wait()
        for row in sent:
            for cp in row:
                cp.wait_send()


class _ScatterChips:
    aliased = False

    def __init__(self, partials):
        self.arrays = list(partials)
        self.out_shapes = [_sds((3,) + p.shape[1:], p.dtype) for p in partials]
        self.sems = [3 * len(partials)] * 2

    def _copies(self, src, dst, sems):
        x, y, c = _place()
        return [pltpu.make_async_remote_copy(
            src_ref=src[t].at[2 * px + py], dst_ref=dst[t].at[r], send_sem=sems[0].at[3 * t + r],
            recv_sem=sems[1].at[3 * t + r], device_id=(px, py, c), device_id_type=MESH)
            for t in range(len(src)) for r, (px, py) in enumerate(_other_chips(x, y))]

    def start(self, src, dst, sems):
        for cp in self._copies(src, dst, sems):
            cp.start()

    def finish(self, src, dst, sems):
        for cp in self._copies(src, dst, sems):
            cp.wait()


def _run_courier(name, courier):
    n_in, n_out = len(courier.arrays), len(courier.out_shapes)

    def body(*refs):
        src, dst, sems = refs[:n_in], refs[n_in:n_in + n_out], refs[n_in + n_out:]
        courier.start(src, dst, sems)
        courier.finish(src, dst, sems)

    return _comm_call(name, body, courier.arrays, courier.out_shapes, courier.sems,
                      aliases={t: t for t in range(n_in)} if courier.aliased else None)


class _SwapHalves:
    aliased = False

    def __init__(self, grads):
        self.arrays = list(grads)
        self.out_shapes = [_sds((g.shape[0],) + g.shape[2:], g.dtype) for g in grads]
        self.sems = [len(grads)] * 2

    def _copies(self, src, dst, sems):
        x, y, c = _place()
        return [pltpu.make_async_remote_copy(
            src_ref=src[t].at[:, 1 - c], dst_ref=dst[t], send_sem=sems[0].at[t], recv_sem=sems[1].at[t],
            device_id=(x, y, 1 - c), device_id_type=MESH) for t in range(len(src))]

    def start(self, src, dst, sems):
        for cp in self._copies(src, dst, sems):
            cp.start()

    def finish(self, src, dst, sems):
        for cp in self._copies(src, dst, sems):
            cp.wait()


class _Both:
    aliased = False

    def __init__(self, first, second):
        assert not first.aliased and not second.aliased
        self.parts = (first, second)
        self.arrays = first.arrays + second.arrays
        self.out_shapes = first.out_shapes + second.out_shapes
        self.sems = first.sems + second.sems

    def _split(self, src, dst, sems):
        a = self.parts[0]
        na, no, ns = len(a.arrays), len(a.out_shapes), len(a.sems)
        return (src[:na], dst[:no], sems[:ns]), (src[na:], dst[no:], sems[ns:])

    def start(self, src, dst, sems):
        for part, args in zip(self.parts, self._split(src, dst, sems)):
            part.start(*args)

    def finish(self, src, dst, sems):
        for part, args in zip(self.parts, self._split(src, dst, sems)):
            part.finish(*args)


def _join_halves(grads):
    n = len(grads)

    def body(*refs):
        buf = refs[n:2 * n]
        send_sems, recv_sems = refs[2 * n:]
        x, y, c = _place()
        copies = [pltpu.make_async_remote_copy(
            src_ref=buf[t].at[:, c], dst_ref=buf[t].at[:, c], send_sem=send_sems.at[t], recv_sem=recv_sems.at[t],
            device_id=(x, y, 1 - c), device_id_type=MESH) for t in range(n)]
        for cp in copies:
            cp.start()
        for cp in copies:
            cp.wait()

    return _comm_call("join_halves", body, grads, [_sds(g.shape, g.dtype) for g in grads], [n, n],
                      aliases={t: t for t in range(n)})


def _gather_devices(slots):
    def body(src, buf, send_sems, recv_sems):
        del src
        x, y, c = _place()
        me = 4 * x + 2 * y + c
        copies = []
        for r in range(1, N_DEV):
            fx, fy, fc = (r >> 2) & 1, (r >> 1) & 1, r & 1
            peer = (1 - x if fx else x, 1 - y if fy else y, 1 - c if fc else c)
            copies.append(pltpu.make_async_remote_copy(
                src_ref=buf.at[me], dst_ref=buf.at[me], send_sem=send_sems.at[r - 1], recv_sem=recv_sems.at[r - 1],
                device_id=peer, device_id_type=MESH))
        for cp in copies:
            cp.start()
        for cp in copies:
            cp.wait()

    return _comm_call("gather_devices", body, [slots], [_sds(slots.shape, slots.dtype)], [N_DEV - 1, N_DEV - 1],
                      aliases={0: 0})[0]


def _add_halves(grad, got, place):
    _, _, rows, cols = grad.shape
    tr = _tile(rows, 2 * STREAM_TILE_ELEMS // cols, 16)

    def body(s, a, b, o):
        del s
        o[...] = (a[...].astype(F32) + b[...].astype(F32)).astype(o.dtype)

    return _call("add_halves", body, (N_CHIPS, rows // tr),
                 [_spec((None, None, tr, cols), lambda j, i, s: (j, s[1], i, 0)), _spec((None, tr, cols), lambda j, i, s: (j, i, 0))],
                 _spec((None, tr, cols), lambda j, i, s: (j, i, 0)), _sds((N_CHIPS, rows, cols), grad.dtype), prefetch=1
                 )(place, grad, got)


def _add_chips(partial, got, place, layer, layers, into=None):
    _, rows, cols = partial.shape
    tr = _tile(rows, STREAM_TILE_ELEMS // cols, 16)

    def body(s, a, b, *rest):
        o = rest[-1]
        o[...] = a[...].astype(F32) + b[0].astype(F32) + b[1].astype(F32) + b[2].astype(F32)

    in_specs = [_spec((None, tr, cols), lambda i, s: (s[0], i, 0)), _spec((3, tr, cols), lambda i, s: (0, i, 0))]
    args = [place, partial, got]
    if into is not None:
        in_specs.append(pl.BlockSpec(memory_space=pl.ANY))
        args.append(into)
    return _call("add_chips", body, (rows // tr,), in_specs,
                 _spec((None, None, tr, cols), lambda i, s: (layer, s[1], i, 0)), _sds((layers, 2, rows, cols), F32),
                 prefetch=1, aliases={3: 0} if into is not None else None)(*args)


def _add_devices(got):
    _, rows, cols = got.shape
    tr = _tile(rows, 512, 8)

    def body(a, o):
        total = a[0]
        for d in range(1, N_DEV):
            total = total + a[d]
        o[...] = total

    return _call("add_devices", body, (rows // tr,), [_spec((N_DEV, tr, cols), lambda i: (0, i, 0))],
                 _spec((tr, cols), lambda i: (i, 0)), _sds((rows, cols), F32))(got)


def _adamw(w, g, m, v):
    layers, rows, cols = w.shape
    tr = _tile(rows, STREAM_TILE_ELEMS // cols, 8)

    def body(w_ref, g_ref, m_ref, v_ref, d_ref, nm_ref, nv_ref):
        grad = g_ref[...]
        new_m = ADAM_B1 * m_ref[...] + (1.0 - ADAM_B1) * grad
        new_v = ADAM_B2 * v_ref[...] + (1.0 - ADAM_B2) * jnp.square(grad)
        m_hat = new_m / (1.0 - ADAM_B1 ** ADAM_STEP)
        v_hat = new_v / (1.0 - ADAM_B2 ** ADAM_STEP)
        d_ref[...] = -ADAM_LR * (m_hat / (jnp.sqrt(v_hat) + ADAM_EPS) + ADAM_WD * w_ref[...])
        nm_ref[...] = new_m
        nv_ref[...] = new_v

    blk = _spec((None, tr, cols), lambda l, i: (l, i, 0))
    return _call("adamw", body, (layers, rows // tr), [blk] * 4, [blk] * 3, [_sds(w.shape, F32)] * 3)(w, g, m, v)


def _rope_layout(w):
    z = jnp.zeros(w.shape[:-1] + (ROPE_HALF,), w.dtype)
    return jnp.concatenate([w[..., :ROPE_HALF], z, w[..., ROPE_HALF:], z], axis=-1)


def _rope_layout_inv(w):
    return jnp.concatenate([w[..., :ROPE_HALF], w[..., 2 * ROPE_HALF:3 * ROPE_HALF]], axis=-1)


def _cols_from_chips(g):
    return jnp.moveaxis(g, 0, 1).reshape(g.shape[1], N_CHIPS * g.shape[2])


def _cols_to_chips(w):
    return jnp.moveaxis(w.reshape(w.shape[0], N_CHIPS, w.shape[1] // N_CHIPS), 1, 0)


def _w_in_layout(w):
    off_kr = Q_LORA + KV_LORA
    return jnp.concatenate([w[:, :off_kr], w[:, off_kr + QK_ROPE:], _rope_layout(w[:, off_kr:off_kr + QK_ROPE])], axis=1)


def _w_in_layout_inv(dw):
    return jnp.concatenate([dw[:, :P_A], _rope_layout_inv(dw[:, P_KR:]), dw[:, P_A:P_KR]], axis=1)


def _w_uq_layout(w):
    w = w.reshape(Q_LORA, HEADS, QK_NOPE + QK_ROPE)
    return jnp.concatenate([w[..., :QK_NOPE].reshape(Q_LORA, HEADS * QK_NOPE),
                            _rope_layout(w[..., QK_NOPE:]).reshape(Q_LORA, HEADS * LANES)], axis=1)


def _w_uq_layout_inv(d_nope, d_rope):
    d_nope = d_nope.reshape(Q_LORA, HEADS, QK_NOPE)
    d_rope = _rope_layout_inv(d_rope.reshape(Q_LORA, HEADS, LANES))
    return jnp.concatenate([d_nope, d_rope], axis=-1).reshape(Q_LORA, HEADS * (QK_NOPE + QK_ROPE))


def _to_lanes(a):
    flat = a.reshape(-1)
    pad = (-flat.shape[0]) % LANES
    if pad:
        flat = jnp.concatenate([flat, jnp.zeros((pad,), flat.dtype)])
    return flat.reshape(-1, LANES)


PACK_ROWS = 64


def _pack(arrays):
    parts, ranges, row = [], [], 0
    for a in arrays:
        p = _to_lanes(a)
        parts.append(p)
        ranges.append((row, row + p.shape[0]))
        row += p.shape[0]
    pad = (-row) % PACK_ROWS
    if pad:
        parts.append(jnp.zeros((pad, LANES), F32))
    return jnp.concatenate(parts, axis=0), ranges


def _unpack(packed, rng, shape):
    n = 1
    for s in shape:
        n *= s
    return packed[rng[0]:rng[1]].reshape(-1)[:n].reshape(shape)


def kernel(x, positions, ffn_a_pre_g, ffn_a_post_g, ffn_a_w_gate, ffn_a_w_up, ffn_a_w_down, ffn_b_pre_g, ffn_b_post_g, ffn_b_w_gate, ffn_b_w_up, ffn_b_w_down, even_pre_g, even_post_g, even_w_in, even_q_norm_g, even_kv_norm_g, even_w_uq, even_w_ukv, even_conv_w, even_conv_b, even_conv_norm_g, even_conv_norm_b, even_w_out, odd_pre_g, odd_post_g, odd_w_in, odd_v_norm_g, odd_v_norm_b, odd_w_s, odd_b_s, odd_w_out, loss_target, m_ffn_a_pre_g, m_ffn_a_post_g, m_ffn_a_w_gate, m_ffn_a_w_up, m_ffn_a_w_down, m_ffn_b_pre_g, m_ffn_b_post_g, m_ffn_b_w_gate, m_ffn_b_w_up, m_ffn_b_w_down, m_even_pre_g, m_even_post_g, m_even_w_in, m_even_q_norm_g, m_even_kv_norm_g, m_even_w_uq, m_even_w_ukv, m_even_conv_w, m_even_conv_b, m_even_conv_norm_g, m_even_conv_norm_b, m_even_w_out, m_odd_pre_g, m_odd_post_g, m_odd_w_in, m_odd_v_norm_g, m_odd_v_norm_b, m_odd_w_s, m_odd_b_s, m_odd_w_out, v_ffn_a_pre_g, v_ffn_a_post_g, v_ffn_a_w_gate, v_ffn_a_w_up, v_ffn_a_w_down, v_ffn_b_pre_g, v_ffn_b_post_g, v_ffn_b_w_gate, v_ffn_b_w_up, v_ffn_b_w_down, v_even_pre_g, v_even_post_g, v_even_w_in, v_even_q_norm_g, v_even_kv_norm_g, v_even_w_uq, v_even_w_ukv, v_even_conv_w, v_even_conv_b, v_even_conv_norm_g, v_even_conv_norm_b, v_even_w_out, v_odd_pre_g, v_odd_post_g, v_odd_w_in, v_odd_v_norm_g, v_odd_v_norm_b, v_odd_w_s, v_odd_b_s, v_odd_w_out):
    given = dict(locals())
    w = {n: given[n] for n in WEIGHTS}
    batch, seq, d = x.shape
    t = batch * seq
    tm = _tile(t, 512, 16)
    tr = _tile(t, 512, 16)
    chip = 2 * lax.axis_index("x") + lax.axis_index("y")
    place = jnp.stack([chip, lax.axis_index("c")]).astype(jnp.int32)

    small_shard, small_shard_rng = _pack([w[n] for n in SMALL_SHARDED])
    wbuf = {(n, layer): _cast_into_slot(w[n][layer:layer + 1], place, MXU_DTYPE)
            for n in BIG for layer in range(w[n].shape[0])}
    wbuf['small'] = _cast_into_slot(small_shard[None], place, F32)

    def ffn_keys(tag, layer):
        return [(f'ffn_{tag}_w_{part}', layer) for part in ('gate', 'up', 'down')]

    def gather(keys):
        return _GatherChips([wbuf[k] for k in keys]), keys

    def landed(order, results):
        for k, r in zip(order[1], results):
            wbuf[k] = r

    def weight(key):
        n, _ = key
        return wbuf[key].reshape((N_CHIPS, 1) + w[n].shape[1:])

    first = gather(ffn_keys('a', 0)[:2] + ['small'])
    landed(first, _run_courier("gather_chips", first[0]))
    shards = wbuf['small'].reshape((N_CHIPS,) + small_shard.shape)
    full = dict(w)
    for n, rng in zip(SMALL_SHARDED, small_shard_rng):
        per_chip = [_unpack(shards[j], rng, w[n].shape) for j in range(N_CHIPS)]
        full[n] = jnp.concatenate(per_chip, axis=-1)

    inv_freq = ROPE_THETA ** (-jnp.arange(0, QK_ROPE, 2, dtype=F32) / QK_ROPE)
    ang = positions.astype(F32).reshape(t, 1) * inv_freq
    zeros = jnp.zeros((t, ROPE_HALF), F32)
    cos = jnp.concatenate([jnp.cos(ang), zeros, jnp.cos(ang), zeros], axis=1)
    sin = jnp.concatenate([-jnp.sin(ang), zeros, jnp.sin(ang), zeros], axis=1)

    def ffn_args(tag, layer):
        gate, up, down = (weight(k) for k in ffn_keys(tag, layer))
        return (w[f'ffn_{tag}_pre_g'][layer:layer + 1], w[f'ffn_{tag}_post_g'][layer:layer + 1], gate, up, down, 0, tm, tr)

    def ffn_forward(xin, tag, layer, up_keys, down_keys, h=None, next_g=None):
        orders = {'up': gather(ffn_keys(tag, layer)[2:] + up_keys)}
        if down_keys:
            orders['down'] = gather(down_keys)
        out, saved, carried, h_next = _ffn_fwd(xin, *ffn_args(tag, layer), couriers={k: o[0] for k, o in orders.items()},
                                               wd_landed=0, h=h, next_g=next_g)
        for k, o in orders.items():
            landed(o, carried[k])
        return out, saved, h_next

    even_keys = [('even_w_in', 0), ('even_w_uq', 0), ('even_w_ukv', 0), ('even_w_out', 0)]
    odd_keys = [('odd_w_in', 0), ('odd_w_out', 0)]

    xs = x.reshape(t, d)
    x1, ffn_a0, h_e = ffn_forward(xs, 'a', 0, even_keys, ffn_keys('b', 0)[:1], next_g=w['even_pre_g'])
    w_in = _w_in_layout(_cols_from_chips(weight(('even_w_in', 0))[:, 0]))
    w_uq = _w_uq_layout(_cols_from_chips(weight(('even_w_uq', 0))[:, 0]))
    w_ukv = _cols_from_chips(weight(('even_w_ukv', 0))[:, 0])
    w_out_e = weight(('even_w_out', 0))[:, 0].reshape(d, d)
    order = gather(ffn_keys('b', 0)[1:2])
    p, res = _mm_nn("mm_w_in", h_e, w_in, F32, tm, tn_want=640, courier=order[0])
    landed(order, res)
    lat = _rows("latent_norm", lambda r, c: ([jnp.concatenate([_rmsnorm(r[0], c[0]), _rmsnorm(r[1], c[1])], axis=1)], []),
                [(p, Q_LORA, 0), (p, KV_LORA, 1)], [w['even_q_norm_g'], w['even_kv_norm_g']], [(Q_LORA + KV_LORA, MXU_DTYPE)], [], tr)[0]
    q_all = _mm_nn("mm_w_uq", lat, w_uq, F32, tm, a_col0=0, tk_want=Q_LORA)
    kv = _mm_nn("mm_w_ukv", lat, w_ukv, MXU_DTYPE, tm, a_col0=Q_LORA, tk_want=KV_LORA)
    cat = _attn_fwd(q_all, kv, p, cos, sin, batch, seq)
    cat = _conv_fwd(p, cat, full['even_conv_w'][0], w['even_conv_b'], w['even_conv_norm_g'], w['even_conv_norm_b'], batch, seq)
    y_e = _mm_nn("mm_w_out", cat, w_out_e, F32, tm)
    x2 = _rms_residual(x1, y_e, w['even_post_g'], 1.0, tr)
    x3, ffn_b0, h_a1 = ffn_forward(x2, 'b', 0, ffn_keys('a', 1)[:1], ffn_keys('a', 1)[1:2], next_g=w['ffn_a_pre_g'][1:2])
    x4, ffn_a1, h_o = ffn_forward(x3, 'a', 1, odd_keys, ffn_keys('b', 1)[:1], h=h_a1, next_g=full['odd_pre_g'])
    w_out_o = weight(('odd_w_out', 0))[:, 0].reshape(GM_WIDTH, d)
    w_in_o = weight(('odd_w_in', 0))
    ns = w_in_o.shape[3]
    tk_o = _tile(d, 1024, LANES)
    order = gather(ffn_keys('b', 1)[1:2])
    (hw,), res = _matmul("mm_w_in_odd", "nn", (t // tm, N_CHIPS, d // tk_o),
                         [[(h_o, _spec((tm, tk_o), lambda i, j, k: (i, k)), w_in_o, _spec((None, None, tk_o, ns), lambda i, j, k: (j, 0, k, 0)))]],
                         [((t, N_CHIPS * ns), F32, _spec((tm, ns), lambda i, j, k: (i, j)))], (tm, ns), courier=order[0])
    landed(order, res)
    b_st = w['odd_b_s'][0].T
    sg = _sgu_fwd(hw, full['odd_v_norm_g'], full['odd_v_norm_b'], w['odd_w_s'][0], b_st)
    y_o = _mm_nn("mm_w_out", sg, w_out_o, F32, tm)
    x5 = _rms_residual(x4, y_o, full['odd_post_g'], 1.0, tr)
    x6, ffn_b1, _ = ffn_forward(x5, 'b', 1, [], [])

    dy, sq_err = _loss_head(x6, loss_target.reshape(t, d), tr)
    loss = lax.psum(0.5 * sq_err[0, 0] / d, ("x", "y", "c"))

    grads = {}
    received = {}

    def swap(named):
        items = [(k, g.reshape(N_CHIPS, 2, g.shape[1] // 2, g.shape[2])) for k, g in named]
        return _SwapHalves([g for _, g in items]), items

    def core_sums_of(order, results):
        return [(k, _add_halves(g, r, place)) for (k, g), r in zip(order[1], results)]

    def core_sums(named):
        order = swap(named)
        return core_sums_of(order, _run_courier("swap_halves", order[0]))

    def scatter(named):
        return _ScatterChips([pt for _, pt in named]), named

    def arrived(order, results):
        for (k, pt), r in zip(order[1], results):
            received[k] = (pt, r)

    def ffn_grads(tag, layer):
        return list(zip(ffn_keys(tag, layer), fg[(tag, layer)][2:]))

    fg = {}
    made = {}
    dx, fg[('b', 1)], _ = _ffn_bwd(dy, ffn_b1, *ffn_args('b', 1))
    swap_b1 = swap(ffn_grads('b', 1))
    dy_o, grads['odd_post_g'] = _rms_bwd(y_o, full['odd_post_g'], dx, 1.0, MXU_DTYPE, tr)
    dsg, res = _mm_nt("mm_dsg", [(dy_o, w_out_o, 0)], F32, tm, courier=swap_b1[0])
    pend = core_sums_of(swap_b1, res)
    (g_w_out_o,) = _mm_tn("mm_dw_out", sg, [dy_o], MXU_DTYPE)
    dhw, grads['odd_v_norm_g'], grads['odd_v_norm_b'], g_ws, g_bst = _sgu_bwd(
        hw, full['odd_v_norm_g'], full['odd_v_norm_b'], w['odd_w_s'][0], b_st, dsg)
    grads['odd_w_s'] = g_ws[None]
    grads['odd_b_s'] = g_bst[:, :GROUPS].T[None]
    tn_o = _tile(d, 1024, LANES)
    order = scatter(pend[0:1])
    (dh_o,), res = _matmul("mm_dh_odd", "nt", (t // tm, d // tn_o, N_CHIPS),
                           [[(dhw, _spec((tm, ns), lambda i, j, k: (i, k)), w_in_o, _spec((None, None, tn_o, ns), lambda i, j, k: (k, 0, j, 0)))]],
                           [((t, d), F32, _spec((tm, tn_o), lambda i, j, k: (i, j)))], (tm, tn_o), courier=order[0])
    arrived(order, res)
    tmw = _tile(d, 512, LANES)
    tkt = _tile(t, TN_ROWS, 16)
    order = scatter(pend[1:2])
    (g_w_in_o,), res = _matmul("mm_dw_in_odd", "tn", (d // tmw, N_CHIPS, t // tkt),
                               [[(h_o, _spec((tkt, tmw), lambda i, j, k: (k, i)), dhw, _spec((tkt, ns), lambda i, j, k: (k, j)))]],
                               [((N_CHIPS, d, ns), MXU_DTYPE, _spec((None, tmw, ns), lambda i, j, k: (j, i, 0)))], (tmw, ns),
                               courier=order[0])
    arrived(order, res)
    swap_odd = swap([(('odd_w_in', 0), g_w_in_o),
                     (('odd_w_out', 0), g_w_out_o.reshape((N_CHIPS,) + w['odd_w_out'].shape[1:]))])
    dx, grads['odd_pre_g'] = _rms_bwd(x4, full['odd_pre_g'], dh_o, 1.0, F32, tr, resid=dx)

    def scatter_after_dact(name, first, swapped):
        def make(known):
            res = known["carried"]["dact"]
            arrived(first, res[:1])
            made[name] = scatter(core_sums_of(swapped, res[1:]))
            return made[name][0]
        return make

    first = scatter(pend[2:3])
    dx, fg[('a', 1)], carried = _ffn_bwd(dx, ffn_a1, *ffn_args('a', 1), couriers={
        'dact': _Both(first[0], swap_odd[0]), 'dwdown': scatter_after_dact('odd', first, swap_odd)})
    arrived(made['odd'], carried['dwdown'])
    swap_a1 = swap(ffn_grads('a', 1))

    def a1_scatter(idx):
        def make(known):
            if 'a1' not in made:
                made['a1'] = core_sums_of(swap_a1, known["carried"]["dact"])
            made[('a1', idx)] = scatter(made['a1'][idx:idx + 1])
            return made[('a1', idx)][0]
        return make

    dx, fg[('b', 0)], carried = _ffn_bwd(dx, ffn_b0, *ffn_args('b', 0), couriers={
        'dact': swap_a1[0], 'dwdown': a1_scatter(0), 'dh': a1_scatter(1), 'dwup': a1_scatter(2)})
    for idx, name in enumerate(('dwdown', 'dh', 'dwup')):
        arrived(made[('a1', idx)], carried[name])
    swap_b0 = swap(ffn_grads('b', 0))
    dy_e, grads['even_post_g'] = _rms_bwd(y_e, w['even_post_g'], dx, 1.0, MXU_DTYPE, tr)
    d_cat, res = _mm_nt("mm_dcat", [(dy_e, w_out_e, 0)], F32, tm, courier=swap_b0[0])
    pend = core_sums_of(swap_b0, res)
    (g_w_out_e,) = _mm_tn("mm_dw_out", cat, [dy_e], MXU_DTYPE)
    dqn, dqp, dkv, dkr = _attn_bwd(q_all, kv, p, cos, sin, d_cat, batch, seq)
    hq = HEADS * LANES
    d_latq = _mm_nt("mm_dlat_q", [(dqn, w_uq, 0), (dqp, w_uq, hq)], F32, tm, tn_want=Q_LORA)
    d_latkv = _mm_nt("mm_dlat_kv", [(dkv, w_ukv, 0)], F32, tm, tn_want=KV_LORA)
    g_uq_n, g_uq_r = _mm_tn("mm_dw_uq", lat, [dqn, dqp], MXU_DTYPE, a_col0=0, m_width=Q_LORA)
    (g_ukv,) = _mm_tn("mm_dw_ukv", lat, [dkv], MXU_DTYPE, a_col0=Q_LORA, m_width=KV_LORA)

    def latent_bwd(r, c):
        _, vjp_q = jax.vjp(_rmsnorm, r[0], c[0])
        _, vjp_kv = jax.vjp(_rmsnorm, r[1], c[1])
        dq, dqg = vjp_q(r[2])
        dk, dkg = vjp_kv(r[3])
        return [jnp.concatenate([dq, dk], axis=1)], [dqg, dkg]

    d_lat, grads['even_q_norm_g'], grads['even_kv_norm_g'] = _rows(
        "latent_norm_bwd", latent_bwd, [(p, Q_LORA, 0), (p, KV_LORA, 1), (d_latq, Q_LORA, 0), (d_latkv, KV_LORA, 0)],
        [w['even_q_norm_g'], w['even_kv_norm_g']], [(Q_LORA + KV_LORA, MXU_DTYPE)], [(1, Q_LORA), (1, KV_LORA)], tr)
    da, dgate, g_conv_w, grads['even_conv_b'], grads['even_conv_norm_g'], grads['even_conv_norm_b'] = _conv_bwd(
        p, d_cat, full['even_conv_w'][0], w['even_conv_b'], w['even_conv_norm_g'], w['even_conv_norm_b'], batch, seq)
    dp = jnp.concatenate([d_lat, da, dgate, dkr.astype(MXU_DTYPE)], axis=1)
    order = scatter(pend[0:1])
    dh_e, res = _mm_nt("mm_dh_even", [(dp, w_in, 0)], F32, tm, tk_want=640, courier=order[0])
    arrived(order, res)
    order = scatter(pend[1:2])
    (g_w_in,), res = _mm_tn("mm_dw_in", h_e, [dp], MXU_DTYPE, tn_want=640, courier=order[0])
    arrived(order, res)
    swap_even = swap([(('even_w_in', 0), _cols_to_chips(_w_in_layout_inv(g_w_in))),
                      (('even_w_uq', 0), _cols_to_chips(_w_uq_layout_inv(g_uq_n, g_uq_r))),
                      (('even_w_ukv', 0), _cols_to_chips(g_ukv)),
                      (('even_w_out', 0), g_w_out_e.reshape((N_CHIPS,) + w['even_w_out'].shape[1:]))])
    dx, grads['even_pre_g'] = _rms_bwd(x1, w['even_pre_g'], dh_e, 1.0, F32, tr, resid=dx)
    first = scatter(pend[2:3])

    def after_w_down(known):
        made['a0_down'] = scatter(core_sums([(ffn_keys('a', 0)[2], known["dwd"])]))
        return made['a0_down'][0]

    def after_w_up(known):
        made['a0_up'] = scatter(core_sums(list(zip(ffn_keys('a', 0)[:2], (known["dwg"], known["dwu"])))))
        return made['a0_up'][0]

    dx, fg[('a', 0)], carried = _ffn_bwd(dx, ffn_a0, *ffn_args('a', 0), weights_first=True, couriers={
        'dact': _Both(first[0], swap_even[0]), 'dwdown': scatter_after_dact('even', first, swap_even),
        'dwup': after_w_down, 'dh': after_w_up})
    arrived(made['even'], carried['dwdown'])
    arrived(made['a0_down'], carried['dwup'])
    arrived(made['a0_up'], carried['dh'])
    grad_x = dx.reshape(batch, seq, d)

    reduced = []
    for n in BIG:
        layers = w[n].shape[0]
        buf = None
        for layer in range(layers):
            pt, r = received[(n, layer)]
            buf = _add_chips(pt, r, place, layer, layers, into=buf)
        reduced.append(buf)
    joined = _join_halves(reduced)
    for n, g in zip(BIG, joined):
        grads[n] = g.reshape(w[n].shape)

    for tag in ('a', 'b'):
        grads[f'ffn_{tag}_pre_g'] = jnp.concatenate([fg[(tag, 0)][0], fg[(tag, 1)][0]], axis=0)
        grads[f'ffn_{tag}_post_g'] = jnp.concatenate([fg[(tag, 0)][1], fg[(tag, 1)][1]], axis=0)
    grads['even_conv_w'] = g_conv_w[None]
    packed, rngs = _pack([grads[n] for n in SMALL])
    device = 2 * chip + lax.axis_index("c")
    slots = lax.dynamic_update_slice_in_dim(jnp.zeros((N_DEV,) + packed.shape, F32), packed[None], device, axis=0)
    summed = _add_devices(_gather_devices(slots))
    for n, rng in zip(SMALL, rngs):
        g = _unpack(summed, rng, grads[n].shape)
        if n in SMALL_SHARDED:
            width = w[n].shape[-1]
            g = lax.dynamic_slice_in_dim(g, chip * width, width, axis=g.ndim - 1)
        grads[n] = g

    delta, new_m, new_v = {}, {}, {}
    for n in BIG:
        delta[n], new_m[n], new_v[n] = _adamw(w[n], grads[n], given['m_' + n], given['v_' + n])
    packs = [_pack([src[n] for n in SMALL])[0][None] for src in
             (w, grads, {n: given['m_' + n] for n in SMALL}, {n: given['v_' + n] for n in SMALL})]
    _, rngs = _pack([w[n] for n in SMALL])
    small_out = _adamw(*packs)
    for n, rng in zip(SMALL, rngs):
        delta[n], new_m[n], new_v[n] = (_unpack(o[0], rng, w[n].shape) for o in small_out)

    return (loss, grad_x, *[grads[n] for n in WEIGHTS], *[delta[n] for n in WEIGHTS],
            *[new_m[n] for n in WEIGHTS], *[new_v[n] for n in WEIGHTS])
```

```python
import functools

import jax
import jax.numpy as jnp
from jax import lax
from jax.experimental import pallas as pl
from jax.experimental.pallas import tpu as pltpu

F32 = jnp.float32
BF16 = jnp.bfloat16
MXU_DTYPE = BF16
MESH = pl.DeviceIdType.MESH
VMEM_LIMIT_BYTES = 56 * 1024 * 1024
LANES = 128
TN_ROWS = 2048
STREAM_TILE_ELEMS = 512 * 1408

D_MODEL = 2048
D_FF = 5632
EPS = 1e-6
HEADS = 8
V_HEAD = 128
QK_NOPE = 128
QK_ROPE = 64
Q_LORA = 512
KV_LORA = 512
ROPE_THETA = 10000.0
CONV_CH = 1024
CONV_WIDTH = 31
GROUPS = 8
CHUNK = 128
GM_WIDTH = 2048
ADAM_LR = 0.001
ADAM_B1 = 0.9
ADAM_B2 = 0.999
ADAM_EPS = 1e-08
ADAM_WD = 0.01
ADAM_STEP = 10

N_CHIPS = 4
N_DEV = 8
P_KV = Q_LORA
P_A = Q_LORA + KV_LORA
P_GATE = P_A + CONV_CH
P_KR = P_GATE + CONV_CH
P_WIDTH = P_KR + LANES
ROPE_HALF = QK_ROPE // 2

WEIGHTS = ['ffn_a_pre_g', 'ffn_a_post_g', 'ffn_a_w_gate', 'ffn_a_w_up', 'ffn_a_w_down', 'ffn_b_pre_g', 'ffn_b_post_g',
           'ffn_b_w_gate', 'ffn_b_w_up', 'ffn_b_w_down', 'even_pre_g', 'even_post_g', 'even_w_in', 'even_q_norm_g',
           'even_kv_norm_g', 'even_w_uq', 'even_w_ukv', 'even_conv_w', 'even_conv_b', 'even_conv_norm_g',
           'even_conv_norm_b', 'even_w_out', 'odd_pre_g', 'odd_post_g', 'odd_w_in', 'odd_v_norm_g', 'odd_v_norm_b',
           'odd_w_s', 'odd_b_s', 'odd_w_out']
BIG = ['ffn_a_w_gate', 'ffn_a_w_up', 'ffn_a_w_down', 'ffn_b_w_gate', 'ffn_b_w_up', 'ffn_b_w_down', 'even_w_in',
       'even_w_uq', 'even_w_ukv', 'even_w_out', 'odd_w_in', 'odd_w_out']
SMALL = [n for n in WEIGHTS if n not in BIG]
SMALL_SHARDED = ['even_conv_w', 'odd_pre_g', 'odd_post_g', 'odd_v_norm_g', 'odd_v_norm_b']


def _call(name, body, grid, in_specs, out_specs, out_shape, scratch=(), prefetch=0, aliases=None):
    params = pltpu.CompilerParams(dimension_semantics=("arbitrary",) * len(grid), vmem_limit_bytes=VMEM_LIMIT_BYTES)
    if prefetch:
        spec = pltpu.PrefetchScalarGridSpec(num_scalar_prefetch=prefetch, grid=grid, in_specs=in_specs,
                                            out_specs=out_specs, scratch_shapes=list(scratch))
        return pl.pallas_call(body, grid_spec=spec, out_shape=out_shape, compiler_params=params, name=name,
                              input_output_aliases=aliases or {})
    return pl.pallas_call(body, grid=grid, in_specs=in_specs, out_specs=out_specs, out_shape=out_shape,
                          scratch_shapes=list(scratch), compiler_params=params, name=name,
                          input_output_aliases=aliases or {})


def _spec(block, index_map):
    return pl.BlockSpec(block, index_map)


def _sds(shape, dtype):
    return jax.ShapeDtypeStruct(tuple(shape), dtype)


def _tile(n, want, mult=8):
    if n <= want:
        return n
    best = None
    for t in range(mult, want + 1, mult):
        if n % t == 0:
            best = t
    assert best is not None, (n, want, mult)
    return best


_DIMS = {"nn": (((1,), (0,)), ((), ())), "nt": (((1,), (1,)), ((), ())), "tn": (((0,), (0,)), ((), ()))}


def _dot(a, b, mode="nn"):
    return lax.dot_general(a.astype(MXU_DTYPE), b.astype(MXU_DTYPE), _DIMS[mode], preferred_element_type=F32)


def _matmul(name, mode, grid, groups, outs, acc_shape, extras=(), epilogue=None, courier=None, sum_outs=0):
    flat, specs = [], []
    for grp in groups:
        for a, a_spec, b, b_spec in grp:
            flat += [a, b]
            specs += [a_spec, b_spec]
    for e, e_spec in extras:
        flat.append(e)
        specs.append(e_spec)
    n_pairs = [len(g) for g in groups]
    one_step = grid[2] == 1
    n_in, n_ex, n_out, n_acc = 2 * sum(n_pairs), len(extras), len(outs), 0 if one_step else len(groups)
    last = tuple(g - 1 for g in grid)
    c_arrays = list(courier.arrays) if courier else []
    c_shapes = list(courier.out_shapes) if courier else []
    n_ci, n_co = len(c_arrays), len(c_shapes)
    any_spec = pl.BlockSpec(memory_space=pl.ANY)

    def body(*refs):
        ins, ex = refs[:n_in], refs[n_in:n_in + n_ex]
        pos = n_in + n_ex
        c_in = refs[pos:pos + n_ci]
        out_refs = refs[pos + n_ci:pos + n_ci + n_out]
        c_out = refs[pos + n_ci + n_out:pos + n_ci + n_out + n_co]
        accs = refs[pos + n_ci + n_out + n_co:pos + n_ci + n_out + n_co + n_acc]
        sems = refs[pos + n_ci + n_out + n_co + n_acc:]
        i, j, k = pl.program_id(0), pl.program_id(1), pl.program_id(2)

        if courier:
            @pl.when((i == 0) & (j == 0) & (k == 0))
            def _():
                courier.start(c_in, c_out, sems)

        def products():
            pos, totals = 0, []
            for n in n_pairs:
                total = None
                for _ in range(n):
                    d = _dot(ins[pos][...], ins[pos + 1][...], mode)
                    total = d if total is None else total + d
                    pos += 2
                totals.append(total)
            return totals

        summed = out_refs[n_out - sum_outs:] if sum_outs else ()
        if summed:
            @pl.when((i == 0) & (j == 0) & (k == 0))
            def _():
                for o in summed:
                    o[...] = jnp.zeros(o.shape, F32)

        def finish(vals):
            res = epilogue(vals, [e[...] for e in ex]) if epilogue else vals
            for idx, (o, r) in enumerate(zip(out_refs, res)):
                if idx >= n_out - sum_outs:
                    o[...] += r
                else:
                    o[...] = r.astype(o.dtype)

        if one_step:
            finish(products())
        else:
            @pl.when(k == 0)
            def _():
                for acc in accs:
                    acc[...] = jnp.zeros(acc.shape, F32)

            for acc, total in zip(accs, products()):
                acc[...] += total

            @pl.when(k == last[2])
            def _():
                finish([acc[...] for acc in accs])

        if courier:
            @pl.when((i == last[0]) & (j == last[1]) & (k == last[2]))
            def _():
                courier.finish(c_in, c_out, sems)

    aliases = {n_in + n_ex + t: n_out + t for t in range(n_ci)} if courier and courier.aliased else None
    scratch = [pltpu.VMEM(acc_shape, F32)] * n_acc
    scratch += [pltpu.SemaphoreType.DMA((n,)) for n in courier.sems] if courier else []
    res = _call(name, body, grid, specs + [any_spec] * n_ci, [o[2] for o in outs] + [any_spec] * n_co,
                [_sds(o[0], o[1]) for o in outs] + c_shapes, scratch=scratch, aliases=aliases)(*flat, *c_arrays)
    return list(res[:n_out]), list(res[n_out:])


def _rows(name, fn, rows, consts, outs, accs, tm):
    t_rows = rows[0][0].shape[0]
    grid = (t_rows // tm,)
    in_specs = [_spec((tm, bw), functools.partial(lambda i, cb: (i, cb), cb=cb)) for _, bw, cb in rows]
    in_specs += [_spec(c.shape, functools.partial(lambda i, nd: (0,) * nd, nd=c.ndim)) for c in consts]
    out_shape = [_sds((t_rows, w), dt) for w, dt in outs] + [_sds(s, F32) for s in accs]
    out_specs = [_spec((tm, w), lambda i: (i, 0)) for w, _ in outs]
    out_specs += [_spec(s, functools.partial(lambda i, nd: (0,) * nd, nd=len(s))) for s in accs]
    nr, nc, no = len(rows), len(consts), len(outs)

    def body(*refs):
        r = [ref[...] for ref in refs[:nr]]
        c = [ref[...] for ref in refs[nr:nr + nc]]
        o_refs, a_refs = refs[nr + nc:nr + nc + no], refs[nr + nc + no:]
        o_vals, a_vals = fn(r, c)
        for ref, v in zip(o_refs, o_vals):
            ref[...] = v.astype(ref.dtype)
        if a_refs:
            @pl.when(pl.program_id(0) == 0)
            def _():
                for ref in a_refs:
                    ref[...] = jnp.zeros(ref.shape, F32)

            for ref, v in zip(a_refs, a_vals):
                ref[...] += v

    return _call(name, body, grid, in_specs, out_specs, out_shape)(*[a for a, _, _ in rows], *consts)


def _rmsnorm(x, g):
    return x * lax.rsqrt(jnp.mean(x * x, axis=-1, keepdims=True) + EPS) * g


def _layernorm(x, g, b):
    mu = jnp.mean(x, axis=-1, keepdims=True)
    var = jnp.mean(jnp.square(x - mu), axis=-1, keepdims=True)
    return (x - mu) * lax.rsqrt(var + EPS) * g + b


def _swiglu_act(g, u):
    return jax.nn.silu(g) * u


def _rope(x, cos, sin_signed):
    return x * cos + pltpu.roll(x, 2 * ROPE_HALF, 1) * sin_signed


def _rope_transposed(dy, cos, sin_signed):
    return dy * cos - pltpu.roll(dy, 2 * ROPE_HALF, 1) * sin_signed


def _rms_fwd(x, g, tm):
    d = x.shape[1]
    return _rows("rms_fwd", lambda r, c: ([_rmsnorm(r[0], c[0])], []), [(x, d, 0)], [g], [(d, MXU_DTYPE)], [], tm)[0]


def _rms_residual(x, y, g, scale, tm):
    d = x.shape[1]
    return _rows("rms_residual", lambda r, c: ([r[0] + scale * _rmsnorm(r[1], c[0])], []),
                 [(x, d, 0), (y, d, 0)], [g], [(d, F32)], [], tm)[0]


def _rms_bwd(y, g, dout, scale, out_dtype, tm, resid=None):
    d = y.shape[1]

    def fn(r, c):
        _, vjp = jax.vjp(_rmsnorm, r[0], c[0])
        dy, dg = vjp(scale * r[1].astype(F32))
        if resid is not None:
            dy = dy + r[2]
        return [dy], [dg]

    rows = [(y, d, 0), (dout, d, 0)] + ([(resid, d, 0)] if resid is not None else [])
    return _rows("rms_bwd" if resid is None else "rms_bwd_resid", fn, rows, [g], [(d, out_dtype)], [(1, d)], tm)


def _loss_head(y, target, tm):
    d = y.shape[1]

    def fn(r, c):
        err = r[0] - r[1]
        sq = jnp.sum(jnp.sum(err * err, axis=1, keepdims=True), axis=0, keepdims=True)
        return [err * (1.0 / d)], [jnp.broadcast_to(sq, (1, LANES))]

    return _rows("loss_head", fn, [(y, d, 0), (target, d, 0)], [], [(d, F32)], [(1, LANES)], tm)


def _ffn_fwd(x, pre_g, post_g, wg, wu, wd, layer, tm, tr, couriers=None, wd_landed=None, h=None, next_g=None):
    t, d = x.shape
    fs = wg.shape[3]
    if h is None:
        h = _rms_fwd(x, pre_g, tr)
    h_spec = _spec((tm, d), lambda j, i, k: (i, 0))
    w_spec = _spec((None, None, d, fs), lambda j, i, k: (j, layer, 0, 0))
    o_spec = _spec((tm, fs), lambda j, i, k: (i, j))
    couriers = couriers or {}
    carried = {}
    (g, u, a), carried["up"] = _matmul(
        "ffn_up", "nn", (N_CHIPS, t // tm, 1), [[(h, h_spec, wg, w_spec)], [(h, h_spec, wu, w_spec)]],
        [((t, N_CHIPS * fs), MXU_DTYPE, o_spec)] * 3, (tm, fs),
        epilogue=lambda accs, _: (accs[0], accs[1], _swiglu_act(accs[0], accs[1])), courier=couriers.get("up"))
    if wd_landed is not None:
        wd = carried["up"][wd_landed].reshape(wd.shape)
    row_spec = _spec((tm, d), lambda i, j, k: (i, 0))
    vec_spec = _spec((1, d), lambda i, j, k: (0, 0))

    def post(accs, ex):
        out = ex[0] + 0.5 * _rmsnorm(accs[0], ex[1])
        return (accs[0], out) + ((_rmsnorm(out, ex[2]),) if next_g is not None else ())

    res, carried["down"] = _matmul(
        "ffn_down", "nn", (t // tm, 1, N_CHIPS),
        [[(a, _spec((tm, fs), lambda i, j, k: (i, k)), wd, _spec((None, None, fs, d), lambda i, j, k: (k, layer, 0, 0)))]],
        [((t, d), F32, row_spec)] * 2 + ([((t, d), MXU_DTYPE, row_spec)] if next_g is not None else []), (tm, d),
        extras=[(x, row_spec), (post_g, vec_spec)] + ([(next_g, vec_spec)] if next_g is not None else []),
        epilogue=post, courier=couriers.get("down"))
    y, out = res[0], res[1]
    return out, (x, h, g, u, a, y), carried, (res[2] if next_g is not None else None)


def _ffn_bwd(dout, saved, pre_g, post_g, wg, wu, wd, layer, tm, tr, couriers=None, weights_first=False):
    x, h, g, u, a, y = saved
    t, d = x.shape
    fs = wg.shape[3]
    tn = _tile(d, 1024, LANES)
    tkt = _tile(t, TN_ROWS, 16)
    dy, dpost = _rms_bwd(y, post_g, dout, 0.5, MXU_DTYPE, tr)

    def act_bwd(accs, ex):
        _, vjp = jax.vjp(_swiglu_act, ex[0].astype(F32), ex[1].astype(F32))
        return vjp(accs[0])

    couriers = couriers or {}
    carried = {}
    known = {"carried": carried}

    def late(name):
        c = couriers.get(name)
        return c(known) if callable(c) else c

    gu_spec = _spec((tm, fs), lambda j, i, k: (i, j))
    (dg, du), carried["dact"] = _matmul(
        "ffn_dact", "nt", (N_CHIPS, t // tm, 1),
        [[(dy, _spec((tm, d), lambda j, i, k: (i, 0)), wd, _spec((None, None, fs, d), lambda j, i, k: (j, layer, 0, 0)))]],
        [((t, N_CHIPS * fs), MXU_DTYPE, gu_spec)] * 2, (tm, fs),
        extras=[(g, gu_spec), (u, gu_spec)], epilogue=act_bwd, courier=couriers.get("dact"))
    (dwd,), carried["dwdown"] = _matmul(
        "ffn_dwdown", "tn", (N_CHIPS, d // tn, t // tkt),
        [[(a, _spec((tkt, fs), lambda i, j, k: (k, i)), dy, _spec((tkt, tn), lambda i, j, k: (k, j)))]],
        [((N_CHIPS, fs, d), MXU_DTYPE, _spec((None, fs, tn), lambda i, j, k: (i, 0, j)))], (fs, tn),
        courier=late("dwdown"))

    def d_hidden(courier):
        da_spec = _spec((tm, fs), lambda i, j, k: (i, k))
        wt_spec = _spec((None, None, d, fs), lambda i, j, k: (k, layer, 0, 0))
        return _matmul("ffn_dh", "nt", (t // tm, 1, N_CHIPS), [[(dg, da_spec, wg, wt_spec), (du, da_spec, wu, wt_spec)]],
                       [((t, d), F32, _spec((tm, d), lambda i, j, k: (i, 0)))], (tm, d), courier=courier)

    def d_w_up(courier):
        tmw = _tile(d, 512, LANES)
        h_spec = _spec((tkt, tmw), lambda i, j, k: (k, i))
        dgu_spec = _spec((tkt, fs), lambda i, j, k: (k, j))
        dw_spec = _spec((None, tmw, fs), lambda i, j, k: (j, i, 0))
        return _matmul("ffn_dwup", "tn", (d // tmw, N_CHIPS, t // tkt), [[(h, h_spec, dg, dgu_spec)], [(h, h_spec, du, dgu_spec)]],
                       [((N_CHIPS, d, fs), MXU_DTYPE, dw_spec)] * 2, (tmw, fs), courier=courier)

    known["dwd"] = dwd
    if weights_first:
        (dwg, dwu), carried["dwup"] = d_w_up(late("dwup"))
        known["dwg"], known["dwu"] = dwg, dwu
        (dh,), carried["dh"] = d_hidden(late("dh"))
    else:
        (dh,), carried["dh"] = d_hidden(late("dh"))
        (dwg, dwu), carried["dwup"] = d_w_up(late("dwup"))
    dx, dpre = _rms_bwd(x, pre_g, dh, 1.0, F32, tr, resid=dout)
    return dx, (dpre, dpost, dwg, dwu, dwd), carried


def _attn_scores(qn, qpe, kn, kpe, qi, tq, seq):
    s = (_dot(qn, kn, "nt") + _dot(qpe, kpe, "nt")) * ((QK_NOPE + QK_ROPE) ** -0.5)
    rows = qi * tq + lax.broadcasted_iota(jnp.int32, (tq, seq), 0)
    cols = lax.broadcasted_iota(jnp.int32, (tq, seq), 1)
    s = jnp.where(cols <= rows, s, -jnp.inf)
    e = jnp.exp(s - jnp.max(s, axis=1, keepdims=True))
    return e / jnp.sum(e, axis=1, keepdims=True)


KEY_EXTENTS = 8


def _for_key_extent(qi, tq, seq, attend):
    n = KEY_EXTENTS if seq % (KEY_EXTENTS * tq) == 0 else 1
    ext = seq // n
    mine = (qi * tq) // ext
    for e in range(n):
        pl.when(mine == e)(functools.partial(attend, (e + 1) * ext))


def _attn_specs(nq, tq, seq):
    q_rows = lambda b, h, qi: b * nq + qi
    return [
        _spec((tq, LANES), lambda b, h, qi: (q_rows(b, h, qi), h)),
        _spec((tq, LANES), lambda b, h, qi: (q_rows(b, h, qi), HEADS + h)),
        _spec((seq, 2 * LANES), lambda b, h, qi: (b, h)),
        _spec((seq, LANES), lambda b, h, qi: (b, P_KR // LANES)),
        _spec((tq, LANES), lambda b, h, qi: (q_rows(b, h, qi), 0)),
        _spec((tq, LANES), lambda b, h, qi: (q_rows(b, h, qi), 0)),
        _spec((seq, LANES), lambda b, h, qi: (b, 0)),
        _spec((seq, LANES), lambda b, h, qi: (b, 0)),
    ]


def _attn_fwd(q_all, kv, p, cos, sin, batch, seq):
    t = q_all.shape[0]
    tq = _tile(seq, 256, 16)
    nq = seq // tq

    def body(qn_ref, qp_ref, kv_ref, kr_ref, cq, sq, ck, sk, o_ref):
        qi = pl.program_id(2)
        qpe = _rope(qp_ref[...], cq[...], sq[...])

        def attend(keys):
            kpe = _rope(kr_ref[:keys, :], ck[:keys, :], sk[:keys, :])
            prob = _attn_scores(qn_ref[...], qpe, kv_ref[:keys, :LANES], kpe, qi, tq, keys)
            o_ref[...] = _dot(prob, kv_ref[:keys, LANES:]).astype(o_ref.dtype)

        _for_key_extent(qi, tq, seq, attend)

    return _call("attn_fwd", body, (batch, HEADS, nq), _attn_specs(nq, tq, seq),
                 _spec((tq, LANES), lambda b, h, qi: (b * nq + qi, h)), _sds((t, 2 * HEADS * V_HEAD), MXU_DTYPE)
                 )(q_all, q_all, kv, p, cos, sin, cos, sin)


def _attn_bwd(q_all, kv, p, cos, sin, d_cat, batch, seq):
    t = q_all.shape[0]
    tq = _tile(seq, 256, 16)
    nq = seq // tq

    def body(qn_ref, qp_ref, kv_ref, kr_ref, cq, sq, ck, sk, do_ref, dqn_ref, dqp_ref, dkv_ref, dkr_ref, dkpe_acc):
        h, qi = pl.program_id(1), pl.program_id(2)
        qn = qn_ref[...]
        qpe = _rope(qp_ref[...], cq[...], sq[...])
        do = do_ref[...]

        @pl.when(qi == 0)
        def _():
            dkv_ref[...] = jnp.zeros(dkv_ref.shape, F32)

        @pl.when((h == 0) & (qi == 0))
        def _():
            dkpe_acc[...] = jnp.zeros(dkpe_acc.shape, F32)

        def attend(keys):
            kn, v = kv_ref[:keys, :LANES], kv_ref[:keys, LANES:]
            kpe = _rope(kr_ref[:keys, :], ck[:keys, :], sk[:keys, :])
            prob = _attn_scores(qn, qpe, kn, kpe, qi, tq, keys)
            dprob = _dot(do, v, "nt")
            ds = prob * (dprob - jnp.sum(prob * dprob, axis=1, keepdims=True)) * ((QK_NOPE + QK_ROPE) ** -0.5)
            dqn_ref[...] = _dot(ds, kn)
            dqp_ref[...] = _rope_transposed(_dot(ds, kpe), cq[...], sq[...])
            dkv_ref[:keys, :LANES] += _dot(ds, qn, "tn")
            dkv_ref[:keys, LANES:] += _dot(prob, do, "tn")
            dkpe_acc[:keys, :] += _dot(ds, qpe, "tn")

        _for_key_extent(qi, tq, seq, attend)

        @pl.when((h == HEADS - 1) & (qi == nq - 1))
        def _():
            dkr_ref[...] = _rope_transposed(dkpe_acc[...], ck[...], sk[...])

    q_out = _spec((tq, LANES), lambda b, h, qi: (b * nq + qi, h))
    return _call("attn_bwd", body, (batch, HEADS, nq),
                 _attn_specs(nq, tq, seq) + [_spec((tq, LANES), lambda b, h, qi: (b * nq + qi, h))],
                 [q_out, q_out, _spec((seq, 2 * LANES), lambda b, h, qi: (b, h)), _spec((seq, LANES), lambda b, h, qi: (b, 0))],
                 [_sds((t, HEADS * LANES), F32), _sds((t, HEADS * LANES), F32), _sds((t, HEADS * 2 * LANES), F32), _sds((t, LANES), F32)],
                 scratch=[pltpu.VMEM((seq, LANES), F32)])(q_all, q_all, kv, p, cos, sin, cos, sin, d_cat)


CONV_PAD = 32


def _conv_taps(w_ref, src_ref, first_row, n_rows, init, offset):
    acc = init
    for k in range(CONV_WIDTH):
        acc = acc + w_ref[k:k + 1, :] * src_ref[pl.ds(first_row + offset(k), n_rows), :]
    return acc


def _norm_act(conv, g, b):
    return jax.nn.silu(_layernorm(conv, g, b))


def _conv_fwd(p, cat, conv_w, conv_b, norm_g, norm_b, batch, seq):
    rc = _tile(seq, 256, 8)
    a_blk, gate_blk, out_blk = P_A // LANES, P_GATE // LANES, (HEADS * V_HEAD) // LANES

    def body(a_ref, gate_ref, w_ref, cb_ref, ng_ref, nb_ref, cat_in, o_ref, zp):
        del cat_in
        zp[pl.ds(0, CONV_PAD), :] = jnp.zeros((CONV_PAD, LANES), F32)
        zp[pl.ds(CONV_PAD, seq), :] = a_ref[...] * jax.nn.sigmoid(gate_ref[...])
        for r0 in range(0, seq, rc):
            conv = _conv_taps(w_ref, zp, r0, rc, jnp.broadcast_to(cb_ref[...], (rc, LANES)),
                              lambda k: CONV_PAD - (CONV_WIDTH - 1) + k)
            o_ref[pl.ds(r0, rc), :] = _norm_act(conv, ng_ref[...], nb_ref[...]).astype(o_ref.dtype)

    vec = _spec((1, LANES), lambda b, g: (0, g))
    return _call("conv_fwd", body, (batch, GROUPS),
                 [_spec((seq, LANES), lambda b, g: (b, a_blk + g)), _spec((seq, LANES), lambda b, g: (b, gate_blk + g)),
                  _spec((CONV_WIDTH, LANES), lambda b, g: (0, g)), vec, vec, vec, pl.BlockSpec(memory_space=pl.ANY)],
                 _spec((seq, LANES), lambda b, g: (b, out_blk + g)), _sds(cat.shape, cat.dtype),
                 scratch=[pltpu.VMEM((seq + CONV_PAD, LANES), F32)], aliases={6: 0}
                 )(p, p, conv_w, conv_b, norm_g, norm_b, cat)


def _conv_bwd(p, d_cat, conv_w, conv_b, norm_g, norm_b, batch, seq):
    t = p.shape[0]
    rc = _tile(seq, 256, 8)
    a_blk, gate_blk, out_blk = P_A // LANES, P_GATE // LANES, (HEADS * V_HEAD) // LANES

    def body(a_ref, gate_ref, w_ref, cb_ref, ng_ref, nb_ref, dc_ref, da_ref, dgate_ref, dw_ref, db_ref, dng_ref, dnb_ref,
             zp, dcp, buf):
        b = pl.program_id(1)
        a, sig = a_ref[...], jax.nn.sigmoid(gate_ref[...])
        zp[pl.ds(0, CONV_PAD), :] = jnp.zeros((CONV_PAD, LANES), F32)
        zp[pl.ds(CONV_PAD, seq), :] = a * sig
        for r0 in range(0, seq, rc):
            buf[pl.ds(r0, rc), :] = _conv_taps(w_ref, zp, r0, rc, jnp.broadcast_to(cb_ref[...], (rc, LANES)),
                                               lambda k: CONV_PAD - (CONV_WIDTH - 1) + k)
        _, vjp = jax.vjp(_norm_act, buf[...], ng_ref[...], nb_ref[...])
        dconv, dng, dnb = vjp(dc_ref[...].astype(F32))
        dcp[pl.ds(0, seq), :] = dconv
        dcp[pl.ds(seq, CONV_PAD), :] = jnp.zeros((CONV_PAD, LANES), F32)

        @pl.when(b == 0)
        def _():
            dw_ref[...] = jnp.zeros(dw_ref.shape, F32)
            db_ref[...] = jnp.zeros(db_ref.shape, F32)
            dng_ref[...] = jnp.zeros(dng_ref.shape, F32)
            dnb_ref[...] = jnp.zeros(dnb_ref.shape, F32)

        db_ref[...] += jnp.sum(dconv, axis=0, keepdims=True)
        dng_ref[...] += dng
        dnb_ref[...] += dnb
        for k in range(CONV_WIDTH):
            shifted = zp[pl.ds(CONV_PAD - (CONV_WIDTH - 1) + k, seq), :]
            dw_ref[k:k + 1, :] += jnp.sum(dconv * shifted, axis=0, keepdims=True)
        for r0 in range(0, seq, rc):
            buf[pl.ds(r0, rc), :] = _conv_taps(w_ref, dcp, r0, rc, jnp.zeros((rc, LANES), F32),
                                               lambda k: CONV_WIDTH - 1 - k)
        dz = buf[...]
        da_ref[...] = (dz * sig).astype(da_ref.dtype)
        dgate_ref[...] = (dz * a * sig * (1.0 - sig)).astype(dgate_ref.dtype)

    vec = _spec((1, LANES), lambda g, b: (0, g))
    row_out = _spec((seq, LANES), lambda g, b: (b, g))
    return _call("conv_bwd", body, (GROUPS, batch),
                 [_spec((seq, LANES), lambda g, b: (b, a_blk + g)), _spec((seq, LANES), lambda g, b: (b, gate_blk + g)),
                  _spec((CONV_WIDTH, LANES), lambda g, b: (0, g)), vec, vec, vec,
                  _spec((seq, LANES), lambda g, b: (b, out_blk + g))],
                 [row_out, row_out, _spec((CONV_WIDTH, LANES), lambda g, b: (0, g)), vec, vec, vec],
                 [_sds((t, CONV_CH), MXU_DTYPE), _sds((t, CONV_CH), MXU_DTYPE), _sds((CONV_WIDTH, CONV_CH), F32),
                  _sds((1, CONV_CH), F32), _sds((1, CONV_CH), F32), _sds((1, CONV_CH), F32)],
                 scratch=[pltpu.VMEM((seq + CONV_PAD, LANES), F32), pltpu.VMEM((seq + CONV_PAD, LANES), F32),
                          pltpu.VMEM((seq, LANES), F32)])(p, p, conv_w, conv_b, norm_g, norm_b, d_cat)


def _sgu_pre(hw, g, b):
    z = jax.nn.gelu(hw)
    return z[:, :GM_WIDTH], _layernorm(z[:, GM_WIDTH:], g, b)


def _causal(w):
    keep = lax.broadcasted_iota(jnp.int32, (CHUNK, CHUNK), 0) >= lax.broadcasted_iota(jnp.int32, (CHUNK, CHUNK), 1)
    return jnp.where(keep, w, 0.0)


def _sgu_fwd(hw, vg, vb, w_s, b_st):
    t = hw.shape[0]
    gw = GM_WIDTH // GROUPS

    def body(hw_ref, vg_ref, vb_ref, w_ref, b_ref, o_ref):
        u, v = _sgu_pre(hw_ref[...], vg_ref[...], vb_ref[...])
        for g in range(GROUPS):
            cols = slice(g * gw, (g + 1) * gw)
            s = _dot(_causal(w_ref[g]), v[:, cols]) + b_ref[:, g:g + 1]
            o_ref[:, cols] = (u[:, cols] * s).astype(o_ref.dtype)

    return _call("sgu_fwd", body, (t // CHUNK,),
                 [_spec((CHUNK, 2 * GM_WIDTH), lambda i: (i, 0)), _spec((1, GM_WIDTH), lambda i: (0, 0)),
                  _spec((1, GM_WIDTH), lambda i: (0, 0)), _spec((GROUPS, CHUNK, CHUNK), lambda i: (0, 0, 0)),
                  _spec((CHUNK, GROUPS), lambda i: (0, 0))],
                 _spec((CHUNK, GM_WIDTH), lambda i: (i, 0)), _sds((t, GM_WIDTH), MXU_DTYPE))(hw, vg, vb, w_s, b_st)


def _sgu_bwd(hw, vg, vb, w_s, b_st, dout):
    t = hw.shape[0]
    gw = GM_WIDTH // GROUPS

    def body(hw_ref, vg_ref, vb_ref, w_ref, b_ref, do_ref, dhw_ref, dvg_ref, dvb_ref, dw_ref, db_ref, du_buf, dv_buf):
        (u, v), vjp = jax.vjp(_sgu_pre, hw_ref[...], vg_ref[...], vb_ref[...])

        @pl.when(pl.program_id(0) == 0)
        def _():
            dvg_ref[...] = jnp.zeros(dvg_ref.shape, F32)
            dvb_ref[...] = jnp.zeros(dvb_ref.shape, F32)
            dw_ref[...] = jnp.zeros(dw_ref.shape, F32)
            db_ref[...] = jnp.zeros(db_ref.shape, F32)

        lane = lax.broadcasted_iota(jnp.int32, (CHUNK, LANES), 1)
        db = jnp.zeros((CHUNK, LANES), F32)
        for g in range(GROUPS):
            cols = slice(g * gw, (g + 1) * gw)
            w = _causal(w_ref[g])
            s = _dot(w, v[:, cols]) + b_ref[:, g:g + 1]
            do = do_ref[:, cols]
            ds = do * u[:, cols]
            du_buf[:, cols] = do * s
            dw_ref[g] += _causal(_dot(ds, v[:, cols], "nt"))
            dv_buf[:, cols] = _dot(w, ds, "tn")
            db = db + jnp.where(lane == g, jnp.sum(ds, axis=1, keepdims=True), 0.0)
        db_ref[...] += db
        dhw, dvg, dvb = vjp((du_buf[...], dv_buf[...]))
        dhw_ref[...] = dhw.astype(dhw_ref.dtype)
        dvg_ref[...] += dvg
        dvb_ref[...] += dvb

    vec = _spec((1, GM_WIDTH), lambda i: (0, 0))
    w_spec = _spec((GROUPS, CHUNK, CHUNK), lambda i: (0, 0, 0))
    return _call("sgu_bwd", body, (t // CHUNK,),
                 [_spec((CHUNK, 2 * GM_WIDTH), lambda i: (i, 0)), vec, vec, w_spec, _spec((CHUNK, GROUPS), lambda i: (0, 0)),
                  _spec((CHUNK, GM_WIDTH), lambda i: (i, 0))],
                 [_spec((CHUNK, 2 * GM_WIDTH), lambda i: (i, 0)), vec, vec, w_spec, _spec((CHUNK, LANES), lambda i: (0, 0))],
                 [_sds((t, 2 * GM_WIDTH), MXU_DTYPE), _sds((1, GM_WIDTH), F32), _sds((1, GM_WIDTH), F32),
                  _sds((GROUPS, CHUNK, CHUNK), F32), _sds((CHUNK, LANES), F32)],
                 scratch=[pltpu.VMEM((CHUNK, GM_WIDTH), F32), pltpu.VMEM((CHUNK, GM_WIDTH), F32)]
                 )(hw, vg, vb, w_s, b_st, dout)


def _mm_nn(name, a, b, out_dtype, tm, a_col0=0, tn_want=1024, tk_want=2048, courier=None):
    t = a.shape[0]
    kk, n = b.shape
    tk = _tile(kk, tk_want, LANES)
    tn = _tile(n, tn_want, LANES)
    k0 = a_col0 // tk
    assert a_col0 % tk == 0
    outs, carried = _matmul(
        name, "nn", (t // tm, n // tn, kk // tk),
        [[(a, _spec((tm, tk), lambda i, j, k: (i, k0 + k)), b, _spec((tk, tn), lambda i, j, k: (k, j)))]],
        [((t, n), out_dtype, _spec((tm, tn), lambda i, j, k: (i, j)))], (tm, tn), courier=courier)
    return (outs[0], carried) if courier else outs[0]


def _mm_nt(name, pairs, out_dtype, tm, tn_want=1024, tk_want=2048, courier=None):
    t, kk = pairs[0][0].shape
    n = pairs[0][1].shape[0]
    tk = _tile(kk, tk_want, LANES)
    tn = _tile(n, tn_want, 16)
    grp = []
    for a, b, b_col0 in pairs:
        assert b_col0 % tk == 0 and a.shape == (t, kk)
        grp.append((a, _spec((tm, tk), lambda i, j, k: (i, k)), b,
                    _spec((tn, tk), functools.partial(lambda i, j, k, k0: (j, k0 + k), k0=b_col0 // tk))))
    outs, carried = _matmul(name, "nt", (t // tm, n // tn, kk // tk), [grp],
                            [((t, n), out_dtype, _spec((tm, tn), lambda i, j, k: (i, j)))], (tm, tn), courier=courier)
    return (outs[0], carried) if courier else outs[0]


def _mm_tn(name, a, bs, out_dtype, a_col0=0, m_width=None, tm_want=512, tn_want=1024, courier=None):
    t = a.shape[0]
    m = m_width or a.shape[1]
    n = bs[0].shape[1]
    tmw = _tile(m, tm_want, LANES)
    tn = _tile(n, tn_want, LANES)
    tkt = _tile(t, TN_ROWS, 16)
    assert a_col0 % tmw == 0
    i0 = a_col0 // tmw
    a_spec = _spec((tkt, tmw), lambda i, j, k: (k, i0 + i))
    groups = [[(a, a_spec, b, _spec((tkt, tn), lambda i, j, k: (k, j)))] for b in bs]
    outs, carried = _matmul(name, "tn", (m // tmw, n // tn, t // tkt), groups,
                            [((m, n), out_dtype, _spec((tmw, tn), lambda i, j, k: (i, j)))] * len(bs), (tmw, tn),
                            courier=courier)
    return (outs, carried) if courier else outs


_ANY = pl.BlockSpec(memory_space=pl.ANY)


def _comm_call(name, body, ins, out_shapes, sems, aliases=None):
    return pl.pallas_call(body, name=name, out_shape=out_shapes, in_specs=[_ANY] * len(ins),
                          out_specs=[_ANY] * len(out_shapes), input_output_aliases=aliases or {},
                          scratch_shapes=[pltpu.SemaphoreType.DMA((n,)) for n in sems])(*ins)


def _place():
    return lax.axis_index("x"), lax.axis_index("y"), lax.axis_index("c")


def _other_chips(x, y):
    return [(1 - x, y), (x, 1 - y), (1 - x, 1 - y)]


def _cast_into_slot(w, place, dtype):
    layers, rows, cols = w.shape
    half = rows // 2
    tr = _tile(half, 2 * STREAM_TILE_ELEMS // cols, 16)
    nt = half // tr

    def body(s, w_ref, o_ref):
        del s
        o_ref[...] = w_ref[...].astype(o_ref.dtype)

    return _call("cast_into_slot", body, (layers, 2, nt),
                 [_spec((None, tr, cols), lambda l, h, i, s: (l, h * nt + i, 0))],
                 _spec((None, None, None, tr, cols), lambda l, h, i, s: (s[0], l, h, i, 0)),
                 _sds((N_CHIPS, layers, 2, half, cols), dtype), prefetch=1)(place, w)


class _GatherChips:
    aliased = True

    def __init__(self, bufs):
        self.arrays = list(bufs)
        self.out_shapes = [_sds(b.shape, b.dtype) for b in bufs]
        self.sems = [3 * len(bufs)] * 4

    def _sent(self, buf, sems):
        x, y, c = _place()
        me = 2 * x + y
        return [[pltpu.make_async_remote_copy(
            src_ref=buf[t].at[me, :, c], dst_ref=buf[t].at[me, :, c], send_sem=sems[0].at[3 * t + r],
            recv_sem=sems[1].at[3 * t + r], device_id=(px, py, c), device_id_type=MESH)
            for r, (px, py) in enumerate(_other_chips(x, y))] for t in range(len(buf))]

    def start(self, _, buf, sems):
        for row in self._sent(buf, sems):
            for cp in row:
                cp.start()

    def finish(self, _, buf, sems):
        x, y, c = _place()
        sent = self._sent(buf, sems)
        passed = []
        for t in range(len(buf)):
            for r, (px, py) in enumerate(_other_chips(x, y)):
                landed = buf[t].at[2 * px + py, :, c]
                sent[t][r].wait_recv()
                cp = pltpu.make_async_remote_copy(
                    src_ref=landed, dst_ref=landed, send_sem=sems[2].at[3 * t + r], recv_sem=sems[3].at[3 * t + r],
                    device_id=(x, y, 1 - c), device_id_type=MESH)
                cp.start()
                passed.append(cp)
        for cp in passed:
            cp.wait()
        for row in sent:
            for cp in row:
                cp.wait_send()


class _ScatterChips:
    aliased = False

    def __init__(self, partials):
        self.arrays = list(partials)
        self.out_shapes = [_sds((3,) + p.shape[1:], p.dtype) for p in partials]
        self.sems = [3 * len(partials)] * 2

    def _copies(self, src, dst, sems):
        x, y, c = _place()
        return [pltpu.make_async_remote_copy(
            src_ref=src[t].at[2 * px + py], dst_ref=dst[t].at[r], send_sem=sems[0].at[3 * t + r],
            recv_sem=sems[1].at[3 * t + r], device_id=(px, py, c), device_id_type=MESH)
            for t in range(len(src)) for r, (px, py) in enumerate(_other_chips(x, y))]

    def start(self, src, dst, sems):
        for cp in self._copies(src, dst, sems):
            cp.start()

    def finish(self, src, dst, sems):
        for cp in self._copies(src, dst, sems):
            cp.wait()


def _run_courier(name, courier):
    n_in, n_out = len(courier.arrays), len(courier.out_shapes)

    def body(*refs):
        src, dst, sems = refs[:n_in], refs[n_in:n_in + n_out], refs[n_in + n_out:]
        courier.start(src, dst, sems)
        courier.finish(src, dst, sems)

    return _comm_call(name, body, courier.arrays, courier.out_shapes, courier.sems,
                      aliases={t: t for t in range(n_in)} if courier.aliased else None)


class _SwapHalves:
    aliased = False

    def __init__(self, grads):
        self.arrays = list(grads)
        self.out_shapes = [_sds((g.shape[0],) + g.shape[2:], g.dtype) for g in grads]
        self.sems = [len(grads)] * 2

    def _copies(self, src, dst, sems):
        x, y, c = _place()
        return [pltpu.make_async_remote_copy(
            src_ref=src[t].at[:, 1 - c], dst_ref=dst[t], send_sem=sems[0].at[t], recv_sem=sems[1].at[t],
            device_id=(x, y, 1 - c), device_id_type=MESH) for t in range(len(src))]

    def start(self, src, dst, sems):
        for cp in self._copies(src, dst, sems):
            cp.start()

    def finish(self, src, dst, sems):
        for cp in self._copies(src, dst, sems):
            cp.wait()


class _Both:
    aliased = False

    def __init__(self, first, second):
        assert not first.aliased and not second.aliased
        self.parts = (first, second)
        self.arrays = first.arrays + second.arrays
        self.out_shapes = first.out_shapes + second.out_shapes
        self.sems = first.sems + second.sems

    def _split(self, src, dst, sems):
        a = self.parts[0]
        na, no, ns = len(a.arrays), len(a.out_shapes), len(a.sems)
        return (src[:na], dst[:no], sems[:ns]), (src[na:], dst[no:], sems[ns:])

    def start(self, src, dst, sems):
        for part, args in zip(self.parts, self._split(src, dst, sems)):
            part.start(*args)

    def finish(self, src, dst, sems):
        for part, args in zip(self.parts, self._split(src, dst, sems)):
            part.finish(*args)


def _join_halves(grads):
    n = len(grads)

    def body(*refs):
        buf = refs[n:2 * n]
        send_sems, recv_sems = refs[2 * n:]
        x, y, c = _place()
        copies = [pltpu.make_async_remote_copy(
            src_ref=buf[t].at[:, c], dst_ref=buf[t].at[:, c], send_sem=send_sems.at[t], recv_sem=recv_sems.at[t],
            device_id=(x, y, 1 - c), device_id_type=MESH) for t in range(n)]
        for cp in copies:
            cp.start()
        for cp in copies:
            cp.wait()

    return _comm_call("join_halves", body, grads, [_sds(g.shape, g.dtype) for g in grads], [n, n],
                      aliases={t: t for t in range(n)})


def _gather_devices(slots):
    def body(src, buf, send_sems, recv_sems):
        del src
        x, y, c = _place()
        me = 4 * x + 2 * y + c
        copies = []
        for r in range(1, N_DEV):
            fx, fy, fc = (r >> 2) & 1, (r >> 1) & 1, r & 1
            peer = (1 - x if fx else x, 1 - y if fy else y, 1 - c if fc else c)
            copies.append(pltpu.make_async_remote_copy(
                src_ref=buf.at[me], dst_ref=buf.at[me], send_sem=send_sems.at[r - 1], recv_sem=recv_sems.at[r - 1],
                device_id=peer, device_id_type=MESH))
        for cp in copies:
            cp.start()
        for cp in copies:
            cp.wait()

    return _comm_call("gather_devices", body, [slots], [_sds(slots.shape, slots.dtype)], [N_DEV - 1, N_DEV - 1],
                      aliases={0: 0})[0]


def _add_halves(grad, got, place):
    _, _, rows, cols = grad.shape
    tr = _tile(rows, 2 * STREAM_TILE_ELEMS // cols, 16)

    def body(s, a, b, o):
        del s
        o[...] = (a[...].astype(F32) + b[...].astype(F32)).astype(o.dtype)

    return _call("add_halves", body, (N_CHIPS, rows // tr),
                 [_spec((None, None, tr, cols), lambda j, i, s: (j, s[1], i, 0)), _spec((None, tr, cols), lambda j, i, s: (j, i, 0))],
                 _spec((None, tr, cols), lambda j, i, s: (j, i, 0)), _sds((N_CHIPS, rows, cols), grad.dtype), prefetch=1
                 )(place, grad, got)


def _add_chips(partial, got, place, layer, layers, into=None):
    _, rows, cols = partial.shape
    tr = _tile(rows, STREAM_TILE_ELEMS // cols, 16)

    def body(s, a, b, *rest):
        o = rest[-1]
        o[...] = a[...].astype(F32) + b[0].astype(F32) + b[1].astype(F32) + b[2].astype(F32)

    in_specs = [_spec((None, tr, cols), lambda i, s: (s[0], i, 0)), _spec((3, tr, cols), lambda i, s: (0, i, 0))]
    args = [place, partial, got]
    if into is not None:
        in_specs.append(pl.BlockSpec(memory_space=pl.ANY))
        args.append(into)
    return _call("add_chips", body, (rows // tr,), in_specs,
                 _spec((None, None, tr, cols), lambda i, s: (layer, s[1], i, 0)), _sds((layers, 2, rows, cols), F32),
                 prefetch=1, aliases={3: 0} if into is not None else None)(*args)


def _add_devices(got):
    _, rows, cols = got.shape
    tr = _tile(rows, 512, 8)

    def body(a, o):
        total = a[0]
        for d in range(1, N_DEV):
            total = total + a[d]
        o[...] = total

    return _call("add_devices", body, (rows // tr,), [_spec((N_DEV, tr, cols), lambda i: (0, i, 0))],
                 _spec((tr, cols), lambda i: (i, 0)), _sds((rows, cols), F32))(got)


def _adamw(w, g, m, v):
    layers, rows, cols = w.shape
    tr = _tile(rows, STREAM_TILE_ELEMS // cols, 8)

    def body(w_ref, g_ref, m_ref, v_ref, d_ref, nm_ref, nv_ref):
        grad = g_ref[...]
        new_m = ADAM_B1 * m_ref[...] + (1.0 - ADAM_B1) * grad
        new_v = ADAM_B2 * v_ref[...] + (1.0 - ADAM_B2) * jnp.square(grad)
        m_hat = new_m / (1.0 - ADAM_B1 ** ADAM_STEP)
        v_hat = new_v / (1.0 - ADAM_B2 ** ADAM_STEP)
        d_ref[...] = -ADAM_LR * (m_hat / (jnp.sqrt(v_hat) + ADAM_EPS) + ADAM_WD * w_ref[...])
        nm_ref[...] = new_m
        nv_ref[...] = new_v

    blk = _spec((None, tr, cols), lambda l, i: (l, i, 0))
    return _call("adamw", body, (layers, rows // tr), [blk] * 4, [blk] * 3, [_sds(w.shape, F32)] * 3)(w, g, m, v)


def _rope_layout(w):
    z = jnp.zeros(w.shape[:-1] + (ROPE_HALF,), w.dtype)
    return jnp.concatenate([w[..., :ROPE_HALF], z, w[..., ROPE_HALF:], z], axis=-1)


def _rope_layout_inv(w):
    return jnp.concatenate([w[..., :ROPE_HALF], w[..., 2 * ROPE_HALF:3 * ROPE_HALF]], axis=-1)


def _cols_from_chips(g):
    return jnp.moveaxis(g, 0, 1).reshape(g.shape[1], N_CHIPS * g.shape[2])


def _cols_to_chips(w):
    return jnp.moveaxis(w.reshape(w.shape[0], N_CHIPS, w.shape[1] // N_CHIPS), 1, 0)


def _w_in_layout(w):
    off_kr = Q_LORA + KV_LORA
    return jnp.concatenate([w[:, :off_kr], w[:, off_kr + QK_ROPE:], _rope_layout(w[:, off_kr:off_kr + QK_ROPE])], axis=1)


def _w_in_layout_inv(dw):
    return jnp.concatenate([dw[:, :P_A], _rope_layout_inv(dw[:, P_KR:]), dw[:, P_A:P_KR]], axis=1)


def _w_uq_layout(w):
    w = w.reshape(Q_LORA, HEADS, QK_NOPE + QK_ROPE)
    return jnp.concatenate([w[..., :QK_NOPE].reshape(Q_LORA, HEADS * QK_NOPE),
                            _rope_layout(w[..., QK_NOPE:]).reshape(Q_LORA, HEADS * LANES)], axis=1)


def _w_uq_layout_inv(d_nope, d_rope):
    d_nope = d_nope.reshape(Q_LORA, HEADS, QK_NOPE)
    d_rope = _rope_layout_inv(d_rope.reshape(Q_LORA, HEADS, LANES))
    return jnp.concatenate([d_nope, d_rope], axis=-1).reshape(Q_LORA, HEADS * (QK_NOPE + QK_ROPE))


def _to_lanes(a):
    flat = a.reshape(-1)
    pad = (-flat.shape[0]) % LANES
    if pad:
        flat = jnp.concatenate([flat, jnp.zeros((pad,), flat.dtype)])
    return flat.reshape(-1, LANES)


PACK_ROWS = 64


def _pack(arrays):
    parts, ranges, row = [], [], 0
    for a in arrays:
        p = _to_lanes(a)
        parts.append(p)
        ranges.append((row, row + p.shape[0]))
        row += p.shape[0]
    pad = (-row) % PACK_ROWS
    if pad:
        parts.append(jnp.zeros((pad, LANES), F32))
    return jnp.concatenate(parts, axis=0), ranges


def _unpack(packed, rng, shape):
    n = 1
    for s in shape:
        n *= s
    return packed[rng[0]:rng[1]].reshape(-1)[:n].reshape(shape)


def kernel(x, positions, ffn_a_pre_g, ffn_a_post_g, ffn_a_w_gate, ffn_a_w_up, ffn_a_w_down, ffn_b_pre_g, ffn_b_post_g, ffn_b_w_gate, ffn_b_w_up, ffn_b_w_down, even_pre_g, even_post_g, even_w_in, even_q_norm_g, even_kv_norm_g, even_w_uq, even_w_ukv, even_conv_w, even_conv_b, even_conv_norm_g, even_conv_norm_b, even_w_out, odd_pre_g, odd_post_g, odd_w_in, odd_v_norm_g, odd_v_norm_b, odd_w_s, odd_b_s, odd_w_out, loss_target, m_ffn_a_pre_g, m_ffn_a_post_g, m_ffn_a_w_gate, m_ffn_a_w_up, m_ffn_a_w_down, m_ffn_b_pre_g, m_ffn_b_post_g, m_ffn_b_w_gate, m_ffn_b_w_up, m_ffn_b_w_down, m_even_pre_g, m_even_post_g, m_even_w_in, m_even_q_norm_g, m_even_kv_norm_g, m_even_w_uq, m_even_w_ukv, m_even_conv_w, m_even_conv_b, m_even_conv_norm_g, m_even_conv_norm_b, m_even_w_out, m_odd_pre_g, m_odd_post_g, m_odd_w_in, m_odd_v_norm_g, m_odd_v_norm_b, m_odd_w_s, m_odd_b_s, m_odd_w_out, v_ffn_a_pre_g, v_ffn_a_post_g, v_ffn_a_w_gate, v_ffn_a_w_up, v_ffn_a_w_down, v_ffn_b_pre_g, v_ffn_b_post_g, v_ffn_b_w_gate, v_ffn_b_w_up, v_ffn_b_w_down, v_even_pre_g, v_even_post_g, v_even_w_in, v_even_q_norm_g, v_even_kv_norm_g, v_even_w_uq, v_even_w_ukv, v_even_conv_w, v_even_conv_b, v_even_conv_norm_g, v_even_conv_norm_b, v_even_w_out, v_odd_pre_g, v_odd_post_g, v_odd_w_in, v_odd_v_norm_g, v_odd_v_norm_b, v_odd_w_s, v_odd_b_s, v_odd_w_out):
    given = dict(locals())
    w = {n: given[n] for n in WEIGHTS}
    batch, seq, d = x.shape
    t = batch * seq
    tm = _tile(t, 512, 16)
    tr = _tile(t, 512, 16)
    chip = 2 * lax.axis_index("x") + lax.axis_index("y")
    place = jnp.stack([chip, lax.axis_index("c")]).astype(jnp.int32)

    small_shard, small_shard_rng = _pack([w[n] for n in SMALL_SHARDED])
    wbuf = {(n, layer): _cast_into_slot(w[n][layer:layer + 1], place, MXU_DTYPE)
            for n in BIG for layer in range(w[n].shape[0])}
    wbuf['small'] = _cast_into_slot(small_shard[None], place, F32)

    def ffn_keys(tag, layer):
        return [(f'ffn_{tag}_w_{part}', layer) for part in ('gate', 'up', 'down')]

    def gather(keys):
        return _GatherChips([wbuf[k] for k in keys]), keys

    def landed(order, results):
        for k, r in zip(order[1], results):
            wbuf[k] = r

    def weight(key):
        n, _ = key
        return wbuf[key].reshape((N_CHIPS, 1) + w[n].shape[1:])

    first = gather(ffn_keys('a', 0)[:2] + ['small'])
    landed(first, _run_courier("gather_chips", first[0]))
    shards = wbuf['small'].reshape((N_CHIPS,) + small_shard.shape)
    full = dict(w)
    for n, rng in zip(SMALL_SHARDED, small_shard_rng):
        per_chip = [_unpack(shards[j], rng, w[n].shape) for j in range(N_CHIPS)]
        full[n] = jnp.concatenate(per_chip, axis=-1)

    inv_freq = ROPE_THETA ** (-jnp.arange(0, QK_ROPE, 2, dtype=F32) / QK_ROPE)
    ang = positions.astype(F32).reshape(t, 1) * inv_freq
    zeros = jnp.zeros((t, ROPE_HALF), F32)
    cos = jnp.concatenate([jnp.cos(ang), zeros, jnp.cos(ang), zeros], axis=1)
    sin = jnp.concatenate([-jnp.sin(ang), zeros, jnp.sin(ang), zeros], axis=1)

    def ffn_args(tag, layer):
        gate, up, down = (weight(k) for k in ffn_keys(tag, layer))
        return (w[f'ffn_{tag}_pre_g'][layer:layer + 1], w[f'ffn_{tag}_post_g'][layer:layer + 1], gate, up, down, 0, tm, tr)

    def ffn_forward(xin, tag, layer, up_keys, down_keys, h=None, next_g=None):
        orders = {'up': gather(ffn_keys(tag, layer)[2:] + up_keys)}
        if down_keys:
            orders['down'] = gather(down_keys)
        out, saved, carried, h_next = _ffn_fwd(xin, *ffn_args(tag, layer), couriers={k: o[0] for k, o in orders.items()},
                                               wd_landed=0, h=h, next_g=next_g)
        for k, o in orders.items():
            landed(o, carried[k])
        return out, saved, h_next

    even_keys = [('even_w_in', 0), ('even_w_uq', 0), ('even_w_ukv', 0), ('even_w_out', 0)]
    odd_keys = [('odd_w_in', 0), ('odd_w_out', 0)]

    xs = x.reshape(t, d)
    x1, ffn_a0, h_e = ffn_forward(xs, 'a', 0, even_keys, ffn_keys('b', 0)[:1], next_g=w['even_pre_g'])
    w_in = _w_in_layout(_cols_from_chips(weight(('even_w_in', 0))[:, 0]))
    w_uq = _w_uq_layout(_cols_from_chips(weight(('even_w_uq', 0))[:, 0]))
    w_ukv = _cols_from_chips(weight(('even_w_ukv', 0))[:, 0])
    w_out_e = weight(('even_w_out', 0))[:, 0].reshape(d, d)
    order = gather(ffn_keys('b', 0)[1:2])
    p, res = _mm_nn("mm_w_in", h_e, w_in, F32, tm, tn_want=640, courier=order[0])
    landed(order, res)
    lat = _rows("latent_norm", lambda r, c: ([jnp.concatenate([_rmsnorm(r[0], c[0]), _rmsnorm(r[1], c[1])], axis=1)], []),
                [(p, Q_LORA, 0), (p, KV_LORA, 1)], [w['even_q_norm_g'], w['even_kv_norm_g']], [(Q_LORA + KV_LORA, MXU_DTYPE)], [], tr)[0]
    q_all = _mm_nn("mm_w_uq", lat, w_uq, F32, tm, a_col0=0, tk_want=Q_LORA)
    kv = _mm_nn("mm_w_ukv", lat, w_ukv, MXU_DTYPE, tm, a_col0=Q_LORA, tk_want=KV_LORA)
    cat = _attn_fwd(q_all, kv, p, cos, sin, batch, seq)
    cat = _conv_fwd(p, cat, full['even_conv_w'][0], w['even_conv_b'], w['even_conv_norm_g'], w['even_conv_norm_b'], batch, seq)
    y_e = _mm_nn("mm_w_out", cat, w_out_e, F32, tm)
    x2 = _rms_residual(x1, y_e, w['even_post_g'], 1.0, tr)
    x3, ffn_b0, h_a1 = ffn_forward(x2, 'b', 0, ffn_keys('a', 1)[:1], ffn_keys('a', 1)[1:2], next_g=w['ffn_a_pre_g'][1:2])
    x4, ffn_a1, h_o = ffn_forward(x3, 'a', 1, odd_keys, ffn_keys('b', 1)[:1], h=h_a1, next_g=full['odd_pre_g'])
    w_out_o = weight(('odd_w_out', 0))[:, 0].reshape(GM_WIDTH, d)
    w_in_o = weight(('odd_w_in', 0))
    ns = w_in_o.shape[3]
    tk_o = _tile(d, 2048, LANES)
    order = gather(ffn_keys('b', 1)[1:2])
    (hw,), res = _matmul("mm_w_in_odd", "nn", (t // tm, N_CHIPS, d // tk_o),
                         [[(h_o, _spec((tm, tk_o), lambda i, j, k: (i, k)), w_in_o, _spec((None, None, tk_o, ns), lambda i, j, k: (j, 0, k, 0)))]],
                         [((t, N_CHIPS * ns), F32, _spec((tm, ns), lambda i, j, k: (i, j)))], (tm, ns), courier=order[0])
    landed(order, res)
    b_st = w['odd_b_s'][0].T
    sg = _sgu_fwd(hw, full['odd_v_norm_g'], full['odd_v_norm_b'], w['odd_w_s'][0], b_st)
    y_o = _mm_nn("mm_w_out", sg, w_out_o, F32, tm)
    x5 = _rms_residual(x4, y_o, full['odd_post_g'], 1.0, tr)
    x6, ffn_b1, _ = ffn_forward(x5, 'b', 1, [], [])

    dy, sq_err = _loss_head(x6, loss_target.reshape(t, d), tr)
    loss = lax.psum(0.5 * sq_err[0, 0] / d, ("x", "y", "c"))

    grads = {}
    received = {}

    def swap(named):
        items = [(k, g.reshape(N_CHIPS, 2, g.shape[1] // 2, g.shape[2])) for k, g in named]
        return _SwapHalves([g for _, g in items]), items

    def core_sums_of(order, results):
        return [(k, _add_halves(g, r, place)) for (k, g), r in zip(order[1], results)]

    def core_sums(named):
        order = swap(named)
        return core_sums_of(order, _run_courier("swap_halves", order[0]))

    def scatter(named):
        return _ScatterChips([pt for _, pt in named]), named

    def arrived(order, results):
        for (k, pt), r in zip(order[1], results):
            received[k] = (pt, r)

    def ffn_grads(tag, layer):
        return list(zip(ffn_keys(tag, layer), fg[(tag, layer)][2:]))

    fg = {}
    made = {}
    dx, fg[('b', 1)], _ = _ffn_bwd(dy, ffn_b1, *ffn_args('b', 1))
    swap_b1 = swap(ffn_grads('b', 1))
    dy_o, grads['odd_post_g'] = _rms_bwd(y_o, full['odd_post_g'], dx, 1.0, MXU_DTYPE, tr)
    dsg, res = _mm_nt("mm_dsg", [(dy_o, w_out_o, 0)], F32, tm, courier=swap_b1[0])
    pend = core_sums_of(swap_b1, res)
    (g_w_out_o,) = _mm_tn("mm_dw_out", sg, [dy_o], MXU_DTYPE)
    dhw, grads['odd_v_norm_g'], grads['odd_v_norm_b'], g_ws, g_bst = _sgu_bwd(
        hw, full['odd_v_norm_g'], full['odd_v_norm_b'], w['odd_w_s'][0], b_st, dsg)
    grads['odd_w_s'] = g_ws[None]
    grads['odd_b_s'] = g_bst[:, :GROUPS].T[None]
    tn_o = _tile(d, 1024, LANES)
    order = scatter(pend[0:1])
    (dh_o,), res = _matmul("mm_dh_odd", "nt", (t // tm, d // tn_o, N_CHIPS),
                           [[(dhw, _spec((tm, ns), lambda i, j, k: (i, k)), w_in_o, _spec((None, None, tn_o, ns), lambda i, j, k: (k, 0, j, 0)))]],
                           [((t, d), F32, _spec((tm, tn_o), lambda i, j, k: (i, j)))], (tm, tn_o), courier=order[0])
    arrived(order, res)
    tmw = _tile(d, 512, LANES)
    tkt = _tile(t, TN_ROWS, 16)
    order = scatter(pend[1:2])
    (g_w_in_o,), res = _matmul("mm_dw_in_odd", "tn", (d // tmw, N_CHIPS, t // tkt),
                               [[(h_o, _spec((tkt, tmw), lambda i, j, k: (k, i)), dhw, _spec((tkt, ns), lambda i, j, k: (k, j)))]],
                               [((N_CHIPS, d, ns), MXU_DTYPE, _spec((None, tmw, ns), lambda i, j, k: (j, i, 0)))], (tmw, ns),
                               courier=order[0])
    arrived(order, res)
    swap_odd = swap([(('odd_w_in', 0), g_w_in_o),
                     (('odd_w_out', 0), g_w_out_o.reshape((N_CHIPS,) + w['odd_w_out'].shape[1:]))])
    dx, grads['odd_pre_g'] = _rms_bwd(x4, full['odd_pre_g'], dh_o, 1.0, F32, tr, resid=dx)

    def scatter_after_dact(name, first, swapped):
        def make(known):
            res = known["carried"]["dact"]
            arrived(first, res[:1])
            made[name] = scatter(core_sums_of(swapped, res[1:]))
            return made[name][0]
        return make

    first = scatter(pend[2:3])
    dx, fg[('a', 1)], carried = _ffn_bwd(dx, ffn_a1, *ffn_args('a', 1), couriers={
        'dact': _Both(first[0], swap_odd[0]), 'dwdown': scatter_after_dact('odd', first, swap_odd)})
    arrived(made['odd'], carried['dwdown'])
    swap_a1 = swap(ffn_grads('a', 1))

    def a1_scatter(idx):
        def make(known):
            if 'a1' not in made:
                made['a1'] = core_sums_of(swap_a1, known["carried"]["dact"])
            made[('a1', idx)] = scatter(made['a1'][idx:idx + 1])
            return made[('a1', idx)][0]
        return make

    dx, fg[('b', 0)], carried = _ffn_bwd(dx, ffn_b0, *ffn_args('b', 0), couriers={
        'dact': swap_a1[0], 'dwdown': a1_scatter(0), 'dh': a1_scatter(1), 'dwup': a1_scatter(2)})
    for idx, name in enumerate(('dwdown', 'dh', 'dwup')):
        arrived(made[('a1', idx)], carried[name])
    swap_b0 = swap(ffn_grads('b', 0))
    dy_e, grads['even_post_g'] = _rms_bwd(y_e, w['even_post_g'], dx, 1.0, MXU_DTYPE, tr)
    d_cat, res = _mm_nt("mm_dcat", [(dy_e, w_out_e, 0)], F32, tm, courier=swap_b0[0])
    pend = core_sums_of(swap_b0, res)
    (g_w_out_e,) = _mm_tn("mm_dw_out", cat, [dy_e], MXU_DTYPE)
    dqn, dqp, dkv, dkr = _attn_bwd(q_all, kv, p, cos, sin, d_cat, batch, seq)
    hq = HEADS * LANES
    d_latq = _mm_nt("mm_dlat_q", [(dqn, w_uq, 0), (dqp, w_uq, hq)], F32, tm, tn_want=Q_LORA)
    d_latkv = _mm_nt("mm_dlat_kv", [(dkv, w_ukv, 0)], F32, tm, tn_want=KV_LORA)
    g_uq_n, g_uq_r = _mm_tn("mm_dw_uq", lat, [dqn, dqp], MXU_DTYPE, a_col0=0, m_width=Q_LORA)
    (g_ukv,) = _mm_tn("mm_dw_ukv", lat, [dkv], MXU_DTYPE, a_col0=Q_LORA, m_width=KV_LORA)

    def latent_bwd(r, c):
        _, vjp_q = jax.vjp(_rmsnorm, r[0], c[0])
        _, vjp_kv = jax.vjp(_rmsnorm, r[1], c[1])
        dq, dqg = vjp_q(r[2])
        dk, dkg = vjp_kv(r[3])
        return [jnp.concatenate([dq, dk], axis=1)], [dqg, dkg]

    d_lat, grads['even_q_norm_g'], grads['even_kv_norm_g'] = _rows(
        "latent_norm_bwd", latent_bwd, [(p, Q_LORA, 0), (p, KV_LORA, 1), (d_latq, Q_LORA, 0), (d_latkv, KV_LORA, 0)],
        [w['even_q_norm_g'], w['even_kv_norm_g']], [(Q_LORA + KV_LORA, MXU_DTYPE)], [(1, Q_LORA), (1, KV_LORA)], tr)
    da, dgate, g_conv_w, grads['even_conv_b'], grads['even_conv_norm_g'], grads['even_conv_norm_b'] = _conv_bwd(
        p, d_cat, full['even_conv_w'][0], w['even_conv_b'], w['even_conv_norm_g'], w['even_conv_norm_b'], batch, seq)
    dp = jnp.concatenate([d_lat, da, dgate, dkr.astype(MXU_DTYPE)], axis=1)
    order = scatter(pend[0:1])
    dh_e, res = _mm_nt("mm_dh_even", [(dp, w_in, 0)], F32, tm, tk_want=640, courier=order[0])
    arrived(order, res)
    order = scatter(pend[1:2])
    (g_w_in,), res = _mm_tn("mm_dw_in", h_e, [dp], MXU_DTYPE, tn_want=640, courier=order[0])
    arrived(order, res)
    swap_even = swap([(('even_w_in', 0), _cols_to_chips(_w_in_layout_inv(g_w_in))),
                      (('even_w_uq', 0), _cols_to_chips(_w_uq_layout_inv(g_uq_n, g_uq_r))),
                      (('even_w_ukv', 0), _cols_to_chips(g_ukv)),
                      (('even_w_out', 0), g_w_out_e.reshape((N_CHIPS,) + w['even_w_out'].shape[1:]))])
    dx, grads['even_pre_g'] = _rms_bwd(x1, w['even_pre_g'], dh_e, 1.0, F32, tr, resid=dx)
    first = scatter(pend[2:3])

    def after_w_down(known):
        made['a0_down'] = scatter(core_sums([(ffn_keys('a', 0)[2], known["dwd"])]))
        return made['a0_down'][0]

    def after_w_up(known):
        made['a0_up'] = scatter(core_sums(list(zip(ffn_keys('a', 0)[:2], (known["dwg"], known["dwu"])))))
        return made['a0_up'][0]

    dx, fg[('a', 0)], carried = _ffn_bwd(dx, ffn_a0, *ffn_args('a', 0), weights_first=True, couriers={
        'dact': _Both(first[0], swap_even[0]), 'dwdown': scatter_after_dact('even', first, swap_even),
        'dwup': after_w_down, 'dh': after_w_up})
    arrived(made['even'], carried['dwdown'])
    arrived(made['a0_down'], carried['dwup'])
    arrived(made['a0_up'], carried['dh'])
    grad_x = dx.reshape(batch, seq, d)

    reduced = []
    for n in BIG:
        layers = w[n].shape[0]
        buf = None
        for layer in range(layers):
            pt, r = received[(n, layer)]
            buf = _add_chips(pt, r, place, layer, layers, into=buf)
        reduced.append(buf)
    joined = _join_halves(reduced)
    for n, g in zip(BIG, joined):
        grads[n] = g.reshape(w[n].shape)

    for tag in ('a', 'b'):
        grads[f'ffn_{tag}_pre_g'] = jnp.concatenate([fg[(tag, 0)][0], fg[(tag, 1)][0]], axis=0)
        grads[f'ffn_{tag}_post_g'] = jnp.concatenate([fg[(tag, 0)][1], fg[(tag, 1)][1]], axis=0)
    grads['even_conv_w'] = g_conv_w[None]
    packed, rngs = _pack([grads[n] for n in SMALL])
    device = 2 * chip + lax.axis_index("c")
    slots = lax.dynamic_update_slice_in_dim(jnp.zeros((N_DEV,) + packed.shape, F32), packed[None], device, axis=0)
    summed = _add_devices(_gather_devices(slots))
    for n, rng in zip(SMALL, rngs):
        g = _unpack(summed, rng, grads[n].shape)
        if n in SMALL_SHARDED:
            width = w[n].shape[-1]
            g = lax.dynamic_slice_in_dim(g, chip * width, width, axis=g.ndim - 1)
        grads[n] = g

    delta, new_m, new_v = {}, {}, {}
    for n in BIG:
        delta[n], new_m[n], new_v[n] = _adamw(w[n], grads[n], given['m_' + n], given['v_' + n])
    packs = [_pack([src[n] for n in SMALL])[0][None] for src in
             (w, grads, {n: given['m_' + n] for n in SMALL}, {n: given['v_' + n] for n in SMALL})]
    _, rngs = _pack([w[n] for n in SMALL])
    small_out = _adamw(*packs)
    for n, rng in zip(SMALL, rngs):
        delta[n], new_m[n], new_v[n] = (_unpack(o[0], rng, w[n].shape) for o in small_out)

    return (loss, grad_x, *[grads[n] for n in WEIGHTS], *[delta[n] for n in WEIGHTS],
            *[new_m[n] for n in WEIGHTS], *[new_v[n] for n in WEIGHTS])
```
